```python
import jax, jax.numpy as jnp
from jax import lax
import numpy as np

D_MODEL = 1024
BATCH = 8
SEQ = 8192
DEPTH = 2

A_WIDTH = D_MODEL // 2
A_HEADS = 8
CONV_WIDTH = 3
B_WIDTH = D_MODEL // 2
POOL_WINDOWS = (2, 4, 8, 16)
B_GROUPS = len(POOL_WINDOWS)
B_GROUP_DIM = B_WIDTH // B_GROUPS
EVEN_IN = 4 * A_WIDTH + 2 * B_WIDTH
EVEN_MIX = A_WIDTH + B_WIDTH
C_WIDTH = D_MODEL
C_HEADS = 8
C_HEAD_DIM = C_WIDTH // C_HEADS
CHUNK = 128
ODD_IN = 3 * C_WIDTH
N_EVEN = (DEPTH + 1) // 2
N_ODD = DEPTH // 2
EPS = 1e-6

kernel_name = "hybrid_conv_pool_gmlp_trunk"


def rmsnorm(x, g):
    xf = x.astype(jnp.float32)
    y = xf * lax.rsqrt(jnp.mean(xf * xf, axis=-1, keepdims=True) + EPS)
    return (y * g.astype(jnp.float32)).astype(x.dtype)


def layernorm(x, g, b):
    xf = x.astype(jnp.float32)
    mu = jnp.mean(xf, axis=-1, keepdims=True)
    var = jnp.mean(jnp.square(xf - mu), axis=-1, keepdims=True)
    y = (xf - mu) * lax.rsqrt(var + EPS)
    return (y * g.astype(jnp.float32) + b.astype(jnp.float32)).astype(x.dtype)


def short_gated_conv(xa, gb, gc, conv_w):
    S = xa.shape[1]
    h = gc * xa
    hp = jnp.pad(h, ((0, 0), (CONV_WIDTH - 1, 0), (0, 0)))
    conv = sum(conv_w[k] * hp[:, k:k + S] for k in range(CONV_WIDTH))
    return gb * conv


def multiscale_pool(xp, pool_w, pool_scale):
    Bsz, S, _ = xp.shape
    xf = xp.astype(jnp.float32)
    cs = jnp.cumsum(xf, axis=1)
    pos = jnp.arange(S)
    outs = []
    for g, w in enumerate(POOL_WINDOWS):
        sl = slice(g * B_GROUP_DIM, (g + 1) * B_GROUP_DIM)
        cs_g = cs[..., sl]
        lower = jnp.pad(cs_g, ((0, 0), (w, 0), (0, 0)))[:, :S]
        count = jnp.minimum(pos + 1, w).astype(jnp.float32)[None, :, None]
        outs.append((cs_g - lower) / count - xf[..., sl])
    pooled = jnp.stack(outs, axis=2).astype(xp.dtype)
    mixed = jnp.einsum('bsgc,gcd->bsgd', pooled, pool_w)
    return mixed.reshape(Bsz, S, B_WIDTH) * pool_scale


def even_layer(h, w_in, conv_w, pool_w, pool_scale, w_out):
    proj = h @ w_in
    xa, gb, gc, za, xp, zp = jnp.split(
        proj, np.cumsum([A_WIDTH] * 4 + [B_WIDTH]).tolist(), axis=-1)
    ya = short_gated_conv(xa, gb, gc, conv_w) * jax.nn.silu(za)
    yb = multiscale_pool(xp, pool_w, pool_scale) * jax.nn.silu(zp)
    return jnp.concatenate([ya, yb], axis=-1) @ w_out


def odd_layer(h, w_in, ln_g, ln_b, w_s, b_s, w_out):
    Bsz, S, _ = h.shape
    proj = h @ w_in
    u, v, z = jnp.split(proj, 3, axis=-1)
    v = layernorm(v, ln_g, ln_b)
    vc = v.reshape(Bsz, S // CHUNK, CHUNK, C_HEADS, C_HEAD_DIM)
    ws = jnp.tril(w_s)
    sv = jnp.einsum('hts,bnshc->bnthc', ws, vc) + b_s.T[None, None, :, :, None]
    y = u * sv.reshape(Bsz, S, C_WIDTH) * jax.nn.silu(z)
    return y @ w_out


def _fwd_setup_inputs(seed: int = 0) -> dict:
    key = jax.random.key(seed)
    ks = jax.random.split(key, 16)
    f32 = jnp.float32
    nrm = lambda k, shape, s: jax.random.normal(k, shape, f32) * s
    return {
        "x": nrm(ks[0], (BATCH, SEQ, D_MODEL), 1.0),
        "pre_norm": 1.0 + nrm(ks[1], (DEPTH, D_MODEL), 0.05),
        "post_norm": 1.0 + nrm(ks[2], (DEPTH, D_MODEL), 0.05),
        "even_w_in": nrm(ks[3], (N_EVEN, D_MODEL, EVEN_IN), D_MODEL ** -0.5),
        "even_conv_w": nrm(ks[4], (N_EVEN, CONV_WIDTH, A_WIDTH), CONV_WIDTH ** -0.5),
        "even_pool_w": nrm(ks[5], (N_EVEN, B_GROUPS, B_GROUP_DIM, B_GROUP_DIM), B_GROUP_DIM ** -0.5),
        "even_pool_scale": 1.0 + nrm(ks[6], (N_EVEN, B_WIDTH), 0.1),
        "even_w_out": nrm(ks[7], (N_EVEN, EVEN_MIX, D_MODEL), EVEN_MIX ** -0.5),
        "odd_w_in": nrm(ks[8], (N_ODD, D_MODEL, ODD_IN), D_MODEL ** -0.5),
        "odd_ln_g": 1.0 + nrm(ks[9], (N_ODD, C_WIDTH), 0.05),
        "odd_ln_b": nrm(ks[10], (N_ODD, C_WIDTH), 0.02),
        "odd_w_s": nrm(ks[11], (N_ODD, C_HEADS, CHUNK, CHUNK), CHUNK ** -0.5),
        "odd_b_s": 1.0 + nrm(ks[12], (N_ODD, C_HEADS, CHUNK), 0.1),
        "odd_w_out": nrm(ks[13], (N_ODD, C_WIDTH, D_MODEL), C_WIDTH ** -0.5),
    }


def _fwd_reference(x, pre_norm, post_norm, even_w_in, even_conv_w, even_pool_w,
              even_pool_scale, even_w_out, odd_w_in, odd_ln_g, odd_ln_b,
              odd_w_s, odd_b_s, odd_w_out):
    for i in range(DEPTH):
        h = rmsnorm(x, pre_norm[i])
        j = i // 2
        if i % 2 == 0:
            m = even_layer(h, even_w_in[j], even_conv_w[j], even_pool_w[j],
                           even_pool_scale[j], even_w_out[j])
        else:
            m = odd_layer(h, odd_w_in[j], odd_ln_g[j], odd_ln_b[j],
                          odd_w_s[j], odd_b_s[j], odd_w_out[j])
        x = x + rmsnorm(m, post_norm[i])
    return x


import jax as _jax
import jax.numpy as _jnp

TWIN_FORMAT = 'train_step'
FWD_PARAMS = ['x', 'pre_norm', 'post_norm', 'even_w_in', 'even_conv_w', 'even_pool_w', 'even_pool_scale', 'even_w_out', 'odd_w_in', 'odd_ln_g', 'odd_ln_b', 'odd_w_s', 'odd_b_s', 'odd_w_out']
TWIN_WEIGHTS = ['pre_norm', 'post_norm', 'even_w_in', 'even_conv_w', 'even_pool_w', 'even_pool_scale', 'even_w_out', 'odd_w_in', 'odd_ln_g', 'odd_ln_b', 'odd_w_s', 'odd_b_s', 'odd_w_out']
TWIN_DIFF_INPUT = 'x'
TWIN_INPUTS = ['x', 'pre_norm', 'post_norm', 'even_w_in', 'even_conv_w', 'even_pool_w', 'even_pool_scale', 'even_w_out', 'odd_w_in', 'odd_ln_g', 'odd_ln_b', 'odd_w_s', 'odd_b_s', 'odd_w_out', 'loss_target', 'm_pre_norm', 'm_post_norm', 'm_even_w_in', 'm_even_conv_w', 'm_even_pool_w', 'm_even_pool_scale', 'm_even_w_out', 'm_odd_w_in', 'm_odd_ln_g', 'm_odd_ln_b', 'm_odd_w_s', 'm_odd_b_s', 'm_odd_w_out', 'v_pre_norm', 'v_post_norm', 'v_even_w_in', 'v_even_conv_w', 'v_even_pool_w', 'v_even_pool_scale', 'v_even_w_out', 'v_odd_w_in', 'v_odd_ln_g', 'v_odd_ln_b', 'v_odd_w_s', 'v_odd_b_s', 'v_odd_w_out']
TWIN_OUTPUTS = ['loss', 'grad_x', 'grad_pre_norm', 'grad_post_norm', 'grad_even_w_in', 'grad_even_conv_w', 'grad_even_pool_w', 'grad_even_pool_scale', 'grad_even_w_out', 'grad_odd_w_in', 'grad_odd_ln_g', 'grad_odd_ln_b', 'grad_odd_w_s', 'grad_odd_b_s', 'grad_odd_w_out', 'delta_pre_norm', 'delta_post_norm', 'delta_even_w_in', 'delta_even_conv_w', 'delta_even_pool_w', 'delta_even_pool_scale', 'delta_even_w_out', 'delta_odd_w_in', 'delta_odd_ln_g', 'delta_odd_ln_b', 'delta_odd_w_s', 'delta_odd_b_s', 'delta_odd_w_out', 'new_m_pre_norm', 'new_m_post_norm', 'new_m_even_w_in', 'new_m_even_conv_w', 'new_m_even_pool_w', 'new_m_even_pool_scale', 'new_m_even_w_out', 'new_m_odd_w_in', 'new_m_odd_ln_g', 'new_m_odd_ln_b', 'new_m_odd_w_s', 'new_m_odd_b_s', 'new_m_odd_w_out', 'new_v_pre_norm', 'new_v_post_norm', 'new_v_even_w_in', 'new_v_even_conv_w', 'new_v_even_pool_w', 'new_v_even_pool_scale', 'new_v_even_w_out', 'new_v_odd_w_in', 'new_v_odd_ln_g', 'new_v_odd_ln_b', 'new_v_odd_w_s', 'new_v_odd_b_s', 'new_v_odd_w_out']
TWIN_LEAF_KINDS = {'loss': 'loss', 'grad_x': 'grad_x', 'grad_pre_norm': 'grad_w', 'grad_post_norm': 'grad_w', 'grad_even_w_in': 'grad_w', 'grad_even_conv_w': 'grad_w', 'grad_even_pool_w': 'grad_w', 'grad_even_pool_scale': 'grad_w', 'grad_even_w_out': 'grad_w', 'grad_odd_w_in': 'grad_w', 'grad_odd_ln_g': 'grad_w', 'grad_odd_ln_b': 'grad_w', 'grad_odd_w_s': 'grad_w', 'grad_odd_b_s': 'grad_w', 'grad_odd_w_out': 'grad_w', 'delta_pre_norm': 'delta_w', 'delta_post_norm': 'delta_w', 'delta_even_w_in': 'delta_w', 'delta_even_conv_w': 'delta_w', 'delta_even_pool_w': 'delta_w', 'delta_even_pool_scale': 'delta_w', 'delta_even_w_out': 'delta_w', 'delta_odd_w_in': 'delta_w', 'delta_odd_ln_g': 'delta_w', 'delta_odd_ln_b': 'delta_w', 'delta_odd_w_s': 'delta_w', 'delta_odd_b_s': 'delta_w', 'delta_odd_w_out': 'delta_w', 'new_m_pre_norm': 'new_m', 'new_m_post_norm': 'new_m', 'new_m_even_w_in': 'new_m', 'new_m_even_conv_w': 'new_m', 'new_m_even_pool_w': 'new_m', 'new_m_even_pool_scale': 'new_m', 'new_m_even_w_out': 'new_m', 'new_m_odd_w_in': 'new_m', 'new_m_odd_ln_g': 'new_m', 'new_m_odd_ln_b': 'new_m', 'new_m_odd_w_s': 'new_m', 'new_m_odd_b_s': 'new_m', 'new_m_odd_w_out': 'new_m', 'new_v_pre_norm': 'new_v', 'new_v_post_norm': 'new_v', 'new_v_even_w_in': 'new_v', 'new_v_even_conv_w': 'new_v', 'new_v_even_pool_w': 'new_v', 'new_v_even_pool_scale': 'new_v', 'new_v_even_w_out': 'new_v', 'new_v_odd_w_in': 'new_v', 'new_v_odd_ln_g': 'new_v', 'new_v_odd_ln_b': 'new_v', 'new_v_odd_w_s': 'new_v', 'new_v_odd_b_s': 'new_v', 'new_v_odd_w_out': 'new_v'}


def _forward(args):
    return _fwd_reference(*[args[k] for k in FWD_PARAMS])


def _output_shape():
    def fwd():
        inp = _fwd_setup_inputs(0)
        return _fwd_reference(*[inp[k] for k in FWD_PARAMS])
    out = _jax.eval_shape(fwd)
    return out.shape, out.dtype

N_MICROBATCH = 1
ADAM_LR = 0.001
ADAM_B1 = 0.9
ADAM_B2 = 0.999
ADAM_EPS = 1e-08
ADAM_WD = 0.01
ADAM_STEP = 10
PER_EXAMPLE_BATCH_AXIS = {'x': 0, 'loss_target': 0}
SHARED_INPUTS = []
_WEIGHT_DTYPES = {'pre_norm': _jnp.float32, 'post_norm': _jnp.float32, 'even_w_in': _jnp.float32, 'even_conv_w': _jnp.float32, 'even_pool_w': _jnp.float32, 'even_pool_scale': _jnp.float32, 'even_w_out': _jnp.float32, 'odd_w_in': _jnp.float32, 'odd_ln_g': _jnp.float32, 'odd_ln_b': _jnp.float32, 'odd_w_s': _jnp.float32, 'odd_b_s': _jnp.float32, 'odd_w_out': _jnp.float32}
MOMENT_SCALE = {'pre_norm': 1.061208e+00, 'post_norm': 6.412272e+01, 'even_w_in': 6.802736e-01, 'even_conv_w': 7.019645e-01, 'even_pool_w': 7.122543e-01, 'even_pool_scale': 7.789454e-01, 'even_w_out': 7.931397e-01, 'odd_w_in': 4.949623e-01, 'odd_ln_g': 2.456420e-01, 'odd_ln_b': 2.267629e-01, 'odd_w_s': 2.399945e-01, 'odd_b_s': 3.486154e-01, 'odd_w_out': 7.151990e-01}


def _to_microbatches(a, axis):
    t = _jnp.moveaxis(a, axis, 0)
    t = t.reshape((N_MICROBATCH, t.shape[0] // N_MICROBATCH) + t.shape[1:])
    return _jnp.moveaxis(t, 1, axis + 1)


def setup_inputs(seed: int = 0) -> dict:
    inp = _fwd_setup_inputs(seed)
    key = _jax.random.fold_in(_jax.random.key(seed), 7919)
    shape, _ = _output_shape()
    out = dict(inp)
    out["loss_target"] = _jax.random.normal(_jax.random.fold_in(key, 0), shape, _jnp.float32)
    for i, name in enumerate(TWIN_WEIGHTS):
        w = inp[name].astype(_jnp.float32)
        if MOMENT_SCALE is None:
            s = _jnp.sqrt(_jnp.mean(_jnp.square(w)) + 1e-30)
        else:
            s = MOMENT_SCALE[name]
        km, kv = _jax.random.split(_jax.random.fold_in(key, i + 1))
        out[name] = w
        out["m_" + name] = s * _jax.random.normal(km, w.shape, _jnp.float32)
        out["v_" + name] = (s * s) * _jax.random.uniform(kv, w.shape, _jnp.float32, 0.5, 1.5)
    if N_MICROBATCH > 1:
        for name, axis in PER_EXAMPLE_BATCH_AXIS.items():
            out[name] = _to_microbatches(out[name], axis)
    return {'x': out['x'], 'pre_norm': out['pre_norm'], 'post_norm': out['post_norm'], 'even_w_in': out['even_w_in'], 'even_conv_w': out['even_conv_w'], 'even_pool_w': out['even_pool_w'], 'even_pool_scale': out['even_pool_scale'], 'even_w_out': out['even_w_out'], 'odd_w_in': out['odd_w_in'], 'odd_ln_g': out['odd_ln_g'], 'odd_ln_b': out['odd_ln_b'], 'odd_w_s': out['odd_w_s'], 'odd_b_s': out['odd_b_s'], 'odd_w_out': out['odd_w_out'], 'loss_target': out['loss_target'], 'm_pre_norm': out['m_pre_norm'], 'm_post_norm': out['m_post_norm'], 'm_even_w_in': out['m_even_w_in'], 'm_even_conv_w': out['m_even_conv_w'], 'm_even_pool_w': out['m_even_pool_w'], 'm_even_pool_scale': out['m_even_pool_scale'], 'm_even_w_out': out['m_even_w_out'], 'm_odd_w_in': out['m_odd_w_in'], 'm_odd_ln_g': out['m_odd_ln_g'], 'm_odd_ln_b': out['m_odd_ln_b'], 'm_odd_w_s': out['m_odd_w_s'], 'm_odd_b_s': out['m_odd_b_s'], 'm_odd_w_out': out['m_odd_w_out'], 'v_pre_norm': out['v_pre_norm'], 'v_post_norm': out['v_post_norm'], 'v_even_w_in': out['v_even_w_in'], 'v_even_conv_w': out['v_even_conv_w'], 'v_even_pool_w': out['v_even_pool_w'], 'v_even_pool_scale': out['v_even_pool_scale'], 'v_even_w_out': out['v_even_w_out'], 'v_odd_w_in': out['v_odd_w_in'], 'v_odd_ln_g': out['v_odd_ln_g'], 'v_odd_ln_b': out['v_odd_ln_b'], 'v_odd_w_s': out['v_odd_w_s'], 'v_odd_b_s': out['v_odd_b_s'], 'v_odd_w_out': out['v_odd_w_out']}


def _loss(weights, diff, rest, loss_target):
    with _jax.named_scope("forward"):
        args = {**rest, TWIN_DIFF_INPUT: diff, **{k: w.astype(_WEIGHT_DTYPES[k]) for k, w in weights.items()}}
        y = _forward(args)
    with _jax.named_scope("loss_head"):
        err = _jnp.square(y.astype(_jnp.float32) - loss_target)
        return 0.5 * _jnp.sum(_jnp.mean(err, axis=-1)) if err.ndim else 0.5 * err


def _adamw(w, g, m, v):
    m = ADAM_B1 * m + (1.0 - ADAM_B1) * g
    v = ADAM_B2 * v + (1.0 - ADAM_B2) * _jnp.square(g)
    m_hat = m / (1.0 - ADAM_B1 ** ADAM_STEP)
    v_hat = v / (1.0 - ADAM_B2 ** ADAM_STEP)
    delta = -ADAM_LR * (m_hat / (_jnp.sqrt(v_hat) + ADAM_EPS) + ADAM_WD * w)
    return delta, m, v


def reference(x, pre_norm, post_norm, even_w_in, even_conv_w, even_pool_w, even_pool_scale, even_w_out, odd_w_in, odd_ln_g, odd_ln_b, odd_w_s, odd_b_s, odd_w_out, loss_target, m_pre_norm, m_post_norm, m_even_w_in, m_even_conv_w, m_even_pool_w, m_even_pool_scale, m_even_w_out, m_odd_w_in, m_odd_ln_g, m_odd_ln_b, m_odd_w_s, m_odd_b_s, m_odd_w_out, v_pre_norm, v_post_norm, v_even_w_in, v_even_conv_w, v_even_pool_w, v_even_pool_scale, v_even_w_out, v_odd_w_in, v_odd_ln_g, v_odd_ln_b, v_odd_w_s, v_odd_b_s, v_odd_w_out):
    given = dict(x=x, pre_norm=pre_norm, post_norm=post_norm, even_w_in=even_w_in, even_conv_w=even_conv_w, even_pool_w=even_pool_w, even_pool_scale=even_pool_scale, even_w_out=even_w_out, odd_w_in=odd_w_in, odd_ln_g=odd_ln_g, odd_ln_b=odd_ln_b, odd_w_s=odd_w_s, odd_b_s=odd_b_s, odd_w_out=odd_w_out, loss_target=loss_target, m_pre_norm=m_pre_norm, m_post_norm=m_post_norm, m_even_w_in=m_even_w_in, m_even_conv_w=m_even_conv_w, m_even_pool_w=m_even_pool_w, m_even_pool_scale=m_even_pool_scale, m_even_w_out=m_even_w_out, m_odd_w_in=m_odd_w_in, m_odd_ln_g=m_odd_ln_g, m_odd_ln_b=m_odd_ln_b, m_odd_w_s=m_odd_w_s, m_odd_b_s=m_odd_b_s, m_odd_w_out=m_odd_w_out, v_pre_norm=v_pre_norm, v_post_norm=v_post_norm, v_even_w_in=v_even_w_in, v_even_conv_w=v_even_conv_w, v_even_pool_w=v_even_pool_w, v_even_pool_scale=v_even_pool_scale, v_even_w_out=v_even_w_out, v_odd_w_in=v_odd_w_in, v_odd_ln_g=v_odd_ln_g, v_odd_ln_b=v_odd_ln_b, v_odd_w_s=v_odd_w_s, v_odd_b_s=v_odd_b_s, v_odd_w_out=v_odd_w_out)
    weights = {n: given[n] for n in TWIN_WEIGHTS}
    shared = {n: given[n] for n in SHARED_INPUTS}
    per_example = {n: given[n] for n in ['x']}
    grad_fn = _jax.value_and_grad(_loss, argnums=(0, 1))

    def one_microbatch(ex, loss_target):
        ex = dict(ex)
        diff = ex.pop(TWIN_DIFF_INPUT)
        return grad_fn(weights, diff, {**shared, **ex}, loss_target)

    if N_MICROBATCH == 1:
        loss, (grad_w, grad_x) = one_microbatch(per_example, given["loss_target"])
    else:
        def body(carry, xs):
            loss_sum, grad_sum = carry
            l_k, (gw_k, gx_k) = one_microbatch(xs[0], xs[1])
            with _jax.named_scope("update"):
                return (loss_sum + l_k, _jax.tree.map(_jnp.add, grad_sum, gw_k)), gx_k

        init = (_jnp.zeros((), _jnp.float32), _jax.tree.map(_jnp.zeros_like, weights))
        (loss, grad_w), grad_x = _jax.lax.scan(body, init, (per_example, given["loss_target"]))
    with _jax.named_scope("update"):
        delta_w, new_m, new_v = {}, {}, {}
        for n in TWIN_WEIGHTS:
            delta_w[n], new_m[n], new_v[n] = _adamw(weights[n], grad_w[n], given["m_" + n], given["v_" + n])
    return (loss, grad_x, *[grad_w[n] for n in TWIN_WEIGHTS], *[delta_w[n] for n in TWIN_WEIGHTS],
            *[new_m[n] for n in TWIN_WEIGHTS], *[new_v[n] for n in TWIN_WEIGHTS])
```

```python
import functools

import jax
import jax.numpy as jnp
from jax import lax
from jax.experimental import pallas as pl
from jax.experimental.pallas import tpu as pltpu

F32 = jnp.float32
BF16 = jnp.bfloat16
MESH = pl.DeviceIdType.MESH

D = 1024
W3 = 3 * D
QW = W3 // 4
AW = 512
GD = 128
CHUNK = 128
HEADS = 8
HALO = 16
POOL_WINDOWS = (2, 4, 8, 16)
EPS = 1e-6
TM = 256
TK = 1024
VMEM_LIMIT = 56 * 1024 * 1024

ADAM_LR, ADAM_B1, ADAM_B2, ADAM_EPS, ADAM_WD, ADAM_STEP = 0.001, 0.9, 0.999, 1e-08, 0.01, 10

S0_PRE, S0_POST, S0_CONV, S0_PS, S0_PW, S0_ROWS = 0, 8, 16, 48, 56, 568
S1_PRE, S1_POST, S1_LN, S1_BS, S1_WS, S1_LOSS, S1_ROWS = 0, 8, 16, 48, 56, 1080, 1088


def _nn(a, b):
    return jnp.dot(a, b, preferred_element_type=F32)


def _nt(a, b):
    return lax.dot_general(a, b, (((1,), (1,)), ((), ())), preferred_element_type=F32)


def _tn(a, b):
    return lax.dot_general(a, b, (((0,), (0,)), ((), ())), preferred_element_type=F32)


def _sigmoid(z):
    return 1.0 / (1.0 + jnp.exp(-z))


def _rms_fwd(x, g):
    r = lax.rsqrt(jnp.mean(x * x, axis=-1, keepdims=True) + EPS)
    return x * r, r


def _rms_bwd(dy, xh, r, g):
    dn = dy * g
    dx = r * (dn - xh * jnp.mean(xh * dn, axis=-1, keepdims=True))
    return dx, jnp.sum(dy * xh, axis=0, keepdims=True)


def _full(shape):
    nd = len(shape)
    return pl.BlockSpec(shape, lambda i, _n=nd: (0,) * _n)


def _rows(tm, width, index=None):
    return pl.BlockSpec((tm, width), (lambda i: (i, 0)) if index is None else index)


def _params():
    return pltpu.CompilerParams(dimension_semantics=("arbitrary",), vmem_limit_bytes=VMEM_LIMIT)


def _position():
    x, y, c = lax.axis_index("x"), lax.axis_index("y"), lax.axis_index("c")
    return x, y, c


def _even_mix(proj_ref, hc_ext, xp_ext, cw_ref, pw_ref, ps_ref, first_row):
    tm = proj_ref.shape[0]
    xa = proj_ref[:, 0:AW]
    gb = proj_ref[:, AW:2 * AW]
    gc = proj_ref[:, 2 * AW:3 * AW]
    za = proj_ref[:, 3 * AW:4 * AW]
    xp = proj_ref[:, 4 * AW:5 * AW]
    zp = proj_ref[:, 5 * AW:6 * AW]
    hc = gc * xa
    hc_ext[HALO:, :] = hc
    e = hc_ext[...]
    conv = cw_ref[2:3, :] * hc + cw_ref[1:2, :] * pltpu.roll(e, 1, 0)[HALO:] + cw_ref[0:1, :] * pltpu.roll(e, 2, 0)[HALO:]
    sa = _sigmoid(za)
    xp_ext[HALO:, :] = xp
    pos = first_row + lax.broadcasted_iota(jnp.int32, (tm, 1), 0)
    pooled, mixed, counts = [], [], []
    for g, w in enumerate(POOL_WINDOWS):
        s = xp_ext[:, g * GD:(g + 1) * GD]
        for k in range(g + 1):
            s = s + pltpu.roll(s, 2 ** k, 0)
        count = jnp.minimum(pos + 1, w).astype(F32)
        pg = s[HALO:] / count - xp[:, g * GD:(g + 1) * GD]
        pooled.append(pg.astype(BF16))
        mixed.append(_nn(pooled[-1], pw_ref[g]))
        counts.append(count)
    mixed = jnp.concatenate(mixed, axis=-1)
    sb = _sigmoid(zp)
    return dict(xa=xa, gb=gb, gc=gc, za=za, zp=zp, hc=hc, conv=conv, sa=sa, sb=sb, pooled=pooled, mixed=mixed, counts=counts)


def _even_fwd(x, pre, post, win, cw, pwb, ps, wout):
    S = x.shape[0]
    tm = TM
    nt = S // tm

    def body(x_ref, pre_ref, post_ref, win_ref, cw_ref, pw_ref, ps_ref, wout_ref,
             x1_ref, proj_ref, m_ref, hb_ref, mixb_ref, hc_ext, xp_ext):
        i = pl.program_id(0)

        @pl.when(i == 0)
        def _():
            hc_ext[0:HALO, :] = jnp.zeros((HALO, AW), F32)
            xp_ext[0:HALO, :] = jnp.zeros((HALO, AW), F32)

        xv = x_ref[...]
        xh, _ = _rms_fwd(xv, None)
        hb = (xh * pre_ref[...]).astype(BF16)
        hb_ref[...] = hb
        for q in range(4):
            proj_ref[:, q * QW:(q + 1) * QW] = _nn(hb, win_ref[q])
        t = _even_mix(proj_ref, hc_ext, xp_ext, cw_ref, pw_ref, ps_ref, i * tm)
        mixb_ref[:, 0:AW] = (t["gb"] * t["conv"] * (t["za"] * t["sa"])).astype(BF16)
        mixb_ref[:, AW:2 * AW] = (t["mixed"] * ps_ref[...] * (t["zp"] * t["sb"])).astype(BF16)
        m = _nn(mixb_ref[...], wout_ref[...])
        m_ref[...] = m
        mh, _ = _rms_fwd(m, None)
        x1_ref[...] = xv + mh * post_ref[...]
        hc_ext[0:HALO, :] = hc_ext[tm:tm + HALO, :]
        xp_ext[0:HALO, :] = xp_ext[tm:tm + HALO, :]

    return pl.pallas_call(
        body, name="even_fwd", grid=(nt,),
        in_specs=[_rows(tm, D), _full((1, D)), _full((1, D)), _full((4, D, QW)), _full((3, AW)), _full((4, GD, GD)),
                  _full((1, AW)), _full((D, D))],
        out_specs=[_rows(tm, D), _rows(tm, W3), _rows(tm, D), _rows(tm, D), _rows(tm, D)],
        out_shape=[jax.ShapeDtypeStruct((S, D), F32), jax.ShapeDtypeStruct((S, W3), F32), jax.ShapeDtypeStruct((S, D), F32),
                   jax.ShapeDtypeStruct((S, D), BF16), jax.ShapeDtypeStruct((S, D), BF16)],
        scratch_shapes=[pltpu.VMEM((tm + HALO, AW), F32), pltpu.VMEM((tm + HALO, AW), F32)],
        compiler_params=_params(),
    )(x, pre, post, win, cw, pwb, ps, wout)


def _odd_mix(proj_ref, lng_ref, lnb_ref, ws_ref, bias_ref, sv_ref):
    tm = proj_ref.shape[0]
    u = proj_ref[:, 0:D]
    v = proj_ref[:, D:2 * D]
    z = proj_ref[:, 2 * D:3 * D]
    mu = jnp.mean(v, axis=-1, keepdims=True)
    vc = v - mu
    rs = lax.rsqrt(jnp.mean(vc * vc, axis=-1, keepdims=True) + EPS)
    vh = vc * rs
    vnb = (vh * lng_ref[...] + lnb_ref[...]).astype(BF16)
    for n in range(tm // CHUNK):
        for h in range(HEADS):
            rows, cols = slice(n * CHUNK, (n + 1) * CHUNK), slice(h * GD, (h + 1) * GD)
            sv_ref[rows, cols] = _nn(ws_ref[h], vnb[rows, cols]) + bias_ref[h]
    return dict(u=u, z=z, vh=vh, rs=rs, vnb=vnb, sz=_sigmoid(z))


def _odd_fwd(x1, tgt, pre, post, win, lng, lnb, wsb, bias, wout):
    S = x1.shape[0]
    tm = TM
    nt = S // tm

    def body(x_ref, tgt_ref, pre_ref, post_ref, win_ref, lng_ref, lnb_ref, ws_ref, bias_ref, wout_ref,
             proj_ref, m_ref, hb_ref, yb_ref, dx2_ref, loss_ref, sv_ref):
        i = pl.program_id(0)

        @pl.when(i == 0)
        def _():
            loss_ref[...] = jnp.zeros((8, GD), F32)

        xv = x_ref[...]
        xh, _ = _rms_fwd(xv, None)
        hb = (xh * pre_ref[...]).astype(BF16)
        hb_ref[...] = hb
        for q in range(4):
            proj_ref[:, q * QW:(q + 1) * QW] = _nn(hb, win_ref[q])
        t = _odd_mix(proj_ref, lng_ref, lnb_ref, ws_ref, bias_ref, sv_ref)
        yb = (t["u"] * sv_ref[...] * (t["z"] * t["sz"])).astype(BF16)
        yb_ref[...] = yb
        m = _nn(yb, wout_ref[...])
        m_ref[...] = m
        mh, _ = _rms_fwd(m, None)
        err = xv + mh * post_ref[...] - tgt_ref[...]
        dx2_ref[...] = err * (1.0 / D)
        part = 0.5 * jnp.sum(jnp.mean(err * err, axis=-1, keepdims=True), axis=0, keepdims=True)
        loss_ref[...] += jnp.broadcast_to(part, (8, GD))

    return pl.pallas_call(
        body, name="odd_fwd", grid=(nt,),
        in_specs=[_rows(tm, D), _rows(tm, D), _full((1, D)), _full((1, D)), _full((4, D, QW)), _full((1, D)), _full((1, D)),
                  _full((HEADS, CHUNK, CHUNK)), _full((HEADS, CHUNK, GD)), _full((D, D))],
        out_specs=[_rows(tm, W3), _rows(tm, D), _rows(tm, D), _rows(tm, D), _rows(tm, D), _full((8, GD))],
        out_shape=[jax.ShapeDtypeStruct((S, W3), F32), jax.ShapeDtypeStruct((S, D), F32), jax.ShapeDtypeStruct((S, D), BF16),
                   jax.ShapeDtypeStruct((S, D), BF16), jax.ShapeDtypeStruct((S, D), F32), jax.ShapeDtypeStruct((8, GD), F32)],
        scratch_shapes=[pltpu.VMEM((tm, D), F32)],
        compiler_params=_params(),
    )(x1, tgt, pre, post, win, lng, lnb, wsb, bias, wout)


def _store_rows(ref, row0, value):
    r, width = value.shape
    for a in range(r):
        for k in range(width // GD):
            ref[row0 + a * (width // GD) + k:row0 + a * (width // GD) + k + 1, :] = value[a:a + 1, k * GD:(k + 1) * GD]


def _odd_bwd(dx2, x1, proj, m, loss, pre, post, win, lng, lnb, wsb, wsbt, bias, wout):
    S = x1.shape[0]
    tm = TM
    nt = S // tm

    def body(dy_ref, x_ref, proj_ref, m_ref, loss_ref, pre_ref, post_ref, win_ref, lng_ref, lnb_ref, ws_ref, wst_ref, bias_ref,
             wout_ref, dx_ref, dproj_ref, dmb_ref, small_ref, sv_ref, dvn_ref, acc1024, dws_acc, dbs_acc):
        i = pl.program_id(0)

        @pl.when(i == 0)
        def _():
            acc1024[...] = jnp.zeros_like(acc1024)
            dws_acc[...] = jnp.zeros_like(dws_acc)
            dbs_acc[...] = jnp.zeros_like(dbs_acc)

        dy = dy_ref[...]
        mh, rm = _rms_fwd(m_ref[...], None)
        dm, dpost = _rms_bwd(dy, mh, rm, post_ref[...])
        dmb = dm.astype(BF16)
        dmb_ref[...] = dmb
        dyv = _nt(dmb, wout_ref[...])
        t = _odd_mix(proj_ref, lng_ref, lnb_ref, ws_ref, bias_ref, sv_ref)
        u, z, sz, sv = t["u"], t["z"], t["sz"], sv_ref[...]
        dproj_ref[:, 0:D] = (dyv * sv * (z * sz)).astype(BF16)
        dproj_ref[:, 2 * D:3 * D] = (dyv * u * sv * (sz * (1.0 + z * (1.0 - sz)))).astype(BF16)
        dsv = dyv * u * (z * sz)
        dsvb = dsv.astype(BF16)
        for n in range(tm // CHUNK):
            for h in range(HEADS):
                rows, cols = slice(n * CHUNK, (n + 1) * CHUNK), slice(h * GD, (h + 1) * GD)
                dvn_ref[rows, cols] = _nn(wst_ref[h], dsvb[rows, cols])
                dws_acc[h] += _nt(dsvb[rows, cols], t["vnb"][rows, cols])
                dbs_acc[h] += dsv[rows, cols]
        dvn = dvn_ref[...]
        vh = t["vh"]
        dvh = dvn * lng_ref[...]
        dv = t["rs"] * (dvh - jnp.mean(dvh, axis=-1, keepdims=True) - vh * jnp.mean(dvh * vh, axis=-1, keepdims=True))
        dproj_ref[:, D:2 * D] = dv.astype(BF16)
        dh = _nt(dproj_ref[:, 0:QW], win_ref[0])
        for q in range(1, 4):
            dh += _nt(dproj_ref[:, q * QW:(q + 1) * QW], win_ref[q])
        xh, r = _rms_fwd(x_ref[...], None)
        dxn, dpre = _rms_bwd(dh, xh, r, pre_ref[...])
        dx_ref[...] = dy + dxn
        acc1024[0:1, :] += dpre
        acc1024[1:2, :] += dpost
        acc1024[2:3, :] += jnp.sum(dvn * vh, axis=0, keepdims=True)
        acc1024[3:4, :] += jnp.sum(dvn, axis=0, keepdims=True)

        @pl.when(i == nt - 1)
        def _():
            small_ref[...] = jnp.zeros_like(small_ref)
            _store_rows(small_ref, S1_PRE, acc1024[0:1, :])
            _store_rows(small_ref, S1_POST, acc1024[1:2, :])
            for q in range(4):
                _store_rows(small_ref, S1_LN + 8 * q, acc1024[2:3, 2 * q * GD:(2 * q + 2) * GD])
                _store_rows(small_ref, S1_LN + 8 * q + 2, acc1024[3:4, 2 * q * GD:(2 * q + 2) * GD])
            lower = lax.broadcasted_iota(jnp.int32, (CHUNK, CHUNK), 0) >= lax.broadcasted_iota(jnp.int32, (CHUNK, CHUNK), 1)
            for h in range(HEADS):
                small_ref[S1_WS + h * CHUNK:S1_WS + (h + 1) * CHUNK, :] = jnp.where(lower, dws_acc[h], 0.0)
                small_ref[S1_BS + h:S1_BS + h + 1, :] = jnp.sum(dbs_acc[h].T, axis=0, keepdims=True)
            small_ref[S1_LOSS:S1_LOSS + 8, :] = loss_ref[...]

    return pl.pallas_call(
        body, name="odd_bwd", grid=(nt,),
        in_specs=[_rows(tm, D), _rows(tm, D), _rows(tm, W3), _rows(tm, D), _full((8, GD)), _full((1, D)), _full((1, D)),
                  _full((4, D, QW)), _full((1, D)), _full((1, D)), _full((HEADS, CHUNK, CHUNK)), _full((HEADS, CHUNK, CHUNK)),
                  _full((HEADS, CHUNK, GD)), _full((D, D))],
        out_specs=[_rows(tm, D), _rows(tm, W3), _rows(tm, D), _full((S1_ROWS, GD))],
        out_shape=[jax.ShapeDtypeStruct((S, D), F32), jax.ShapeDtypeStruct((S, W3), BF16), jax.ShapeDtypeStruct((S, D), BF16),
                   jax.ShapeDtypeStruct((S1_ROWS, GD), F32)],
        scratch_shapes=[pltpu.VMEM((tm, D), F32), pltpu.VMEM((tm, D), F32), pltpu.VMEM((8, D), F32),
                        pltpu.VMEM((HEADS, CHUNK, CHUNK), F32), pltpu.VMEM((HEADS, CHUNK, GD), F32)],
        compiler_params=_params(),
    )(dx2, x1, proj, m, loss, pre, post, win, lng, lnb, wsb, wsbt, bias, wout)


def _even_bwd(dx1, x, proj, m, pre, post, win, cw, pwb, ps, wout):
    S = x.shape[0]
    tm = TM
    nt = S // tm
    L = tm + HALO

    def rev(i):
        return (nt - 1 - i, 0)

    def halo_index(i):
        return (jnp.maximum((nt - 1 - i) * (tm // HALO) - 1, 0), 0)

    def body(dy_ref, x_ref, proj_ref, halo_ref, m_ref, pre_ref, post_ref, win_ref, cw_ref, pw_ref, ps_ref, wout_ref,
             dx_ref, dproj_ref, dmb_ref, small_ref, hc_ext, xp_ext, dconv_ext, q_ext, acc1024, acc512, dpw_acc):
        i = pl.program_id(0)
        tile = nt - 1 - i

        @pl.when(i == 0)
        def _():
            dconv_ext[tm:L, :] = jnp.zeros((HALO, AW), F32)
            q_ext[tm:L, :] = jnp.zeros((HALO, AW), F32)
            acc1024[...] = jnp.zeros_like(acc1024)
            acc512[...] = jnp.zeros_like(acc512)
            dpw_acc[...] = jnp.zeros_like(dpw_acc)

        keep = (tile > 0).astype(F32)
        hc_ext[0:HALO, :] = halo_ref[:, 2 * AW:3 * AW] * halo_ref[:, 0:AW] * keep
        xp_ext[0:HALO, :] = halo_ref[:, 4 * AW:5 * AW] * keep

        dy = dy_ref[...]
        mh, rm = _rms_fwd(m_ref[...], None)
        dm, dpost = _rms_bwd(dy, mh, rm, post_ref[...])
        dmb = dm.astype(BF16)
        dmb_ref[...] = dmb
        dmix = _nt(dmb, wout_ref[...])
        dya, dyb = dmix[:, 0:AW], dmix[:, AW:2 * AW]
        t = _even_mix(proj_ref, hc_ext, xp_ext, cw_ref, pw_ref, ps_ref, tile * tm)
        gb, conv, za, sa, hc = t["gb"], t["conv"], t["za"], t["sa"], t["hc"]
        silu_a = za * sa
        dproj_ref[:, AW:2 * AW] = (dya * conv * silu_a).astype(BF16)
        dproj_ref[:, 3 * AW:4 * AW] = (dya * gb * conv * (sa * (1.0 + za * (1.0 - sa)))).astype(BF16)
        dconv = dya * gb * silu_a
        dconv_ext[0:tm, :] = dconv
        e = dconv_ext[...]
        dc1 = pltpu.roll(e, L - 1, 0)[0:tm]
        dc2 = pltpu.roll(e, L - 2, 0)[0:tm]
        dhc = cw_ref[2:3, :] * dconv + cw_ref[1:2, :] * dc1 + cw_ref[0:1, :] * dc2
        dproj_ref[:, 0:AW] = (dhc * t["gc"]).astype(BF16)
        dproj_ref[:, 2 * AW:3 * AW] = (dhc * t["xa"]).astype(BF16)
        acc512[0:1, :] += jnp.sum(dc2 * hc, axis=0, keepdims=True)
        acc512[1:2, :] += jnp.sum(dc1 * hc, axis=0, keepdims=True)
        acc512[2:3, :] += jnp.sum(dconv * hc, axis=0, keepdims=True)

        zp, sb, mixed = t["zp"], t["sb"], t["mixed"]
        silu_b = zp * sb
        acc512[3:4, :] += jnp.sum(dyb * mixed * silu_b, axis=0, keepdims=True)
        dmixedb = (dyb * ps_ref[...] * silu_b).astype(BF16)
        dproj_ref[:, 5 * AW:6 * AW] = (dyb * mixed * ps_ref[...] * (sb * (1.0 + zp * (1.0 - sb)))).astype(BF16)
        for g in range(4):
            cols = slice(g * GD, (g + 1) * GD)
            dpw_acc[g] += _tn(t["pooled"][g], dmixedb[:, cols])
            dpooled = _nt(dmixedb[:, cols], pw_ref[g])
            q_ext[0:tm, cols] = dpooled / t["counts"][g]
            s = q_ext[:, cols]
            for k in range(g + 1):
                s = s + pltpu.roll(s, L - 2 ** k, 0)
            dproj_ref[:, 4 * AW + g * GD:4 * AW + (g + 1) * GD] = (s[0:tm] - dpooled).astype(BF16)
        dconv_ext[tm:L, :] = dconv_ext[0:HALO, :]
        q_ext[tm:L, :] = q_ext[0:HALO, :]

        dh = _nt(dproj_ref[:, 0:QW], win_ref[0])
        for q in range(1, 4):
            dh += _nt(dproj_ref[:, q * QW:(q + 1) * QW], win_ref[q])
        xh, r = _rms_fwd(x_ref[...], None)
        dxn, dpre = _rms_bwd(dh, xh, r, pre_ref[...])
        dx_ref[...] = dy + dxn
        acc1024[0:1, :] += dpre
        acc1024[1:2, :] += dpost

        @pl.when(i == nt - 1)
        def _():
            small_ref[...] = jnp.zeros_like(small_ref)
            _store_rows(small_ref, S0_PRE, acc1024[0:1, :])
            _store_rows(small_ref, S0_POST, acc1024[1:2, :])
            for q in range(4):
                for k in range(3):
                    small_ref[S0_CONV + 8 * q + k:S0_CONV + 8 * q + k + 1, :] = acc512[k:k + 1, q * GD:(q + 1) * GD]
            _store_rows(small_ref, S0_PS, acc512[3:4, :])
            for g in range(4):
                small_ref[S0_PW + g * GD:S0_PW + (g + 1) * GD, :] = dpw_acc[g]

    return pl.pallas_call(
        body, name="even_bwd", grid=(nt,),
        in_specs=[_rows(tm, D, rev), _rows(tm, D, rev), _rows(tm, W3, rev), pl.BlockSpec((HALO, W3), halo_index), _rows(tm, D, rev),
                  _full((1, D)), _full((1, D)), _full((4, D, QW)), _full((3, AW)), _full((4, GD, GD)), _full((1, AW)), _full((D, D))],
        out_specs=[_rows(tm, D, rev), _rows(tm, W3, rev), _rows(tm, D, rev), _full((S0_ROWS, GD))],
        out_shape=[jax.ShapeDtypeStruct((S, D), F32), jax.ShapeDtypeStruct((S, W3), BF16), jax.ShapeDtypeStruct((S, D), BF16),
                   jax.ShapeDtypeStruct((S0_ROWS, GD), F32)],
        scratch_shapes=[pltpu.VMEM((L, AW), F32), pltpu.VMEM((L, AW), F32), pltpu.VMEM((L, AW), F32), pltpu.VMEM((L, AW), F32),
                        pltpu.VMEM((8, D), F32), pltpu.VMEM((8, AW), F32), pltpu.VMEM((4, GD, GD), F32)],
        compiler_params=_params(),
    )(dx1, x, proj, proj, m, pre, post, win, cw, pwb, ps, wout)


def _wgrad(lhs, rhs, nblk, name):
    S, M = lhs.shape
    N = rhs.shape[1] // nblk
    nk = S // TK

    def body(a_ref, b_ref, o_ref):
        @pl.when(pl.program_id(1) == 0)
        def _():
            o_ref[...] = jnp.zeros_like(o_ref)

        o_ref[...] += _tn(a_ref[...], b_ref[...])

    return pl.pallas_call(
        body, name=name, grid=(nblk, nk),
        in_specs=[pl.BlockSpec((TK, M), lambda j, k: (k, 0)), pl.BlockSpec((TK, N), lambda j, k: (k, j))],
        out_specs=pl.BlockSpec((None, M, N), lambda j, k: (j, 0, 0)),
        out_shape=jax.ShapeDtypeStruct((nblk, M, N), F32),
        compiler_params=pltpu.CompilerParams(dimension_semantics=("arbitrary", "arbitrary"), vmem_limit_bytes=VMEM_LIMIT),
    )(lhs, rhs)


def _chips(x, y):
    return [(1 - x, y), (x, 1 - y), (1 - x, 1 - y)]


def _gather_weights(parts, split):
    n = len(parts)

    def body(*refs):
        ins, outs = refs[:n], refs[n:2 * n]
        send_sems, recv_sems, local_sems = refs[2 * n:]
        x, y, c = _position()
        me = 2 * x + y
        chips = _chips(x, y)
        local = [pltpu.make_async_copy(ins[a], outs[a].at[me], local_sems.at[a]) for a in range(n)]
        for cp in local:
            cp.start()

        def half(a, ref, who):
            rows = parts[a].shape[0] // 2
            return ref.at[pl.ds(pl.multiple_of(who * rows, 8), rows), :] if split[a] else ref

        sends = []
        for a in range(n):
            for j, chip in enumerate(chips):
                sends.append(pltpu.make_async_remote_copy(
                    src_ref=half(a, ins[a], c), dst_ref=half(a, outs[a].at[me], c),
                    send_sem=send_sems.at[a, j], recv_sem=recv_sems.at[a, j], device_id=(*chip, c), device_id_type=MESH))
        for cp in sends:
            cp.start()
        for j, chip in enumerate(chips):
            src = 2 * chip[0] + chip[1]
            for a in range(n):
                pltpu.make_async_remote_copy(
                    src_ref=half(a, ins[a], c), dst_ref=half(a, outs[a].at[src], c),
                    send_sem=send_sems.at[a, j], recv_sem=recv_sems.at[a, j], device_id=(*chip, c), device_id_type=MESH).wait_recv()
                if split[a]:
                    fwd = pltpu.make_async_remote_copy(
                        src_ref=half(a, outs[a].at[src], c), dst_ref=half(a, outs[a].at[src], c),
                        send_sem=send_sems.at[a, 3 + j], recv_sem=recv_sems.at[a, 3 + j], device_id=(x, y, 1 - c), device_id_type=MESH)
                    fwd.start()
                    sends.append(fwd)
        for j, chip in enumerate(chips):
            src = 2 * chip[0] + chip[1]
            for a in range(n):
                if split[a]:
                    pltpu.make_async_remote_copy(
                        src_ref=half(a, outs[a].at[src], 1 - c), dst_ref=half(a, outs[a].at[src], 1 - c),
                        send_sem=send_sems.at[a, 3 + j], recv_sem=recv_sems.at[a, 3 + j], device_id=(x, y, 1 - c),
                        device_id_type=MESH).wait_recv()
        for cp in sends:
            cp.wait_send()
        for cp in local:
            cp.wait()

    any_spec = pl.BlockSpec(memory_space=pl.ANY)
    return pl.pallas_call(
        body, name="gather_weights",
        in_specs=[any_spec] * n, out_specs=[any_spec] * n,
        out_shape=[jax.ShapeDtypeStruct((4, *p.shape), p.dtype) for p in parts],
        scratch_shapes=[pltpu.SemaphoreType.DMA((n, 6)), pltpu.SemaphoreType.DMA((n, 6)), pltpu.SemaphoreType.DMA((n,))],
    )(*parts)


def _swap_halves(grads):
    n = len(grads)

    def body(*refs):
        ins, outs = refs[:n], refs[n:2 * n]
        send_sems, recv_sems = refs[2 * n:]
        x, y, c = _position()
        copies = []
        for a in range(n):
            rows = grads[a].shape[1] // 2
            src = ins[a].at[:, pl.ds(pl.multiple_of((1 - c) * rows, 8), rows), :]
            copies.append(pltpu.make_async_remote_copy(
                src_ref=src, dst_ref=outs[a], send_sem=send_sems.at[a], recv_sem=recv_sems.at[a],
                device_id=(x, y, 1 - c), device_id_type=MESH))
        for cp in copies:
            cp.start()
        for cp in copies:
            cp.wait()

    any_spec = pl.BlockSpec(memory_space=pl.ANY)
    return pl.pallas_call(
        body, name="swap_halves",
        in_specs=[any_spec] * n, out_specs=[any_spec] * n,
        out_shape=[jax.ShapeDtypeStruct((4, g.shape[1] // 2, g.shape[2]), g.dtype) for g in grads],
        scratch_shapes=[pltpu.SemaphoreType.DMA((n,)), pltpu.SemaphoreType.DMA((n,))],
    )(*grads)


def _pair_sum(grad, recv, name):
    nb, rows, cols = grad.shape
    hr = rows // 2
    tr = 128

    def body(g_ref, r_ref, o_ref, ob_ref):
        s = g_ref[lax.axis_index("c")] + r_ref[...]
        o_ref[...] = s
        ob_ref[...] = s.astype(BF16)

    spec = pl.BlockSpec((None, tr, cols), lambda b, i: (b, i, 0))
    return pl.pallas_call(
        body, name=name, grid=(nb, hr // tr),
        in_specs=[pl.BlockSpec((None, 2, tr, cols), lambda b, i: (b, 0, i, 0)), spec],
        out_specs=[spec, spec],
        out_shape=[jax.ShapeDtypeStruct((nb, hr, cols), F32), jax.ShapeDtypeStruct((nb, hr, cols), BF16)],
        compiler_params=pltpu.CompilerParams(dimension_semantics=("arbitrary", "arbitrary")),
    )(grad.reshape(nb, 2, hr, cols), recv)


def _scatter_to_owners(sums):
    n = len(sums)

    def body(*refs):
        ins, outs = refs[:n], refs[n:2 * n]
        send_sems, recv_sems = refs[2 * n:]
        x, y, c = _position()
        copies = []
        for a in range(n):
            for j, chip in enumerate(_chips(x, y)):
                copies.append(pltpu.make_async_remote_copy(
                    src_ref=ins[a].at[2 * chip[0] + chip[1]], dst_ref=outs[a].at[j],
                    send_sem=send_sems.at[a, j], recv_sem=recv_sems.at[a, j], device_id=(*chip, c), device_id_type=MESH))
        for cp in copies:
            cp.start()
        for cp in copies:
            cp.wait()

    any_spec = pl.BlockSpec(memory_space=pl.ANY)
    return pl.pallas_call(
        body, name="scatter_to_owners",
        in_specs=[any_spec] * n, out_specs=[any_spec] * n,
        out_shape=[jax.ShapeDtypeStruct((3, *s.shape[1:]), s.dtype) for s in sums],
        scratch_shapes=[pltpu.SemaphoreType.DMA((n, 3)), pltpu.SemaphoreType.DMA((n, 3))],
    )(*sums)


def _owner_sum(own, recv, name):
    nb, rows, cols = own.shape
    tr = 128

    def body(o_ref, r_ref, out_ref):
        s = o_ref[2 * lax.axis_index("x") + lax.axis_index("y")]
        for j in range(3):
            s = s + r_ref[j].astype(F32)
        out_ref[...] = s

    return pl.pallas_call(
        body, name=name, grid=(rows // tr,),
        in_specs=[pl.BlockSpec((nb, tr, cols), lambda i: (0, i, 0)), pl.BlockSpec((3, tr, cols), lambda i: (0, i, 0))],
        out_specs=pl.BlockSpec((tr, cols), lambda i: (i, 0)),
        out_shape=jax.ShapeDtypeStruct((rows, cols), F32),
        compiler_params=pltpu.CompilerParams(dimension_semantics=("arbitrary",)),
    )(own, recv)


def _share_with_sibling(halves):
    n = len(halves)

    def body(*refs):
        ins, outs = refs[:n], refs[n:2 * n]
        send_sems, recv_sems, local_sems = refs[2 * n:]
        x, y, c = _position()
        local = [pltpu.make_async_copy(ins[a], outs[a].at[c], local_sems.at[a]) for a in range(n)]
        remote = [pltpu.make_async_remote_copy(
            src_ref=ins[a], dst_ref=outs[a].at[c], send_sem=send_sems.at[a], recv_sem=recv_sems.at[a],
            device_id=(x, y, 1 - c), device_id_type=MESH) for a in range(n)]
        for cp in local + remote:
            cp.start()
        for cp in remote:
            cp.wait()
        for cp in local:
            cp.wait()

    any_spec = pl.BlockSpec(memory_space=pl.ANY)
    return pl.pallas_call(
        body, name="share_with_sibling",
        in_specs=[any_spec] * n, out_specs=[any_spec] * n,
        out_shape=[jax.ShapeDtypeStruct((2, *h.shape), h.dtype) for h in halves],
        scratch_shapes=[pltpu.SemaphoreType.DMA((n,)), pltpu.SemaphoreType.DMA((n,)), pltpu.SemaphoreType.DMA((n,))],
    )(*halves)


def _gather_small(parts):
    n = len(parts)

    def body(*refs):
        ins, outs = refs[:n], refs[n:2 * n]
        send_sems, recv_sems, local_sems = refs[2 * n:]
        x, y, c = _position()
        chips = _chips(x, y)

        def slot(chip, core):
            return 4 * chip[0] + 2 * chip[1] + core

        def copy(a, k, src, block, to):
            return pltpu.make_async_remote_copy(src_ref=src, dst_ref=outs[a].at[block], send_sem=send_sems.at[a, k],
                                                recv_sem=recv_sems.at[a, k], device_id=to, device_id_type=MESH)

        local = [pltpu.make_async_copy(ins[a], outs[a].at[slot((x, y), c)], local_sems.at[a]) for a in range(n)]
        for cp in local:
            cp.start()
        sends = []
        for a in range(n):
            sends.append(copy(a, 0, ins[a], slot((x, y), c), (x, y, 1 - c)))
            for j, chip in enumerate(chips):
                sends.append(copy(a, 1 + j, ins[a], slot((x, y), c), (*chip, c)))
        for cp in sends:
            cp.start()
        for j, chip in enumerate(chips):
            for a in range(n):
                copy(a, 1 + j, ins[a], slot(chip, c), (*chip, c)).wait_recv()
                fwd = copy(a, 4 + j, outs[a].at[slot(chip, c)], slot(chip, c), (x, y, 1 - c))
                fwd.start()
                sends.append(fwd)
        for a in range(n):
            copy(a, 0, ins[a], slot((x, y), 1 - c), (x, y, 1 - c)).wait_recv()
            for j, chip in enumerate(chips):
                copy(a, 4 + j, ins[a], slot(chip, 1 - c), (x, y, 1 - c)).wait_recv()
        for cp in sends:
            cp.wait_send()
        for cp in local:
            cp.wait()

    any_spec = pl.BlockSpec(memory_space=pl.ANY)
    return pl.pallas_call(
        body, name="gather_small",
        in_specs=[any_spec] * n, out_specs=[any_spec] * n,
        out_shape=[jax.ShapeDtypeStruct((8, *p.shape), p.dtype) for p in parts],
        scratch_shapes=[pltpu.SemaphoreType.DMA((n, 7)), pltpu.SemaphoreType.DMA((n, 7)), pltpu.SemaphoreType.DMA((n,))],
    )(*parts)


def _adamw(w, g, m, v):
    m = ADAM_B1 * m + (1.0 - ADAM_B1) * g
    v = ADAM_B2 * v + (1.0 - ADAM_B2) * (g * g)
    m_hat = m / (1.0 - ADAM_B1 ** ADAM_STEP)
    v_hat = v / (1.0 - ADAM_B2 ** ADAM_STEP)
    delta = -ADAM_LR * (m_hat / (jnp.sqrt(v_hat) + ADAM_EPS) + ADAM_WD * w)
    return delta, m, v


def _adamw_big(w, g, m, v, name):
    rows, cols = w.shape
    tr = 128

    def body(w_ref, g_ref, m_ref, v_ref, go_ref, d_ref, mo_ref, vo_ref):
        gv = g_ref[...]
        go_ref[...] = gv
        d_ref[...], mo_ref[...], vo_ref[...] = _adamw(w_ref[...], gv, m_ref[...], v_ref[...])

    spec = pl.BlockSpec((tr, cols), lambda i: (i, 0))
    return pl.pallas_call(
        body, name=name, grid=(rows // tr,),
        in_specs=[spec] * 4, out_specs=[spec] * 4,
        out_shape=[jax.ShapeDtypeStruct((rows, cols), F32)] * 4,
        compiler_params=pltpu.CompilerParams(dimension_semantics=("arbitrary",)),
    )(w, g, m, v)


def _adamw_small(g0, g1, weights, moms, vels):
    names = ["pre", "post", "conv", "pw", "ps", "lng", "lnb", "ws", "bs"]
    shapes = [w.shape for w in weights]

    def body(*refs):
        me = 2 * lax.axis_index("x") + lax.axis_index("y")
        g0_ref, g1_ref = refs[0], refs[1]
        w_refs, m_refs, v_refs = refs[2:11], refs[11:20], refs[20:29]
        outs = refs[29:29 + 36]
        loss_ref = refs[65]
        t0_ref, t1_ref = refs[66], refs[67]
        t0 = g0_ref[0]
        t1 = g1_ref[0]
        for d in range(1, 8):
            t0 = t0 + g0_ref[d]
            t1 = t1 + g1_ref[d]
        t0_ref[...] = t0
        t1_ref[...] = t1
        loss_ref[...] = t1_ref[S1_LOSS:S1_LOSS + 1, 0:1]
        my_conv = pl.multiple_of(S0_CONV + 8 * me, 8)
        my_ln = pl.multiple_of(S1_LN + 8 * me, 8)

        def update(idx, piece, grad):
            go, do, mo, vo = outs[4 * idx:4 * idx + 4]
            go[piece] = grad
            do[piece], mo[piece], vo[piece] = _adamw(w_refs[idx][piece], grad, m_refs[idx][piece], v_refs[idx][piece])

        for layer in range(2):
            for k in range(D // GD):
                lanes = slice(k * GD, (k + 1) * GD)
                tref, pre0, post0 = (t0_ref, S0_PRE, S0_POST) if layer == 0 else (t1_ref, S1_PRE, S1_POST)
                update(0, (slice(layer, layer + 1), lanes), tref[pre0 + k:pre0 + k + 1, :])
                update(1, (slice(layer, layer + 1), lanes), tref[post0 + k:post0 + k + 1, :])
        conv_rows = t0_ref[pl.ds(my_conv, 8), :]
        update(2, (slice(0, 3), slice(None)), conv_rows[0:3, :])
        for g in range(4):
            update(3, (g,), t0_ref[S0_PW + g * GD:S0_PW + (g + 1) * GD, :])
            update(4, (slice(0, 1), slice(g * GD, (g + 1) * GD)), t0_ref[S0_PS + g:S0_PS + g + 1, :])
        ln_rows = t1_ref[pl.ds(my_ln, 8), :]
        for k in range(2):
            update(5, (slice(0, 1), slice(k * GD, (k + 1) * GD)), ln_rows[k:k + 1, :])
            update(6, (slice(0, 1), slice(k * GD, (k + 1) * GD)), ln_rows[2 + k:3 + k, :])
        for h in range(HEADS):
            update(7, (h,), t1_ref[S1_WS + h * CHUNK:S1_WS + (h + 1) * CHUNK, :])
        update(8, (slice(None), slice(None)), t1_ref[S1_BS:S1_BS + HEADS, :])

    vm = pl.BlockSpec(memory_space=pltpu.VMEM)
    out_shape = []
    for s in shapes:
        out_shape += [jax.ShapeDtypeStruct(s, F32)] * 4
    out_shape.append(jax.ShapeDtypeStruct((1, 1), F32))
    res = pl.pallas_call(
        body, name="adamw_small",
        in_specs=[vm] * 29, out_specs=[vm] * 37, out_shape=out_shape,
        scratch_shapes=[pltpu.VMEM((S0_ROWS, GD), F32), pltpu.VMEM((S1_ROWS, GD), F32)],
        compiler_params=pltpu.CompilerParams(vmem_limit_bytes=VMEM_LIMIT),
    )(g0, g1, *weights, *moms, *vels)
    per_weight = {nm: res[4 * i:4 * i + 4] for i, nm in enumerate(names)}
    return per_weight, res[36]


def _pad8(a):
    return jnp.pad(a, ((0, 8 - a.shape[0]), (0, 0)))


def kernel(x, pre_norm, post_norm, even_w_in, even_conv_w, even_pool_w, even_pool_scale, even_w_out, odd_w_in, odd_ln_g, odd_ln_b, odd_w_s, odd_b_s, odd_w_out, loss_target, m_pre_norm, m_post_norm, m_even_w_in, m_even_conv_w, m_even_pool_w, m_even_pool_scale, m_even_w_out, m_odd_w_in, m_odd_ln_g, m_odd_ln_b, m_odd_w_s, m_odd_b_s, m_odd_w_out, v_pre_norm, v_post_norm, v_even_w_in, v_even_conv_w, v_even_pool_w, v_even_pool_scale, v_even_w_out, v_odd_w_in, v_odd_ln_g, v_odd_ln_b, v_odd_w_s, v_odd_b_s, v_odd_w_out):
    xs = x[0]
    tgt = loss_target[0]

    small_shard = jnp.concatenate([_pad8(even_conv_w[0]), _pad8(odd_ln_g.reshape(2, GD)), _pad8(odd_ln_b.reshape(2, GD))], axis=0)
    win0, wout0, win1, wout1, shard = _gather_weights(
        [even_w_in[0].astype(BF16), even_w_out[0].astype(BF16), odd_w_in[0].astype(BF16), odd_w_out[0].astype(BF16), small_shard],
        [True, True, True, True, False])
    wout0 = wout0.reshape(D, D)
    wout1 = wout1.reshape(D, D)
    conv_w = shard[:, 0:3, :].transpose(1, 0, 2).reshape(3, AW)
    ln_g = shard[:, 8:10, :].reshape(1, D)
    ln_b = shard[:, 16:18, :].reshape(1, D)
    pool_wb = even_pool_w[0].astype(BF16)
    ws_tril = jnp.tril(odd_w_s[0]).astype(BF16)
    ws_tril_t = jnp.swapaxes(ws_tril, 1, 2)
    bias = jnp.broadcast_to(odd_b_s[0][:, :, None], (HEADS, CHUNK, GD))
    pre0, pre1 = pre_norm[0:1], pre_norm[1:2]
    post0, post1 = post_norm[0:1], post_norm[1:2]

    x1, proj0, m0, hb0, mixb0 = _even_fwd(xs, pre0, post0, win0, conv_w, pool_wb, even_pool_scale, wout0)
    proj1, m1, hb1, yb1, dx2, loss_part = _odd_fwd(x1, tgt, pre1, post1, win1, ln_g, ln_b, ws_tril, bias, wout1)
    dx1, dproj1, dmb1, small1 = _odd_bwd(dx2, x1, proj1, m1, loss_part, pre1, post1, win1, ln_g, ln_b, ws_tril, ws_tril_t, bias, wout1)
    g_win1 = _wgrad(hb1, dproj1, 4, "wgrad_odd_in")
    g_wout1 = _wgrad(yb1, dmb1, 1, "wgrad_odd_out").reshape(4, D // 4, D)
    gx, dproj0, dmb0, small0 = _even_bwd(dx1, xs, proj0, m0, pre0, post0, win0, conv_w, pool_wb, even_pool_scale, wout0)
    g_win0 = _wgrad(hb0, dproj0, 4, "wgrad_even_in")
    g_wout0 = _wgrad(mixb0, dmb0, 1, "wgrad_even_out").reshape(4, D // 4, D)

    grads = [g_win0, g_wout0, g_win1, g_wout1]
    tags = ["even_in", "even_out", "odd_in", "odd_out"]
    from_sibling = _swap_halves(grads)
    pair = [_pair_sum(g, r, "pair_sum_" + t) for g, r, t in zip(grads, from_sibling, tags)]
    from_chips = _scatter_to_owners([p[1] for p in pair])
    mine = [_owner_sum(p[0], r, "owner_sum_" + t) for p, r, t in zip(pair, from_chips, tags)]
    both = _share_with_sibling(mine)
    big_w = [even_w_in[0], even_w_out[0], odd_w_in[0], odd_w_out[0]]
    big_m = [m_even_w_in[0], m_even_w_out[0], m_odd_w_in[0], m_odd_w_out[0]]
    big_v = [v_even_w_in[0], v_even_w_out[0], v_odd_w_in[0], v_odd_w_out[0]]
    big = [_adamw_big(w, g.reshape(w.shape), m, v, "adamw_" + t) for w, g, m, v, t in zip(big_w, both, big_m, big_v, tags)]

    all0, all1 = _gather_small([small0, small1])
    small_w = [pre_norm, post_norm, even_conv_w[0], even_pool_w[0], even_pool_scale, odd_ln_g, odd_ln_b, odd_w_s[0], odd_b_s[0]]
    small_m = [m_pre_norm, m_post_norm, m_even_conv_w[0], m_even_pool_w[0], m_even_pool_scale, m_odd_ln_g, m_odd_ln_b, m_odd_w_s[0], m_odd_b_s[0]]
    small_v = [v_pre_norm, v_post_norm, v_even_conv_w[0], v_even_pool_w[0], v_even_pool_scale, v_odd_ln_g, v_odd_ln_b, v_odd_w_s[0], v_odd_b_s[0]]
    sm, loss = _adamw_small(all0, all1, small_w, small_m, small_v)

    def lead(a):
        return a[None]

    per = {
        "pre_norm": sm["pre"], "post_norm": sm["post"],
        "even_w_in": [lead(a) for a in big[0]], "even_conv_w": [lead(a) for a in sm["conv"]],
        "even_pool_w": [lead(a) for a in sm["pw"]], "even_pool_scale": sm["ps"],
        "even_w_out": [lead(a) for a in big[1]], "odd_w_in": [lead(a) for a in big[2]],
        "odd_ln_g": sm["lng"], "odd_ln_b": sm["lnb"],
        "odd_w_s": [lead(a) for a in sm["ws"]], "odd_b_s": [lead(a) for a in sm["bs"]],
        "odd_w_out": [lead(a) for a in big[3]],
    }
    order = ["pre_norm", "post_norm", "even_w_in", "even_conv_w", "even_pool_w", "even_pool_scale", "even_w_out", "odd_w_in",
             "odd_ln_g", "odd_ln_b", "odd_w_s", "odd_b_s", "odd_w_out"]
    outs = [loss.reshape(()), gx[None]]
    for kind in range(4):
        outs += [per[nm][kind] for nm in order]
    return tuple(outs)
```

```python
import functools

import jax
import jax.numpy as jnp
from jax import lax
from jax.experimental import pallas as pl
from jax.experimental.pallas import tpu as pltpu

F32 = jnp.float32
BF16 = jnp.bfloat16
MESH = pl.DeviceIdType.MESH

D = 1024
W3 = 3 * D
QW = W3 // 4
AW = 512
GD = 128
CHUNK = 128
HEADS = 8
HALO = 16
POOL_WINDOWS = (2, 4, 8, 16)
EPS = 1e-6
TM = 256
TK = 1024
VMEM_LIMIT = 56 * 1024 * 1024

ADAM_LR, ADAM_B1, ADAM_B2, ADAM_EPS, ADAM_WD, ADAM_STEP = 0.001, 0.9, 0.999, 1e-08, 0.01, 10

S0_PRE, S0_POST, S0_CONV, S0_PS, S0_PW, S0_ROWS = 0, 8, 16, 48, 56, 568
S1_PRE, S1_POST, S1_LN, S1_BS, S1_WS, S1_LOSS, S1_ROWS = 0, 8, 16, 48, 56, 1080, 1088


def _nn(a, b):
    return jnp.dot(a, b, preferred_element_type=F32)


def _nt(a, b):
    return lax.dot_general(a, b, (((1,), (1,)), ((), ())), preferred_element_type=F32)


def _tn(a, b):
    return lax.dot_general(a, b, (((0,), (0,)), ((), ())), preferred_element_type=F32)


def _sigmoid(z):
    return 1.0 / (1.0 + jnp.exp(-z))


def _rms_fwd(x, g):
    r = lax.rsqrt(jnp.mean(x * x, axis=-1, keepdims=True) + EPS)
    return x * r, r


def _rms_bwd(dy, xh, r, g):
    dn = dy * g
    dx = r * (dn - xh * jnp.mean(xh * dn, axis=-1, keepdims=True))
    return dx, jnp.sum(dy * xh, axis=0, keepdims=True)


def _full(shape):
    nd = len(shape)
    return pl.BlockSpec(shape, lambda i, _n=nd: (0,) * _n)


def _rows(tm, width, index=None):
    return pl.BlockSpec((tm, width), (lambda i: (i, 0)) if index is None else index)


def _params():
    return pltpu.CompilerParams(dimension_semantics=("arbitrary",), vmem_limit_bytes=VMEM_LIMIT)


def _position():
    x, y, c = lax.axis_index("x"), lax.axis_index("y"), lax.axis_index("c")
    return x, y, c


def _even_mix(proj_ref, hc_ext, xp_ext, cw_ref, pw_ref, ps_ref, first_row):
    tm = proj_ref.shape[0]
    xa = proj_ref[:, 0:AW]
    gb = proj_ref[:, AW:2 * AW]
    gc = proj_ref[:, 2 * AW:3 * AW]
    za = proj_ref[:, 3 * AW:4 * AW]
    xp = proj_ref[:, 4 * AW:5 * AW]
    zp = proj_ref[:, 5 * AW:6 * AW]
    hc = gc * xa
    hc_ext[HALO:, :] = hc
    e = hc_ext[...]
    conv = cw_ref[2:3, :] * hc + cw_ref[1:2, :] * pltpu.roll(e, 1, 0)[HALO:] + cw_ref[0:1, :] * pltpu.roll(e, 2, 0)[HALO:]
    sa = _sigmoid(za)
    xp_ext[HALO:, :] = xp
    pos = first_row + lax.broadcasted_iota(jnp.int32, (tm, 1), 0)
    pooled, mixed, counts = [], [], []
    for g, w in enumerate(POOL_WINDOWS):
        s = xp_ext[:, g * GD:(g + 1) * GD]
        for k in range(g + 1):
            s = s + pltpu.roll(s, 2 ** k, 0)
        count = jnp.minimum(pos + 1, w).astype(F32)
        pg = s[HALO:] / count - xp[:, g * GD:(g + 1) * GD]
        pooled.append(pg.astype(BF16))
        mixed.append(_nn(pooled[-1], pw_ref[g]))
        counts.append(count)
    mixed = jnp.concatenate(mixed, axis=-1)
    sb = _sigmoid(zp)
    return dict(xa=xa, gb=gb, gc=gc, za=za, zp=zp, hc=hc, conv=conv, sa=sa, sb=sb, pooled=pooled, mixed=mixed, counts=counts)


def _half_rows(ref, rows, who):
    return ref.at[pl.ds(pl.multiple_of(who * (rows // 2), 8), rows // 2), :]


def _store_permuted(ref, value):
    for ob in range(D // GD):
        nb = 4 * (ob % 2) + ob // 2
        ref[:, nb * GD:(nb + 1) * GD] = value[:, ob * GD:(ob + 1) * GD]


def _even_fwd(x, pre, post, win, cw, pwb, ps, wout, next_shards):
    S = x.shape[0]
    tm = TM
    nt = S // tm
    relay = (3 * nt) // 4
    n = len(next_shards)
    shard_rows = [p.shape[0] for p in next_shards]

    def body(x_ref, pre_ref, post_ref, win_ref, cw_ref, pw_ref, ps_ref, wout_ref, *rest):
        shard_refs, rest = rest[:n], rest[n:]
        x1_ref, proj_ref, m_ref, hb_ref, mixp_ref = rest[:5]
        full_refs, rest = rest[5:5 + n], rest[5 + n:]
        hc_ext, xp_ext, mix_sc = rest[:3]
        stage, rest = rest[3:3 + n], rest[3 + n:]
        send_sems, recv_sems, local_sems = rest
        i = pl.program_id(0)
        px, py, pc = _position()
        me = 2 * px + py
        chips = _chips(px, py)

        def ici(a, j):
            return pltpu.make_async_remote_copy(
                src_ref=_half_rows(shard_refs[a], shard_rows[a], pc), dst_ref=_half_rows(full_refs[a].at[me], shard_rows[a], pc),
                send_sem=send_sems.at[a, j], recv_sem=recv_sems.at[a, j], device_id=(*chips[j], pc), device_id_type=MESH)

        def ici_arrival(a, j):
            src = 2 * chips[j][0] + chips[j][1]
            return pltpu.make_async_remote_copy(
                src_ref=_half_rows(shard_refs[a], shard_rows[a], pc), dst_ref=_half_rows(full_refs[a].at[src], shard_rows[a], pc),
                send_sem=send_sems.at[a, j], recv_sem=recv_sems.at[a, j], device_id=(*chips[j], pc), device_id_type=MESH)

        def relay_copy(a, j, who):
            src = 2 * chips[j][0] + chips[j][1]
            region = _half_rows(full_refs[a].at[src], shard_rows[a], who)
            return pltpu.make_async_remote_copy(
                src_ref=region, dst_ref=region, send_sem=send_sems.at[a, 3 + j], recv_sem=recv_sems.at[a, 3 + j],
                device_id=(px, py, 1 - pc), device_id_type=MESH)

        def own_copy(a):
            return pltpu.make_async_copy(stage[a], full_refs[a].at[me], local_sems.at[a])

        @pl.when(i == 0)
        def _():
            hc_ext[0:HALO, :] = jnp.zeros((HALO, AW), F32)
            xp_ext[0:HALO, :] = jnp.zeros((HALO, AW), F32)
            for a in range(n):
                for j in range(3):
                    ici(a, j).start()
            for a in range(n):
                load = pltpu.make_async_copy(shard_refs[a], stage[a], local_sems.at[a])
                load.start()
                load.wait()
                own_copy(a).start()

        @pl.when(i == relay)
        def _():
            for j in range(3):
                for a in range(n):
                    ici_arrival(a, j).wait_recv()
                    relay_copy(a, j, pc).start()

        xv = x_ref[...]
        xh, _ = _rms_fwd(xv, None)
        hb = (xh * pre_ref[...]).astype(BF16)
        hb_ref[...] = hb
        for q in range(4):
            proj_ref[:, q * QW:(q + 1) * QW] = _nn(hb, win_ref[q])
        t = _even_mix(proj_ref, hc_ext, xp_ext, cw_ref, pw_ref, ps_ref, i * tm)
        mix_sc[:, 0:AW] = (t["gb"] * t["conv"] * (t["za"] * t["sa"])).astype(BF16)
        mix_sc[:, AW:2 * AW] = (t["mixed"] * ps_ref[...] * (t["zp"] * t["sb"])).astype(BF16)
        mix = mix_sc[...]
        _store_permuted(mixp_ref, mix)
        m = _nn(mix, wout_ref[...])
        m_ref[...] = m
        mh, _ = _rms_fwd(m, None)
        x1_ref[...] = xv + mh * post_ref[...]
        hc_ext[0:HALO, :] = hc_ext[tm:tm + HALO, :]
        xp_ext[0:HALO, :] = xp_ext[tm:tm + HALO, :]

        @pl.when(i == nt - 1)
        def _():
            for j in range(3):
                for a in range(n):
                    relay_copy(a, j, 1 - pc).wait_recv()
            for a in range(n):
                for j in range(3):
                    ici(a, j).wait_send()
                    relay_copy(a, j, pc).wait_send()
                own_copy(a).wait()

    any_spec = pl.BlockSpec(memory_space=pl.ANY)
    return pl.pallas_call(
        body, name="even_fwd", grid=(nt,),
        in_specs=[_rows(tm, D), _full((1, D)), _full((1, D)), _full((4, D, QW)), _full((3, AW)), _full((4, GD, GD)),
                  _full((1, AW)), _full((D, D))] + [any_spec] * n,
        out_specs=[_rows(tm, D), _rows(tm, W3), _rows(tm, D), _rows(tm, D), _rows(tm, D)] + [any_spec] * n,
        out_shape=[jax.ShapeDtypeStruct((S, D), F32), jax.ShapeDtypeStruct((S, W3), F32), jax.ShapeDtypeStruct((S, D), F32),
                   jax.ShapeDtypeStruct((S, D), BF16), jax.ShapeDtypeStruct((S, D), BF16)]
        + [jax.ShapeDtypeStruct((4, *p.shape), p.dtype) for p in next_shards],
        scratch_shapes=[pltpu.VMEM((tm + HALO, AW), F32), pltpu.VMEM((tm + HALO, AW), F32), pltpu.VMEM((tm, D), BF16)]
        + [pltpu.VMEM(p.shape, p.dtype) for p in next_shards]
        + [pltpu.SemaphoreType.DMA((n, 6)), pltpu.SemaphoreType.DMA((n, 6)), pltpu.SemaphoreType.DMA((n,))],
        compiler_params=_params(),
    )(x, pre, post, win, cw, pwb, ps, wout, *next_shards)


def _odd_mix(proj_ref, lng_ref, lnb_ref, ws_ref, bias_ref, sv_ref):
    tm = proj_ref.shape[0]
    u = proj_ref[:, 0:D]
    v = proj_ref[:, D:2 * D]
    z = proj_ref[:, 2 * D:3 * D]
    mu = jnp.mean(v, axis=-1, keepdims=True)
    vc = v - mu
    rs = lax.rsqrt(jnp.mean(vc * vc, axis=-1, keepdims=True) + EPS)
    vh = vc * rs
    vnb = (vh * lng_ref[...] + lnb_ref[...]).astype(BF16)
    for n in range(tm // CHUNK):
        for h in range(HEADS):
            rows, cols = slice(n * CHUNK, (n + 1) * CHUNK), slice(h * GD, (h + 1) * GD)
            sv_ref[rows, cols] = _nn(ws_ref[h], vnb[rows, cols]) + bias_ref[h]
    return dict(u=u, z=z, vh=vh, rs=rs, vnb=vnb, sz=_sigmoid(z))


def _odd_fwd(x1, tgt, pre, post, win, lng, lnb, wsb, bias, wout):
    S = x1.shape[0]
    tm = TM
    nt = S // tm

    def body(x_ref, tgt_ref, pre_ref, post_ref, win_ref, lng_ref, lnb_ref, ws_ref, bias_ref, wout_ref,
             proj_ref, m_ref, hb_ref, yb_ref, dx2_ref, loss_ref, sv_ref):
        i = pl.program_id(0)

        @pl.when(i == 0)
        def _():
            loss_ref[...] = jnp.zeros((8, GD), F32)

        xv = x_ref[...]
        xh, _ = _rms_fwd(xv, None)
        hb = (xh * pre_ref[...]).astype(BF16)
        hb_ref[...] = hb
        for q in range(4):
            proj_ref[:, q * QW:(q + 1) * QW] = _nn(hb, win_ref[q])
        t = _odd_mix(proj_ref, lng_ref, lnb_ref, ws_ref, bias_ref, sv_ref)
        yb = (t["u"] * sv_ref[...] * (t["z"] * t["sz"])).astype(BF16)
        _store_permuted(yb_ref, yb)
        m = _nn(yb, wout_ref[...])
        m_ref[...] = m
        mh, _ = _rms_fwd(m, None)
        err = xv + mh * post_ref[...] - tgt_ref[...]
        dx2_ref[...] = err * (1.0 / D)
        part = 0.5 * jnp.sum(jnp.mean(err * err, axis=-1, keepdims=True), axis=0, keepdims=True)
        loss_ref[...] += jnp.broadcast_to(part, (8, GD))

    return pl.pallas_call(
        body, name="odd_fwd", grid=(nt,),
        in_specs=[_rows(tm, D), _rows(tm, D), _full((1, D)), _full((1, D)), _full((4, D, QW)), _full((1, D)), _full((1, D)),
                  _full((HEADS, CHUNK, CHUNK)), _full((HEADS, CHUNK, GD)), _full((D, D))],
        out_specs=[_rows(tm, W3), _rows(tm, D), _rows(tm, D), _rows(tm, D), _rows(tm, D), _full((8, GD))],
        out_shape=[jax.ShapeDtypeStruct((S, W3), F32), jax.ShapeDtypeStruct((S, D), F32), jax.ShapeDtypeStruct((S, D), BF16),
                   jax.ShapeDtypeStruct((S, D), BF16), jax.ShapeDtypeStruct((S, D), F32), jax.ShapeDtypeStruct((8, GD), F32)],
        scratch_shapes=[pltpu.VMEM((tm, D), F32)],
        compiler_params=_params(),
    )(x1, tgt, pre, post, win, lng, lnb, wsb, bias, wout)


def _store_rows(ref, row0, value):
    r, width = value.shape
    for a in range(r):
        for k in range(width // GD):
            ref[row0 + a * (width // GD) + k:row0 + a * (width // GD) + k + 1, :] = value[a:a + 1, k * GD:(k + 1) * GD]


def _odd_bwd(dx2, x1, proj, m, loss, pre, post, win, lng, lnb, wsb, wsbt, bias, wout):
    S = x1.shape[0]
    tm = TM
    nt = S // tm

    def body(dy_ref, x_ref, proj_ref, m_ref, loss_ref, pre_ref, post_ref, win_ref, lng_ref, lnb_ref, ws_ref, wst_ref, bias_ref,
             wout_ref, dx_ref, dproj_ref, dmb_ref, small_ref, sv_ref, dvn_ref, acc1024, dws_acc, dbs_acc):
        i = pl.program_id(0)

        @pl.when(i == 0)
        def _():
            acc1024[...] = jnp.zeros_like(acc1024)
            dws_acc[...] = jnp.zeros_like(dws_acc)
            dbs_acc[...] = jnp.zeros_like(dbs_acc)

        dy = dy_ref[...]
        mh, rm = _rms_fwd(m_ref[...], None)
        dm, dpost = _rms_bwd(dy, mh, rm, post_ref[...])
        dmb = dm.astype(BF16)
        dmb_ref[...] = dmb
        dyv = _nt(dmb, wout_ref[...])
        t = _odd_mix(proj_ref, lng_ref, lnb_ref, ws_ref, bias_ref, sv_ref)
        u, z, sz, sv = t["u"], t["z"], t["sz"], sv_ref[...]
        dproj_ref[:, 0:D] = (dyv * sv * (z * sz)).astype(BF16)
        dproj_ref[:, 2 * D:3 * D] = (dyv * u * sv * (sz * (1.0 + z * (1.0 - sz)))).astype(BF16)
        dsv = dyv * u * (z * sz)
        dsvb = dsv.astype(BF16)
        for n in range(tm // CHUNK):
            for h in range(HEADS):
                rows, cols = slice(n * CHUNK, (n + 1) * CHUNK), slice(h * GD, (h + 1) * GD)
                dvn_ref[rows, cols] = _nn(wst_ref[h], dsvb[rows, cols])
                dws_acc[h] += _nt(dsvb[rows, cols], t["vnb"][rows, cols])
                dbs_acc[h] += dsv[rows, cols]
        dvn = dvn_ref[...]
        vh = t["vh"]
        dvh = dvn * lng_ref[...]
        dv = t["rs"] * (dvh - jnp.mean(dvh, axis=-1, keepdims=True) - vh * jnp.mean(dvh * vh, axis=-1, keepdims=True))
        dproj_ref[:, D:2 * D] = dv.astype(BF16)
        dh = _nt(dproj_ref[:, 0:QW], win_ref[0])
        for q in range(1, 4):
            dh += _nt(dproj_ref[:, q * QW:(q + 1) * QW], win_ref[q])
        xh, r = _rms_fwd(x_ref[...], None)
        dxn, dpre = _rms_bwd(dh, xh, r, pre_ref[...])
        dx_ref[...] = dy + dxn
        acc1024[0:1, :] += dpre
        acc1024[1:2, :] += dpost
        acc1024[2:3, :] += jnp.sum(dvn * vh, axis=0, keepdims=True)
        acc1024[3:4, :] += jnp.sum(dvn, axis=0, keepdims=True)

        @pl.when(i == nt - 1)
        def _():
            small_ref[...] = jnp.zeros_like(small_ref)
            _store_rows(small_ref, S1_PRE, acc1024[0:1, :])
            _store_rows(small_ref, S1_POST, acc1024[1:2, :])
            for q in range(4):
                _store_rows(small_ref, S1_LN + 8 * q, acc1024[2:3, 2 * q * GD:(2 * q + 2) * GD])
                _store_rows(small_ref, S1_LN + 8 * q + 2, acc1024[3:4, 2 * q * GD:(2 * q + 2) * GD])
            lower = lax.broadcasted_iota(jnp.int32, (CHUNK, CHUNK), 0) >= lax.broadcasted_iota(jnp.int32, (CHUNK, CHUNK), 1)
            for h in range(HEADS):
                small_ref[S1_WS + h * CHUNK:S1_WS + (h + 1) * CHUNK, :] = jnp.where(lower, dws_acc[h], 0.0)
                small_ref[S1_BS + h:S1_BS + h + 1, :] = jnp.sum(dbs_acc[h].T, axis=0, keepdims=True)
            small_ref[S1_LOSS:S1_LOSS + 8, :] = loss_ref[...]

    return pl.pallas_call(
        body, name="odd_bwd", grid=(nt,),
        in_specs=[_rows(tm, D), _rows(tm, D), _rows(tm, W3), _rows(tm, D), _full((8, GD)), _full((1, D)), _full((1, D)),
                  _full((4, D, QW)), _full((1, D)), _full((1, D)), _full((HEADS, CHUNK, CHUNK)), _full((HEADS, CHUNK, CHUNK)),
                  _full((HEADS, CHUNK, GD)), _full((D, D))],
        out_specs=[_rows(tm, D), _rows(tm, W3), _rows(tm, D), _full((S1_ROWS, GD))],
        out_shape=[jax.ShapeDtypeStruct((S, D), F32), jax.ShapeDtypeStruct((S, W3), BF16), jax.ShapeDtypeStruct((S, D), BF16),
                   jax.ShapeDtypeStruct((S1_ROWS, GD), F32)],
        scratch_shapes=[pltpu.VMEM((tm, D), F32), pltpu.VMEM((tm, D), F32), pltpu.VMEM((8, D), F32),
                        pltpu.VMEM((HEADS, CHUNK, CHUNK), F32), pltpu.VMEM((HEADS, CHUNK, GD), F32)],
        compiler_params=_params(),
    )(dx2, x1, proj, m, loss, pre, post, win, lng, lnb, wsb, wsbt, bias, wout)


def _even_bwd(dx1, x, proj, m, pre, post, win, cw, pwb, ps, wout):
    S = x.shape[0]
    tm = TM
    nt = S // tm
    L = tm + HALO

    def rev(i):
        return (nt - 1 - i, 0)

    def halo_index(i):
        return (jnp.maximum((nt - 1 - i) * (tm // HALO) - 1, 0), 0)

    def body(dy_ref, x_ref, proj_ref, halo_ref, m_ref, pre_ref, post_ref, win_ref, cw_ref, pw_ref, ps_ref, wout_ref,
             dx_ref, dproj_ref, dmb_ref, small_ref, hc_ext, xp_ext, dconv_ext, q_ext, acc1024, acc512, dpw_acc):
        i = pl.program_id(0)
        tile = nt - 1 - i

        @pl.when(i == 0)
        def _():
            dconv_ext[tm:L, :] = jnp.zeros((HALO, AW), F32)
            q_ext[tm:L, :] = jnp.zeros((HALO, AW), F32)
            acc1024[...] = jnp.zeros_like(acc1024)
            acc512[...] = jnp.zeros_like(acc512)
            dpw_acc[...] = jnp.zeros_like(dpw_acc)

        keep = (tile > 0).astype(F32)
        hc_ext[0:HALO, :] = halo_ref[:, 2 * AW:3 * AW] * halo_ref[:, 0:AW] * keep
        xp_ext[0:HALO, :] = halo_ref[:, 4 * AW:5 * AW] * keep

        dy = dy_ref[...]
        mh, rm = _rms_fwd(m_ref[...], None)
        dm, dpost = _rms_bwd(dy, mh, rm, post_ref[...])
        dmb = dm.astype(BF16)
        dmb_ref[...] = dmb
        dmix = _nt(dmb, wout_ref[...])
        dya, dyb = dmix[:, 0:AW], dmix[:, AW:2 * AW]
        t = _even_mix(proj_ref, hc_ext, xp_ext, cw_ref, pw_ref, ps_ref, tile * tm)
        gb, conv, za, sa, hc = t["gb"], t["conv"], t["za"], t["sa"], t["hc"]
        silu_a = za * sa
        dproj_ref[:, AW:2 * AW] = (dya * conv * silu_a).astype(BF16)
        dproj_ref[:, 3 * AW:4 * AW] = (dya * gb * conv * (sa * (1.0 + za * (1.0 - sa)))).astype(BF16)
        dconv = dya * gb * silu_a
        dconv_ext[0:tm, :] = dconv
        e = dconv_ext[...]
        dc1 = pltpu.roll(e, L - 1, 0)[0:tm]
        dc2 = pltpu.roll(e, L - 2, 0)[0:tm]
        dhc = cw_ref[2:3, :] * dconv + cw_ref[1:2, :] * dc1 + cw_ref[0:1, :] * dc2
        dproj_ref[:, 0:AW] = (dhc * t["gc"]).astype(BF16)
        dproj_ref[:, 2 * AW:3 * AW] = (dhc * t["xa"]).astype(BF16)
        acc512[0:1, :] += jnp.sum(dc2 * hc, axis=0, keepdims=True)
        acc512[1:2, :] += jnp.sum(dc1 * hc, axis=0, keepdims=True)
        acc512[2:3, :] += jnp.sum(dconv * hc, axis=0, keepdims=True)

        zp, sb, mixed = t["zp"], t["sb"], t["mixed"]
        silu_b = zp * sb
        acc512[3:4, :] += jnp.sum(dyb * mixed * silu_b, axis=0, keepdims=True)
        dmixedb = (dyb * ps_ref[...] * silu_b).astype(BF16)
        dproj_ref[:, 5 * AW:6 * AW] = (dyb * mixed * ps_ref[...] * (sb * (1.0 + zp * (1.0 - sb)))).astype(BF16)
        for g in range(4):
            cols = slice(g * GD, (g + 1) * GD)
            dpw_acc[g] += _tn(t["pooled"][g], dmixedb[:, cols])
            dpooled = _nt(dmixedb[:, cols], pw_ref[g])
            q_ext[0:tm, cols] = dpooled / t["counts"][g]
            s = q_ext[:, cols]
            for k in range(g + 1):
                s = s + pltpu.roll(s, L - 2 ** k, 0)
            dproj_ref[:, 4 * AW + g * GD:4 * AW + (g + 1) * GD] = (s[0:tm] - dpooled).astype(BF16)
        dconv_ext[tm:L, :] = dconv_ext[0:HALO, :]
        q_ext[tm:L, :] = q_ext[0:HALO, :]

        dh = _nt(dproj_ref[:, 0:QW], win_ref[0])
        for q in range(1, 4):
            dh += _nt(dproj_ref[:, q * QW:(q + 1) * QW], win_ref[q])
        xh, r = _rms_fwd(x_ref[...], None)
        dxn, dpre = _rms_bwd(dh, xh, r, pre_ref[...])
        dx_ref[...] = dy + dxn
        acc1024[0:1, :] += dpre
        acc1024[1:2, :] += dpost

        @pl.when(i == nt - 1)
        def _():
            small_ref[...] = jnp.zeros_like(small_ref)
            _store_rows(small_ref, S0_PRE, acc1024[0:1, :])
            _store_rows(small_ref, S0_POST, acc1024[1:2, :])
            for q in range(4):
                for k in range(3):
                    small_ref[S0_CONV + 8 * q + k:S0_CONV + 8 * q + k + 1, :] = acc512[k:k + 1, q * GD:(q + 1) * GD]
            _store_rows(small_ref, S0_PS, acc512[3:4, :])
            for g in range(4):
                small_ref[S0_PW + g * GD:S0_PW + (g + 1) * GD, :] = dpw_acc[g]

    return pl.pallas_call(
        body, name="even_bwd", grid=(nt,),
        in_specs=[_rows(tm, D, rev), _rows(tm, D, rev), _rows(tm, W3, rev), pl.BlockSpec((HALO, W3), halo_index), _rows(tm, D, rev),
                  _full((1, D)), _full((1, D)), _full((4, D, QW)), _full((3, AW)), _full((4, GD, GD)), _full((1, AW)), _full((D, D))],
        out_specs=[_rows(tm, D, rev), _rows(tm, W3, rev), _rows(tm, D, rev), _full((S0_ROWS, GD))],
        out_shape=[jax.ShapeDtypeStruct((S, D), F32), jax.ShapeDtypeStruct((S, W3), BF16), jax.ShapeDtypeStruct((S, D), BF16),
                   jax.ShapeDtypeStruct((S0_ROWS, GD), F32)],
        scratch_shapes=[pltpu.VMEM((L, AW), F32), pltpu.VMEM((L, AW), F32), pltpu.VMEM((L, AW), F32), pltpu.VMEM((L, AW), F32),
                        pltpu.VMEM((8, D), F32), pltpu.VMEM((8, AW), F32), pltpu.VMEM((4, GD, GD), F32)],
        compiler_params=_params(),
    )(dx1, x, proj, proj, m, pre, post, win, cw, pwb, ps, wout)


def _owner_id(me, relation, c):
    q = jnp.bitwise_xor(me, relation)
    return (q // 2, q % 2, c)


def _wgrad_in_reduce(a, b, pos, name):
    S = a.shape[0]
    nk = S // TK
    hm = D // 2

    def a_index(p, j, k, pos_ref):
        return (k, jnp.where(p == 1, pos_ref[0], 1 - pos_ref[0]))

    def b_index(p, j, k, pos_ref):
        return (k, jnp.bitwise_xor(pos_ref[1], (j + 1) % 4))

    def body(pos_ref, a_ref, b_ref, own_ref, arr_ref, acc, rbuf, sbuf, d2d_send, d2d_recv, ici_send, ici_recv):
        p, j, k = pl.program_id(0), pl.program_id(1), pl.program_id(2)
        x, y, c = _position()
        me = 2 * x + y
        slot = j % 2
        last = k == nk - 1

        def to_sibling(jj):
            return pltpu.make_async_remote_copy(
                src_ref=acc.at[jj % 2], dst_ref=rbuf.at[jj], send_sem=d2d_send.at[jj], recv_sem=d2d_recv.at[jj],
                device_id=(x, y, 1 - c), device_id_type=MESH)

        def to_owner(jj):
            return pltpu.make_async_remote_copy(
                src_ref=sbuf.at[jj], dst_ref=arr_ref.at[jj], send_sem=ici_send.at[jj], recv_sem=ici_recv.at[jj],
                device_id=_owner_id(me, jj + 1, c), device_id_type=MESH)

        @pl.when((k == 0) & (p == 0) & (j >= 2))
        def _():
            to_sibling(j - 2).wait_send()

        @pl.when((k == 0) & (p == 1) & (j < 2))
        def _():
            to_sibling(j + 2).wait_send()

        @pl.when(k == 0)
        def _():
            acc[slot] = jnp.zeros((hm, QW), F32)

        acc[slot] += _tn(a_ref[...], b_ref[...])

        @pl.when(last & (p == 0))
        def _():
            to_sibling(j).start()

        @pl.when(last & (p == 1))
        def _():
            to_sibling(j).wait_recv()
            s = acc[slot] + rbuf[j]

            @pl.when(j < 3)
            def _():
                sbuf[j] = s.astype(BF16)
                to_owner(j).start()

            @pl.when(j == 3)
            def _():
                own_ref[...] = s
                for jj in range(3):
                    to_owner(jj).wait()

    grid_spec = pltpu.PrefetchScalarGridSpec(
        num_scalar_prefetch=1, grid=(2, 4, nk),
        in_specs=[pl.BlockSpec((TK, hm), a_index), pl.BlockSpec((TK, QW), b_index)],
        out_specs=[pl.BlockSpec((hm, QW), lambda p, j, k, pos_ref: (0, 0)), pl.BlockSpec(memory_space=pl.ANY)],
        scratch_shapes=[pltpu.VMEM((2, hm, QW), F32), pltpu.VMEM((4, hm, QW), F32), pltpu.VMEM((3, hm, QW), BF16),
                        pltpu.SemaphoreType.DMA((4,)), pltpu.SemaphoreType.DMA((4,)),
                        pltpu.SemaphoreType.DMA((3,)), pltpu.SemaphoreType.DMA((3,))])
    return pl.pallas_call(
        body, name=name, grid_spec=grid_spec,
        out_shape=[jax.ShapeDtypeStruct((hm, QW), F32), jax.ShapeDtypeStruct((3, hm, QW), BF16)],
        compiler_params=pltpu.CompilerParams(dimension_semantics=("arbitrary",) * 3, vmem_limit_bytes=VMEM_LIMIT),
    )(pos, a, b)


def _wgrad_out_reduce(a, b, pos, name):
    S = a.shape[0]
    nk = S // TK
    hm = D // 2
    qr = hm // 4

    def a_index(p, k, pos_ref):
        return (k, jnp.where(p == 1, pos_ref[0], 1 - pos_ref[0]))

    def body(pos_ref, a_ref, b_ref, own_ref, arr_ref, acc, rbuf, total, sbuf, d2d_send, d2d_recv, ici_send, ici_recv):
        p, k = pl.program_id(0), pl.program_id(1)
        x, y, c = _position()
        me = 2 * x + y
        last = k == nk - 1
        to_sibling = pltpu.make_async_remote_copy(
            src_ref=acc.at[0], dst_ref=rbuf, send_sem=d2d_send, recv_sem=d2d_recv, device_id=(x, y, 1 - c), device_id_type=MESH)

        def to_owner(jj):
            return pltpu.make_async_remote_copy(
                src_ref=sbuf.at[jj], dst_ref=arr_ref.at[jj], send_sem=ici_send.at[jj], recv_sem=ici_recv.at[jj],
                device_id=_owner_id(me, jj + 1, c), device_id_type=MESH)

        @pl.when(k == 0)
        def _():
            acc[p] = jnp.zeros((hm, D), F32)

        acc[p] += _tn(a_ref[...], b_ref[...])

        @pl.when(last & (p == 0))
        def _():
            to_sibling.start()

        @pl.when(last & (p == 1))
        def _():
            to_sibling.wait_recv()
            total[...] = acc[1] + rbuf[...]
            for jj in range(3):
                q = jnp.bitwise_xor(me, jj + 1)
                sbuf[jj] = total[pl.ds(pl.multiple_of(q * qr, qr), qr), :].astype(BF16)
                to_owner(jj).start()
            own_ref[...] = total[pl.ds(pl.multiple_of(me * qr, qr), qr), :]
            to_sibling.wait_send()
            for jj in range(3):
                to_owner(jj).wait()

    grid_spec = pltpu.PrefetchScalarGridSpec(
        num_scalar_prefetch=1, grid=(2, nk),
        in_specs=[pl.BlockSpec((TK, hm), a_index), pl.BlockSpec((TK, D), lambda p, k, pos_ref: (k, 0))],
        out_specs=[pl.BlockSpec((qr, D), lambda p, k, pos_ref: (0, 0)), pl.BlockSpec(memory_space=pl.ANY)],
        scratch_shapes=[pltpu.VMEM((2, hm, D), F32), pltpu.VMEM((hm, D), F32), pltpu.VMEM((hm, D), F32), pltpu.VMEM((3, qr, D), BF16),
                        pltpu.SemaphoreType.DMA, pltpu.SemaphoreType.DMA,
                        pltpu.SemaphoreType.DMA((3,)), pltpu.SemaphoreType.DMA((3,))])
    return pl.pallas_call(
        body, name=name, grid_spec=grid_spec,
        out_shape=[jax.ShapeDtypeStruct((qr, D), F32), jax.ShapeDtypeStruct((3, qr, D), BF16)],
        compiler_params=pltpu.CompilerParams(dimension_semantics=("arbitrary",) * 2, vmem_limit_bytes=VMEM_LIMIT),
    )(pos, a, b)


def _chips(x, y):
    return [(1 - x, y), (x, 1 - y), (1 - x, 1 - y)]


def _gather_weights(parts, split):
    n = len(parts)

    def body(*refs):
        ins, outs = refs[:n], refs[n:2 * n]
        send_sems, recv_sems, local_sems = refs[2 * n:]
        x, y, c = _position()
        me = 2 * x + y
        chips = _chips(x, y)
        local = [pltpu.make_async_copy(ins[a], outs[a].at[me], local_sems.at[a]) for a in range(n)]
        for cp in local:
            cp.start()

        def half(a, ref, who):
            rows = parts[a].shape[0] // 2
            return ref.at[pl.ds(pl.multiple_of(who * rows, 8), rows), :] if split[a] else ref

        sends = []
        for a in range(n):
            for j, chip in enumerate(chips):
                sends.append(pltpu.make_async_remote_copy(
                    src_ref=half(a, ins[a], c), dst_ref=half(a, outs[a].at[me], c),
                    send_sem=send_sems.at[a, j], recv_sem=recv_sems.at[a, j], device_id=(*chip, c), device_id_type=MESH))
        for cp in sends:
            cp.start()
        for j, chip in enumerate(chips):
            src = 2 * chip[0] + chip[1]
            for a in range(n):
                pltpu.make_async_remote_copy(
                    src_ref=half(a, ins[a], c), dst_ref=half(a, outs[a].at[src], c),
                    send_sem=send_sems.at[a, j], recv_sem=recv_sems.at[a, j], device_id=(*chip, c), device_id_type=MESH).wait_recv()
                if split[a]:
                    fwd = pltpu.make_async_remote_copy(
                        src_ref=half(a, outs[a].at[src], c), dst_ref=half(a, outs[a].at[src], c),
                        send_sem=send_sems.at[a, 3 + j], recv_sem=recv_sems.at[a, 3 + j], device_id=(x, y, 1 - c), device_id_type=MESH)
                    fwd.start()
                    sends.append(fwd)
        for j, chip in enumerate(chips):
            src = 2 * chip[0] + chip[1]
            for a in range(n):
                if split[a]:
                    pltpu.make_async_remote_copy(
                        src_ref=half(a, outs[a].at[src], 1 - c), dst_ref=half(a, outs[a].at[src], 1 - c),
                        send_sem=send_sems.at[a, 3 + j], recv_sem=recv_sems.at[a, 3 + j], device_id=(x, y, 1 - c),
                        device_id_type=MESH).wait_recv()
        for cp in sends:
            cp.wait_send()
        for cp in local:
            cp.wait()

    any_spec = pl.BlockSpec(memory_space=pl.ANY)
    return pl.pallas_call(
        body, name="gather_weights",
        in_specs=[pl.BlockSpec(memory_space=pltpu.VMEM)] * n, out_specs=[any_spec] * n,
        out_shape=[jax.ShapeDtypeStruct((4, *p.shape), p.dtype) for p in parts],
        scratch_shapes=[pltpu.SemaphoreType.DMA((n, 6)), pltpu.SemaphoreType.DMA((n, 6)), pltpu.SemaphoreType.DMA((n,))],
    )(*parts)


def _gather_small(parts):
    n = len(parts)

    def body(*refs):
        ins, outs = refs[:n], refs[n:2 * n]
        send_sems, recv_sems, local_sems = refs[2 * n:]
        x, y, c = _position()
        chips = _chips(x, y)

        def slot(chip, core):
            return 4 * chip[0] + 2 * chip[1] + core

        def copy(a, k, src, block, to):
            return pltpu.make_async_remote_copy(src_ref=src, dst_ref=outs[a].at[block], send_sem=send_sems.at[a, k],
                                                recv_sem=recv_sems.at[a, k], device_id=to, device_id_type=MESH)

        local = [pltpu.make_async_copy(ins[a], outs[a].at[slot((x, y), c)], local_sems.at[a]) for a in range(n)]
        for cp in local:
            cp.start()
        sends = []
        for a in range(n):
            sends.append(copy(a, 0, ins[a], slot((x, y), c), (x, y, 1 - c)))
            for j, chip in enumerate(chips):
                sends.append(copy(a, 1 + j, ins[a], slot((x, y), c), (*chip, c)))
        for cp in sends:
            cp.start()
        for j, chip in enumerate(chips):
            for a in range(n):
                copy(a, 1 + j, ins[a], slot(chip, c), (*chip, c)).wait_recv()
                fwd = copy(a, 4 + j, outs[a].at[slot(chip, c)], slot(chip, c), (x, y, 1 - c))
                fwd.start()
                sends.append(fwd)
        for a in range(n):
            copy(a, 0, ins[a], slot((x, y), 1 - c), (x, y, 1 - c)).wait_recv()
            for j, chip in enumerate(chips):
                copy(a, 4 + j, ins[a], slot(chip, 1 - c), (x, y, 1 - c)).wait_recv()
        for cp in sends:
            cp.wait_send()
        for cp in local:
            cp.wait()

    any_spec = pl.BlockSpec(memory_space=pl.ANY)
    return pl.pallas_call(
        body, name="gather_small",
        in_specs=[any_spec] * n, out_specs=[any_spec] * n,
        out_shape=[jax.ShapeDtypeStruct((8, *p.shape), p.dtype) for p in parts],
        scratch_shapes=[pltpu.SemaphoreType.DMA((n, 7)), pltpu.SemaphoreType.DMA((n, 7)), pltpu.SemaphoreType.DMA((n,))],
    )(*parts)


def _adamw(w, g, m, v):
    m = ADAM_B1 * m + (1.0 - ADAM_B1) * g
    v = ADAM_B2 * v + (1.0 - ADAM_B2) * (g * g)
    m_hat = m / (1.0 - ADAM_B1 ** ADAM_STEP)
    v_hat = v / (1.0 - ADAM_B2 ** ADAM_STEP)
    delta = -ADAM_LR * (m_hat / (jnp.sqrt(v_hat) + ADAM_EPS) + ADAM_WD * w)
    return delta, m, v


def _finalize(own, arrivals, w, m, v, pos, name):
    hr, cols = own.shape
    tr = 128
    nth = hr // tr

    def half_index(p, i, pos_ref):
        return (jnp.where(p == 0, pos_ref[0], 1 - pos_ref[0]) * nth + i, 0)

    def own_index(p, i, pos_ref):
        return (jnp.where(p == 0, i, nth - 1), 0)

    def arr_index(p, i, pos_ref):
        return (0, jnp.where(p == 0, i, nth - 1), 0)

    def body(pos_ref, own_ref, arr_ref, w_ref, m_ref, v_ref, g_out, d_out, m_out, v_out, mine, theirs, send_sem, recv_sem):
        p, i = pl.program_id(0), pl.program_id(1)
        x, y, c = _position()

        def share(rows):
            return pltpu.make_async_remote_copy(
                src_ref=mine.at[rows, :], dst_ref=theirs.at[rows, :], send_sem=send_sem, recv_sem=recv_sem,
                device_id=(x, y, 1 - c), device_id_type=MESH)

        def emit(g):
            g_out[...] = g
            d_out[...], m_out[...], v_out[...] = _adamw(w_ref[...], g, m_ref[...], v_ref[...])

        tile = pl.ds(pl.multiple_of(i * tr, tr), tr)

        @pl.when(p == 0)
        def _():
            g = own_ref[...]
            for j in range(3):
                g = g + arr_ref[j].astype(F32)
            mine[tile, :] = g
            share(tile).start()
            emit(g)

        @pl.when((p == 1) & (i == 0))
        def _():
            share(pl.ds(0, hr)).wait()

        @pl.when(p == 1)
        def _():
            emit(theirs[tile, :])

    spec = pl.BlockSpec((tr, cols), half_index)
    grid_spec = pltpu.PrefetchScalarGridSpec(
        num_scalar_prefetch=1, grid=(2, nth),
        in_specs=[pl.BlockSpec((tr, cols), own_index), pl.BlockSpec((3, tr, cols), arr_index), spec, spec, spec],
        out_specs=[spec] * 4,
        scratch_shapes=[pltpu.VMEM((hr, cols), F32), pltpu.VMEM((hr, cols), F32), pltpu.SemaphoreType.DMA, pltpu.SemaphoreType.DMA])
    return pl.pallas_call(
        body, name=name, grid_spec=grid_spec,
        out_shape=[jax.ShapeDtypeStruct((2 * hr, cols), F32)] * 4,
        compiler_params=pltpu.CompilerParams(dimension_semantics=("arbitrary",) * 2, vmem_limit_bytes=VMEM_LIMIT),
    )(pos, own, arrivals, w, m, v)


def _adamw_small(g0, g1, weights, moms, vels):
    names = ["pre", "post", "conv", "pw", "ps", "lng", "lnb", "ws", "bs"]
    shapes = [w.shape for w in weights]

    def body(*refs):
        me = 2 * lax.axis_index("x") + lax.axis_index("y")
        g0_ref, g1_ref = refs[0], refs[1]
        w_refs, m_refs, v_refs = refs[2:11], refs[11:20], refs[20:29]
        outs = refs[29:29 + 36]
        loss_ref = refs[65]
        t0_ref, t1_ref = refs[66], refs[67]
        t0 = g0_ref[0]
        t1 = g1_ref[0]
        for d in range(1, 8):
            t0 = t0 + g0_ref[d]
            t1 = t1 + g1_ref[d]
        t0_ref[...] = t0
        t1_ref[...] = t1
        loss_ref[...] = t1_ref[S1_LOSS:S1_LOSS + 1, 0:1]
        my_conv = pl.multiple_of(S0_CONV + 8 * me, 8)
        my_ln = pl.multiple_of(S1_LN + 8 * me, 8)

        def update(idx, piece, grad):
            go, do, mo, vo = outs[4 * idx:4 * idx + 4]
            go[piece] = grad
            do[piece], mo[piece], vo[piece] = _adamw(w_refs[idx][piece], grad, m_refs[idx][piece], v_refs[idx][piece])

        for layer in range(2):
            for k in range(D // GD):
                lanes = slice(k * GD, (k + 1) * GD)
                tref, pre0, post0 = (t0_ref, S0_PRE, S0_POST) if layer == 0 else (t1_ref, S1_PRE, S1_POST)
                update(0, (slice(layer, layer + 1), lanes), tref[pre0 + k:pre0 + k + 1, :])
                update(1, (slice(layer, layer + 1), lanes), tref[post0 + k:post0 + k + 1, :])
        conv_rows = t0_ref[pl.ds(my_conv, 8), :]
        update(2, (slice(0, 3), slice(None)), conv_rows[0:3, :])
        for g in range(4):
            update(3, (g,), t0_ref[S0_PW + g * GD:S0_PW + (g + 1) * GD, :])
            update(4, (slice(0, 1), slice(g * GD, (g + 1) * GD)), t0_ref[S0_PS + g:S0_PS + g + 1, :])
        ln_rows = t1_ref[pl.ds(my_ln, 8), :]
        for k in range(2):
            update(5, (slice(0, 1), slice(k * GD, (k + 1) * GD)), ln_rows[k:k + 1, :])
            update(6, (slice(0, 1), slice(k * GD, (k + 1) * GD)), ln_rows[2 + k:3 + k, :])
        for h in range(HEADS):
            update(7, (h,), t1_ref[S1_WS + h * CHUNK:S1_WS + (h + 1) * CHUNK, :])
        update(8, (slice(None), slice(None)), t1_ref[S1_BS:S1_BS + HEADS, :])

    vm = pl.BlockSpec(memory_space=pltpu.VMEM)
    out_shape = []
    for s in shapes:
        out_shape += [jax.ShapeDtypeStruct(s, F32)] * 4
    out_shape.append(jax.ShapeDtypeStruct((1, 1), F32))
    res = pl.pallas_call(
        body, name="adamw_small",
        in_specs=[vm] * 29, out_specs=[vm] * 37, out_shape=out_shape,
        scratch_shapes=[pltpu.VMEM((S0_ROWS, GD), F32), pltpu.VMEM((S1_ROWS, GD), F32)],
        compiler_params=pltpu.CompilerParams(vmem_limit_bytes=VMEM_LIMIT),
    )(g0, g1, *weights, *moms, *vels)
    per_weight = {nm: res[4 * i:4 * i + 4] for i, nm in enumerate(names)}
    return per_weight, res[36]


def _pad8(a):
    return jnp.pad(a, ((0, 8 - a.shape[0]), (0, 0)))


def kernel(x, pre_norm, post_norm, even_w_in, even_conv_w, even_pool_w, even_pool_scale, even_w_out, odd_w_in, odd_ln_g, odd_ln_b, odd_w_s, odd_b_s, odd_w_out, loss_target, m_pre_norm, m_post_norm, m_even_w_in, m_even_conv_w, m_even_pool_w, m_even_pool_scale, m_even_w_out, m_odd_w_in, m_odd_ln_g, m_odd_ln_b, m_odd_w_s, m_odd_b_s, m_odd_w_out, v_pre_norm, v_post_norm, v_even_w_in, v_even_conv_w, v_even_pool_w, v_even_pool_scale, v_even_w_out, v_odd_w_in, v_odd_ln_g, v_odd_ln_b, v_odd_w_s, v_odd_b_s, v_odd_w_out):
    xs = x[0]
    tgt = loss_target[0]

    small_shard = jnp.concatenate([_pad8(even_conv_w[0]), _pad8(odd_ln_g.reshape(2, GD)), _pad8(odd_ln_b.reshape(2, GD))], axis=0)
    win0, wout0, shard = _gather_weights([even_w_in[0].astype(BF16), even_w_out[0].astype(BF16), small_shard], [True, True, False])
    wout0 = wout0.reshape(D, D)
    px, py, pc = _position()
    pos = jnp.stack([pc, 2 * px + py]).astype(jnp.int32)
    conv_w = shard[:, 0:3, :].transpose(1, 0, 2).reshape(3, AW)
    ln_g = shard[:, 8:10, :].reshape(1, D)
    ln_b = shard[:, 16:18, :].reshape(1, D)
    pool_wb = even_pool_w[0].astype(BF16)
    ws_tril = jnp.tril(odd_w_s[0]).astype(BF16)
    ws_tril_t = jnp.swapaxes(ws_tril, 1, 2)
    bias = jnp.broadcast_to(odd_b_s[0][:, :, None], (HEADS, CHUNK, GD))
    pre0, pre1 = pre_norm[0:1], pre_norm[1:2]
    post0, post1 = post_norm[0:1], post_norm[1:2]

    x1, proj0, m0, hb0, mixp0, win1, wout1 = _even_fwd(
        xs, pre0, post0, win0, conv_w, pool_wb, even_pool_scale, wout0, [odd_w_in[0].astype(BF16), odd_w_out[0].astype(BF16)])
    wout1 = wout1.reshape(D, D)
    proj1, m1, hb1, yp1, dx2, loss_part = _odd_fwd(x1, tgt, pre1, post1, win1, ln_g, ln_b, ws_tril, bias, wout1)
    dx1, dproj1, dmb1, small1 = _odd_bwd(dx2, x1, proj1, m1, loss_part, pre1, post1, win1, ln_g, ln_b, ws_tril, ws_tril_t, bias, wout1)
    red_out1 = _wgrad_out_reduce(yp1, dmb1, pos, "wgrad_odd_out")
    red_in1 = _wgrad_in_reduce(hb1, dproj1, pos, "wgrad_odd_in")
    gx, dproj0, dmb0, small0 = _even_bwd(dx1, xs, proj0, m0, pre0, post0, win0, conv_w, pool_wb, even_pool_scale, wout0)
    red_out0 = _wgrad_out_reduce(mixp0, dmb0, pos, "wgrad_even_out")
    red_in0 = _wgrad_in_reduce(hb0, dproj0, pos, "wgrad_even_in")

    reduced = [red_in0, red_out0, red_in1, red_out1]
    tags = ["even_in", "even_out", "odd_in", "odd_out"]
    big_w = [even_w_in[0], even_w_out[0], odd_w_in[0], odd_w_out[0]]
    big_m = [m_even_w_in[0], m_even_w_out[0], m_odd_w_in[0], m_odd_w_out[0]]
    big_v = [v_even_w_in[0], v_even_w_out[0], v_odd_w_in[0], v_odd_w_out[0]]
    big = [_finalize(r[0], r[1], w, m, v, pos, "finalize_" + t) for r, w, m, v, t in zip(reduced, big_w, big_m, big_v, tags)]

    all0, all1 = _gather_small([small0, small1])
    small_w = [pre_norm, post_norm, even_conv_w[0], even_pool_w[0], even_pool_scale, odd_ln_g, odd_ln_b, odd_w_s[0], odd_b_s[0]]
    small_m = [m_pre_norm, m_post_norm, m_even_conv_w[0], m_even_pool_w[0], m_even_pool_scale, m_odd_ln_g, m_odd_ln_b, m_odd_w_s[0], m_odd_b_s[0]]
    small_v = [v_pre_norm, v_post_norm, v_even_conv_w[0], v_even_pool_w[0], v_even_pool_scale, v_odd_ln_g, v_odd_ln_b, v_odd_w_s[0], v_odd_b_s[0]]
    sm, loss = _adamw_small(all0, all1, small_w, small_m, small_v)

    def lead(a):
        return a[None]

    per = {
        "pre_norm": sm["pre"], "post_norm": sm["post"],
        "even_w_in": [lead(a) for a in big[0]], "even_conv_w": [lead(a) for a in sm["conv"]],
        "even_pool_w": [lead(a) for a in sm["pw"]], "even_pool_scale": sm["ps"],
        "even_w_out": [lead(a) for a in big[1]], "odd_w_in": [lead(a) for a in big[2]],
        "odd_ln_g": sm["lng"], "odd_ln_b": sm["lnb"],
        "odd_w_s": [lead(a) for a in sm["ws"]], "odd_b_s": [lead(a) for a in sm["bs"]],
        "odd_w_out": [lead(a) for a in big[3]],
    }
    order = ["pre_norm", "post_norm", "even_w_in", "even_conv_w", "even_pool_w", "even_pool_scale", "even_w_out", "odd_w_in",
             "odd_ln_g", "odd_ln_b", "odd_w_s", "odd_b_s", "odd_w_out"]
    outs = [loss.reshape(()), gx[None]]
    for kind in range(4):
        outs += [per[nm][kind] for nm in order]
    return tuple(outs)
```

```python
import functools

import jax
import jax.numpy as jnp
from jax import lax
from jax.experimental import pallas as pl
from jax.experimental.pallas import tpu as pltpu

F32 = jnp.float32
BF16 = jnp.bfloat16
MESH = pl.DeviceIdType.MESH

D = 1024
W3 = 3 * D
QW = W3 // 4
AW = 512
GD = 128
CHUNK = 128
HEADS = 8
HALO = 16
POOL_WINDOWS = (2, 4, 8, 16)
EPS = 1e-6
TM = 256
TK = 1024
VMEM_LIMIT = 56 * 1024 * 1024

ADAM_LR, ADAM_B1, ADAM_B2, ADAM_EPS, ADAM_WD, ADAM_STEP = 0.001, 0.9, 0.999, 1e-08, 0.01, 10

S0_PRE, S0_POST, S0_CONV, S0_PS, S0_PW, S0_ROWS = 0, 8, 16, 48, 56, 568
S1_PRE, S1_POST, S1_LN, S1_BS, S1_WS, S1_LOSS, S1_ROWS = 0, 8, 16, 48, 56, 1080, 1088


def _nn(a, b):
    return jnp.dot(a, b, preferred_element_type=F32)


def _nt(a, b):
    return lax.dot_general(a, b, (((1,), (1,)), ((), ())), preferred_element_type=F32)


def _tn(a, b):
    return lax.dot_general(a, b, (((0,), (0,)), ((), ())), preferred_element_type=F32)


def _sigmoid(z):
    return 1.0 / (1.0 + jnp.exp(-z))


def _rms_fwd(x, g):
    r = lax.rsqrt(jnp.mean(x * x, axis=-1, keepdims=True) + EPS)
    return x * r, r


def _rms_bwd(dy, xh, r, g):
    dn = dy * g
    dx = r * (dn - xh * jnp.mean(xh * dn, axis=-1, keepdims=True))
    return dx, jnp.sum(dy * xh, axis=0, keepdims=True)


def _full(shape):
    nd = len(shape)
    return pl.BlockSpec(shape, lambda i, _n=nd: (0,) * _n)


def _rows(tm, width, index=None):
    return pl.BlockSpec((tm, width), (lambda i: (i, 0)) if index is None else index)


def _params():
    return pltpu.CompilerParams(dimension_semantics=("arbitrary",), vmem_limit_bytes=VMEM_LIMIT)


def _position():
    x, y, c = lax.axis_index("x"), lax.axis_index("y"), lax.axis_index("c")
    return x, y, c


def _even_mix(proj_ref, hc_ext, xp_ext, cw_ref, pw_ref, ps_ref, first_row):
    tm = proj_ref.shape[0]
    xa = proj_ref[:, 0:AW]
    gb = proj_ref[:, AW:2 * AW]
    gc = proj_ref[:, 2 * AW:3 * AW]
    za = proj_ref[:, 3 * AW:4 * AW]
    xp = proj_ref[:, 4 * AW:5 * AW]
    zp = proj_ref[:, 5 * AW:6 * AW]
    hc = gc * xa
    hc_ext[HALO:, :] = hc
    e = hc_ext[...]
    conv = cw_ref[2:3, :] * hc + cw_ref[1:2, :] * pltpu.roll(e, 1, 0)[HALO:] + cw_ref[0:1, :] * pltpu.roll(e, 2, 0)[HALO:]
    sa = _sigmoid(za)
    xp_ext[HALO:, :] = xp
    pos = first_row + lax.broadcasted_iota(jnp.int32, (tm, 1), 0)
    pooled, mixed, counts = [], [], []
    for g, w in enumerate(POOL_WINDOWS):
        s = xp_ext[:, g * GD:(g + 1) * GD]
        for k in range(g + 1):
            s = s + pltpu.roll(s, 2 ** k, 0)
        count = jnp.minimum(pos + 1, w).astype(F32)
        pg = s[HALO:] / count - xp[:, g * GD:(g + 1) * GD]
        pooled.append(pg.astype(BF16))
        mixed.append(_nn(pooled[-1], pw_ref[g]))
        counts.append(count)
    mixed = jnp.concatenate(mixed, axis=-1)
    sb = _sigmoid(zp)
    return dict(xa=xa, gb=gb, gc=gc, za=za, zp=zp, hc=hc, conv=conv, sa=sa, sb=sb, pooled=pooled, mixed=mixed, counts=counts)


def _half_rows(ref, rows, who):
    return ref.at[pl.ds(pl.multiple_of(who * (rows // 2), 8), rows // 2), :]


def _store_permuted(ref, value):
    for ob in range(D // GD):
        nb = 4 * (ob % 2) + ob // 2
        ref[:, nb * GD:(nb + 1) * GD] = value[:, ob * GD:(ob + 1) * GD]


def _even_fwd(x, pre, post, win, cw, pwb, ps, wout, next_shards):
    S = x.shape[0]
    tm = TM
    nt = S // tm
    relay = (3 * nt) // 4
    n = len(next_shards)
    shard_rows = [p.shape[0] for p in next_shards]

    def body(x_ref, pre_ref, post_ref, win_ref, cw_ref, pw_ref, ps_ref, wout_ref, *rest):
        shard_refs, rest = rest[:n], rest[n:]
        x1_ref, proj_ref, m_ref, hb_ref, mixp_ref = rest[:5]
        full_refs, rest = rest[5:5 + n], rest[5 + n:]
        hc_ext, xp_ext, mix_sc = rest[:3]
        stage, rest = rest[3:3 + n], rest[3 + n:]
        send_sems, recv_sems, local_sems = rest
        i = pl.program_id(0)
        px, py, pc = _position()
        me = 2 * px + py
        chips = _chips(px, py)

        def ici(a, j):
            return pltpu.make_async_remote_copy(
                src_ref=_half_rows(shard_refs[a], shard_rows[a], pc), dst_ref=_half_rows(full_refs[a].at[me], shard_rows[a], pc),
                send_sem=send_sems.at[a, j], recv_sem=recv_sems.at[a, j], device_id=(*chips[j], pc), device_id_type=MESH)

        def ici_arrival(a, j):
            src = 2 * chips[j][0] + chips[j][1]
            return pltpu.make_async_remote_copy(
                src_ref=_half_rows(shard_refs[a], shard_rows[a], pc), dst_ref=_half_rows(full_refs[a].at[src], shard_rows[a], pc),
                send_sem=send_sems.at[a, j], recv_sem=recv_sems.at[a, j], device_id=(*chips[j], pc), device_id_type=MESH)

        def relay_copy(a, j, who):
            src = 2 * chips[j][0] + chips[j][1]
            region = _half_rows(full_refs[a].at[src], shard_rows[a], who)
            return pltpu.make_async_remote_copy(
                src_ref=region, dst_ref=region, send_sem=send_sems.at[a, 3 + j], recv_sem=recv_sems.at[a, 3 + j],
                device_id=(px, py, 1 - pc), device_id_type=MESH)

        def own_copy(a):
            return pltpu.make_async_copy(stage[a], full_refs[a].at[me], local_sems.at[a])

        @pl.when(i == 0)
        def _():
            hc_ext[0:HALO, :] = jnp.zeros((HALO, AW), F32)
            xp_ext[0:HALO, :] = jnp.zeros((HALO, AW), F32)
            for a in range(n):
                for j in range(3):
                    ici(a, j).start()
            for a in range(n):
                load = pltpu.make_async_copy(shard_refs[a], stage[a], local_sems.at[a])
                load.start()
                load.wait()
                own_copy(a).start()

        @pl.when(i == relay)
        def _():
            for j in range(3):
                for a in range(n):
                    ici_arrival(a, j).wait_recv()
                    relay_copy(a, j, pc).start()

        xv = x_ref[...]
        xh, _ = _rms_fwd(xv, None)
        hb = (xh * pre_ref[...]).astype(BF16)
        hb_ref[...] = hb
        for q in range(4):
            proj_ref[:, q * QW:(q + 1) * QW] = _nn(hb, win_ref[q])
        t = _even_mix(proj_ref, hc_ext, xp_ext, cw_ref, pw_ref, ps_ref, i * tm)
        mix_sc[:, 0:AW] = (t["gb"] * t["conv"] * (t["za"] * t["sa"])).astype(BF16)
        mix_sc[:, AW:2 * AW] = (t["mixed"] * ps_ref[...] * (t["zp"] * t["sb"])).astype(BF16)
        mix = mix_sc[...]
        _store_permuted(mixp_ref, mix)
        m = _nn(mix, wout_ref[...])
        m_ref[...] = m
        mh, _ = _rms_fwd(m, None)
        x1_ref[...] = xv + mh * post_ref[...]
        hc_ext[0:HALO, :] = hc_ext[tm:tm + HALO, :]
        xp_ext[0:HALO, :] = xp_ext[tm:tm + HALO, :]

        @pl.when(i == nt - 1)
        def _():
            for j in range(3):
                for a in range(n):
                    relay_copy(a, j, 1 - pc).wait_recv()
            for a in range(n):
                for j in range(3):
                    ici(a, j).wait_send()
                    relay_copy(a, j, pc).wait_send()
                own_copy(a).wait()

    any_spec = pl.BlockSpec(memory_space=pl.ANY)
    return pl.pallas_call(
        body, name="even_fwd", grid=(nt,),
        in_specs=[_rows(tm, D), _full((1, D)), _full((1, D)), _full((4, D, QW)), _full((3, AW)), _full((4, GD, GD)),
                  _full((1, AW)), _full((D, D))] + [any_spec] * n,
        out_specs=[_rows(tm, D), _rows(tm, W3), _rows(tm, D), _rows(tm, D), _rows(tm, D)] + [any_spec] * n,
        out_shape=[jax.ShapeDtypeStruct((S, D), F32), jax.ShapeDtypeStruct((S, W3), F32), jax.ShapeDtypeStruct((S, D), F32),
                   jax.ShapeDtypeStruct((S, D), BF16), jax.ShapeDtypeStruct((S, D), BF16)]
        + [jax.ShapeDtypeStruct((4, *p.shape), p.dtype) for p in next_shards],
        scratch_shapes=[pltpu.VMEM((tm + HALO, AW), F32), pltpu.VMEM((tm + HALO, AW), F32), pltpu.VMEM((tm, D), BF16)]
        + [pltpu.VMEM(p.shape, p.dtype) for p in next_shards]
        + [pltpu.SemaphoreType.DMA((n, 6)), pltpu.SemaphoreType.DMA((n, 6)), pltpu.SemaphoreType.DMA((n,))],
        compiler_params=_params(),
    )(x, pre, post, win, cw, pwb, ps, wout, *next_shards)


def _odd_mix(proj_ref, lng_ref, lnb_ref, ws_ref, bias_ref, sv_ref):
    tm = proj_ref.shape[0]
    u = proj_ref[:, 0:D]
    v = proj_ref[:, D:2 * D]
    z = proj_ref[:, 2 * D:3 * D]
    mu = jnp.mean(v, axis=-1, keepdims=True)
    vc = v - mu
    rs = lax.rsqrt(jnp.mean(vc * vc, axis=-1, keepdims=True) + EPS)
    vh = vc * rs
    vnb = (vh * lng_ref[...] + lnb_ref[...]).astype(BF16)
    for n in range(tm // CHUNK):
        for h in range(HEADS):
            rows, cols = slice(n * CHUNK, (n + 1) * CHUNK), slice(h * GD, (h + 1) * GD)
            sv_ref[rows, cols] = _nn(ws_ref[h], vnb[rows, cols]) + bias_ref[h]
    return dict(u=u, z=z, vh=vh, rs=rs, vnb=vnb, sz=_sigmoid(z))


def _odd_fwd(x1, tgt, pre, post, win, lng, lnb, wsb, bias, wout):
    S = x1.shape[0]
    tm = TM
    nt = S // tm

    def body(x_ref, tgt_ref, pre_ref, post_ref, win_ref, lng_ref, lnb_ref, ws_ref, bias_ref, wout_ref,
             proj_ref, m_ref, hb_ref, yb_ref, dx2_ref, loss_ref, sv_ref):
        i = pl.program_id(0)

        @pl.when(i == 0)
        def _():
            loss_ref[...] = jnp.zeros((8, GD), F32)

        xv = x_ref[...]
        xh, _ = _rms_fwd(xv, None)
        hb = (xh * pre_ref[...]).astype(BF16)
        hb_ref[...] = hb
        for q in range(4):
            proj_ref[:, q * QW:(q + 1) * QW] = _nn(hb, win_ref[q])
        t = _odd_mix(proj_ref, lng_ref, lnb_ref, ws_ref, bias_ref, sv_ref)
        yb = (t["u"] * sv_ref[...] * (t["z"] * t["sz"])).astype(BF16)
        _store_permuted(yb_ref, yb)
        m = _nn(yb, wout_ref[...])
        m_ref[...] = m
        mh, _ = _rms_fwd(m, None)
        err = xv + mh * post_ref[...] - tgt_ref[...]
        dx2_ref[...] = err * (1.0 / D)
        part = 0.5 * jnp.sum(jnp.mean(err * err, axis=-1, keepdims=True), axis=0, keepdims=True)
        loss_ref[...] += jnp.broadcast_to(part, (8, GD))

    return pl.pallas_call(
        body, name="odd_fwd", grid=(nt,),
        in_specs=[_rows(tm, D), _rows(tm, D), _full((1, D)), _full((1, D)), _full((4, D, QW)), _full((1, D)), _full((1, D)),
                  _full((HEADS, CHUNK, CHUNK)), _full((HEADS, CHUNK, GD)), _full((D, D))],
        out_specs=[_rows(tm, W3), _rows(tm, D), _rows(tm, D), _rows(tm, D), _rows(tm, D), _full((8, GD))],
        out_shape=[jax.ShapeDtypeStruct((S, W3), F32), jax.ShapeDtypeStruct((S, D), F32), jax.ShapeDtypeStruct((S, D), BF16),
                   jax.ShapeDtypeStruct((S, D), BF16), jax.ShapeDtypeStruct((S, D), F32), jax.ShapeDtypeStruct((8, GD), F32)],
        scratch_shapes=[pltpu.VMEM((tm, D), F32)],
        compiler_params=_params(),
    )(x1, tgt, pre, post, win, lng, lnb, wsb, bias, wout)


def _store_rows(ref, row0, value):
    r, width = value.shape
    for a in range(r):
        for k in range(width // GD):
            ref[row0 + a * (width // GD) + k:row0 + a * (width // GD) + k + 1, :] = value[a:a + 1, k * GD:(k + 1) * GD]


def _odd_bwd(dx2, x1, proj, m, loss, pre, post, win, lng, lnb, wsb, wsbt, bias, wout):
    S = x1.shape[0]
    tm = TM
    nt = S // tm

    def body(dy_ref, x_ref, proj_ref, m_ref, loss_ref, pre_ref, post_ref, win_ref, lng_ref, lnb_ref, ws_ref, wst_ref, bias_ref,
             wout_ref, dx_ref, dproj_ref, dmb_ref, small_ref, sv_ref, dvn_ref, acc1024, dws_acc, dbs_acc):
        i = pl.program_id(0)

        @pl.when(i == 0)
        def _():
            acc1024[...] = jnp.zeros_like(acc1024)
            dws_acc[...] = jnp.zeros_like(dws_acc)
            dbs_acc[...] = jnp.zeros_like(dbs_acc)

        dy = dy_ref[...]
        mh, rm = _rms_fwd(m_ref[...], None)
        dm, dpost = _rms_bwd(dy, mh, rm, post_ref[...])
        dmb = dm.astype(BF16)
        dmb_ref[...] = dmb
        dyv = _nt(dmb, wout_ref[...])
        t = _odd_mix(proj_ref, lng_ref, lnb_ref, ws_ref, bias_ref, sv_ref)
        u, z, sz, sv = t["u"], t["z"], t["sz"], sv_ref[...]
        dproj_ref[:, 0:D] = (dyv * sv * (z * sz)).astype(BF16)
        dproj_ref[:, 2 * D:3 * D] = (dyv * u * sv * (sz * (1.0 + z * (1.0 - sz)))).astype(BF16)
        dsv = dyv * u * (z * sz)
        dsvb = dsv.astype(BF16)
        for n in range(tm // CHUNK):
            for h in range(HEADS):
                rows, cols = slice(n * CHUNK, (n + 1) * CHUNK), slice(h * GD, (h + 1) * GD)
                dvn_ref[rows, cols] = _nn(wst_ref[h], dsvb[rows, cols])
                dws_acc[h] += _nt(dsvb[rows, cols], t["vnb"][rows, cols])
                dbs_acc[h] += dsv[rows, cols]
        dvn = dvn_ref[...]
        vh = t["vh"]
        dvh = dvn * lng_ref[...]
        dv = t["rs"] * (dvh - jnp.mean(dvh, axis=-1, keepdims=True) - vh * jnp.mean(dvh * vh, axis=-1, keepdims=True))
        dproj_ref[:, D:2 * D] = dv.astype(BF16)
        dh = _nt(dproj_ref[:, 0:QW], win_ref[0])
        for q in range(1, 4):
            dh += _nt(dproj_ref[:, q * QW:(q + 1) * QW], win_ref[q])
        xh, r = _rms_fwd(x_ref[...], None)
        dxn, dpre = _rms_bwd(dh, xh, r, pre_ref[...])
        dx_ref[...] = dy + dxn
        acc1024[0:1, :] += dpre
        acc1024[1:2, :] += dpost
        acc1024[2:3, :] += jnp.sum(dvn * vh, axis=0, keepdims=True)
        acc1024[3:4, :] += jnp.sum(dvn, axis=0, keepdims=True)

        @pl.when(i == nt - 1)
        def _():
            small_ref[...] = jnp.zeros_like(small_ref)
            _store_rows(small_ref, S1_PRE, acc1024[0:1, :])
            _store_rows(small_ref, S1_POST, acc1024[1:2, :])
            for q in range(4):
                _store_rows(small_ref, S1_LN + 8 * q, acc1024[2:3, 2 * q * GD:(2 * q + 2) * GD])
                _store_rows(small_ref, S1_LN + 8 * q + 2, acc1024[3:4, 2 * q * GD:(2 * q + 2) * GD])
            lower = lax.broadcasted_iota(jnp.int32, (CHUNK, CHUNK), 0) >= lax.broadcasted_iota(jnp.int32, (CHUNK, CHUNK), 1)
            for h in range(HEADS):
                small_ref[S1_WS + h * CHUNK:S1_WS + (h + 1) * CHUNK, :] = jnp.where(lower, dws_acc[h], 0.0)
                small_ref[S1_BS + h:S1_BS + h + 1, :] = jnp.sum(dbs_acc[h].T, axis=0, keepdims=True)
            small_ref[S1_LOSS:S1_LOSS + 8, :] = loss_ref[...]

    return pl.pallas_call(
        body, name="odd_bwd", grid=(nt,),
        in_specs=[_rows(tm, D), _rows(tm, D), _rows(tm, W3), _rows(tm, D), _full((8, GD)), _full((1, D)), _full((1, D)),
                  _full((4, D, QW)), _full((1, D)), _full((1, D)), _full((HEADS, CHUNK, CHUNK)), _full((HEADS, CHUNK, CHUNK)),
                  _full((HEADS, CHUNK, GD)), _full((D, D))],
        out_specs=[_rows(tm, D), _rows(tm, W3), _rows(tm, D), _full((S1_ROWS, GD))],
        out_shape=[jax.ShapeDtypeStruct((S, D), F32), jax.ShapeDtypeStruct((S, W3), BF16), jax.ShapeDtypeStruct((S, D), BF16),
                   jax.ShapeDtypeStruct((S1_ROWS, GD), F32)],
        scratch_shapes=[pltpu.VMEM((tm, D), F32), pltpu.VMEM((tm, D), F32), pltpu.VMEM((8, D), F32),
                        pltpu.VMEM((HEADS, CHUNK, CHUNK), F32), pltpu.VMEM((HEADS, CHUNK, GD), F32)],
        compiler_params=_params(),
    )(dx2, x1, proj, m, loss, pre, post, win, lng, lnb, wsb, wsbt, bias, wout)


def _even_bwd(dx1, x, proj, m, pre, post, win, cw, pwb, ps, wout):
    S = x.shape[0]
    tm = TM
    nt = S // tm
    L = tm + HALO

    def rev(i):
        return (nt - 1 - i, 0)

    def halo_index(i):
        return (jnp.maximum((nt - 1 - i) * (tm // HALO) - 1, 0), 0)

    def body(dy_ref, x_ref, proj_ref, halo_ref, m_ref, pre_ref, post_ref, win_ref, cw_ref, pw_ref, ps_ref, wout_ref,
             dx_ref, dproj_ref, dmb_ref, small_ref, hc_ext, xp_ext, dconv_ext, q_ext, acc1024, acc512, dpw_acc):
        i = pl.program_id(0)
        tile = nt - 1 - i

        @pl.when(i == 0)
        def _():
            dconv_ext[tm:L, :] = jnp.zeros((HALO, AW), F32)
            q_ext[tm:L, :] = jnp.zeros((HALO, AW), F32)
            acc1024[...] = jnp.zeros_like(acc1024)
            acc512[...] = jnp.zeros_like(acc512)
            dpw_acc[...] = jnp.zeros_like(dpw_acc)

        keep = (tile > 0).astype(F32)
        hc_ext[0:HALO, :] = halo_ref[:, 2 * AW:3 * AW] * halo_ref[:, 0:AW] * keep
        xp_ext[0:HALO, :] = halo_ref[:, 4 * AW:5 * AW] * keep

        dy = dy_ref[...]
        mh, rm = _rms_fwd(m_ref[...], None)
        dm, dpost = _rms_bwd(dy, mh, rm, post_ref[...])
        dmb = dm.astype(BF16)
        dmb_ref[...] = dmb
        dmix = _nt(dmb, wout_ref[...])
        dya, dyb = dmix[:, 0:AW], dmix[:, AW:2 * AW]
        t = _even_mix(proj_ref, hc_ext, xp_ext, cw_ref, pw_ref, ps_ref, tile * tm)
        gb, conv, za, sa, hc = t["gb"], t["conv"], t["za"], t["sa"], t["hc"]
        silu_a = za * sa
        dproj_ref[:, AW:2 * AW] = (dya * conv * silu_a).astype(BF16)
        dproj_ref[:, 3 * AW:4 * AW] = (dya * gb * conv * (sa * (1.0 + za * (1.0 - sa)))).astype(BF16)
        dconv = dya * gb * silu_a
        dconv_ext[0:tm, :] = dconv
        e = dconv_ext[...]
        dc1 = pltpu.roll(e, L - 1, 0)[0:tm]
        dc2 = pltpu.roll(e, L - 2, 0)[0:tm]
        dhc = cw_ref[2:3, :] * dconv + cw_ref[1:2, :] * dc1 + cw_ref[0:1, :] * dc2
        dproj_ref[:, 0:AW] = (dhc * t["gc"]).astype(BF16)
        dproj_ref[:, 2 * AW:3 * AW] = (dhc * t["xa"]).astype(BF16)
        acc512[0:1, :] += jnp.sum(dc2 * hc, axis=0, keepdims=True)
        acc512[1:2, :] += jnp.sum(dc1 * hc, axis=0, keepdims=True)
        acc512[2:3, :] += jnp.sum(dconv * hc, axis=0, keepdims=True)

        zp, sb, mixed = t["zp"], t["sb"], t["mixed"]
        silu_b = zp * sb
        acc512[3:4, :] += jnp.sum(dyb * mixed * silu_b, axis=0, keepdims=True)
        dmixedb = (dyb * ps_ref[...] * silu_b).astype(BF16)
        dproj_ref[:, 5 * AW:6 * AW] = (dyb * mixed * ps_ref[...] * (sb * (1.0 + zp * (1.0 - sb)))).astype(BF16)
        for g in range(4):
            cols = slice(g * GD, (g + 1) * GD)
            dpw_acc[g] += _tn(t["pooled"][g], dmixedb[:, cols])
            dpooled = _nt(dmixedb[:, cols], pw_ref[g])
            q_ext[0:tm, cols] = dpooled / t["counts"][g]
            s = q_ext[:, cols]
            for k in range(g + 1):
                s = s + pltpu.roll(s, L - 2 ** k, 0)
            dproj_ref[:, 4 * AW + g * GD:4 * AW + (g + 1) * GD] = (s[0:tm] - dpooled).astype(BF16)
        dconv_ext[tm:L, :] = dconv_ext[0:HALO, :]
        q_ext[tm:L, :] = q_ext[0:HALO, :]

        dh = _nt(dproj_ref[:, 0:QW], win_ref[0])
        for q in range(1, 4):
            dh += _nt(dproj_ref[:, q * QW:(q + 1) * QW], win_ref[q])
        xh, r = _rms_fwd(x_ref[...], None)
        dxn, dpre = _rms_bwd(dh, xh, r, pre_ref[...])
        dx_ref[...] = dy + dxn
        acc1024[0:1, :] += dpre
        acc1024[1:2, :] += dpost

        @pl.when(i == nt - 1)
        def _():
            small_ref[...] = jnp.zeros_like(small_ref)
            _store_rows(small_ref, S0_PRE, acc1024[0:1, :])
            _store_rows(small_ref, S0_POST, acc1024[1:2, :])
            for q in range(4):
                for k in range(3):
                    small_ref[S0_CONV + 8 * q + k:S0_CONV + 8 * q + k + 1, :] = acc512[k:k + 1, q * GD:(q + 1) * GD]
            _store_rows(small_ref, S0_PS, acc512[3:4, :])
            for g in range(4):
                small_ref[S0_PW + g * GD:S0_PW + (g + 1) * GD, :] = dpw_acc[g]

    return pl.pallas_call(
        body, name="even_bwd", grid=(nt,),
        in_specs=[_rows(tm, D, rev), _rows(tm, D, rev), _rows(tm, W3, rev), pl.BlockSpec((HALO, W3), halo_index), _rows(tm, D, rev),
                  _full((1, D)), _full((1, D)), _full((4, D, QW)), _full((3, AW)), _full((4, GD, GD)), _full((1, AW)), _full((D, D))],
        out_specs=[_rows(tm, D, rev), _rows(tm, W3, rev), _rows(tm, D, rev), _full((S0_ROWS, GD))],
        out_shape=[jax.ShapeDtypeStruct((S, D), F32), jax.ShapeDtypeStruct((S, W3), BF16), jax.ShapeDtypeStruct((S, D), BF16),
                   jax.ShapeDtypeStruct((S0_ROWS, GD), F32)],
        scratch_shapes=[pltpu.VMEM((L, AW), F32), pltpu.VMEM((L, AW), F32), pltpu.VMEM((L, AW), F32), pltpu.VMEM((L, AW), F32),
                        pltpu.VMEM((8, D), F32), pltpu.VMEM((8, AW), F32), pltpu.VMEM((4, GD, GD), F32)],
        compiler_params=_params(),
    )(dx1, x, proj, proj, m, pre, post, win, cw, pwb, ps, wout)


def _owner_id(me, relation, c):
    q = jnp.bitwise_xor(me, relation)
    return (q // 2, q % 2, c)


def _small_gather_steps(small_ref, all_ref, stage, send_sems, recv_sems, local_sem):
    x, y, c = _position()
    chips = _chips(x, y)

    def slot(chip, core):
        return 4 * chip[0] + 2 * chip[1] + core

    def copy(k, src, block, to):
        return pltpu.make_async_remote_copy(src_ref=src, dst_ref=all_ref.at[block], send_sem=send_sems.at[k],
                                            recv_sem=recv_sems.at[k], device_id=to, device_id_type=MESH)

    def own_copy():
        return pltpu.make_async_copy(stage, all_ref.at[slot((x, y), c)], local_sem)

    def first_sends():
        mine = slot((x, y), c)
        return [copy(0, small_ref, mine, (x, y, 1 - c))] + [copy(1 + j, small_ref, mine, (*chip, c)) for j, chip in enumerate(chips)]

    def relays():
        return [copy(4 + j, all_ref.at[slot(chip, c)], slot(chip, c), (x, y, 1 - c)) for j, chip in enumerate(chips)]

    def start():
        for cp in first_sends():
            cp.start()
        load = pltpu.make_async_copy(small_ref, stage, local_sem)
        load.start()
        load.wait()
        own_copy().start()

    def relay():
        for j, chip in enumerate(chips):
            copy(1 + j, small_ref, slot(chip, c), (*chip, c)).wait_recv()
        for cp in relays():
            cp.start()

    def finish():
        copy(0, small_ref, slot((x, y), 1 - c), (x, y, 1 - c)).wait_recv()
        for j, chip in enumerate(chips):
            copy(4 + j, small_ref, slot(chip, 1 - c), (x, y, 1 - c)).wait_recv()
        for cp in first_sends() + relays():
            cp.wait_send()
        own_copy().wait()

    return start, relay, finish


def _wgrad_in_reduce(a, b, small, pos, name):
    S = a.shape[0]
    nk = S // TK
    hm = D // 2

    def b_index(j, k, pos_ref):
        return (k, jnp.bitwise_xor(pos_ref[1], 3 - j))

    def body(pos_ref, a_ref, b_ref, small_ref, own_ref, arr_ref, all_ref, acc, rbuf, sbuf, stage,
             d2d_send, d2d_recv, ici_send, ici_recv, g_send, g_recv, g_local):
        j, k = pl.program_id(0), pl.program_id(1)
        x, y, c = _position()
        me = 2 * x + y
        last = k == nk - 1
        gather_start, gather_relay, gather_finish = _small_gather_steps(small_ref, all_ref, stage, g_send, g_recv, g_local)

        def to_sibling(jj):
            src = acc.at[jj % 2, pl.ds(pl.multiple_of((1 - c) * hm, hm), hm), :]
            return pltpu.make_async_remote_copy(
                src_ref=src, dst_ref=rbuf.at[jj], send_sem=d2d_send.at[jj], recv_sem=d2d_recv.at[jj],
                device_id=(x, y, 1 - c), device_id_type=MESH)

        def to_owner(jj):
            return pltpu.make_async_remote_copy(
                src_ref=sbuf.at[jj], dst_ref=arr_ref.at[2 - jj], send_sem=ici_send.at[jj], recv_sem=ici_recv.at[jj],
                device_id=_owner_id(me, 3 - jj, c), device_id_type=MESH)

        def pair_sum(jj):
            to_sibling(jj).wait_recv()
            return acc[jj % 2, pl.ds(pl.multiple_of(c * hm, hm), hm), :] + rbuf[jj]

        def send_block(jj):
            sbuf[jj] = pair_sum(jj).astype(BF16)
            to_owner(jj).start()

        @pl.when((j == 0) & (k == 0))
        def _():
            gather_start()

        @pl.when((j == 2) & (k == 0))
        def _():
            gather_relay()

        @pl.when((k == 0) & (j >= 2))
        def _():
            to_sibling(j - 2).wait_send()

        @pl.when(k == 0)
        def _():
            acc[j % 2] = jnp.zeros((D, QW), F32)

        acc[j % 2] += _tn(a_ref[...], b_ref[...])

        @pl.when(last)
        def _():
            to_sibling(j).start()

        @pl.when(last & (j == 1))
        def _():
            send_block(0)

        @pl.when(last & (j == 2))
        def _():
            send_block(1)
            send_block(2)

        @pl.when(last & (j == 3))
        def _():
            own_ref[...] = pair_sum(3)
            to_sibling(2).wait_send()
            to_sibling(3).wait_send()
            for jj in range(3):
                to_owner(jj).wait()
            gather_finish()

    any_spec = pl.BlockSpec(memory_space=pl.ANY)
    grid_spec = pltpu.PrefetchScalarGridSpec(
        num_scalar_prefetch=1, grid=(4, nk),
        in_specs=[pl.BlockSpec((TK, D), lambda j, k, pos_ref: (k, 0)), pl.BlockSpec((TK, QW), b_index), any_spec],
        out_specs=[pl.BlockSpec((hm, QW), lambda j, k, pos_ref: (0, 0)), any_spec, any_spec],
        scratch_shapes=[pltpu.VMEM((2, D, QW), F32), pltpu.VMEM((4, hm, QW), F32), pltpu.VMEM((3, hm, QW), BF16),
                        pltpu.VMEM(small.shape, F32),
                        pltpu.SemaphoreType.DMA((4,)), pltpu.SemaphoreType.DMA((4,)),
                        pltpu.SemaphoreType.DMA((3,)), pltpu.SemaphoreType.DMA((3,)),
                        pltpu.SemaphoreType.DMA((7,)), pltpu.SemaphoreType.DMA((7,)), pltpu.SemaphoreType.DMA])
    return pl.pallas_call(
        body, name=name, grid_spec=grid_spec,
        out_shape=[jax.ShapeDtypeStruct((hm, QW), F32), jax.ShapeDtypeStruct((3, hm, QW), BF16),
                   jax.ShapeDtypeStruct((8, *small.shape), F32)],
        compiler_params=pltpu.CompilerParams(dimension_semantics=("arbitrary",) * 2, vmem_limit_bytes=VMEM_LIMIT),
    )(pos, a, b, small)


def _wgrad_out_reduce(a, b, name):
    S = a.shape[0]
    nk = S // TK
    hm = D // 2
    qr = hm // 4

    def body(a_ref, b_ref, own_ref, arr_ref, acc, rbuf, total, sbuf, d2d_send, d2d_recv, ici_send, ici_recv):
        k = pl.program_id(0)
        x, y, c = _position()
        me = 2 * x + y

        def to_owner(jj):
            return pltpu.make_async_remote_copy(
                src_ref=sbuf.at[jj], dst_ref=arr_ref.at[jj], send_sem=ici_send.at[jj], recv_sem=ici_recv.at[jj],
                device_id=_owner_id(me, jj + 1, c), device_id_type=MESH)

        @pl.when(k == 0)
        def _():
            acc[...] = jnp.zeros((D, D), F32)

        acc[...] += _tn(a_ref[...], b_ref[...])

        @pl.when(k == nk - 1)
        def _():
            to_sibling = pltpu.make_async_remote_copy(
                src_ref=acc.at[pl.ds(pl.multiple_of((1 - c) * hm, hm), hm), :], dst_ref=rbuf, send_sem=d2d_send, recv_sem=d2d_recv,
                device_id=(x, y, 1 - c), device_id_type=MESH)
            to_sibling.start()
            to_sibling.wait_recv()
            total[...] = acc[pl.ds(pl.multiple_of(c * hm, hm), hm), :] + rbuf[...]
            for jj in range(3):
                q = jnp.bitwise_xor(me, jj + 1)
                sbuf[jj] = total[pl.ds(pl.multiple_of(q * qr, qr), qr), :].astype(BF16)
                to_owner(jj).start()
            own_ref[...] = total[pl.ds(pl.multiple_of(me * qr, qr), qr), :]
            to_sibling.wait_send()
            for jj in range(3):
                to_owner(jj).wait()

    return pl.pallas_call(
        body, name=name, grid=(nk,),
        in_specs=[pl.BlockSpec((TK, D), lambda k: (k, 0)), pl.BlockSpec((TK, D), lambda k: (k, 0))],
        out_specs=[pl.BlockSpec((qr, D), lambda k: (0, 0)), pl.BlockSpec(memory_space=pl.ANY)],
        out_shape=[jax.ShapeDtypeStruct((qr, D), F32), jax.ShapeDtypeStruct((3, qr, D), BF16)],
        scratch_shapes=[pltpu.VMEM((D, D), F32), pltpu.VMEM((hm, D), F32), pltpu.VMEM((hm, D), F32), pltpu.VMEM((3, qr, D), BF16),
                        pltpu.SemaphoreType.DMA, pltpu.SemaphoreType.DMA,
                        pltpu.SemaphoreType.DMA((3,)), pltpu.SemaphoreType.DMA((3,))],
        compiler_params=_params(),
    )(a, b)


def _chips(x, y):
    return [(1 - x, y), (x, 1 - y), (1 - x, 1 - y)]


def _gather_weights(parts, split):
    n = len(parts)

    def body(*refs):
        ins, outs = refs[:n], refs[n:2 * n]
        send_sems, recv_sems, local_sems = refs[2 * n:]
        x, y, c = _position()
        me = 2 * x + y
        chips = _chips(x, y)
        local = [pltpu.make_async_copy(ins[a], outs[a].at[me], local_sems.at[a]) for a in range(n)]
        for cp in local:
            cp.start()

        def half(a, ref, who):
            rows = parts[a].shape[0] // 2
            return ref.at[pl.ds(pl.multiple_of(who * rows, 8), rows), :] if split[a] else ref

        sends = []
        for a in range(n):
            for j, chip in enumerate(chips):
                sends.append(pltpu.make_async_remote_copy(
                    src_ref=half(a, ins[a], c), dst_ref=half(a, outs[a].at[me], c),
                    send_sem=send_sems.at[a, j], recv_sem=recv_sems.at[a, j], device_id=(*chip, c), device_id_type=MESH))
        for cp in sends:
            cp.start()
        for j, chip in enumerate(chips):
            src = 2 * chip[0] + chip[1]
            for a in range(n):
                pltpu.make_async_remote_copy(
                    src_ref=half(a, ins[a], c), dst_ref=half(a, outs[a].at[src], c),
                    send_sem=send_sems.at[a, j], recv_sem=recv_sems.at[a, j], device_id=(*chip, c), device_id_type=MESH).wait_recv()
                if split[a]:
                    fwd = pltpu.make_async_remote_copy(
                        src_ref=half(a, outs[a].at[src], c), dst_ref=half(a, outs[a].at[src], c),
                        send_sem=send_sems.at[a, 3 + j], recv_sem=recv_sems.at[a, 3 + j], device_id=(x, y, 1 - c), device_id_type=MESH)
                    fwd.start()
                    sends.append(fwd)
        for j, chip in enumerate(chips):
            src = 2 * chip[0] + chip[1]
            for a in range(n):
                if split[a]:
                    pltpu.make_async_remote_copy(
                        src_ref=half(a, outs[a].at[src], 1 - c), dst_ref=half(a, outs[a].at[src], 1 - c),
                        send_sem=send_sems.at[a, 3 + j], recv_sem=recv_sems.at[a, 3 + j], device_id=(x, y, 1 - c),
                        device_id_type=MESH).wait_recv()
        for cp in sends:
            cp.wait_send()
        for cp in local:
            cp.wait()

    any_spec = pl.BlockSpec(memory_space=pl.ANY)
    return pl.pallas_call(
        body, name="gather_weights",
        in_specs=[pl.BlockSpec(memory_space=pltpu.VMEM)] * n, out_specs=[any_spec] * n,
        out_shape=[jax.ShapeDtypeStruct((4, *p.shape), p.dtype) for p in parts],
        scratch_shapes=[pltpu.SemaphoreType.DMA((n, 6)), pltpu.SemaphoreType.DMA((n, 6)), pltpu.SemaphoreType.DMA((n,))],
    )(*parts)


def _adamw(w, g, m, v):
    m = ADAM_B1 * m + (1.0 - ADAM_B1) * g
    v = ADAM_B2 * v + (1.0 - ADAM_B2) * (g * g)
    m_hat = m / (1.0 - ADAM_B1 ** ADAM_STEP)
    v_hat = v / (1.0 - ADAM_B2 ** ADAM_STEP)
    delta = -ADAM_LR * (m_hat / (jnp.sqrt(v_hat) + ADAM_EPS) + ADAM_WD * w)
    return delta, m, v


def _reduce_share(own, arrivals, pos, name):
    hr, cols = own.shape
    tr = 128
    nth = hr // tr

    def half_index(p, i, pos_ref):
        return (jnp.where(p == 0, pos_ref[0], 1 - pos_ref[0]) * nth + i, 0)

    def own_index(p, i, pos_ref):
        return (jnp.where(p == 0, i, nth - 1), 0)

    def arr_index(p, i, pos_ref):
        return (0, jnp.where(p == 0, i, nth - 1), 0)

    def body(pos_ref, own_ref, arr_ref, g_out, mine, theirs, send_sem, recv_sem):
        p, i = pl.program_id(0), pl.program_id(1)
        x, y, c = _position()

        def share(rows):
            return pltpu.make_async_remote_copy(
                src_ref=mine.at[rows, :], dst_ref=theirs.at[rows, :], send_sem=send_sem, recv_sem=recv_sem,
                device_id=(x, y, 1 - c), device_id_type=MESH)

        tile = pl.ds(pl.multiple_of(i * tr, tr), tr)

        @pl.when(p == 0)
        def _():
            g = own_ref[...]
            for j in range(3):
                g = g + arr_ref[j].astype(F32)
            mine[tile, :] = g
            share(tile).start()
            g_out[...] = g

        @pl.when((p == 1) & (i == 0))
        def _():
            share(pl.ds(0, hr)).wait()

        @pl.when(p == 1)
        def _():
            g_out[...] = theirs[tile, :]

    grid_spec = pltpu.PrefetchScalarGridSpec(
        num_scalar_prefetch=1, grid=(2, nth),
        in_specs=[pl.BlockSpec((tr, cols), own_index), pl.BlockSpec((3, tr, cols), arr_index)],
        out_specs=pl.BlockSpec((tr, cols), half_index),
        scratch_shapes=[pltpu.VMEM((hr, cols), F32), pltpu.VMEM((hr, cols), F32), pltpu.SemaphoreType.DMA, pltpu.SemaphoreType.DMA])
    return pl.pallas_call(
        body, name=name, grid_spec=grid_spec,
        out_shape=jax.ShapeDtypeStruct((2 * hr, cols), F32),
        compiler_params=pltpu.CompilerParams(dimension_semantics=("arbitrary",) * 2, vmem_limit_bytes=VMEM_LIMIT),
    )(pos, own, arrivals)


def _adamw_big(w, g, m, v, name):
    rows, cols = w.shape
    tr = 128

    def body(w_ref, g_ref, m_ref, v_ref, go_ref, d_ref, mo_ref, vo_ref):
        gv = g_ref[...]
        go_ref[...] = gv
        d_ref[...], mo_ref[...], vo_ref[...] = _adamw(w_ref[...], gv, m_ref[...], v_ref[...])

    spec = pl.BlockSpec((tr, cols), lambda i: (i, 0))
    return pl.pallas_call(
        body, name=name, grid=(rows // tr,),
        in_specs=[spec] * 4, out_specs=[spec] * 4,
        out_shape=[jax.ShapeDtypeStruct((rows, cols), F32)] * 4,
        compiler_params=pltpu.CompilerParams(dimension_semantics=("arbitrary",)),
    )(w, g, m, v)


def _adamw_small(g0, g1, weights, moms, vels):
    names = ["pre", "post", "conv", "pw", "ps", "lng", "lnb", "ws", "bs"]
    shapes = [w.shape for w in weights]

    def body(*refs):
        me = 2 * lax.axis_index("x") + lax.axis_index("y")
        g0_ref, g1_ref = refs[0], refs[1]
        w_refs, m_refs, v_refs = refs[2:11], refs[11:20], refs[20:29]
        outs = refs[29:29 + 36]
        loss_ref = refs[65]
        t0_ref, t1_ref = refs[66], refs[67]
        t0 = g0_ref[0]
        t1 = g1_ref[0]
        for d in range(1, 8):
            t0 = t0 + g0_ref[d]
            t1 = t1 + g1_ref[d]
        t0_ref[...] = t0
        t1_ref[...] = t1
        loss_ref[...] = t1_ref[S1_LOSS:S1_LOSS + 1, 0:1]
        my_conv = pl.multiple_of(S0_CONV + 8 * me, 8)
        my_ln = pl.multiple_of(S1_LN + 8 * me, 8)

        def update(idx, piece, grad):
            go, do, mo, vo = outs[4 * idx:4 * idx + 4]
            go[piece] = grad
            do[piece], mo[piece], vo[piece] = _adamw(w_refs[idx][piece], grad, m_refs[idx][piece], v_refs[idx][piece])

        for layer in range(2):
            for k in range(D // GD):
                lanes = slice(k * GD, (k + 1) * GD)
                tref, pre0, post0 = (t0_ref, S0_PRE, S0_POST) if layer == 0 else (t1_ref, S1_PRE, S1_POST)
                update(0, (slice(layer, layer + 1), lanes), tref[pre0 + k:pre0 + k + 1, :])
                update(1, (slice(layer, layer + 1), lanes), tref[post0 + k:post0 + k + 1, :])
        conv_rows = t0_ref[pl.ds(my_conv, 8), :]
        update(2, (slice(0, 3), slice(None)), conv_rows[0:3, :])
        for g in range(4):
            update(3, (g,), t0_ref[S0_PW + g * GD:S0_PW + (g + 1) * GD, :])
            update(4, (slice(0, 1), slice(g * GD, (g + 1) * GD)), t0_ref[S0_PS + g:S0_PS + g + 1, :])
        ln_rows = t1_ref[pl.ds(my_ln, 8), :]
        for k in range(2):
            update(5, (slice(0, 1), slice(k * GD, (k + 1) * GD)), ln_rows[k:k + 1, :])
            update(6, (slice(0, 1), slice(k * GD, (k + 1) * GD)), ln_rows[2 + k:3 + k, :])
        for h in range(HEADS):
            update(7, (h,), t1_ref[S1_WS + h * CHUNK:S1_WS + (h + 1) * CHUNK, :])
        update(8, (slice(None), slice(None)), t1_ref[S1_BS:S1_BS + HEADS, :])

    vm = pl.BlockSpec(memory_space=pltpu.VMEM)
    out_shape = []
    for s in shapes:
        out_shape += [jax.ShapeDtypeStruct(s, F32)] * 4
    out_shape.append(jax.ShapeDtypeStruct((1, 1), F32))
    res = pl.pallas_call(
        body, name="adamw_small",
        in_specs=[vm] * 29, out_specs=[vm] * 37, out_shape=out_shape,
        scratch_shapes=[pltpu.VMEM((S0_ROWS, GD), F32), pltpu.VMEM((S1_ROWS, GD), F32)],
        compiler_params=pltpu.CompilerParams(vmem_limit_bytes=VMEM_LIMIT),
    )(g0, g1, *weights, *moms, *vels)
    per_weight = {nm: res[4 * i:4 * i + 4] for i, nm in enumerate(names)}
    return per_weight, res[36]


def _pad8(a):
    return jnp.pad(a, ((0, 8 - a.shape[0]), (0, 0)))


def kernel(x, pre_norm, post_norm, even_w_in, even_conv_w, even_pool_w, even_pool_scale, even_w_out, odd_w_in, odd_ln_g, odd_ln_b, odd_w_s, odd_b_s, odd_w_out, loss_target, m_pre_norm, m_post_norm, m_even_w_in, m_even_conv_w, m_even_pool_w, m_even_pool_scale, m_even_w_out, m_odd_w_in, m_odd_ln_g, m_odd_ln_b, m_odd_w_s, m_odd_b_s, m_odd_w_out, v_pre_norm, v_post_norm, v_even_w_in, v_even_conv_w, v_even_pool_w, v_even_pool_scale, v_even_w_out, v_odd_w_in, v_odd_ln_g, v_odd_ln_b, v_odd_w_s, v_odd_b_s, v_odd_w_out):
    xs = x[0]
    tgt = loss_target[0]

    small_shard = jnp.concatenate([_pad8(even_conv_w[0]), _pad8(odd_ln_g.reshape(2, GD)), _pad8(odd_ln_b.reshape(2, GD))], axis=0)
    win0, wout0, shard = _gather_weights([even_w_in[0].astype(BF16), even_w_out[0].astype(BF16), small_shard], [True, True, False])
    wout0 = wout0.reshape(D, D)
    px, py, pc = _position()
    pos = jnp.stack([pc, 2 * px + py]).astype(jnp.int32)
    conv_w = shard[:, 0:3, :].transpose(1, 0, 2).reshape(3, AW)
    ln_g = shard[:, 8:10, :].reshape(1, D)
    ln_b = shard[:, 16:18, :].reshape(1, D)
    pool_wb = even_pool_w[0].astype(BF16)
    ws_tril = jnp.tril(odd_w_s[0]).astype(BF16)
    ws_tril_t = jnp.swapaxes(ws_tril, 1, 2)
    bias = jnp.broadcast_to(odd_b_s[0][:, :, None], (HEADS, CHUNK, GD))
    pre0, pre1 = pre_norm[0:1], pre_norm[1:2]
    post0, post1 = post_norm[0:1], post_norm[1:2]

    x1, proj0, m0, hb0, mixp0, win1, wout1 = _even_fwd(
        xs, pre0, post0, win0, conv_w, pool_wb, even_pool_scale, wout0, [odd_w_in[0].astype(BF16), odd_w_out[0].astype(BF16)])
    wout1 = wout1.reshape(D, D)
    proj1, m1, hb1, yp1, dx2, loss_part = _odd_fwd(x1, tgt, pre1, post1, win1, ln_g, ln_b, ws_tril, bias, wout1)
    dx1, dproj1, dmb1, small1 = _odd_bwd(dx2, x1, proj1, m1, loss_part, pre1, post1, win1, ln_g, ln_b, ws_tril, ws_tril_t, bias, wout1)
    red_out1 = _wgrad_out_reduce(yp1, dmb1, "wgrad_odd_out")
    own_in1, arr_in1, all1 = _wgrad_in_reduce(hb1, dproj1, small1, pos, "wgrad_odd_in")
    gx, dproj0, dmb0, small0 = _even_bwd(dx1, xs, proj0, m0, pre0, post0, win0, conv_w, pool_wb, even_pool_scale, wout0)
    red_out0 = _wgrad_out_reduce(mixp0, dmb0, "wgrad_even_out")
    own_in0, arr_in0, all0 = _wgrad_in_reduce(hb0, dproj0, small0, pos, "wgrad_even_in")

    reduced = [(own_in0, arr_in0), red_out0, (own_in1, arr_in1), red_out1]
    tags = ["even_in", "even_out", "odd_in", "odd_out"]
    big_w = [even_w_in[0], even_w_out[0], odd_w_in[0], odd_w_out[0]]
    big_m = [m_even_w_in[0], m_even_w_out[0], m_odd_w_in[0], m_odd_w_out[0]]
    big_v = [v_even_w_in[0], v_even_w_out[0], v_odd_w_in[0], v_odd_w_out[0]]
    big = [_adamw_big(w, _reduce_share(r[0], r[1], pos, "reduce_share_" + t), m, v, "adamw_" + t)
           for r, w, m, v, t in zip(reduced, big_w, big_m, big_v, tags)]

    small_w = [pre_norm, post_norm, even_conv_w[0], even_pool_w[0], even_pool_scale, odd_ln_g, odd_ln_b, odd_w_s[0], odd_b_s[0]]
    small_m = [m_pre_norm, m_post_norm, m_even_conv_w[0], m_even_pool_w[0], m_even_pool_scale, m_odd_ln_g, m_odd_ln_b, m_odd_w_s[0], m_odd_b_s[0]]
    small_v = [v_pre_norm, v_post_norm, v_even_conv_w[0], v_even_pool_w[0], v_even_pool_scale, v_odd_ln_g, v_odd_ln_b, v_odd_w_s[0], v_odd_b_s[0]]
    sm, loss = _adamw_small(all0, all1, small_w, small_m, small_v)

    def lead(a):
        return a[None]

    per = {
        "pre_norm": sm["pre"], "post_norm": sm["post"],
        "even_w_in": [lead(a) for a in big[0]], "even_conv_w": [lead(a) for a in sm["conv"]],
        "even_pool_w": [lead(a) for a in sm["pw"]], "even_pool_scale": sm["ps"],
        "even_w_out": [lead(a) for a in big[1]], "odd_w_in": [lead(a) for a in big[2]],
        "odd_ln_g": sm["lng"], "odd_ln_b": sm["lnb"],
        "odd_w_s": [lead(a) for a in sm["ws"]], "odd_b_s": [lead(a) for a in sm["bs"]],
        "odd_w_out": [lead(a) for a in big[3]],
    }
    order = ["pre_norm", "post_norm", "even_w_in", "even_conv_w", "even_pool_w", "even_pool_scale", "even_w_out", "odd_w_in",
             "odd_ln_g", "odd_ln_b", "odd_w_s", "odd_b_s", "odd_w_out"]
    outs = [loss.reshape(()), gx[None]]
    for kind in range(4):
        outs += [per[nm][kind] for nm in order]
    return tuple(outs)
```

```python
import functools

import jax
import jax.numpy as jnp
from jax import lax
from jax.experimental import pallas as pl
from jax.experimental.pallas import tpu as pltpu

F32 = jnp.float32
BF16 = jnp.bfloat16
MESH = pl.DeviceIdType.MESH

D = 1024
W3 = 3 * D
QW = W3 // 4
AW = 512
GD = 128
CHUNK = 128
HEADS = 8
HALO = 16
POOL_WINDOWS = (2, 4, 8, 16)
EPS = 1e-6
TM = 256
TK = 1024
VMEM_LIMIT = 56 * 1024 * 1024

ADAM_LR, ADAM_B1, ADAM_B2, ADAM_EPS, ADAM_WD, ADAM_STEP = 0.001, 0.9, 0.999, 1e-08, 0.01, 10

S0_PRE, S0_POST, S0_CONV, S0_PS, S0_PW, S0_ROWS = 0, 8, 16, 48, 56, 568
S1_PRE, S1_POST, S1_LN, S1_BS, S1_WS, S1_LOSS, S1_ROWS = 0, 8, 16, 48, 56, 1080, 1088


def _nn(a, b):
    return jnp.dot(a, b, preferred_element_type=F32)


def _nt(a, b):
    return lax.dot_general(a, b, (((1,), (1,)), ((), ())), preferred_element_type=F32)


def _tn(a, b):
    return lax.dot_general(a, b, (((0,), (0,)), ((), ())), preferred_element_type=F32)


def _sigmoid(z):
    return 1.0 / (1.0 + jnp.exp(-z))


def _rms_fwd(x, g):
    r = lax.rsqrt(jnp.mean(x * x, axis=-1, keepdims=True) + EPS)
    return x * r, r


def _rms_bwd(dy, xh, r, g):
    dn = dy * g
    dx = r * (dn - xh * jnp.mean(xh * dn, axis=-1, keepdims=True))
    return dx, jnp.sum(dy * xh, axis=0, keepdims=True)


def _full(shape):
    nd = len(shape)
    return pl.BlockSpec(shape, lambda i, _n=nd: (0,) * _n)


def _rows(tm, width, index=None):
    return pl.BlockSpec((tm, width), (lambda i: (i, 0)) if index is None else index)


def _params():
    return pltpu.CompilerParams(dimension_semantics=("arbitrary",), vmem_limit_bytes=VMEM_LIMIT)


def _position():
    x, y, c = lax.axis_index("x"), lax.axis_index("y"), lax.axis_index("c")
    return x, y, c


def _even_mix(proj_ref, hc_ext, xp_ext, cw_ref, pw_ref, ps_ref, first_row):
    tm = proj_ref.shape[0]
    xa = proj_ref[:, 0:AW]
    gb = proj_ref[:, AW:2 * AW]
    gc = proj_ref[:, 2 * AW:3 * AW]
    za = proj_ref[:, 3 * AW:4 * AW]
    xp = proj_ref[:, 4 * AW:5 * AW]
    zp = proj_ref[:, 5 * AW:6 * AW]
    hc = gc * xa
    hc_ext[HALO:, :] = hc
    e = hc_ext[...]
    conv = cw_ref[2:3, :] * hc + cw_ref[1:2, :] * pltpu.roll(e, 1, 0)[HALO:] + cw_ref[0:1, :] * pltpu.roll(e, 2, 0)[HALO:]
    sa = _sigmoid(za)
    xp_ext[HALO:, :] = xp
    pos = first_row + lax.broadcasted_iota(jnp.int32, (tm, 1), 0)
    pooled, mixed, counts = [], [], []
    for g, w in enumerate(POOL_WINDOWS):
        cols = slice(g * GD, (g + 1) * GD)
        s = xp_ext[:, cols]
        for k in range(g + 1):
            s = s + pltpu.roll(s, 2 ** k, 0)
        count = jnp.minimum(pos + 1, w).astype(F32)
        pg = s[HALO:] / count - xp[:, cols]
        pooled.append(pg.astype(BF16))
        mixed.append(_nn(pooled[-1], pw_ref[g]))
        counts.append(count)
    mixed = jnp.concatenate(mixed, axis=-1)
    sb = _sigmoid(zp)
    return dict(xa=xa, gb=gb, gc=gc, za=za, zp=zp, hc=hc, conv=conv, sa=sa, sb=sb, pooled=pooled, mixed=mixed, counts=counts)


def _half_rows(ref, rows, who):
    return ref.at[pl.ds(pl.multiple_of(who * (rows // 2), 8), rows // 2), :]


def _store_permuted(ref, value):
    for ob in range(D // GD):
        nb = 4 * (ob % 2) + ob // 2
        ref[:, nb * GD:(nb + 1) * GD] = value[:, ob * GD:(ob + 1) * GD]


def _even_fwd(x, pre, post, win, cw, pwb, ps, wout, next_shards):
    S = x.shape[0]
    tm = 2 * TM
    nt = S // tm
    relay = (3 * nt) // 4
    n = len(next_shards)
    shard_rows = [p.shape[0] for p in next_shards]

    def body(x_ref, pre_ref, post_ref, win_ref, cw_ref, pw_ref, ps_ref, wout_ref, *rest):
        shard_refs, rest = rest[:n], rest[n:]
        x1_ref, proj_ref, m_ref, hb_ref, mixp_ref = rest[:5]
        full_refs, rest = rest[5:5 + n], rest[5 + n:]
        hc_ext, xp_ext, mix_sc = rest[:3]
        stage, rest = rest[3:3 + n], rest[3 + n:]
        send_sems, recv_sems, local_sems = rest
        i = pl.program_id(0)
        px, py, pc = _position()
        me = 2 * px + py
        chips = _chips(px, py)

        def ici(a, j):
            return pltpu.make_async_remote_copy(
                src_ref=_half_rows(shard_refs[a], shard_rows[a], pc), dst_ref=_half_rows(full_refs[a].at[me], shard_rows[a], pc),
                send_sem=send_sems.at[a, j], recv_sem=recv_sems.at[a, j], device_id=(*chips[j], pc), device_id_type=MESH)

        def ici_arrival(a, j):
            src = 2 * chips[j][0] + chips[j][1]
            return pltpu.make_async_remote_copy(
                src_ref=_half_rows(shard_refs[a], shard_rows[a], pc), dst_ref=_half_rows(full_refs[a].at[src], shard_rows[a], pc),
                send_sem=send_sems.at[a, j], recv_sem=recv_sems.at[a, j], device_id=(*chips[j], pc), device_id_type=MESH)

        def relay_copy(a, j, who):
            src = 2 * chips[j][0] + chips[j][1]
            region = _half_rows(full_refs[a].at[src], shard_rows[a], who)
            return pltpu.make_async_remote_copy(
                src_ref=region, dst_ref=region, send_sem=send_sems.at[a, 3 + j], recv_sem=recv_sems.at[a, 3 + j],
                device_id=(px, py, 1 - pc), device_id_type=MESH)

        def own_copy(a):
            return pltpu.make_async_copy(stage[a], full_refs[a].at[me], local_sems.at[a])

        @pl.when(i == 0)
        def _():
            hc_ext[0:HALO, :] = jnp.zeros((HALO, AW), F32)
            xp_ext[0:HALO, :] = jnp.zeros((HALO, AW), F32)
            for a in range(n):
                for j in range(3):
                    ici(a, j).start()
            for a in range(n):
                load = pltpu.make_async_copy(shard_refs[a], stage[a], local_sems.at[a])
                load.start()
                load.wait()
                own_copy(a).start()

        @pl.when(i == relay)
        def _():
            for j in range(3):
                for a in range(n):
                    ici_arrival(a, j).wait_recv()
                    relay_copy(a, j, pc).start()

        xv = x_ref[...]
        xh, _ = _rms_fwd(xv, None)
        hb = (xh * pre_ref[...]).astype(BF16)
        hb_ref[...] = hb
        for q in range(4):
            proj_ref[:, q * QW:(q + 1) * QW] = _nn(hb, win_ref[q])
        t = _even_mix(proj_ref, hc_ext, xp_ext, cw_ref, pw_ref, ps_ref, i * tm)
        mix_sc[:, 0:AW] = (t["gb"] * t["conv"] * (t["za"] * t["sa"])).astype(BF16)
        mix_sc[:, AW:2 * AW] = (t["mixed"] * ps_ref[...] * (t["zp"] * t["sb"])).astype(BF16)
        mix = mix_sc[...]
        _store_permuted(mixp_ref, mix)
        m = _nn(mix, wout_ref[...])
        m_ref[...] = m
        mh, _ = _rms_fwd(m, None)
        x1_ref[...] = xv + mh * post_ref[...]
        hc_ext[0:HALO, :] = hc_ext[tm:tm + HALO, :]
        xp_ext[0:HALO, :] = xp_ext[tm:tm + HALO, :]

        @pl.when(i == nt - 1)
        def _():
            for j in range(3):
                for a in range(n):
                    relay_copy(a, j, 1 - pc).wait_recv()
            for a in range(n):
                for j in range(3):
                    ici(a, j).wait_send()
                    relay_copy(a, j, pc).wait_send()
                own_copy(a).wait()

    any_spec = pl.BlockSpec(memory_space=pl.ANY)
    return pl.pallas_call(
        body, name="even_fwd", grid=(nt,),
        in_specs=[_rows(tm, D), _full((1, D)), _full((1, D)), _full((4, D, QW)), _full((3, AW)), _full((4, GD, GD)),
                  _full((1, AW)), _full((D, D))] + [any_spec] * n,
        out_specs=[_rows(tm, D), _rows(tm, W3), _rows(tm, D), _rows(tm, D), _rows(tm, D)] + [any_spec] * n,
        out_shape=[jax.ShapeDtypeStruct((S, D), F32), jax.ShapeDtypeStruct((S, W3), F32), jax.ShapeDtypeStruct((S, D), F32),
                   jax.ShapeDtypeStruct((S, D), BF16), jax.ShapeDtypeStruct((S, D), BF16)]
        + [jax.ShapeDtypeStruct((4, *p.shape), p.dtype) for p in next_shards],
        scratch_shapes=[pltpu.VMEM((tm + HALO, AW), F32), pltpu.VMEM((tm + HALO, AW), F32), pltpu.VMEM((tm, D), BF16)]
        + [pltpu.VMEM(p.shape, p.dtype) for p in next_shards]
        + [pltpu.SemaphoreType.DMA((n, 6)), pltpu.SemaphoreType.DMA((n, 6)), pltpu.SemaphoreType.DMA((n,))],
        compiler_params=_params(),
    )(x, pre, post, win, cw, pwb, ps, wout, *next_shards)


def _chunks_side_by_side(a, h):
    return jnp.concatenate([a[n * CHUNK:(n + 1) * CHUNK, h * GD:(h + 1) * GD] for n in range(a.shape[0] // CHUNK)], axis=1)


def _odd_mix(proj_ref, lng_ref, lnb_ref, ws_ref, bias_ref, sv_ref):
    tm = proj_ref.shape[0]
    u = proj_ref[:, 0:D]
    v = proj_ref[:, D:2 * D]
    z = proj_ref[:, 2 * D:3 * D]
    mu = jnp.mean(v, axis=-1, keepdims=True)
    vc = v - mu
    rs = lax.rsqrt(jnp.mean(vc * vc, axis=-1, keepdims=True) + EPS)
    vh = vc * rs
    vnb = (vh * lng_ref[...] + lnb_ref[...]).astype(BF16)
    for h in range(HEADS):
        sv = _nn(ws_ref[h], _chunks_side_by_side(vnb, h))
        for n in range(tm // CHUNK):
            sv_ref[n * CHUNK:(n + 1) * CHUNK, h * GD:(h + 1) * GD] = sv[:, n * GD:(n + 1) * GD] + bias_ref[h]
    return dict(u=u, z=z, vh=vh, rs=rs, vnb=vnb, sz=_sigmoid(z))


def _odd_fwd(x1, tgt, pre, post, win, lng, lnb, wsb, bias, wout):
    S = x1.shape[0]
    tm = 2 * TM
    nt = S // tm

    def body(x_ref, tgt_ref, pre_ref, post_ref, win_ref, lng_ref, lnb_ref, ws_ref, bias_ref, wout_ref,
             proj_ref, m_ref, hb_ref, yb_ref, dx2_ref, loss_ref, sv_ref):
        i = pl.program_id(0)

        @pl.when(i == 0)
        def _():
            loss_ref[...] = jnp.zeros((8, GD), F32)

        xv = x_ref[...]
        xh, _ = _rms_fwd(xv, None)
        hb = (xh * pre_ref[...]).astype(BF16)
        hb_ref[...] = hb
        for q in range(4):
            proj_ref[:, q * QW:(q + 1) * QW] = _nn(hb, win_ref[q])
        t = _odd_mix(proj_ref, lng_ref, lnb_ref, ws_ref, bias_ref, sv_ref)
        yb = (t["u"] * sv_ref[...] * (t["z"] * t["sz"])).astype(BF16)
        _store_permuted(yb_ref, yb)
        m = _nn(yb, wout_ref[...])
        m_ref[...] = m
        mh, _ = _rms_fwd(m, None)
        err = xv + mh * post_ref[...] - tgt_ref[...]
        dx2_ref[...] = err * (1.0 / D)
        part = 0.5 * jnp.sum(jnp.mean(err * err, axis=-1, keepdims=True), axis=0, keepdims=True)
        loss_ref[...] += jnp.broadcast_to(part, (8, GD))

    return pl.pallas_call(
        body, name="odd_fwd", grid=(nt,),
        in_specs=[_rows(tm, D), _rows(tm, D), _full((1, D)), _full((1, D)), _full((4, D, QW)), _full((1, D)), _full((1, D)),
                  _full((HEADS, CHUNK, CHUNK)), _full((HEADS, CHUNK, GD)), _full((D, D))],
        out_specs=[_rows(tm, W3), _rows(tm, D), _rows(tm, D), _rows(tm, D), _rows(tm, D), _full((8, GD))],
        out_shape=[jax.ShapeDtypeStruct((S, W3), F32), jax.ShapeDtypeStruct((S, D), F32), jax.ShapeDtypeStruct((S, D), BF16),
                   jax.ShapeDtypeStruct((S, D), BF16), jax.ShapeDtypeStruct((S, D), F32), jax.ShapeDtypeStruct((8, GD), F32)],
        scratch_shapes=[pltpu.VMEM((tm, D), F32)],
        compiler_params=_params(),
    )(x1, tgt, pre, post, win, lng, lnb, wsb, bias, wout)


def _store_rows(ref, row0, value):
    r, width = value.shape
    for a in range(r):
        for k in range(width // GD):
            ref[row0 + a * (width // GD) + k:row0 + a * (width // GD) + k + 1, :] = value[a:a + 1, k * GD:(k + 1) * GD]


def _proj_bwd(dproj, win_ref, x, dy, pre):
    dh = _nt(dproj[:, 0:QW], win_ref[0])
    for q in range(1, 4):
        dh += _nt(dproj[:, q * QW:(q + 1) * QW], win_ref[q])
    xh, r = _rms_fwd(x, None)
    dxn, dpre = _rms_bwd(dh, xh, r, pre)
    return dy + dxn, dpre


def _odd_bwd(dx2, x1, proj, m, loss, pre, post, win, lng, lnb, wsb, wsbt, bias, wout):
    S = x1.shape[0]
    tm = TM
    nt = S // tm

    def body(dy_ref, x_ref, proj_ref, m_ref, loss_ref, pre_ref, post_ref, win_ref, lng_ref, lnb_ref, ws_ref, wst_ref, bias_ref,
             wout_ref, dx_ref, dproj_ref, dmb_ref, small_ref, sv_ref, dvn_ref, acc1024, dws_acc, dbs_acc):
        i = pl.program_id(0)

        @pl.when(i == 0)
        def _():
            acc1024[...] = jnp.zeros_like(acc1024)
            dws_acc[...] = jnp.zeros_like(dws_acc)
            dbs_acc[...] = jnp.zeros_like(dbs_acc)

        dy = dy_ref[...]
        mh, rm = _rms_fwd(m_ref[...], None)
        dm, dpost = _rms_bwd(dy, mh, rm, post_ref[...])
        dmb = dm.astype(BF16)
        dmb_ref[...] = dmb
        dyv = _nt(dmb, wout_ref[...])
        t = _odd_mix(proj_ref, lng_ref, lnb_ref, ws_ref, bias_ref, sv_ref)
        u, z, sz, sv = t["u"], t["z"], t["sz"], sv_ref[...]
        dproj_ref[:, 0:D] = (dyv * sv * (z * sz)).astype(BF16)
        dproj_ref[:, 2 * D:3 * D] = (dyv * u * sv * (sz * (1.0 + z * (1.0 - sz)))).astype(BF16)
        dsv = dyv * u * (z * sz)
        dsvb = dsv.astype(BF16)
        for h in range(HEADS):
            dsv_h = _chunks_side_by_side(dsvb, h)
            dvn_h = _nn(wst_ref[h], dsv_h)
            dws_acc[h] += _nt(dsv_h, _chunks_side_by_side(t["vnb"], h))
            for n in range(tm // CHUNK):
                rows, cols = slice(n * CHUNK, (n + 1) * CHUNK), slice(h * GD, (h + 1) * GD)
                dvn_ref[rows, cols] = dvn_h[:, n * GD:(n + 1) * GD]
                dbs_acc[h] += dsv[rows, cols]
        dvn = dvn_ref[...]
        vh = t["vh"]
        dvh = dvn * lng_ref[...]
        dv = t["rs"] * (dvh - jnp.mean(dvh, axis=-1, keepdims=True) - vh * jnp.mean(dvh * vh, axis=-1, keepdims=True))
        dproj_ref[:, D:2 * D] = dv.astype(BF16)
        dx_ref[...], dpre = _proj_bwd(dproj_ref[...], win_ref, x_ref[...], dy, pre_ref[...])
        acc1024[0:1, :] += dpre
        acc1024[1:2, :] += dpost
        acc1024[2:3, :] += jnp.sum(dvn * vh, axis=0, keepdims=True)
        acc1024[3:4, :] += jnp.sum(dvn, axis=0, keepdims=True)

        @pl.when(i == nt - 1)
        def _():
            small_ref[...] = jnp.zeros_like(small_ref)
            _store_rows(small_ref, S1_PRE, acc1024[0:1, :])
            _store_rows(small_ref, S1_POST, acc1024[1:2, :])
            for q in range(4):
                _store_rows(small_ref, S1_LN + 8 * q, acc1024[2:3, 2 * q * GD:(2 * q + 2) * GD])
                _store_rows(small_ref, S1_LN + 8 * q + 2, acc1024[3:4, 2 * q * GD:(2 * q + 2) * GD])
            lower = lax.broadcasted_iota(jnp.int32, (CHUNK, CHUNK), 0) >= lax.broadcasted_iota(jnp.int32, (CHUNK, CHUNK), 1)
            for h in range(HEADS):
                small_ref[S1_WS + h * CHUNK:S1_WS + (h + 1) * CHUNK, :] = jnp.where(lower, dws_acc[h], 0.0)
                small_ref[S1_BS + h:S1_BS + h + 1, :] = jnp.sum(dbs_acc[h].T, axis=0, keepdims=True)
            small_ref[S1_LOSS:S1_LOSS + 8, :] = loss_ref[...]

    return pl.pallas_call(
        body, name="odd_bwd", grid=(nt,),
        in_specs=[_rows(tm, D), _rows(tm, D), _rows(tm, W3), _rows(tm, D), _full((8, GD)), _full((1, D)), _full((1, D)),
                  _full((4, D, QW)), _full((1, D)), _full((1, D)), _full((HEADS, CHUNK, CHUNK)), _full((HEADS, CHUNK, CHUNK)),
                  _full((HEADS, CHUNK, GD)), _full((D, D))],
        out_specs=[_rows(tm, D), _rows(tm, W3), _rows(tm, D), _full((S1_ROWS, GD))],
        out_shape=[jax.ShapeDtypeStruct((S, D), F32), jax.ShapeDtypeStruct((S, W3), BF16), jax.ShapeDtypeStruct((S, D), BF16),
                   jax.ShapeDtypeStruct((S1_ROWS, GD), F32)],
        scratch_shapes=[pltpu.VMEM((tm, D), F32), pltpu.VMEM((tm, D), F32), pltpu.VMEM((8, D), F32),
                        pltpu.VMEM((HEADS, CHUNK, CHUNK), F32), pltpu.VMEM((HEADS, CHUNK, GD), F32)],
        compiler_params=_params(),
    )(dx2, x1, proj, m, loss, pre, post, win, lng, lnb, wsb, wsbt, bias, wout)


def _even_bwd(dx1, x, proj, m, pre, post, win, cw, pwb, ps, wout):
    S = x.shape[0]
    tm = TM
    nt = S // tm
    L = tm + HALO

    def rev(i):
        return (nt - 1 - i, 0)

    def halo_index(i):
        return (jnp.maximum((nt - 1 - i) * (tm // HALO) - 1, 0), 0)

    def body(dy_ref, x_ref, proj_ref, halo_ref, m_ref, pre_ref, post_ref, win_ref, cw_ref, pw_ref, ps_ref, wout_ref,
             dx_ref, dproj_ref, dmb_ref, small_ref, hc_ext, xp_ext, dconv_ext, q_ext, acc1024, acc512, dpw_acc):
        i = pl.program_id(0)
        tile = nt - 1 - i

        @pl.when(i == 0)
        def _():
            dconv_ext[tm:L, :] = jnp.zeros((HALO, AW), F32)
            q_ext[tm:L, :] = jnp.zeros((HALO, AW), F32)
            acc1024[...] = jnp.zeros_like(acc1024)
            acc512[...] = jnp.zeros_like(acc512)
            dpw_acc[...] = jnp.zeros_like(dpw_acc)

        keep = (tile > 0).astype(F32)
        hc_ext[0:HALO, :] = halo_ref[:, 2 * AW:3 * AW] * halo_ref[:, 0:AW] * keep
        xp_ext[0:HALO, :] = halo_ref[:, 4 * AW:5 * AW] * keep

        dy = dy_ref[...]
        mh, rm = _rms_fwd(m_ref[...], None)
        dm, dpost = _rms_bwd(dy, mh, rm, post_ref[...])
        dmb = dm.astype(BF16)
        dmb_ref[...] = dmb
        dmix = _nt(dmb, wout_ref[...])
        dya, dyb = dmix[:, 0:AW], dmix[:, AW:2 * AW]
        t = _even_mix(proj_ref, hc_ext, xp_ext, cw_ref, pw_ref, ps_ref, tile * tm)
        gb, conv, za, sa, hc = t["gb"], t["conv"], t["za"], t["sa"], t["hc"]
        silu_a = za * sa
        dproj_ref[:, AW:2 * AW] = (dya * conv * silu_a).astype(BF16)
        dproj_ref[:, 3 * AW:4 * AW] = (dya * gb * conv * (sa * (1.0 + za * (1.0 - sa)))).astype(BF16)
        dconv = dya * gb * silu_a
        dconv_ext[0:tm, :] = dconv
        e = dconv_ext[...]
        dc1 = pltpu.roll(e, L - 1, 0)[0:tm]
        dc2 = pltpu.roll(e, L - 2, 0)[0:tm]
        dhc = cw_ref[2:3, :] * dconv + cw_ref[1:2, :] * dc1 + cw_ref[0:1, :] * dc2
        dproj_ref[:, 0:AW] = (dhc * t["gc"]).astype(BF16)
        dproj_ref[:, 2 * AW:3 * AW] = (dhc * t["xa"]).astype(BF16)
        acc512[0:1, :] += jnp.sum(dc2 * hc, axis=0, keepdims=True)
        acc512[1:2, :] += jnp.sum(dc1 * hc, axis=0, keepdims=True)
        acc512[2:3, :] += jnp.sum(dconv * hc, axis=0, keepdims=True)

        zp, sb, mixed = t["zp"], t["sb"], t["mixed"]
        silu_b = zp * sb
        acc512[3:4, :] += jnp.sum(dyb * mixed * silu_b, axis=0, keepdims=True)
        dmixedb = (dyb * ps_ref[...] * silu_b).astype(BF16)
        dproj_ref[:, 5 * AW:6 * AW] = (dyb * mixed * ps_ref[...] * (sb * (1.0 + zp * (1.0 - sb)))).astype(BF16)
        for g in range(4):
            cols = slice(g * GD, (g + 1) * GD)
            dpw_acc[g] += _tn(t["pooled"][g], dmixedb[:, cols])
            dpooled = _nt(dmixedb[:, cols], pw_ref[g])
            q_ext[0:tm, cols] = dpooled / t["counts"][g]
            s = q_ext[:, cols]
            for k in range(g + 1):
                s = s + pltpu.roll(s, L - 2 ** k, 0)
            dproj_ref[:, 4 * AW + g * GD:4 * AW + (g + 1) * GD] = (s[0:tm] - dpooled).astype(BF16)
        dconv_ext[tm:L, :] = dconv_ext[0:HALO, :]
        q_ext[tm:L, :] = q_ext[0:HALO, :]

        dx_ref[...], dpre = _proj_bwd(dproj_ref[...], win_ref, x_ref[...], dy, pre_ref[...])
        acc1024[0:1, :] += dpre
        acc1024[1:2, :] += dpost

        @pl.when(i == nt - 1)
        def _():
            small_ref[...] = jnp.zeros_like(small_ref)
            _store_rows(small_ref, S0_PRE, acc1024[0:1, :])
            _store_rows(small_ref, S0_POST, acc1024[1:2, :])
            for q in range(4):
                for k in range(3):
                    small_ref[S0_CONV + 8 * q + k:S0_CONV + 8 * q + k + 1, :] = acc512[k:k + 1, q * GD:(q + 1) * GD]
            _store_rows(small_ref, S0_PS, acc512[3:4, :])
            for g in range(4):
                small_ref[S0_PW + g * GD:S0_PW + (g + 1) * GD, :] = dpw_acc[g]

    return pl.pallas_call(
        body, name="even_bwd", grid=(nt,),
        in_specs=[_rows(tm, D, rev), _rows(tm, D, rev), _rows(tm, W3, rev), pl.BlockSpec((HALO, W3), halo_index), _rows(tm, D, rev),
                  _full((1, D)), _full((1, D)), _full((4, D, QW)), _full((3, AW)), _full((4, GD, GD)), _full((1, AW)), _full((D, D))],
        out_specs=[_rows(tm, D, rev), _rows(tm, W3, rev), _rows(tm, D, rev), _full((S0_ROWS, GD))],
        out_shape=[jax.ShapeDtypeStruct((S, D), F32), jax.ShapeDtypeStruct((S, W3), BF16), jax.ShapeDtypeStruct((S, D), BF16),
                   jax.ShapeDtypeStruct((S0_ROWS, GD), F32)],
        scratch_shapes=[pltpu.VMEM((L, AW), F32), pltpu.VMEM((L, AW), F32), pltpu.VMEM((L, AW), F32), pltpu.VMEM((L, AW), F32),
                        pltpu.VMEM((8, D), F32), pltpu.VMEM((8, AW), F32), pltpu.VMEM((4, GD, GD), F32)],
        compiler_params=_params(),
    )(dx1, x, proj, proj, m, pre, post, win, cw, pwb, ps, wout)


def _owner_id(me, relation, c):
    q = jnp.bitwise_xor(me, relation)
    return (q // 2, q % 2, c)


def _small_gather_steps(small_ref, all_ref, stage, send_sems, recv_sems, local_sem):
    x, y, c = _position()
    chips = _chips(x, y)

    def slot(chip, core):
        return 4 * chip[0] + 2 * chip[1] + core

    def copy(k, src, block, to):
        return pltpu.make_async_remote_copy(src_ref=src, dst_ref=all_ref.at[block], send_sem=send_sems.at[k],
                                            recv_sem=recv_sems.at[k], device_id=to, device_id_type=MESH)

    def own_copy():
        return pltpu.make_async_copy(stage, all_ref.at[slot((x, y), c)], local_sem)

    def first_sends():
        mine = slot((x, y), c)
        return [copy(0, small_ref, mine, (x, y, 1 - c))] + [copy(1 + j, small_ref, mine, (*chip, c)) for j, chip in enumerate(chips)]

    def relays():
        return [copy(4 + j, all_ref.at[slot(chip, c)], slot(chip, c), (x, y, 1 - c)) for j, chip in enumerate(chips)]

    def start():
        for cp in first_sends():
            cp.start()
        load = pltpu.make_async_copy(small_ref, stage, local_sem)
        load.start()
        load.wait()
        own_copy().start()

    def relay():
        for j, chip in enumerate(chips):
            copy(1 + j, small_ref, slot(chip, c), (*chip, c)).wait_recv()
        for cp in relays():
            cp.start()

    def finish():
        copy(0, small_ref, slot((x, y), 1 - c), (x, y, 1 - c)).wait_recv()
        for j, chip in enumerate(chips):
            copy(4 + j, small_ref, slot(chip, 1 - c), (x, y, 1 - c)).wait_recv()
        for cp in first_sends() + relays():
            cp.wait_send()
        own_copy().wait()

    return start, relay, finish


def _wgrad_in_reduce(a, b, small, pos, name):
    S = a.shape[0]
    nk = S // TK
    hm = D // 2

    def b_index(j, k, pos_ref):
        return (k, jnp.bitwise_xor(pos_ref[1], 3 - j))

    def body(pos_ref, a_ref, b_ref, small_ref, own_ref, arr_ref, all_ref, acc, rbuf, sbuf, stage,
             d2d_send, d2d_recv, ici_send, ici_recv, g_send, g_recv, g_local):
        j, k = pl.program_id(0), pl.program_id(1)
        x, y, c = _position()
        me = 2 * x + y
        last = k == nk - 1
        gather_start, gather_relay, gather_finish = _small_gather_steps(small_ref, all_ref, stage, g_send, g_recv, g_local)

        def to_sibling(jj):
            src = acc.at[jj % 2, pl.ds(pl.multiple_of((1 - c) * hm, hm), hm), :]
            return pltpu.make_async_remote_copy(
                src_ref=src, dst_ref=rbuf.at[jj], send_sem=d2d_send.at[jj], recv_sem=d2d_recv.at[jj],
                device_id=(x, y, 1 - c), device_id_type=MESH)

        def to_owner(jj):
            return pltpu.make_async_remote_copy(
                src_ref=sbuf.at[jj], dst_ref=arr_ref.at[2 - jj], send_sem=ici_send.at[jj], recv_sem=ici_recv.at[jj],
                device_id=_owner_id(me, 3 - jj, c), device_id_type=MESH)

        def pair_sum(jj):
            to_sibling(jj).wait_recv()
            return acc[jj % 2, pl.ds(pl.multiple_of(c * hm, hm), hm), :] + rbuf[jj]

        def send_block(jj):
            sbuf[jj] = pair_sum(jj).astype(BF16)
            to_owner(jj).start()

        @pl.when((j == 0) & (k == 0))
        def _():
            gather_start()

        @pl.when((j == 2) & (k == 0))
        def _():
            gather_relay()

        @pl.when((k == 0) & (j >= 2))
        def _():
            to_sibling(j - 2).wait_send()

        @pl.when(k == 0)
        def _():
            acc[j % 2] = jnp.zeros((D, QW), F32)

        acc[j % 2] += _tn(a_ref[...], b_ref[...])

        @pl.when(last)
        def _():
            to_sibling(j).start()

        @pl.when(last & (j == 1))
        def _():
            send_block(0)

        @pl.when(last & (j == 2))
        def _():
            send_block(1)
            send_block(2)

        @pl.when(last & (j == 3))
        def _():
            own_ref[...] = pair_sum(3)
            to_sibling(2).wait_send()
            to_sibling(3).wait_send()
            for jj in range(3):
                to_owner(jj).wait()
            gather_finish()

    any_spec = pl.BlockSpec(memory_space=pl.ANY)
    grid_spec = pltpu.PrefetchScalarGridSpec(
        num_scalar_prefetch=1, grid=(4, nk),
        in_specs=[pl.BlockSpec((TK, D), lambda j, k, pos_ref: (k, 0)), pl.BlockSpec((TK, QW), b_index), any_spec],
        out_specs=[pl.BlockSpec((hm, QW), lambda j, k, pos_ref: (0, 0)), any_spec, any_spec],
        scratch_shapes=[pltpu.VMEM((2, D, QW), F32), pltpu.VMEM((4, hm, QW), F32), pltpu.VMEM((3, hm, QW), BF16),
                        pltpu.VMEM(small.shape, F32),
                        pltpu.SemaphoreType.DMA((4,)), pltpu.SemaphoreType.DMA((4,)),
                        pltpu.SemaphoreType.DMA((3,)), pltpu.SemaphoreType.DMA((3,)),
                        pltpu.SemaphoreType.DMA((7,)), pltpu.SemaphoreType.DMA((7,)), pltpu.SemaphoreType.DMA])
    return pl.pallas_call(
        body, name=name, grid_spec=grid_spec,
        out_shape=[jax.ShapeDtypeStruct((hm, QW), F32), jax.ShapeDtypeStruct((3, hm, QW), BF16),
                   jax.ShapeDtypeStruct((8, *small.shape), F32)],
        compiler_params=pltpu.CompilerParams(dimension_semantics=("arbitrary",) * 2, vmem_limit_bytes=VMEM_LIMIT),
    )(pos, a, b, small)


def _wgrad_out_reduce(a, b, name):
    S = a.shape[0]
    nk = S // TK
    hm = D // 2
    qr = hm // 4

    def body(a_ref, b_ref, own_ref, arr_ref, acc, rbuf, total, sbuf, d2d_send, d2d_recv, ici_send, ici_recv):
        k = pl.program_id(0)
        x, y, c = _position()
        me = 2 * x + y

        def to_owner(jj):
            return pltpu.make_async_remote_copy(
                src_ref=sbuf.at[jj], dst_ref=arr_ref.at[jj], send_sem=ici_send.at[jj], recv_sem=ici_recv.at[jj],
                device_id=_owner_id(me, jj + 1, c), device_id_type=MESH)

        @pl.when(k == 0)
        def _():
            acc[...] = jnp.zeros((D, D), F32)

        acc[...] += _tn(a_ref[...], b_ref[...])

        @pl.when(k == nk - 1)
        def _():
            to_sibling = pltpu.make_async_remote_copy(
                src_ref=acc.at[pl.ds(pl.multiple_of((1 - c) * hm, hm), hm), :], dst_ref=rbuf, send_sem=d2d_send, recv_sem=d2d_recv,
                device_id=(x, y, 1 - c), device_id_type=MESH)
            to_sibling.start()
            to_sibling.wait_recv()
            total[...] = acc[pl.ds(pl.multiple_of(c * hm, hm), hm), :] + rbuf[...]
            for jj in range(3):
                q = jnp.bitwise_xor(me, jj + 1)
                sbuf[jj] = total[pl.ds(pl.multiple_of(q * qr, qr), qr), :].astype(BF16)
                to_owner(jj).start()
            own_ref[...] = total[pl.ds(pl.multiple_of(me * qr, qr), qr), :]
            to_sibling.wait_send()
            for jj in range(3):
                to_owner(jj).wait()

    return pl.pallas_call(
        body, name=name, grid=(nk,),
        in_specs=[pl.BlockSpec((TK, D), lambda k: (k, 0)), pl.BlockSpec((TK, D), lambda k: (k, 0))],
        out_specs=[pl.BlockSpec((qr, D), lambda k: (0, 0)), pl.BlockSpec(memory_space=pl.ANY)],
        out_shape=[jax.ShapeDtypeStruct((qr, D), F32), jax.ShapeDtypeStruct((3, qr, D), BF16)],
        scratch_shapes=[pltpu.VMEM((D, D), F32), pltpu.VMEM((hm, D), F32), pltpu.VMEM((hm, D), F32), pltpu.VMEM((3, qr, D), BF16),
                        pltpu.SemaphoreType.DMA, pltpu.SemaphoreType.DMA,
                        pltpu.SemaphoreType.DMA((3,)), pltpu.SemaphoreType.DMA((3,))],
        compiler_params=_params(),
    )(a, b)


def _chips(x, y):
    return [(1 - x, y), (x, 1 - y), (1 - x, 1 - y)]


def _gather_weights(parts, split):
    n = len(parts)

    def body(*refs):
        ins, outs = refs[:n], refs[n:2 * n]
        send_sems, recv_sems, local_sems = refs[2 * n:]
        x, y, c = _position()
        me = 2 * x + y
        chips = _chips(x, y)
        local = [pltpu.make_async_copy(ins[a], outs[a].at[me], local_sems.at[a]) for a in range(n)]
        for cp in local:
            cp.start()

        def half(a, ref, who):
            rows = parts[a].shape[0] // 2
            return ref.at[pl.ds(pl.multiple_of(who * rows, 8), rows), :] if split[a] else ref

        sends = []
        for a in range(n):
            for j, chip in enumerate(chips):
                sends.append(pltpu.make_async_remote_copy(
                    src_ref=half(a, ins[a], c), dst_ref=half(a, outs[a].at[me], c),
                    send_sem=send_sems.at[a, j], recv_sem=recv_sems.at[a, j], device_id=(*chip, c), device_id_type=MESH))
        for cp in sends:
            cp.start()
        for j, chip in enumerate(chips):
            src = 2 * chip[0] + chip[1]
            for a in range(n):
                pltpu.make_async_remote_copy(
                    src_ref=half(a, ins[a], c), dst_ref=half(a, outs[a].at[src], c),
                    send_sem=send_sems.at[a, j], recv_sem=recv_sems.at[a, j], device_id=(*chip, c), device_id_type=MESH).wait_recv()
                if split[a]:
                    fwd = pltpu.make_async_remote_copy(
                        src_ref=half(a, outs[a].at[src], c), dst_ref=half(a, outs[a].at[src], c),
                        send_sem=send_sems.at[a, 3 + j], recv_sem=recv_sems.at[a, 3 + j], device_id=(x, y, 1 - c), device_id_type=MESH)
                    fwd.start()
                    sends.append(fwd)
        for j, chip in enumerate(chips):
            src = 2 * chip[0] + chip[1]
            for a in range(n):
                if split[a]:
                    pltpu.make_async_remote_copy(
                        src_ref=half(a, outs[a].at[src], 1 - c), dst_ref=half(a, outs[a].at[src], 1 - c),
                        send_sem=send_sems.at[a, 3 + j], recv_sem=recv_sems.at[a, 3 + j], device_id=(x, y, 1 - c),
                        device_id_type=MESH).wait_recv()
        for cp in sends:
            cp.wait_send()
        for cp in local:
            cp.wait()

    any_spec = pl.BlockSpec(memory_space=pl.ANY)
    return pl.pallas_call(
        body, name="gather_weights",
        in_specs=[pl.BlockSpec(memory_space=pltpu.VMEM)] * n, out_specs=[any_spec] * n,
        out_shape=[jax.ShapeDtypeStruct((4, *p.shape), p.dtype) for p in parts],
        scratch_shapes=[pltpu.SemaphoreType.DMA((n, 6)), pltpu.SemaphoreType.DMA((n, 6)), pltpu.SemaphoreType.DMA((n,))],
    )(*parts)


def _adamw(w, g, m, v):
    m = ADAM_B1 * m + (1.0 - ADAM_B1) * g
    v = ADAM_B2 * v + (1.0 - ADAM_B2) * (g * g)
    m_hat = m / (1.0 - ADAM_B1 ** ADAM_STEP)
    v_hat = v / (1.0 - ADAM_B2 ** ADAM_STEP)
    delta = -ADAM_LR * (m_hat / (jnp.sqrt(v_hat) + ADAM_EPS) + ADAM_WD * w)
    return delta, m, v


def _reduce_share(own, arrivals, pos, name):
    hr, cols = own.shape
    tr = 128
    nth = hr // tr

    def half_index(p, i, pos_ref):
        return (jnp.where(p == 0, pos_ref[0], 1 - pos_ref[0]) * nth + i, 0)

    def own_index(p, i, pos_ref):
        return (jnp.where(p == 0, i, nth - 1), 0)

    def arr_index(p, i, pos_ref):
        return (0, jnp.where(p == 0, i, nth - 1), 0)

    def body(pos_ref, own_ref, arr_ref, g_out, mine, theirs, send_sem, recv_sem):
        p, i = pl.program_id(0), pl.program_id(1)
        x, y, c = _position()

        def share(rows):
            return pltpu.make_async_remote_copy(
                src_ref=mine.at[rows, :], dst_ref=theirs.at[rows, :], send_sem=send_sem, recv_sem=recv_sem,
                device_id=(x, y, 1 - c), device_id_type=MESH)

        tile = pl.ds(pl.multiple_of(i * tr, tr), tr)

        @pl.when(p == 0)
        def _():
            g = own_ref[...]
            for j in range(3):
                g = g + arr_ref[j].astype(F32)
            mine[tile, :] = g
            share(tile).start()
            g_out[...] = g

        @pl.when((p == 1) & (i == 0))
        def _():
            share(pl.ds(0, hr)).wait()

        @pl.when(p == 1)
        def _():
            g_out[...] = theirs[tile, :]

    grid_spec = pltpu.PrefetchScalarGridSpec(
        num_scalar_prefetch=1, grid=(2, nth),
        in_specs=[pl.BlockSpec((tr, cols), own_index), pl.BlockSpec((3, tr, cols), arr_index)],
        out_specs=pl.BlockSpec((tr, cols), half_index),
        scratch_shapes=[pltpu.VMEM((hr, cols), F32), pltpu.VMEM((hr, cols), F32), pltpu.SemaphoreType.DMA, pltpu.SemaphoreType.DMA])
    return pl.pallas_call(
        body, name=name, grid_spec=grid_spec,
        out_shape=jax.ShapeDtypeStruct((2 * hr, cols), F32),
        compiler_params=pltpu.CompilerParams(dimension_semantics=("arbitrary",) * 2, vmem_limit_bytes=VMEM_LIMIT),
    )(pos, own, arrivals)


def _adamw_big(w, g, m, v, name):
    rows, cols = w.shape
    tr = 128

    def body(w_ref, g_ref, m_ref, v_ref, go_ref, d_ref, mo_ref, vo_ref):
        gv = g_ref[...]
        go_ref[...] = gv
        d_ref[...], mo_ref[...], vo_ref[...] = _adamw(w_ref[...], gv, m_ref[...], v_ref[...])

    spec = pl.BlockSpec((tr, cols), lambda i: (i, 0))
    return pl.pallas_call(
        body, name=name, grid=(rows // tr,),
        in_specs=[spec] * 4, out_specs=[spec] * 4,
        out_shape=[jax.ShapeDtypeStruct((rows, cols), F32)] * 4,
        compiler_params=pltpu.CompilerParams(dimension_semantics=("arbitrary",)),
    )(w, g, m, v)


def _adamw_small(g0, g1, weights, moms, vels):
    names = ["pre", "post", "conv", "pw", "ps", "lng", "lnb", "ws", "bs"]
    shapes = [w.shape for w in weights]

    def body(*refs):
        me = 2 * lax.axis_index("x") + lax.axis_index("y")
        g0_ref, g1_ref = refs[0], refs[1]
        w_refs, m_refs, v_refs = refs[2:11], refs[11:20], refs[20:29]
        outs = refs[29:29 + 36]
        loss_ref = refs[65]
        t0_ref, t1_ref = refs[66], refs[67]
        t0 = g0_ref[0]
        t1 = g1_ref[0]
        for d in range(1, 8):
            t0 = t0 + g0_ref[d]
            t1 = t1 + g1_ref[d]
        t0_ref[...] = t0
        t1_ref[...] = t1
        loss_ref[...] = t1_ref[S1_LOSS:S1_LOSS + 1, 0:1]
        my_conv = pl.multiple_of(S0_CONV + 8 * me, 8)
        my_ln = pl.multiple_of(S1_LN + 8 * me, 8)

        def update(idx, piece, grad):
            go, do, mo, vo = outs[4 * idx:4 * idx + 4]
            go[piece] = grad
            do[piece], mo[piece], vo[piece] = _adamw(w_refs[idx][piece], grad, m_refs[idx][piece], v_refs[idx][piece])

        for layer in range(2):
            for k in range(D // GD):
                lanes = slice(k * GD, (k + 1) * GD)
                tref, pre0, post0 = (t0_ref, S0_PRE, S0_POST) if layer == 0 else (t1_ref, S1_PRE, S1_POST)
                update(0, (slice(layer, layer + 1), lanes), tref[pre0 + k:pre0 + k + 1, :])
                update(1, (slice(layer, layer + 1), lanes), tref[post0 + k:post0 + k + 1, :])
        conv_rows = t0_ref[pl.ds(my_conv, 8), :]
        update(2, (slice(0, 3), slice(None)), conv_rows[0:3, :])
        for g in range(4):
            update(3, (g,), t0_ref[S0_PW + g * GD:S0_PW + (g + 1) * GD, :])
            update(4, (slice(0, 1), slice(g * GD, (g + 1) * GD)), t0_ref[S0_PS + g:S0_PS + g + 1, :])
        ln_rows = t1_ref[pl.ds(my_ln, 8), :]
        for k in range(2):
            update(5, (slice(0, 1), slice(k * GD, (k + 1) * GD)), ln_rows[k:k + 1, :])
            update(6, (slice(0, 1), slice(k * GD, (k + 1) * GD)), ln_rows[2 + k:3 + k, :])
        for h in range(HEADS):
            update(7, (h,), t1_ref[S1_WS + h * CHUNK:S1_WS + (h + 1) * CHUNK, :])
        update(8, (slice(None), slice(None)), t1_ref[S1_BS:S1_BS + HEADS, :])

    vm = pl.BlockSpec(memory_space=pltpu.VMEM)
    out_shape = []
    for s in shapes:
        out_shape += [jax.ShapeDtypeStruct(s, F32)] * 4
    out_shape.append(jax.ShapeDtypeStruct((1, 1), F32))
    res = pl.pallas_call(
        body, name="adamw_small",
        in_specs=[vm] * 29, out_specs=[vm] * 37, out_shape=out_shape,
        scratch_shapes=[pltpu.VMEM((S0_ROWS, GD), F32), pltpu.VMEM((S1_ROWS, GD), F32)],
        compiler_params=pltpu.CompilerParams(vmem_limit_bytes=VMEM_LIMIT),
    )(g0, g1, *weights, *moms, *vels)
    per_weight = {nm: res[4 * i:4 * i + 4] for i, nm in enumerate(names)}
    return per_weight, res[36]


def _pad8(a):
    return jnp.pad(a, ((0, 8 - a.shape[0]), (0, 0)))


def kernel(x, pre_norm, post_norm, even_w_in, even_conv_w, even_pool_w, even_pool_scale, even_w_out, odd_w_in, odd_ln_g, odd_ln_b, odd_w_s, odd_b_s, odd_w_out, loss_target, m_pre_norm, m_post_norm, m_even_w_in, m_even_conv_w, m_even_pool_w, m_even_pool_scale, m_even_w_out, m_odd_w_in, m_odd_ln_g, m_odd_ln_b, m_odd_w_s, m_odd_b_s, m_odd_w_out, v_pre_norm, v_post_norm, v_even_w_in, v_even_conv_w, v_even_pool_w, v_even_pool_scale, v_even_w_out, v_odd_w_in, v_odd_ln_g, v_odd_ln_b, v_odd_w_s, v_odd_b_s, v_odd_w_out):
    xs = x[0]
    tgt = loss_target[0]

    small_shard = jnp.concatenate([_pad8(even_conv_w[0]), _pad8(odd_ln_g.reshape(2, GD)), _pad8(odd_ln_b.reshape(2, GD))], axis=0)
    win0, wout0, shard = _gather_weights([even_w_in[0].astype(BF16), even_w_out[0].astype(BF16), small_shard], [True, True, False])
    wout0 = wout0.reshape(D, D)
    px, py, pc = _position()
    pos = jnp.stack([pc, 2 * px + py]).astype(jnp.int32)
    conv_w = shard[:, 0:3, :].transpose(1, 0, 2).reshape(3, AW)
    ln_g = shard[:, 8:10, :].reshape(1, D)
    ln_b = shard[:, 16:18, :].reshape(1, D)
    pool_wb = even_pool_w[0].astype(BF16)
    ws_tril = jnp.tril(odd_w_s[0]).astype(BF16)
    ws_tril_t = jnp.swapaxes(ws_tril, 1, 2)
    bias = jnp.broadcast_to(odd_b_s[0][:, :, None], (HEADS, CHUNK, GD))
    pre0, pre1 = pre_norm[0:1], pre_norm[1:2]
    post0, post1 = post_norm[0:1], post_norm[1:2]

    x1, proj0, m0, hb0, mixp0, win1, wout1 = _even_fwd(
        xs, pre0, post0, win0, conv_w, pool_wb, even_pool_scale, wout0, [odd_w_in[0].astype(BF16), odd_w_out[0].astype(BF16)])
    wout1 = wout1.reshape(D, D)
    proj1, m1, hb1, yp1, dx2, loss_part = _odd_fwd(x1, tgt, pre1, post1, win1, ln_g, ln_b, ws_tril, bias, wout1)
    dx1, dproj1, dmb1, small1 = _odd_bwd(dx2, x1, proj1, m1, loss_part, pre1, post1, win1, ln_g, ln_b, ws_tril, ws_tril_t, bias, wout1)
    red_out1 = _wgrad_out_reduce(yp1, dmb1, "wgrad_odd_out")
    own_in1, arr_in1, all1 = _wgrad_in_reduce(hb1, dproj1, small1, pos, "wgrad_odd_in")
    gx, dproj0, dmb0, small0 = _even_bwd(dx1, xs, proj0, m0, pre0, post0, win0, conv_w, pool_wb, even_pool_scale, wout0)
    red_out0 = _wgrad_out_reduce(mixp0, dmb0, "wgrad_even_out")
    own_in0, arr_in0, all0 = _wgrad_in_reduce(hb0, dproj0, small0, pos, "wgrad_even_in")

    reduced = [(own_in0, arr_in0), red_out0, (own_in1, arr_in1), red_out1]
    tags = ["even_in", "even_out", "odd_in", "odd_out"]
    big_w = [even_w_in[0], even_w_out[0], odd_w_in[0], odd_w_out[0]]
    big_m = [m_even_w_in[0], m_even_w_out[0], m_odd_w_in[0], m_odd_w_out[0]]
    big_v = [v_even_w_in[0], v_even_w_out[0], v_odd_w_in[0], v_odd_w_out[0]]
    big = [_adamw_big(w, _reduce_share(r[0], r[1], pos, "reduce_share_" + t), m, v, "adamw_" + t)
           for r, w, m, v, t in zip(reduced, big_w, big_m, big_v, tags)]

    small_w = [pre_norm, post_norm, even_conv_w[0], even_pool_w[0], even_pool_scale, odd_ln_g, odd_ln_b, odd_w_s[0], odd_b_s[0]]
    small_m = [m_pre_norm, m_post_norm, m_even_conv_w[0], m_even_pool_w[0], m_even_pool_scale, m_odd_ln_g, m_odd_ln_b, m_odd_w_s[0], m_odd_b_s[0]]
    small_v = [v_pre_norm, v_post_norm, v_even_conv_w[0], v_even_pool_w[0], v_even_pool_scale, v_odd_ln_g, v_odd_ln_b, v_odd_w_s[0], v_odd_b_s[0]]
    sm, loss = _adamw_small(all0, all1, small_w, small_m, small_v)

    def lead(a):
        return a[None]

    per = {
        "pre_norm": sm["pre"], "post_norm": sm["post"],
        "even_w_in": [lead(a) for a in big[0]], "even_conv_w": [lead(a) for a in sm["conv"]],
        "even_pool_w": [lead(a) for a in sm["pw"]], "even_pool_scale": sm["ps"],
        "even_w_out": [lead(a) for a in big[1]], "odd_w_in": [lead(a) for a in big[2]],
        "odd_ln_g": sm["lng"], "odd_ln_b": sm["lnb"],
        "odd_w_s": [lead(a) for a in sm["ws"]], "odd_b_s": [lead(a) for a in sm["bs"]],
        "odd_w_out": [lead(a) for a in big[3]],
    }
    order = ["pre_norm", "post_norm", "even_w_in", "even_conv_w", "even_pool_w", "even_pool_scale", "even_w_out", "odd_w_in",
             "odd_ln_g", "odd_ln_b", "odd_w_s", "odd_b_s", "odd_w_out"]
    outs = [loss.reshape(()), gx[None]]
    for kind in range(4):
        outs += [per[nm][kind] for nm in order]
    return tuple(outs)
```

```python
import functools

import jax
import jax.numpy as jnp
from jax import lax
from jax.experimental import pallas as pl
from jax.experimental.pallas import tpu as pltpu

F32 = jnp.float32
BF16 = jnp.bfloat16
MESH = pl.DeviceIdType.MESH

D = 1024
W3 = 3 * D
QW = W3 // 4
AW = 512
GD = 128
CHUNK = 128
HEADS = 8
HALO = 16
POOL_WINDOWS = (2, 4, 8, 16)
EPS = 1e-6
TM = 256
TK = 1024
VMEM_LIMIT = 56 * 1024 * 1024

ADAM_LR, ADAM_B1, ADAM_B2, ADAM_EPS, ADAM_WD, ADAM_STEP = 0.001, 0.9, 0.999, 1e-08, 0.01, 10

S0_PRE, S0_POST, S0_CONV, S0_PS, S0_PW, S0_ROWS = 0, 8, 16, 48, 56, 568
S1_PRE, S1_POST, S1_LN, S1_BS, S1_WS, S1_LOSS, S1_ROWS = 0, 8, 16, 48, 56, 1080, 1088


def _nn(a, b):
    return jnp.dot(a, b, preferred_element_type=F32)


def _nt(a, b):
    return lax.dot_general(a, b, (((1,), (1,)), ((), ())), preferred_element_type=F32)


def _tn(a, b):
    return lax.dot_general(a, b, (((0,), (0,)), ((), ())), preferred_element_type=F32)


def _sigmoid(z):
    return 1.0 / (1.0 + jnp.exp(-z))


def _rms_fwd(x, g):
    r = lax.rsqrt(jnp.mean(x * x, axis=-1, keepdims=True) + EPS)
    return x * r, r


def _rms_bwd(dy, xh, r, g):
    dn = dy * g
    dx = r * (dn - xh * jnp.mean(xh * dn, axis=-1, keepdims=True))
    return dx, jnp.sum(dy * xh, axis=0, keepdims=True)


def _full(shape):
    nd = len(shape)
    return pl.BlockSpec(shape, lambda i, _n=nd: (0,) * _n)


def _rows(tm, width, index=None):
    return pl.BlockSpec((tm, width), (lambda i: (i, 0)) if index is None else index)


def _params():
    return pltpu.CompilerParams(dimension_semantics=("arbitrary",), vmem_limit_bytes=VMEM_LIMIT)


def _position():
    x, y, c = lax.axis_index("x"), lax.axis_index("y"), lax.axis_index("c")
    return x, y, c


def _even_mix(proj_ref, hc_ext, xp_ext, cw_ref, pw_ref, ps_ref, first_row):
    tm = proj_ref.shape[0]
    xa = proj_ref[:, 0:AW]
    gb = proj_ref[:, AW:2 * AW]
    gc = proj_ref[:, 2 * AW:3 * AW]
    za = proj_ref[:, 3 * AW:4 * AW]
    xp = proj_ref[:, 4 * AW:5 * AW]
    zp = proj_ref[:, 5 * AW:6 * AW]
    hc = gc * xa
    hc_ext[HALO:, :] = hc
    e = hc_ext[...]
    conv = cw_ref[2:3, :] * hc + cw_ref[1:2, :] * pltpu.roll(e, 1, 0)[HALO:] + cw_ref[0:1, :] * pltpu.roll(e, 2, 0)[HALO:]
    sa = _sigmoid(za)
    xp_ext[HALO:, :] = xp
    pos = first_row + lax.broadcasted_iota(jnp.int32, (tm, 1), 0)
    pooled, mixed, counts = [], [], []
    for g, w in enumerate(POOL_WINDOWS):
        cols = slice(g * GD, (g + 1) * GD)
        s = xp_ext[:, cols]
        for k in range(g + 1):
            s = s + pltpu.roll(s, 2 ** k, 0)
        count = jnp.minimum(pos + 1, w).astype(F32)
        pg = s[HALO:] / count - xp[:, cols]
        pooled.append(pg.astype(BF16))
        mixed.append(_nn(pooled[-1], pw_ref[g]))
        counts.append(count)
    mixed = jnp.concatenate(mixed, axis=-1)
    sb = _sigmoid(zp)
    return dict(xa=xa, gb=gb, gc=gc, za=za, zp=zp, hc=hc, conv=conv, sa=sa, sb=sb, pooled=pooled, mixed=mixed, counts=counts)


def _half_rows(ref, rows, who):
    return ref.at[pl.ds(pl.multiple_of(who * (rows // 2), 8), rows // 2), :]


def _store_permuted(ref, value):
    for ob in range(D // GD):
        nb = 4 * (ob % 2) + ob // 2
        ref[:, nb * GD:(nb + 1) * GD] = value[:, ob * GD:(ob + 1) * GD]


def _even_fwd(x, pre, post, win, cw, pwb, ps, wout, next_shards):
    S = x.shape[0]
    tm = 2 * TM
    nt = S // tm
    relay = (3 * nt) // 4
    n = len(next_shards)
    shard_rows = [p.shape[0] for p in next_shards]

    def body(x_ref, pre_ref, post_ref, win_ref, cw_ref, pw_ref, ps_ref, wout_ref, *rest):
        shard_refs, rest = rest[:n], rest[n:]
        x1_ref, proj_ref, m_ref, hb_ref, mixp_ref = rest[:5]
        full_refs, rest = rest[5:5 + n], rest[5 + n:]
        hc_ext, xp_ext, mix_sc = rest[:3]
        stage, rest = rest[3:3 + n], rest[3 + n:]
        send_sems, recv_sems, local_sems = rest
        i = pl.program_id(0)
        px, py, pc = _position()
        me = 2 * px + py
        chips = _chips(px, py)

        def ici(a, j):
            return pltpu.make_async_remote_copy(
                src_ref=_half_rows(shard_refs[a], shard_rows[a], pc), dst_ref=_half_rows(full_refs[a].at[me], shard_rows[a], pc),
                send_sem=send_sems.at[a, j], recv_sem=recv_sems.at[a, j], device_id=(*chips[j], pc), device_id_type=MESH)

        def ici_arrival(a, j):
            src = 2 * chips[j][0] + chips[j][1]
            return pltpu.make_async_remote_copy(
                src_ref=_half_rows(shard_refs[a], shard_rows[a], pc), dst_ref=_half_rows(full_refs[a].at[src], shard_rows[a], pc),
                send_sem=send_sems.at[a, j], recv_sem=recv_sems.at[a, j], device_id=(*chips[j], pc), device_id_type=MESH)

        def relay_copy(a, j, who):
            src = 2 * chips[j][0] + chips[j][1]
            region = _half_rows(full_refs[a].at[src], shard_rows[a], who)
            return pltpu.make_async_remote_copy(
                src_ref=region, dst_ref=region, send_sem=send_sems.at[a, 3 + j], recv_sem=recv_sems.at[a, 3 + j],
                device_id=(px, py, 1 - pc), device_id_type=MESH)

        def own_copy(a):
            return pltpu.make_async_copy(stage[a], full_refs[a].at[me], local_sems.at[a])

        @pl.when(i == 0)
        def _():
            hc_ext[0:HALO, :] = jnp.zeros((HALO, AW), F32)
            xp_ext[0:HALO, :] = jnp.zeros((HALO, AW), F32)
            for a in range(n):
                for j in range(3):
                    ici(a, j).start()
            for a in range(n):
                load = pltpu.make_async_copy(shard_refs[a], stage[a], local_sems.at[a])
                load.start()
                load.wait()
                own_copy(a).start()

        @pl.when(i == relay)
        def _():
            for j in range(3):
                for a in range(n):
                    ici_arrival(a, j).wait_recv()
                    relay_copy(a, j, pc).start()

        xv = x_ref[...]
        xh, _ = _rms_fwd(xv, None)
        hb = (xh * pre_ref[...]).astype(BF16)
        hb_ref[...] = hb
        for q in range(4):
            proj_ref[:, q * QW:(q + 1) * QW] = _nn(hb, win_ref[q])
        t = _even_mix(proj_ref, hc_ext, xp_ext, cw_ref, pw_ref, ps_ref, i * tm)
        mix_sc[:, 0:AW] = (t["gb"] * t["conv"] * (t["za"] * t["sa"])).astype(BF16)
        mix_sc[:, AW:2 * AW] = (t["mixed"] * ps_ref[...] * (t["zp"] * t["sb"])).astype(BF16)
        mix = mix_sc[...]
        _store_permuted(mixp_ref, mix)
        m = _nn(mix, wout_ref[...])
        m_ref[...] = m
        mh, _ = _rms_fwd(m, None)
        x1_ref[...] = xv + mh * post_ref[...]
        hc_ext[0:HALO, :] = hc_ext[tm:tm + HALO, :]
        xp_ext[0:HALO, :] = xp_ext[tm:tm + HALO, :]

        @pl.when(i == nt - 1)
        def _():
            for j in range(3):
                for a in range(n):
                    relay_copy(a, j, 1 - pc).wait_recv()
            for a in range(n):
                for j in range(3):
                    ici(a, j).wait_send()
                    relay_copy(a, j, pc).wait_send()
                own_copy(a).wait()

    any_spec = pl.BlockSpec(memory_space=pl.ANY)
    return pl.pallas_call(
        body, name="even_fwd", grid=(nt,),
        in_specs=[_rows(tm, D), _full((1, D)), _full((1, D)), _full((4, D, QW)), _full((3, AW)), _full((4, GD, GD)),
                  _full((1, AW)), _full((D, D))] + [any_spec] * n,
        out_specs=[_rows(tm, D), _rows(tm, W3), _rows(tm, D), _rows(tm, D), _rows(tm, D)] + [any_spec] * n,
        out_shape=[jax.ShapeDtypeStruct((S, D), F32), jax.ShapeDtypeStruct((S, W3), F32), jax.ShapeDtypeStruct((S, D), F32),
                   jax.ShapeDtypeStruct((S, D), BF16), jax.ShapeDtypeStruct((S, D), BF16)]
        + [jax.ShapeDtypeStruct((4, *p.shape), p.dtype) for p in next_shards],
        scratch_shapes=[pltpu.VMEM((tm + HALO, AW), F32), pltpu.VMEM((tm + HALO, AW), F32), pltpu.VMEM((tm, D), BF16)]
        + [pltpu.VMEM(p.shape, p.dtype) for p in next_shards]
        + [pltpu.SemaphoreType.DMA((n, 6)), pltpu.SemaphoreType.DMA((n, 6)), pltpu.SemaphoreType.DMA((n,))],
        compiler_params=_params(),
    )(x, pre, post, win, cw, pwb, ps, wout, *next_shards)


def _chunks_side_by_side(a, h):
    return jnp.concatenate([a[n * CHUNK:(n + 1) * CHUNK, h * GD:(h + 1) * GD] for n in range(a.shape[0] // CHUNK)], axis=1)


def _odd_mix(proj_ref, lng_ref, lnb_ref, ws_ref, bias_ref, sv_ref):
    tm = proj_ref.shape[0]
    u = proj_ref[:, 0:D]
    v = proj_ref[:, D:2 * D]
    z = proj_ref[:, 2 * D:3 * D]
    mu = jnp.mean(v, axis=-1, keepdims=True)
    vc = v - mu
    rs = lax.rsqrt(jnp.mean(vc * vc, axis=-1, keepdims=True) + EPS)
    vh = vc * rs
    vnb = (vh * lng_ref[...] + lnb_ref[...]).astype(BF16)
    for h in range(HEADS):
        sv = _nn(ws_ref[h], _chunks_side_by_side(vnb, h))
        for n in range(tm // CHUNK):
            sv_ref[n * CHUNK:(n + 1) * CHUNK, h * GD:(h + 1) * GD] = sv[:, n * GD:(n + 1) * GD] + bias_ref[h]
    return dict(u=u, z=z, vh=vh, rs=rs, vnb=vnb, sz=_sigmoid(z))


def _odd_fwd(x1, tgt, pre, post, win, lng, lnb, wsb, bias, wout):
    S = x1.shape[0]
    tm = 2 * TM
    nt = S // tm

    def body(x_ref, tgt_ref, pre_ref, post_ref, win_ref, lng_ref, lnb_ref, ws_ref, bias_ref, wout_ref,
             proj_ref, m_ref, hb_ref, yb_ref, dx2_ref, loss_ref, sv_ref):
        i = pl.program_id(0)

        @pl.when(i == 0)
        def _():
            loss_ref[...] = jnp.zeros((8, GD), F32)

        xv = x_ref[...]
        xh, _ = _rms_fwd(xv, None)
        hb = (xh * pre_ref[...]).astype(BF16)
        hb_ref[...] = hb
        for q in range(4):
            proj_ref[:, q * QW:(q + 1) * QW] = _nn(hb, win_ref[q])
        t = _odd_mix(proj_ref, lng_ref, lnb_ref, ws_ref, bias_ref, sv_ref)
        yb = (t["u"] * sv_ref[...] * (t["z"] * t["sz"])).astype(BF16)
        _store_permuted(yb_ref, yb)
        m = _nn(yb, wout_ref[...])
        m_ref[...] = m
        mh, _ = _rms_fwd(m, None)
        err = xv + mh * post_ref[...] - tgt_ref[...]
        dx2_ref[...] = err * (1.0 / D)
        part = 0.5 * jnp.sum(jnp.mean(err * err, axis=-1, keepdims=True), axis=0, keepdims=True)
        loss_ref[...] += jnp.broadcast_to(part, (8, GD))

    return pl.pallas_call(
        body, name="odd_fwd", grid=(nt,),
        in_specs=[_rows(tm, D), _rows(tm, D), _full((1, D)), _full((1, D)), _full((4, D, QW)), _full((1, D)), _full((1, D)),
                  _full((HEADS, CHUNK, CHUNK)), _full((HEADS, CHUNK, GD)), _full((D, D))],
        out_specs=[_rows(tm, W3), _rows(tm, D), _rows(tm, D), _rows(tm, D), _rows(tm, D), _full((8, GD))],
        out_shape=[jax.ShapeDtypeStruct((S, W3), F32), jax.ShapeDtypeStruct((S, D), F32), jax.ShapeDtypeStruct((S, D), BF16),
                   jax.ShapeDtypeStruct((S, D), BF16), jax.ShapeDtypeStruct((S, D), F32), jax.ShapeDtypeStruct((8, GD), F32)],
        scratch_shapes=[pltpu.VMEM((tm, D), F32)],
        compiler_params=_params(),
    )(x1, tgt, pre, post, win, lng, lnb, wsb, bias, wout)


def _store_rows(ref, row0, value):
    r, width = value.shape
    for a in range(r):
        for k in range(width // GD):
            ref[row0 + a * (width // GD) + k:row0 + a * (width // GD) + k + 1, :] = value[a:a + 1, k * GD:(k + 1) * GD]


def _proj_bwd(dproj, win_ref, x, dy, pre):
    dh = _nt(dproj[:, 0:QW], win_ref[0])
    for q in range(1, 4):
        dh += _nt(dproj[:, q * QW:(q + 1) * QW], win_ref[q])
    xh, r = _rms_fwd(x, None)
    dxn, dpre = _rms_bwd(dh, xh, r, pre)
    return dy + dxn, dpre


def _odd_bwd(dx2, x1, proj, m, loss, pre, post, win, lng, lnb, wsb, wsbt, bias, wout):
    S = x1.shape[0]
    tm = TM
    nt = S // tm

    def body(dy_ref, x_ref, proj_ref, m_ref, loss_ref, pre_ref, post_ref, win_ref, lng_ref, lnb_ref, ws_ref, wst_ref, bias_ref,
             wout_ref, dx_ref, dproj_ref, dmb_ref, small_ref, sv_ref, dvn_ref, acc1024, dws_acc, dbs_acc):
        i = pl.program_id(0)

        @pl.when(i == 0)
        def _():
            acc1024[...] = jnp.zeros_like(acc1024)
            dws_acc[...] = jnp.zeros_like(dws_acc)
            dbs_acc[...] = jnp.zeros_like(dbs_acc)

        dy = dy_ref[...]
        mh, rm = _rms_fwd(m_ref[...], None)
        dm, dpost = _rms_bwd(dy, mh, rm, post_ref[...])
        dmb = dm.astype(BF16)
        dmb_ref[...] = dmb
        dyv = _nt(dmb, wout_ref[...])
        t = _odd_mix(proj_ref, lng_ref, lnb_ref, ws_ref, bias_ref, sv_ref)
        u, z, sz, sv = t["u"], t["z"], t["sz"], sv_ref[...]
        dproj_ref[:, 0:D] = (dyv * sv * (z * sz)).astype(BF16)
        dproj_ref[:, 2 * D:3 * D] = (dyv * u * sv * (sz * (1.0 + z * (1.0 - sz)))).astype(BF16)
        dsv = dyv * u * (z * sz)
        dsvb = dsv.astype(BF16)
        for h in range(HEADS):
            dsv_h = _chunks_side_by_side(dsvb, h)
            dvn_h = _nn(wst_ref[h], dsv_h)
            dws_acc[h] += _nt(dsv_h, _chunks_side_by_side(t["vnb"], h))
            for n in range(tm // CHUNK):
                rows, cols = slice(n * CHUNK, (n + 1) * CHUNK), slice(h * GD, (h + 1) * GD)
                dvn_ref[rows, cols] = dvn_h[:, n * GD:(n + 1) * GD]
                dbs_acc[h] += dsv[rows, cols]
        dvn = dvn_ref[...]
        vh = t["vh"]
        dvh = dvn * lng_ref[...]
        dv = t["rs"] * (dvh - jnp.mean(dvh, axis=-1, keepdims=True) - vh * jnp.mean(dvh * vh, axis=-1, keepdims=True))
        dproj_ref[:, D:2 * D] = dv.astype(BF16)
        dx_ref[...], dpre = _proj_bwd(dproj_ref[...], win_ref, x_ref[...], dy, pre_ref[...])
        acc1024[0:1, :] += dpre
        acc1024[1:2, :] += dpost
        acc1024[2:3, :] += jnp.sum(dvn * vh, axis=0, keepdims=True)
        acc1024[3:4, :] += jnp.sum(dvn, axis=0, keepdims=True)

        @pl.when(i == nt - 1)
        def _():
            small_ref[...] = jnp.zeros_like(small_ref)
            _store_rows(small_ref, S1_PRE, acc1024[0:1, :])
            _store_rows(small_ref, S1_POST, acc1024[1:2, :])
            for q in range(4):
                _store_rows(small_ref, S1_LN + 8 * q, acc1024[2:3, 2 * q * GD:(2 * q + 2) * GD])
                _store_rows(small_ref, S1_LN + 8 * q + 2, acc1024[3:4, 2 * q * GD:(2 * q + 2) * GD])
            lower = lax.broadcasted_iota(jnp.int32, (CHUNK, CHUNK), 0) >= lax.broadcasted_iota(jnp.int32, (CHUNK, CHUNK), 1)
            for h in range(HEADS):
                small_ref[S1_WS + h * CHUNK:S1_WS + (h + 1) * CHUNK, :] = jnp.where(lower, dws_acc[h], 0.0)
                small_ref[S1_BS + h:S1_BS + h + 1, :] = jnp.sum(dbs_acc[h].T, axis=0, keepdims=True)
            small_ref[S1_LOSS:S1_LOSS + 8, :] = loss_ref[...]

    return pl.pallas_call(
        body, name="odd_bwd", grid=(nt,),
        in_specs=[_rows(tm, D), _rows(tm, D), _rows(tm, W3), _rows(tm, D), _full((8, GD)), _full((1, D)), _full((1, D)),
                  _full((4, D, QW)), _full((1, D)), _full((1, D)), _full((HEADS, CHUNK, CHUNK)), _full((HEADS, CHUNK, CHUNK)),
                  _full((HEADS, CHUNK, GD)), _full((D, D))],
        out_specs=[_rows(tm, D), _rows(tm, W3), _rows(tm, D), _full((S1_ROWS, GD))],
        out_shape=[jax.ShapeDtypeStruct((S, D), F32), jax.ShapeDtypeStruct((S, W3), BF16), jax.ShapeDtypeStruct((S, D), BF16),
                   jax.ShapeDtypeStruct((S1_ROWS, GD), F32)],
        scratch_shapes=[pltpu.VMEM((tm, D), F32), pltpu.VMEM((tm, D), F32), pltpu.VMEM((8, D), F32),
                        pltpu.VMEM((HEADS, CHUNK, CHUNK), F32), pltpu.VMEM((HEADS, CHUNK, GD), F32)],
        compiler_params=_params(),
    )(dx2, x1, proj, m, loss, pre, post, win, lng, lnb, wsb, wsbt, bias, wout)


def _even_bwd(dx1, x, proj, m, pre, post, win, cw, pwb, ps, wout):
    S = x.shape[0]
    tm = TM
    nt = S // tm
    L = tm + HALO

    def rev(i):
        return (nt - 1 - i, 0)

    def halo_index(i):
        return (jnp.maximum((nt - 1 - i) * (tm // HALO) - 1, 0), 0)

    def body(dy_ref, x_ref, proj_ref, halo_ref, m_ref, pre_ref, post_ref, win_ref, cw_ref, pw_ref, ps_ref, wout_ref,
             dx_ref, dproj_ref, dmb_ref, small_ref, hc_ext, xp_ext, dconv_ext, q_ext, acc1024, acc512, dpw_acc):
        i = pl.program_id(0)
        tile = nt - 1 - i

        @pl.when(i == 0)
        def _():
            dconv_ext[tm:L, :] = jnp.zeros((HALO, AW), F32)
            q_ext[tm:L, :] = jnp.zeros((HALO, AW), F32)
            acc1024[...] = jnp.zeros_like(acc1024)
            acc512[...] = jnp.zeros_like(acc512)
            dpw_acc[...] = jnp.zeros_like(dpw_acc)

        keep = (tile > 0).astype(F32)
        hc_ext[0:HALO, :] = halo_ref[:, 2 * AW:3 * AW] * halo_ref[:, 0:AW] * keep
        xp_ext[0:HALO, :] = halo_ref[:, 4 * AW:5 * AW] * keep

        dy = dy_ref[...]
        mh, rm = _rms_fwd(m_ref[...], None)
        dm, dpost = _rms_bwd(dy, mh, rm, post_ref[...])
        dmb = dm.astype(BF16)
        dmb_ref[...] = dmb
        dmix = _nt(dmb, wout_ref[...])
        dya, dyb = dmix[:, 0:AW], dmix[:, AW:2 * AW]
        t = _even_mix(proj_ref, hc_ext, xp_ext, cw_ref, pw_ref, ps_ref, tile * tm)
        gb, conv, za, sa, hc = t["gb"], t["conv"], t["za"], t["sa"], t["hc"]
        silu_a = za * sa
        dproj_ref[:, AW:2 * AW] = (dya * conv * silu_a).astype(BF16)
        dproj_ref[:, 3 * AW:4 * AW] = (dya * gb * conv * (sa * (1.0 + za * (1.0 - sa)))).astype(BF16)
        dconv = dya * gb * silu_a
        dconv_ext[0:tm, :] = dconv
        e = dconv_ext[...]
        dc1 = pltpu.roll(e, L - 1, 0)[0:tm]
        dc2 = pltpu.roll(e, L - 2, 0)[0:tm]
        dhc = cw_ref[2:3, :] * dconv + cw_ref[1:2, :] * dc1 + cw_ref[0:1, :] * dc2
        dproj_ref[:, 0:AW] = (dhc * t["gc"]).astype(BF16)
        dproj_ref[:, 2 * AW:3 * AW] = (dhc * t["xa"]).astype(BF16)
        acc512[0:1, :] += jnp.sum(dc2 * hc, axis=0, keepdims=True)
        acc512[1:2, :] += jnp.sum(dc1 * hc, axis=0, keepdims=True)
        acc512[2:3, :] += jnp.sum(dconv * hc, axis=0, keepdims=True)

        zp, sb, mixed = t["zp"], t["sb"], t["mixed"]
        silu_b = zp * sb
        acc512[3:4, :] += jnp.sum(dyb * mixed * silu_b, axis=0, keepdims=True)
        dmixedb = (dyb * ps_ref[...] * silu_b).astype(BF16)
        dproj_ref[:, 5 * AW:6 * AW] = (dyb * mixed * ps_ref[...] * (sb * (1.0 + zp * (1.0 - sb)))).astype(BF16)
        for g in range(4):
            cols = slice(g * GD, (g + 1) * GD)
            dpw_acc[g] += _tn(t["pooled"][g], dmixedb[:, cols])
            dpooled = _nt(dmixedb[:, cols], pw_ref[g])
            q_ext[0:tm, cols] = dpooled / t["counts"][g]
            s = q_ext[:, cols]
            for k in range(g + 1):
                s = s + pltpu.roll(s, L - 2 ** k, 0)
            dproj_ref[:, 4 * AW + g * GD:4 * AW + (g + 1) * GD] = (s[0:tm] - dpooled).astype(BF16)
        dconv_ext[tm:L, :] = dconv_ext[0:HALO, :]
        q_ext[tm:L, :] = q_ext[0:HALO, :]

        dx_ref[...], dpre = _proj_bwd(dproj_ref[...], win_ref, x_ref[...], dy, pre_ref[...])
        acc1024[0:1, :] += dpre
        acc1024[1:2, :] += dpost

        @pl.when(i == nt - 1)
        def _():
            small_ref[...] = jnp.zeros_like(small_ref)
            _store_rows(small_ref, S0_PRE, acc1024[0:1, :])
            _store_rows(small_ref, S0_POST, acc1024[1:2, :])
            for q in range(4):
                for k in range(3):
                    small_ref[S0_CONV + 8 * q + k:S0_CONV + 8 * q + k + 1, :] = acc512[k:k + 1, q * GD:(q + 1) * GD]
            _store_rows(small_ref, S0_PS, acc512[3:4, :])
            for g in range(4):
                small_ref[S0_PW + g * GD:S0_PW + (g + 1) * GD, :] = dpw_acc[g]

    return pl.pallas_call(
        body, name="even_bwd", grid=(nt,),
        in_specs=[_rows(tm, D, rev), _rows(tm, D, rev), _rows(tm, W3, rev), pl.BlockSpec((HALO, W3), halo_index), _rows(tm, D, rev),
                  _full((1, D)), _full((1, D)), _full((4, D, QW)), _full((3, AW)), _full((4, GD, GD)), _full((1, AW)), _full((D, D))],
        out_specs=[_rows(tm, D, rev), _rows(tm, W3, rev), _rows(tm, D, rev), _full((S0_ROWS, GD))],
        out_shape=[jax.ShapeDtypeStruct((S, D), F32), jax.ShapeDtypeStruct((S, W3), BF16), jax.ShapeDtypeStruct((S, D), BF16),
                   jax.ShapeDtypeStruct((S0_ROWS, GD), F32)],
        scratch_shapes=[pltpu.VMEM((L, AW), F32), pltpu.VMEM((L, AW), F32), pltpu.VMEM((L, AW), F32), pltpu.VMEM((L, AW), F32),
                        pltpu.VMEM((8, D), F32), pltpu.VMEM((8, AW), F32), pltpu.VMEM((4, GD, GD), F32)],
        compiler_params=_params(),
    )(dx1, x, proj, proj, m, pre, post, win, cw, pwb, ps, wout)


def _owner_id(me, relation, c):
    q = jnp.bitwise_xor(me, relation)
    return (q // 2, q % 2, c)


def _small_gather_steps(small_ref, all_ref, stage, send_sems, recv_sems, local_sem):
    x, y, c = _position()
    chips = _chips(x, y)

    def slot(chip, core):
        return 4 * chip[0] + 2 * chip[1] + core

    def copy(k, src, block, to):
        return pltpu.make_async_remote_copy(src_ref=src, dst_ref=all_ref.at[block], send_sem=send_sems.at[k],
                                            recv_sem=recv_sems.at[k], device_id=to, device_id_type=MESH)

    def own_copy():
        return pltpu.make_async_copy(stage, all_ref.at[slot((x, y), c)], local_sem)

    def first_sends():
        mine = slot((x, y), c)
        return [copy(0, small_ref, mine, (x, y, 1 - c))] + [copy(1 + j, small_ref, mine, (*chip, c)) for j, chip in enumerate(chips)]

    def relays():
        return [copy(4 + j, all_ref.at[slot(chip, c)], slot(chip, c), (x, y, 1 - c)) for j, chip in enumerate(chips)]

    def start():
        for cp in first_sends():
            cp.start()
        load = pltpu.make_async_copy(small_ref, stage, local_sem)
        load.start()
        load.wait()
        own_copy().start()

    def relay():
        for j, chip in enumerate(chips):
            copy(1 + j, small_ref, slot(chip, c), (*chip, c)).wait_recv()
        for cp in relays():
            cp.start()

    def finish():
        copy(0, small_ref, slot((x, y), 1 - c), (x, y, 1 - c)).wait_recv()
        for j, chip in enumerate(chips):
            copy(4 + j, small_ref, slot(chip, 1 - c), (x, y, 1 - c)).wait_recv()
        for cp in first_sends() + relays():
            cp.wait_send()
        own_copy().wait()

    return start, relay, finish


def _wgrad_layer(a_out, b_out, a_in, b_in, small, pos, name):
    S = a_in.shape[0]
    nk = S // TK
    hm = D // 2
    qr = hm // 4

    def out_index(s, pos_ref):
        return (jnp.minimum(s, nk - 1), 0)

    def a_in_index(s, pos_ref):
        return (jnp.where(s >= nk, s % nk, 0), 0)

    def b_in_index(s, pos_ref):
        return (jnp.where(s >= nk, s % nk, 0), jnp.bitwise_xor(pos_ref[1], 3 - jnp.maximum(s // nk - 1, 0)))

    def body(pos_ref, ao_ref, bo_ref, a_ref, b_ref, small_ref, gout_ref, gin_ref, all_ref,
             acc, rbuf, sbuf, arr, mine, acc_o, rbuf_o, total_o, sbuf_o, arr_o, mine_o, stage,
             d2d_send, d2d_recv, ici_send, ici_recv, d2d_o_send, d2d_o_recv, ici_o_send, ici_o_recv,
             share_send, share_recv, local_sems, g_send, g_recv, g_local):
        s = pl.program_id(0)
        blk, k = s // nk, s % nk
        j = blk - 1
        x, y, c = _position()
        me = 2 * x + y
        sibling = (x, y, 1 - c)
        last = k == nk - 1
        gather_start, gather_relay, gather_finish = _small_gather_steps(small_ref, all_ref, stage, g_send, g_recv, g_local)

        def other_half(ref):
            return ref.at[pl.ds(pl.multiple_of((1 - c) * hm, hm), hm), :]

        def own_half(ref):
            return ref[pl.ds(pl.multiple_of(c * hm, hm), hm), :]

        def to_sibling(jj):
            return pltpu.make_async_remote_copy(
                src_ref=other_half(acc.at[jj % 2]), dst_ref=rbuf.at[jj], send_sem=d2d_send.at[jj], recv_sem=d2d_recv.at[jj],
                device_id=sibling, device_id_type=MESH)

        def to_owner(jj):
            return pltpu.make_async_remote_copy(
                src_ref=sbuf.at[jj], dst_ref=arr.at[2 - jj], send_sem=ici_send.at[jj], recv_sem=ici_recv.at[jj],
                device_id=_owner_id(me, 3 - jj, c), device_id_type=MESH)

        def out_to_sibling():
            return pltpu.make_async_remote_copy(
                src_ref=other_half(acc_o), dst_ref=rbuf_o, send_sem=d2d_o_send, recv_sem=d2d_o_recv,
                device_id=sibling, device_id_type=MESH)

        def out_to_owner(r):
            return pltpu.make_async_remote_copy(
                src_ref=sbuf_o.at[r], dst_ref=arr_o.at[r], send_sem=ici_o_send.at[r], recv_sem=ici_o_recv.at[r],
                device_id=_owner_id(me, r + 1, c), device_id_type=MESH)

        def pair_sum(jj):
            to_sibling(jj).wait_recv()
            return own_half(acc.at[jj % 2]) + rbuf[jj]

        def send_block(jj):
            sbuf[jj] = pair_sum(jj).astype(BF16)
            to_owner(jj).start()

        @pl.when(s == 0)
        def _():
            gather_start()

        @pl.when((blk == 3) & (k == 0))
        def _():
            gather_relay()

        @pl.when((blk == 0) & (k == 0))
        def _():
            acc_o[...] = jnp.zeros((D, D), F32)

        @pl.when(blk == 0)
        def _():
            acc_o[...] += _tn(ao_ref[...], bo_ref[...])

        @pl.when((blk == 0) & last)
        def _():
            out_to_sibling().start()

        @pl.when((blk >= 3) & (k == 0))
        def _():
            to_sibling(j - 2).wait_send()

        @pl.when((blk >= 1) & (k == 0))
        def _():
            acc[j % 2] = jnp.zeros((D, QW), F32)

        @pl.when(blk >= 1)
        def _():
            acc[j % 2] += _tn(a_ref[...], b_ref[...])

        @pl.when((blk >= 1) & last)
        def _():
            to_sibling(j).start()

        @pl.when((blk == 1) & last)
        def _():
            out_to_sibling().wait_recv()
            total_o[...] = own_half(acc_o) + rbuf_o[...]
            for r in range(3):
                q = jnp.bitwise_xor(me, r + 1)
                sbuf_o[r] = total_o[pl.ds(pl.multiple_of(q * qr, qr), qr), :].astype(BF16)
                out_to_owner(r).start()

        @pl.when((blk == 2) & last)
        def _():
            send_block(0)

        @pl.when((blk == 3) & last)
        def _():
            send_block(1)
            send_block(2)

        @pl.when((blk == 4) & last)
        def _():
            g_in = pair_sum(3)
            g_out = total_o[pl.ds(pl.multiple_of(me * qr, qr), qr), :]
            to_sibling(2).wait_send()
            to_sibling(3).wait_send()
            out_to_sibling().wait_send()
            for r in range(3):
                to_owner(r).wait()
                out_to_owner(r).wait()
            for r in range(3):
                g_in = g_in + arr[r].astype(F32)
                g_out = g_out + arr_o[r].astype(F32)
            mine[...] = g_in
            mine_o[...] = g_out
            copies = []
            for idx, (src, dst) in enumerate([(mine, gin_ref), (mine_o, gout_ref)]):
                copies.append(pltpu.make_async_remote_copy(
                    src_ref=src, dst_ref=dst.at[c], send_sem=share_send.at[idx], recv_sem=share_recv.at[idx],
                    device_id=sibling, device_id_type=MESH))
                copies.append(pltpu.make_async_copy(src, dst.at[c], local_sems.at[idx]))
            for cp in copies:
                cp.start()
            for cp in copies:
                cp.wait()
            gather_finish()

    any_spec = pl.BlockSpec(memory_space=pl.ANY)
    grid_spec = pltpu.PrefetchScalarGridSpec(
        num_scalar_prefetch=1, grid=(5 * nk,),
        in_specs=[pl.BlockSpec((TK, D), out_index), pl.BlockSpec((TK, D), out_index),
                  pl.BlockSpec((TK, D), a_in_index), pl.BlockSpec((TK, QW), b_in_index), any_spec],
        out_specs=[any_spec, any_spec, any_spec],
        scratch_shapes=[pltpu.VMEM((2, D, QW), F32), pltpu.VMEM((4, hm, QW), F32), pltpu.VMEM((3, hm, QW), BF16),
                        pltpu.VMEM((3, hm, QW), BF16), pltpu.VMEM((hm, QW), F32),
                        pltpu.VMEM((D, D), F32), pltpu.VMEM((hm, D), F32), pltpu.VMEM((hm, D), F32), pltpu.VMEM((3, qr, D), BF16),
                        pltpu.VMEM((3, qr, D), BF16), pltpu.VMEM((qr, D), F32),
                        pltpu.VMEM(small.shape, F32),
                        pltpu.SemaphoreType.DMA((4,)), pltpu.SemaphoreType.DMA((4,)),
                        pltpu.SemaphoreType.DMA((3,)), pltpu.SemaphoreType.DMA((3,)),
                        pltpu.SemaphoreType.DMA, pltpu.SemaphoreType.DMA,
                        pltpu.SemaphoreType.DMA((3,)), pltpu.SemaphoreType.DMA((3,)),
                        pltpu.SemaphoreType.DMA((2,)), pltpu.SemaphoreType.DMA((2,)), pltpu.SemaphoreType.DMA((2,)),
                        pltpu.SemaphoreType.DMA((7,)), pltpu.SemaphoreType.DMA((7,)), pltpu.SemaphoreType.DMA])
    return pl.pallas_call(
        body, name=name, grid_spec=grid_spec,
        out_shape=[jax.ShapeDtypeStruct((2, qr, D), F32), jax.ShapeDtypeStruct((2, hm, QW), F32),
                   jax.ShapeDtypeStruct((8, *small.shape), F32)],
        compiler_params=pltpu.CompilerParams(dimension_semantics=("arbitrary",), vmem_limit_bytes=VMEM_LIMIT),
    )(pos, a_out, b_out, a_in, b_in, small)


def _chips(x, y):
    return [(1 - x, y), (x, 1 - y), (1 - x, 1 - y)]


def _gather_weights(parts, split):
    n = len(parts)

    def body(*refs):
        ins, outs = refs[:n], refs[n:2 * n]
        send_sems, recv_sems, local_sems = refs[2 * n:]
        x, y, c = _position()
        me = 2 * x + y
        chips = _chips(x, y)
        local = [pltpu.make_async_copy(ins[a], outs[a].at[me], local_sems.at[a]) for a in range(n)]
        for cp in local:
            cp.start()

        def half(a, ref, who):
            rows = parts[a].shape[0] // 2
            return ref.at[pl.ds(pl.multiple_of(who * rows, 8), rows), :] if split[a] else ref

        sends = []
        for a in range(n):
            for j, chip in enumerate(chips):
                sends.append(pltpu.make_async_remote_copy(
                    src_ref=half(a, ins[a], c), dst_ref=half(a, outs[a].at[me], c),
                    send_sem=send_sems.at[a, j], recv_sem=recv_sems.at[a, j], device_id=(*chip, c), device_id_type=MESH))
        for cp in sends:
            cp.start()
        for j, chip in enumerate(chips):
            src = 2 * chip[0] + chip[1]
            for a in range(n):
                pltpu.make_async_remote_copy(
                    src_ref=half(a, ins[a], c), dst_ref=half(a, outs[a].at[src], c),
                    send_sem=send_sems.at[a, j], recv_sem=recv_sems.at[a, j], device_id=(*chip, c), device_id_type=MESH).wait_recv()
                if split[a]:
                    fwd = pltpu.make_async_remote_copy(
                        src_ref=half(a, outs[a].at[src], c), dst_ref=half(a, outs[a].at[src], c),
                        send_sem=send_sems.at[a, 3 + j], recv_sem=recv_sems.at[a, 3 + j], device_id=(x, y, 1 - c), device_id_type=MESH)
                    fwd.start()
                    sends.append(fwd)
        for j, chip in enumerate(chips):
            src = 2 * chip[0] + chip[1]
            for a in range(n):
                if split[a]:
                    pltpu.make_async_remote_copy(
                        src_ref=half(a, outs[a].at[src], 1 - c), dst_ref=half(a, outs[a].at[src], 1 - c),
                        send_sem=send_sems.at[a, 3 + j], recv_sem=recv_sems.at[a, 3 + j], device_id=(x, y, 1 - c),
                        device_id_type=MESH).wait_recv()
        for cp in sends:
            cp.wait_send()
        for cp in local:
            cp.wait()

    any_spec = pl.BlockSpec(memory_space=pl.ANY)
    return pl.pallas_call(
        body, name="gather_weights",
        in_specs=[pl.BlockSpec(memory_space=pltpu.VMEM)] * n, out_specs=[any_spec] * n,
        out_shape=[jax.ShapeDtypeStruct((4, *p.shape), p.dtype) for p in parts],
        scratch_shapes=[pltpu.SemaphoreType.DMA((n, 6)), pltpu.SemaphoreType.DMA((n, 6)), pltpu.SemaphoreType.DMA((n,))],
    )(*parts)


def _adamw(w, g, m, v):
    m = ADAM_B1 * m + (1.0 - ADAM_B1) * g
    v = ADAM_B2 * v + (1.0 - ADAM_B2) * (g * g)
    m_hat = m / (1.0 - ADAM_B1 ** ADAM_STEP)
    v_hat = v / (1.0 - ADAM_B2 ** ADAM_STEP)
    delta = -ADAM_LR * (m_hat / (jnp.sqrt(v_hat) + ADAM_EPS) + ADAM_WD * w)
    return delta, m, v


def _adamw_big(w, g, m, v, name):
    rows, cols = w.shape
    tr = 128

    def body(w_ref, g_ref, m_ref, v_ref, go_ref, d_ref, mo_ref, vo_ref):
        gv = g_ref[...]
        go_ref[...] = gv
        d_ref[...], mo_ref[...], vo_ref[...] = _adamw(w_ref[...], gv, m_ref[...], v_ref[...])

    spec = pl.BlockSpec((tr, cols), lambda i: (i, 0))
    return pl.pallas_call(
        body, name=name, grid=(rows // tr,),
        in_specs=[spec] * 4, out_specs=[spec] * 4,
        out_shape=[jax.ShapeDtypeStruct((rows, cols), F32)] * 4,
        compiler_params=pltpu.CompilerParams(dimension_semantics=("arbitrary",)),
    )(w, g, m, v)


def _adamw_small(g0, g1, weights, moms, vels):
    names = ["pre", "post", "conv", "pw", "ps", "lng", "lnb", "ws", "bs"]
    shapes = [w.shape for w in weights]

    def body(*refs):
        me = 2 * lax.axis_index("x") + lax.axis_index("y")
        g0_ref, g1_ref = refs[0], refs[1]
        w_refs, m_refs, v_refs = refs[2:11], refs[11:20], refs[20:29]
        outs = refs[29:29 + 36]
        loss_ref = refs[65]
        t0_ref, t1_ref = refs[66], refs[67]
        t0 = g0_ref[0]
        t1 = g1_ref[0]
        for d in range(1, 8):
            t0 = t0 + g0_ref[d]
            t1 = t1 + g1_ref[d]
        t0_ref[...] = t0
        t1_ref[...] = t1
        loss_ref[...] = t1_ref[S1_LOSS:S1_LOSS + 1, 0:1]
        my_conv = pl.multiple_of(S0_CONV + 8 * me, 8)
        my_ln = pl.multiple_of(S1_LN + 8 * me, 8)

        def update(idx, piece, grad):
            go, do, mo, vo = outs[4 * idx:4 * idx + 4]
            go[piece] = grad
            do[piece], mo[piece], vo[piece] = _adamw(w_refs[idx][piece], grad, m_refs[idx][piece], v_refs[idx][piece])

        for layer in range(2):
            for k in range(D // GD):
                lanes = slice(k * GD, (k + 1) * GD)
                tref, pre0, post0 = (t0_ref, S0_PRE, S0_POST) if layer == 0 else (t1_ref, S1_PRE, S1_POST)
                update(0, (slice(layer, layer + 1), lanes), tref[pre0 + k:pre0 + k + 1, :])
                update(1, (slice(layer, layer + 1), lanes), tref[post0 + k:post0 + k + 1, :])
        conv_rows = t0_ref[pl.ds(my_conv, 8), :]
        update(2, (slice(0, 3), slice(None)), conv_rows[0:3, :])
        for g in range(4):
            update(3, (g,), t0_ref[S0_PW + g * GD:S0_PW + (g + 1) * GD, :])
            update(4, (slice(0, 1), slice(g * GD, (g + 1) * GD)), t0_ref[S0_PS + g:S0_PS + g + 1, :])
        ln_rows = t1_ref[pl.ds(my_ln, 8), :]
        for k in range(2):
            update(5, (slice(0, 1), slice(k * GD, (k + 1) * GD)), ln_rows[k:k + 1, :])
            update(6, (slice(0, 1), slice(k * GD, (k + 1) * GD)), ln_rows[2 + k:3 + k, :])
        for h in range(HEADS):
            update(7, (h,), t1_ref[S1_WS + h * CHUNK:S1_WS + (h + 1) * CHUNK, :])
        update(8, (slice(None), slice(None)), t1_ref[S1_BS:S1_BS + HEADS, :])

    vm = pl.BlockSpec(memory_space=pltpu.VMEM)
    out_shape = []
    for s in shapes:
        out_shape += [jax.ShapeDtypeStruct(s, F32)] * 4
    out_shape.append(jax.ShapeDtypeStruct((1, 1), F32))
    res = pl.pallas_call(
        body, name="adamw_small",
        in_specs=[vm] * 29, out_specs=[vm] * 37, out_shape=out_shape,
        scratch_shapes=[pltpu.VMEM((S0_ROWS, GD), F32), pltpu.VMEM((S1_ROWS, GD), F32)],
        compiler_params=pltpu.CompilerParams(vmem_limit_bytes=VMEM_LIMIT),
    )(g0, g1, *weights, *moms, *vels)
    per_weight = {nm: res[4 * i:4 * i + 4] for i, nm in enumerate(names)}
    return per_weight, res[36]


def _pad8(a):
    return jnp.pad(a, ((0, 8 - a.shape[0]), (0, 0)))


def kernel(x, pre_norm, post_norm, even_w_in, even_conv_w, even_pool_w, even_pool_scale, even_w_out, odd_w_in, odd_ln_g, odd_ln_b, odd_w_s, odd_b_s, odd_w_out, loss_target, m_pre_norm, m_post_norm, m_even_w_in, m_even_conv_w, m_even_pool_w, m_even_pool_scale, m_even_w_out, m_odd_w_in, m_odd_ln_g, m_odd_ln_b, m_odd_w_s, m_odd_b_s, m_odd_w_out, v_pre_norm, v_post_norm, v_even_w_in, v_even_conv_w, v_even_pool_w, v_even_pool_scale, v_even_w_out, v_odd_w_in, v_odd_ln_g, v_odd_ln_b, v_odd_w_s, v_odd_b_s, v_odd_w_out):
    xs = x[0]
    tgt = loss_target[0]

    small_shard = jnp.concatenate([_pad8(even_conv_w[0]), _pad8(odd_ln_g.reshape(2, GD)), _pad8(odd_ln_b.reshape(2, GD))], axis=0)
    win0, wout0, shard = _gather_weights([even_w_in[0].astype(BF16), even_w_out[0].astype(BF16), small_shard], [True, True, False])
    wout0 = wout0.reshape(D, D)
    px, py, pc = _position()
    pos = jnp.stack([pc, 2 * px + py]).astype(jnp.int32)
    conv_w = shard[:, 0:3, :].transpose(1, 0, 2).reshape(3, AW)
    ln_g = shard[:, 8:10, :].reshape(1, D)
    ln_b = shard[:, 16:18, :].reshape(1, D)
    pool_wb = even_pool_w[0].astype(BF16)
    ws_tril = jnp.tril(odd_w_s[0]).astype(BF16)
    ws_tril_t = jnp.swapaxes(ws_tril, 1, 2)
    bias = jnp.broadcast_to(odd_b_s[0][:, :, None], (HEADS, CHUNK, GD))
    pre0, pre1 = pre_norm[0:1], pre_norm[1:2]
    post0, post1 = post_norm[0:1], post_norm[1:2]

    x1, proj0, m0, hb0, mixp0, win1, wout1 = _even_fwd(
        xs, pre0, post0, win0, conv_w, pool_wb, even_pool_scale, wout0, [odd_w_in[0].astype(BF16), odd_w_out[0].astype(BF16)])
    wout1 = wout1.reshape(D, D)
    proj1, m1, hb1, yp1, dx2, loss_part = _odd_fwd(x1, tgt, pre1, post1, win1, ln_g, ln_b, ws_tril, bias, wout1)
    dx1, dproj1, dmb1, small1 = _odd_bwd(dx2, x1, proj1, m1, loss_part, pre1, post1, win1, ln_g, ln_b, ws_tril, ws_tril_t, bias, wout1)
    g_out1, g_in1, all1 = _wgrad_layer(yp1, dmb1, hb1, dproj1, small1, pos, "wgrad_odd")
    gx, dproj0, dmb0, small0 = _even_bwd(dx1, xs, proj0, m0, pre0, post0, win0, conv_w, pool_wb, even_pool_scale, wout0)
    g_out0, g_in0, all0 = _wgrad_layer(mixp0, dmb0, hb0, dproj0, small0, pos, "wgrad_even")

    grads = [g_in0, g_out0, g_in1, g_out1]
    tags = ["even_in", "even_out", "odd_in", "odd_out"]
    big_w = [even_w_in[0], even_w_out[0], odd_w_in[0], odd_w_out[0]]
    big_m = [m_even_w_in[0], m_even_w_out[0], m_odd_w_in[0], m_odd_w_out[0]]
    big_v = [v_even_w_in[0], v_even_w_out[0], v_odd_w_in[0], v_odd_w_out[0]]
    big = [_adamw_big(w, g.reshape(w.shape), m, v, "adamw_" + t) for g, w, m, v, t in zip(grads, big_w, big_m, big_v, tags)]

    small_w = [pre_norm, post_norm, even_conv_w[0], even_pool_w[0], even_pool_scale, odd_ln_g, odd_ln_b, odd_w_s[0], odd_b_s[0]]
    small_m = [m_pre_norm, m_post_norm, m_even_conv_w[0], m_even_pool_w[0], m_even_pool_scale, m_odd_ln_g, m_odd_ln_b, m_odd_w_s[0], m_odd_b_s[0]]
    small_v = [v_pre_norm, v_post_norm, v_even_conv_w[0], v_even_pool_w[0], v_even_pool_scale, v_odd_ln_g, v_odd_ln_b, v_odd_w_s[0], v_odd_b_s[0]]
    sm, loss = _adamw_small(all0, all1, small_w, small_m, small_v)

    def lead(a):
        return a[None]

    per = {
        "pre_norm": sm["pre"], "post_norm": sm["post"],
        "even_w_in": [lead(a) for a in big[0]], "even_conv_w": [lead(a) for a in sm["conv"]],
        "even_pool_w": [lead(a) for a in sm["pw"]], "even_pool_scale": sm["ps"],
        "even_w_out": [lead(a) for a in big[1]], "odd_w_in": [lead(a) for a in big[2]],
        "odd_ln_g": sm["lng"], "odd_ln_b": sm["lnb"],
        "odd_w_s": [lead(a) for a in sm["ws"]], "odd_b_s": [lead(a) for a in sm["bs"]],
        "odd_w_out": [lead(a) for a in big[3]],
    }
    order = ["pre_norm", "post_norm", "even_w_in", "even_conv_w", "even_pool_w", "even_pool_scale", "even_w_out", "odd_w_in",
             "odd_ln_g", "odd_ln_b", "odd_w_s", "odd_b_s", "odd_w_out"]
    outs = [loss.reshape(()), gx[None]]
    for kind in range(4):
        outs += [per[nm][kind] for nm in order]
    return tuple(outs)
```

```python
import functools

import jax
import jax.numpy as jnp
from jax import lax
from jax.experimental import pallas as pl
from jax.experimental.pallas import tpu as pltpu

F32 = jnp.float32
BF16 = jnp.bfloat16
MESH = pl.DeviceIdType.MESH

D = 1024
W3 = 3 * D
QW = W3 // 4
AW = 512
GD = 128
CHUNK = 128
HEADS = 8
HALO = 16
POOL_WINDOWS = (2, 4, 8, 16)
EPS = 1e-6
TM_FWD = 512
TM_BWD = 256
TK = 1024
TK_IN = 2048
VMEM_LIMIT = 56 * 1024 * 1024

ADAM_LR, ADAM_B1, ADAM_B2, ADAM_EPS, ADAM_WD, ADAM_STEP = 0.001, 0.9, 0.999, 1e-08, 0.01, 10

S0_PRE, S0_POST, S0_CONV, S0_PS, S0_PW, S0_ROWS = 0, 8, 16, 48, 56, 568
S1_PRE, S1_POST, S1_LN, S1_BS, S1_WS, S1_LOSS, S1_ROWS = 0, 8, 16, 48, 56, 1080, 1088


def _nn(a, b):
    return jnp.dot(a, b, preferred_element_type=F32)


def _nt(a, b):
    return lax.dot_general(a, b, (((1,), (1,)), ((), ())), preferred_element_type=F32)


def _tn(a, b):
    return lax.dot_general(a, b, (((0,), (0,)), ((), ())), preferred_element_type=F32)


def _sigmoid(z):
    return 1.0 / (1.0 + jnp.exp(-z))


def _rms_fwd(x, g):
    r = lax.rsqrt(jnp.mean(x * x, axis=-1, keepdims=True) + EPS)
    return x * r, r


def _rms_bwd(dy, xh, r, g):
    dn = dy * g
    dx = r * (dn - xh * jnp.mean(xh * dn, axis=-1, keepdims=True))
    return dx, jnp.sum(dy * xh, axis=0, keepdims=True)


def _full(shape):
    nd = len(shape)
    return pl.BlockSpec(shape, lambda i, _n=nd: (0,) * _n, pipeline_mode=pl.Buffered(1))


def _full_out(shape):
    nd = len(shape)
    return pl.BlockSpec(shape, lambda i, _n=nd: (0,) * _n)


def _rows(tm, width, index=None):
    return pl.BlockSpec((tm, width), (lambda i: (i, 0)) if index is None else index)


def _params():
    return pltpu.CompilerParams(dimension_semantics=("arbitrary",), vmem_limit_bytes=VMEM_LIMIT)


def _position():
    x, y, c = lax.axis_index("x"), lax.axis_index("y"), lax.axis_index("c")
    return x, y, c


def _even_mix(proj_ref, hc_ext, xp_ext, cw_ref, pw_ref, ps_ref, first_row):
    tm = proj_ref.shape[0]
    xa = proj_ref[:, 0:AW]
    gb = proj_ref[:, AW:2 * AW]
    gc = proj_ref[:, 2 * AW:3 * AW]
    za = proj_ref[:, 3 * AW:4 * AW]
    xp = proj_ref[:, 4 * AW:5 * AW]
    zp = proj_ref[:, 5 * AW:6 * AW]
    hc = gc * xa
    hc_ext[HALO:, :] = hc
    e = hc_ext[...]
    conv = cw_ref[2:3, :] * hc + cw_ref[1:2, :] * pltpu.roll(e, 1, 0)[HALO:] + cw_ref[0:1, :] * pltpu.roll(e, 2, 0)[HALO:]
    sa = _sigmoid(za)
    xp_ext[HALO:, :] = xp
    pos = first_row + lax.broadcasted_iota(jnp.int32, (tm, 1), 0)
    pooled, mixed, counts = [], [], []
    for g, w in enumerate(POOL_WINDOWS):
        cols = slice(g * GD, (g + 1) * GD)
        s = xp_ext[:, cols]
        for k in range(g + 1):
            s = s + pltpu.roll(s, 2 ** k, 0)
        count = jnp.minimum(pos + 1, w).astype(F32)
        pg = s[HALO:] / count - xp[:, cols]
        pooled.append(pg.astype(BF16))
        mixed.append(_nn(pooled[-1], pw_ref[g]))
        counts.append(count)
    mixed = jnp.concatenate(mixed, axis=-1)
    sb = _sigmoid(zp)
    return dict(xa=xa, gb=gb, gc=gc, za=za, zp=zp, hc=hc, conv=conv, sa=sa, sb=sb, pooled=pooled, mixed=mixed, counts=counts)


def _half_rows(ref, rows, who):
    return ref.at[pl.ds(pl.multiple_of(who * (rows // 2), 8), rows // 2), :]


def _store_permuted(ref, value):
    for ob in range(D // GD):
        nb = 4 * (ob % 2) + ob // 2
        ref[:, nb * GD:(nb + 1) * GD] = value[:, ob * GD:(ob + 1) * GD]


def _even_fwd(x, pre, post, win, cw, pwb, ps, wout, next_shards):
    S = x.shape[0]
    tm = TM_FWD
    nt = S // tm
    relay = (3 * nt) // 4
    n = len(next_shards)
    shard_rows = [p.shape[0] for p in next_shards]

    def body(x_ref, pre_ref, post_ref, win_ref, cw_ref, pw_ref, ps_ref, wout_ref, *rest):
        shard_refs, rest = rest[:n], rest[n:]
        x1_ref, proj_ref, m_ref, hb_ref, mixp_ref = rest[:5]
        full_refs, rest = rest[5:5 + n], rest[5 + n:]
        hc_ext, xp_ext, mix_sc = rest[:3]
        stage, rest = rest[3:3 + n], rest[3 + n:]
        send_sems, recv_sems, local_sems = rest
        i = pl.program_id(0)
        px, py, pc = _position()
        me = 2 * px + py
        chips = _chips(px, py)

        def ici(a, j):
            return pltpu.make_async_remote_copy(
                src_ref=_half_rows(shard_refs[a], shard_rows[a], pc), dst_ref=_half_rows(full_refs[a].at[me], shard_rows[a], pc),
                send_sem=send_sems.at[a, j], recv_sem=recv_sems.at[a, j], device_id=(*chips[j], pc), device_id_type=MESH)

        def ici_arrival(a, j):
            src = 2 * chips[j][0] + chips[j][1]
            return pltpu.make_async_remote_copy(
                src_ref=_half_rows(shard_refs[a], shard_rows[a], pc), dst_ref=_half_rows(full_refs[a].at[src], shard_rows[a], pc),
                send_sem=send_sems.at[a, j], recv_sem=recv_sems.at[a, j], device_id=(*chips[j], pc), device_id_type=MESH)

        def relay_copy(a, j, who):
            src = 2 * chips[j][0] + chips[j][1]
            region = _half_rows(full_refs[a].at[src], shard_rows[a], who)
            return pltpu.make_async_remote_copy(
                src_ref=region, dst_ref=region, send_sem=send_sems.at[a, 3 + j], recv_sem=recv_sems.at[a, 3 + j],
                device_id=(px, py, 1 - pc), device_id_type=MESH)

        def own_copy(a):
            return pltpu.make_async_copy(stage[a], full_refs[a].at[me], local_sems.at[a])

        @pl.when(i == 0)
        def _():
            hc_ext[0:HALO, :] = jnp.zeros((HALO, AW), F32)
            xp_ext[0:HALO, :] = jnp.zeros((HALO, AW), F32)
            for a in range(n):
                for j in range(3):
                    ici(a, j).start()
            for a in range(n):
                load = pltpu.make_async_copy(shard_refs[a], stage[a], local_sems.at[a])
                load.start()
                load.wait()
                own_copy(a).start()

        @pl.when(i == relay)
        def _():
            for j in range(3):
                for a in range(n):
                    ici_arrival(a, j).wait_recv()
                    relay_copy(a, j, pc).start()

        xv = x_ref[...]
        xh, _ = _rms_fwd(xv, None)
        hb = (xh * pre_ref[...]).astype(BF16)
        hb_ref[...] = hb
        for q in range(4):
            proj_ref[:, q * QW:(q + 1) * QW] = _nn(hb, win_ref[q])
        t = _even_mix(proj_ref, hc_ext, xp_ext, cw_ref, pw_ref, ps_ref, i * tm)
        mix_sc[:, 0:AW] = (t["gb"] * t["conv"] * (t["za"] * t["sa"])).astype(BF16)
        mix_sc[:, AW:2 * AW] = (t["mixed"] * ps_ref[...] * (t["zp"] * t["sb"])).astype(BF16)
        mix = mix_sc[...]
        _store_permuted(mixp_ref, mix)
        m = _nn(mix, wout_ref[...])
        m_ref[...] = m
        mh, _ = _rms_fwd(m, None)
        x1_ref[...] = xv + mh * post_ref[...]
        hc_ext[0:HALO, :] = hc_ext[tm:tm + HALO, :]
        xp_ext[0:HALO, :] = xp_ext[tm:tm + HALO, :]

        @pl.when(i == nt - 1)
        def _():
            for j in range(3):
                for a in range(n):
                    relay_copy(a, j, 1 - pc).wait_recv()
            for a in range(n):
                for j in range(3):
                    ici(a, j).wait_send()
                    relay_copy(a, j, pc).wait_send()
                own_copy(a).wait()

    any_spec = pl.BlockSpec(memory_space=pl.ANY)
    return pl.pallas_call(
        body, name="even_fwd", grid=(nt,),
        in_specs=[_rows(tm, D), _full((1, D)), _full((1, D)), _full((4, D, QW)), _full((3, AW)), _full((4, GD, GD)),
                  _full((1, AW)), _full((D, D))] + [any_spec] * n,
        out_specs=[_rows(tm, D), _rows(tm, W3), _rows(tm, D), _rows(tm, D), _rows(tm, D)] + [any_spec] * n,
        out_shape=[jax.ShapeDtypeStruct((S, D), F32), jax.ShapeDtypeStruct((S, W3), F32), jax.ShapeDtypeStruct((S, D), F32),
                   jax.ShapeDtypeStruct((S, D), BF16), jax.ShapeDtypeStruct((S, D), BF16)]
        + [jax.ShapeDtypeStruct((4, *p.shape), p.dtype) for p in next_shards],
        scratch_shapes=[pltpu.VMEM((tm + HALO, AW), F32), pltpu.VMEM((tm + HALO, AW), F32), pltpu.VMEM((tm, D), BF16)]
        + [pltpu.VMEM(p.shape, p.dtype) for p in next_shards]
        + [pltpu.SemaphoreType.DMA((n, 6)), pltpu.SemaphoreType.DMA((n, 6)), pltpu.SemaphoreType.DMA((n,))],
        compiler_params=_params(),
    )(x, pre, post, win, cw, pwb, ps, wout, *next_shards)


def _chunks_side_by_side(a, h):
    return jnp.concatenate([a[n * CHUNK:(n + 1) * CHUNK, h * GD:(h + 1) * GD] for n in range(a.shape[0] // CHUNK)], axis=1)


def _odd_mix(proj_ref, lng_ref, lnb_ref, ws_ref, bias_ref, sv_ref):
    tm = proj_ref.shape[0]
    u = proj_ref[:, 0:D]
    v = proj_ref[:, D:2 * D]
    z = proj_ref[:, 2 * D:3 * D]
    mu = jnp.mean(v, axis=-1, keepdims=True)
    vc = v - mu
    rs = lax.rsqrt(jnp.mean(vc * vc, axis=-1, keepdims=True) + EPS)
    vh = vc * rs
    vnb = (vh * lng_ref[...] + lnb_ref[...]).astype(BF16)
    for h in range(HEADS):
        sv = _nn(ws_ref[h], _chunks_side_by_side(vnb, h))
        for n in range(tm // CHUNK):
            sv_ref[n * CHUNK:(n + 1) * CHUNK, h * GD:(h + 1) * GD] = sv[:, n * GD:(n + 1) * GD] + bias_ref[h]
    return dict(u=u, z=z, vh=vh, rs=rs, vnb=vnb, sz=_sigmoid(z))


def _odd_fwd(x1, tgt, pre, post, win, lng, lnb, wsb, bias, wout):
    S = x1.shape[0]
    tm = TM_FWD
    nt = S // tm

    def body(x_ref, tgt_ref, pre_ref, post_ref, win_ref, lng_ref, lnb_ref, ws_ref, bias_ref, wout_ref,
             proj_ref, m_ref, hb_ref, yb_ref, dx2_ref, loss_ref, sv_ref):
        i = pl.program_id(0)

        @pl.when(i == 0)
        def _():
            loss_ref[...] = jnp.zeros((8, GD), F32)

        xv = x_ref[...]
        xh, _ = _rms_fwd(xv, None)
        hb = (xh * pre_ref[...]).astype(BF16)
        hb_ref[...] = hb
        for q in range(4):
            proj_ref[:, q * QW:(q + 1) * QW] = _nn(hb, win_ref[q])
        t = _odd_mix(proj_ref, lng_ref, lnb_ref, ws_ref, bias_ref, sv_ref)
        yb = (t["u"] * sv_ref[...] * (t["z"] * t["sz"])).astype(BF16)
        _store_permuted(yb_ref, yb)
        m = _nn(yb, wout_ref[...])
        m_ref[...] = m
        mh, _ = _rms_fwd(m, None)
        err = xv + mh * post_ref[...] - tgt_ref[...]
        dx2_ref[...] = err * (1.0 / D)
        part = 0.5 * jnp.sum(jnp.mean(err * err, axis=-1, keepdims=True), axis=0, keepdims=True)
        loss_ref[...] += jnp.broadcast_to(part, (8, GD))

    return pl.pallas_call(
        body, name="odd_fwd", grid=(nt,),
        in_specs=[_rows(tm, D), _rows(tm, D), _full((1, D)), _full((1, D)), _full((4, D, QW)), _full((1, D)), _full((1, D)),
                  _full((HEADS, CHUNK, CHUNK)), _full((HEADS, CHUNK, GD)), _full((D, D))],
        out_specs=[_rows(tm, W3), _rows(tm, D), _rows(tm, D), _rows(tm, D), _rows(tm, D), _full_out((8, GD))],
        out_shape=[jax.ShapeDtypeStruct((S, W3), F32), jax.ShapeDtypeStruct((S, D), F32), jax.ShapeDtypeStruct((S, D), BF16),
                   jax.ShapeDtypeStruct((S, D), BF16), jax.ShapeDtypeStruct((S, D), F32), jax.ShapeDtypeStruct((8, GD), F32)],
        scratch_shapes=[pltpu.VMEM((tm, D), F32)],
        compiler_params=_params(),
    )(x1, tgt, pre, post, win, lng, lnb, wsb, bias, wout)


def _store_rows(ref, row0, value):
    r, width = value.shape
    for a in range(r):
        for k in range(width // GD):
            ref[row0 + a * (width // GD) + k:row0 + a * (width // GD) + k + 1, :] = value[a:a + 1, k * GD:(k + 1) * GD]


def _proj_bwd(dproj, win_ref, x, dy, pre):
    dh = _nt(dproj[:, 0:QW], win_ref[0])
    for q in range(1, 4):
        dh += _nt(dproj[:, q * QW:(q + 1) * QW], win_ref[q])
    xh, r = _rms_fwd(x, None)
    dxn, dpre = _rms_bwd(dh, xh, r, pre)
    return dy + dxn, dpre


def _odd_bwd(dx2, x1, proj, m, loss, pre, post, win, lng, lnb, wsb, wsbt, bias, wout):
    S = x1.shape[0]
    tm = TM_BWD
    nt = S // tm

    def body(dy_ref, x_ref, proj_ref, m_ref, loss_ref, pre_ref, post_ref, win_ref, lng_ref, lnb_ref, ws_ref, wst_ref, bias_ref,
             wout_ref, dx_ref, dproj_ref, dmb_ref, small_ref, sv_ref, dvn_ref, acc1024, dws_acc, dbs_acc):
        i = pl.program_id(0)

        @pl.when(i == 0)
        def _():
            acc1024[...] = jnp.zeros_like(acc1024)
            dws_acc[...] = jnp.zeros_like(dws_acc)
            dbs_acc[...] = jnp.zeros_like(dbs_acc)

        dy = dy_ref[...]
        mh, rm = _rms_fwd(m_ref[...], None)
        dm, dpost = _rms_bwd(dy, mh, rm, post_ref[...])
        dmb = dm.astype(BF16)
        dmb_ref[...] = dmb
        dyv = _nt(dmb, wout_ref[...])
        t = _odd_mix(proj_ref, lng_ref, lnb_ref, ws_ref, bias_ref, sv_ref)
        u, z, sz, sv = t["u"], t["z"], t["sz"], sv_ref[...]
        dproj_ref[:, 0:D] = (dyv * sv * (z * sz)).astype(BF16)
        dproj_ref[:, 2 * D:3 * D] = (dyv * u * sv * (sz * (1.0 + z * (1.0 - sz)))).astype(BF16)
        dsv = dyv * u * (z * sz)
        dsvb = dsv.astype(BF16)
        for h in range(HEADS):
            dsv_h = _chunks_side_by_side(dsvb, h)
            dvn_h = _nn(wst_ref[h], dsv_h)
            dws_acc[h] += _nt(dsv_h, _chunks_side_by_side(t["vnb"], h))
            for n in range(tm // CHUNK):
                rows, cols = slice(n * CHUNK, (n + 1) * CHUNK), slice(h * GD, (h + 1) * GD)
                dvn_ref[rows, cols] = dvn_h[:, n * GD:(n + 1) * GD]
                dbs_acc[h] += dsv[rows, cols]
        dvn = dvn_ref[...]
        vh = t["vh"]
        dvh = dvn * lng_ref[...]
        dv = t["rs"] * (dvh - jnp.mean(dvh, axis=-1, keepdims=True) - vh * jnp.mean(dvh * vh, axis=-1, keepdims=True))
        dproj_ref[:, D:2 * D] = dv.astype(BF16)
        dx_ref[...], dpre = _proj_bwd(dproj_ref[...], win_ref, x_ref[...], dy, pre_ref[...])
        acc1024[0:1, :] += dpre
        acc1024[1:2, :] += dpost
        acc1024[2:3, :] += jnp.sum(dvn * vh, axis=0, keepdims=True)
        acc1024[3:4, :] += jnp.sum(dvn, axis=0, keepdims=True)

        @pl.when(i == nt - 1)
        def _():
            small_ref[...] = jnp.zeros_like(small_ref)
            _store_rows(small_ref, S1_PRE, acc1024[0:1, :])
            _store_rows(small_ref, S1_POST, acc1024[1:2, :])
            for q in range(4):
                _store_rows(small_ref, S1_LN + 8 * q, acc1024[2:3, 2 * q * GD:(2 * q + 2) * GD])
                _store_rows(small_ref, S1_LN + 8 * q + 2, acc1024[3:4, 2 * q * GD:(2 * q + 2) * GD])
            lower = lax.broadcasted_iota(jnp.int32, (CHUNK, CHUNK), 0) >= lax.broadcasted_iota(jnp.int32, (CHUNK, CHUNK), 1)
            for h in range(HEADS):
                small_ref[S1_WS + h * CHUNK:S1_WS + (h + 1) * CHUNK, :] = jnp.where(lower, dws_acc[h], 0.0)
                small_ref[S1_BS + h:S1_BS + h + 1, :] = jnp.sum(dbs_acc[h].T, axis=0, keepdims=True)
            small_ref[S1_LOSS:S1_LOSS + 8, :] = loss_ref[...]

    return pl.pallas_call(
        body, name="odd_bwd", grid=(nt,),
        in_specs=[_rows(tm, D), _rows(tm, D), _rows(tm, W3), _rows(tm, D), _full((8, GD)), _full((1, D)), _full((1, D)),
                  _full((4, D, QW)), _full((1, D)), _full((1, D)), _full((HEADS, CHUNK, CHUNK)), _full((HEADS, CHUNK, CHUNK)),
                  _full((HEADS, CHUNK, GD)), _full((D, D))],
        out_specs=[_rows(tm, D), _rows(tm, W3), _rows(tm, D), _full_out((S1_ROWS, GD))],
        out_shape=[jax.ShapeDtypeStruct((S, D), F32), jax.ShapeDtypeStruct((S, W3), BF16), jax.ShapeDtypeStruct((S, D), BF16),
                   jax.ShapeDtypeStruct((S1_ROWS, GD), F32)],
        scratch_shapes=[pltpu.VMEM((tm, D), F32), pltpu.VMEM((tm, D), F32), pltpu.VMEM((8, D), F32),
                        pltpu.VMEM((HEADS, CHUNK, CHUNK), F32), pltpu.VMEM((HEADS, CHUNK, GD), F32)],
        compiler_params=_params(),
    )(dx2, x1, proj, m, loss, pre, post, win, lng, lnb, wsb, wsbt, bias, wout)


def _even_bwd(dx1, x, proj, m, pre, post, win, cw, pwb, ps, wout):
    S = x.shape[0]
    tm = TM_BWD
    nt = S // tm
    L = tm + HALO

    def rev(i):
        return (nt - 1 - i, 0)

    def halo_index(i):
        return (jnp.maximum((nt - 1 - i) * (tm // HALO) - 1, 0), 0)

    def body(dy_ref, x_ref, proj_ref, halo_ref, m_ref, pre_ref, post_ref, win_ref, cw_ref, pw_ref, ps_ref, wout_ref,
             dx_ref, dproj_ref, dmb_ref, small_ref, hc_ext, xp_ext, dconv_ext, q_ext, acc1024, acc512, dpw_acc):
        i = pl.program_id(0)
        tile = nt - 1 - i

        @pl.when(i == 0)
        def _():
            dconv_ext[tm:L, :] = jnp.zeros((HALO, AW), F32)
            q_ext[tm:L, :] = jnp.zeros((HALO, AW), F32)
            acc1024[...] = jnp.zeros_like(acc1024)
            acc512[...] = jnp.zeros_like(acc512)
            dpw_acc[...] = jnp.zeros_like(dpw_acc)

        keep = (tile > 0).astype(F32)
        hc_ext[0:HALO, :] = halo_ref[:, 2 * AW:3 * AW] * halo_ref[:, 0:AW] * keep
        xp_ext[0:HALO, :] = halo_ref[:, 4 * AW:5 * AW] * keep

        dy = dy_ref[...]
        mh, rm = _rms_fwd(m_ref[...], None)
        dm, dpost = _rms_bwd(dy, mh, rm, post_ref[...])
        dmb = dm.astype(BF16)
        dmb_ref[...] = dmb
        dmix = _nt(dmb, wout_ref[...])
        dya, dyb = dmix[:, 0:AW], dmix[:, AW:2 * AW]
        t = _even_mix(proj_ref, hc_ext, xp_ext, cw_ref, pw_ref, ps_ref, tile * tm)
        gb, conv, za, sa, hc = t["gb"], t["conv"], t["za"], t["sa"], t["hc"]
        silu_a = za * sa
        dproj_ref[:, AW:2 * AW] = (dya * conv * silu_a).astype(BF16)
        dproj_ref[:, 3 * AW:4 * AW] = (dya * gb * conv * (sa * (1.0 + za * (1.0 - sa)))).astype(BF16)
        dconv = dya * gb * silu_a
        dconv_ext[0:tm, :] = dconv
        e = dconv_ext[...]
        dc1 = pltpu.roll(e, L - 1, 0)[0:tm]
        dc2 = pltpu.roll(e, L - 2, 0)[0:tm]
        dhc = cw_ref[2:3, :] * dconv + cw_ref[1:2, :] * dc1 + cw_ref[0:1, :] * dc2
        dproj_ref[:, 0:AW] = (dhc * t["gc"]).astype(BF16)
        dproj_ref[:, 2 * AW:3 * AW] = (dhc * t["xa"]).astype(BF16)
        acc512[0:1, :] += jnp.sum(dc2 * hc, axis=0, keepdims=True)
        acc512[1:2, :] += jnp.sum(dc1 * hc, axis=0, keepdims=True)
        acc512[2:3, :] += jnp.sum(dconv * hc, axis=0, keepdims=True)

        zp, sb, mixed = t["zp"], t["sb"], t["mixed"]
        silu_b = zp * sb
        acc512[3:4, :] += jnp.sum(dyb * mixed * silu_b, axis=0, keepdims=True)
        dmixedb = (dyb * ps_ref[...] * silu_b).astype(BF16)
        dproj_ref[:, 5 * AW:6 * AW] = (dyb * mixed * ps_ref[...] * (sb * (1.0 + zp * (1.0 - sb)))).astype(BF16)
        for g in range(4):
            cols = slice(g * GD, (g + 1) * GD)
            dpw_acc[g] += _tn(t["pooled"][g], dmixedb[:, cols])
            dpooled = _nt(dmixedb[:, cols], pw_ref[g])
            q_ext[0:tm, cols] = dpooled / t["counts"][g]
            s = q_ext[:, cols]
            for k in range(g + 1):
                s = s + pltpu.roll(s, L - 2 ** k, 0)
            dproj_ref[:, 4 * AW + g * GD:4 * AW + (g + 1) * GD] = (s[0:tm] - dpooled).astype(BF16)
        dconv_ext[tm:L, :] = dconv_ext[0:HALO, :]
        q_ext[tm:L, :] = q_ext[0:HALO, :]

        dx_ref[...], dpre = _proj_bwd(dproj_ref[...], win_ref, x_ref[...], dy, pre_ref[...])
        acc1024[0:1, :] += dpre
        acc1024[1:2, :] += dpost

        @pl.when(i == nt - 1)
        def _():
            small_ref[...] = jnp.zeros_like(small_ref)
            _store_rows(small_ref, S0_PRE, acc1024[0:1, :])
            _store_rows(small_ref, S0_POST, acc1024[1:2, :])
            for q in range(4):
                for k in range(3):
                    small_ref[S0_CONV + 8 * q + k:S0_CONV + 8 * q + k + 1, :] = acc512[k:k + 1, q * GD:(q + 1) * GD]
            _store_rows(small_ref, S0_PS, acc512[3:4, :])
            for g in range(4):
                small_ref[S0_PW + g * GD:S0_PW + (g + 1) * GD, :] = dpw_acc[g]

    return pl.pallas_call(
        body, name="even_bwd", grid=(nt,),
        in_specs=[_rows(tm, D, rev), _rows(tm, D, rev), _rows(tm, W3, rev), pl.BlockSpec((HALO, W3), halo_index), _rows(tm, D, rev),
                  _full((1, D)), _full((1, D)), _full((4, D, QW)), _full((3, AW)), _full((4, GD, GD)), _full((1, AW)), _full((D, D))],
        out_specs=[_rows(tm, D, rev), _rows(tm, W3, rev), _rows(tm, D, rev), _full_out((S0_ROWS, GD))],
        out_shape=[jax.ShapeDtypeStruct((S, D), F32), jax.ShapeDtypeStruct((S, W3), BF16), jax.ShapeDtypeStruct((S, D), BF16),
                   jax.ShapeDtypeStruct((S0_ROWS, GD), F32)],
        scratch_shapes=[pltpu.VMEM((L, AW), F32), pltpu.VMEM((L, AW), F32), pltpu.VMEM((L, AW), F32), pltpu.VMEM((L, AW), F32),
                        pltpu.VMEM((8, D), F32), pltpu.VMEM((8, AW), F32), pltpu.VMEM((4, GD, GD), F32)],
        compiler_params=_params(),
    )(dx1, x, proj, proj, m, pre, post, win, cw, pwb, ps, wout)


def _owner_id(me, relation, c):
    q = jnp.bitwise_xor(me, relation)
    return (q // 2, q % 2, c)


def _small_gather_steps(small_ref, all_ref, stage, send_sems, recv_sems, local_sem):
    x, y, c = _position()
    chips = _chips(x, y)

    def slot(chip, core):
        return 4 * chip[0] + 2 * chip[1] + core

    def copy(k, src, block, to):
        return pltpu.make_async_remote_copy(src_ref=src, dst_ref=all_ref.at[block], send_sem=send_sems.at[k],
                                            recv_sem=recv_sems.at[k], device_id=to, device_id_type=MESH)

    def own_copy():
        return pltpu.make_async_copy(stage, all_ref.at[slot((x, y), c)], local_sem)

    def first_sends():
        mine = slot((x, y), c)
        return [copy(0, small_ref, mine, (x, y, 1 - c))] + [copy(1 + j, small_ref, mine, (*chip, c)) for j, chip in enumerate(chips)]

    def relays():
        return [copy(4 + j, all_ref.at[slot(chip, c)], slot(chip, c), (x, y, 1 - c)) for j, chip in enumerate(chips)]

    def start():
        for cp in first_sends():
            cp.start()
        load = pltpu.make_async_copy(small_ref, stage, local_sem)
        load.start()
        load.wait()
        own_copy().start()

    def relay():
        for j, chip in enumerate(chips):
            copy(1 + j, small_ref, slot(chip, c), (*chip, c)).wait_recv()
        for cp in relays():
            cp.start()

    def finish():
        copy(0, small_ref, slot((x, y), 1 - c), (x, y, 1 - c)).wait_recv()
        for j, chip in enumerate(chips):
            copy(4 + j, small_ref, slot(chip, 1 - c), (x, y, 1 - c)).wait_recv()
        for cp in first_sends() + relays():
            cp.wait_send()
        own_copy().wait()

    return start, relay, finish


def _wgrad_layer(a_out, b_out, a_in, b_in, small, pos, name):
    S = a_in.shape[0]
    nko = S // TK
    nki = S // TK_IN
    hm = D // 2
    qr = hm // 4

    def out_index(s, pos_ref):
        return (jnp.minimum(s, nko - 1), 0)

    def a_in_index(s, pos_ref):
        return (jnp.where(s >= nko, (s - nko) % nki, 0), 0)

    def b_in_index(s, pos_ref):
        return (jnp.where(s >= nko, (s - nko) % nki, 0), jnp.bitwise_xor(pos_ref[1], 3 - jnp.maximum((s - nko) // nki, 0)))

    def body(pos_ref, ao_ref, bo_ref, a_ref, b_ref, small_ref, gout_ref, gin_ref, all_ref,
             acc, rbuf, sbuf, arr, mine, acc_o, rbuf_o, total_o, sbuf_o, arr_o, mine_o, stage,
             d2d_send, d2d_recv, ici_send, ici_recv, d2d_o_send, d2d_o_recv, ici_o_send, ici_o_recv,
             share_send, share_recv, local_sems, g_send, g_recv, g_local):
        s = pl.program_id(0)
        in_step = jnp.maximum(s - nko, 0)
        blk = jnp.where(s < nko, 0, 1 + in_step // nki)
        k = jnp.where(s < nko, s, in_step % nki)
        j = blk - 1
        x, y, c = _position()
        me = 2 * x + y
        sibling = (x, y, 1 - c)
        last = k == jnp.where(s < nko, nko - 1, nki - 1)
        gather_start, gather_relay, gather_finish = _small_gather_steps(small_ref, all_ref, stage, g_send, g_recv, g_local)

        def other_half(ref):
            return ref.at[pl.ds(pl.multiple_of((1 - c) * hm, hm), hm), :]

        def own_half(ref):
            return ref[pl.ds(pl.multiple_of(c * hm, hm), hm), :]

        def to_sibling(jj):
            return pltpu.make_async_remote_copy(
                src_ref=other_half(acc.at[jj % 2]), dst_ref=rbuf.at[jj], send_sem=d2d_send.at[jj], recv_sem=d2d_recv.at[jj],
                device_id=sibling, device_id_type=MESH)

        def to_owner(jj):
            return pltpu.make_async_remote_copy(
                src_ref=sbuf.at[jj], dst_ref=arr.at[2 - jj], send_sem=ici_send.at[jj], recv_sem=ici_recv.at[jj],
                device_id=_owner_id(me, 3 - jj, c), device_id_type=MESH)

        def out_to_sibling():
            return pltpu.make_async_remote_copy(
                src_ref=other_half(acc_o), dst_ref=rbuf_o, send_sem=d2d_o_send, recv_sem=d2d_o_recv,
                device_id=sibling, device_id_type=MESH)

        def out_to_owner(r):
            return pltpu.make_async_remote_copy(
                src_ref=sbuf_o.at[r], dst_ref=arr_o.at[r], send_sem=ici_o_send.at[r], recv_sem=ici_o_recv.at[r],
                device_id=_owner_id(me, r + 1, c), device_id_type=MESH)

        def pair_sum(jj):
            to_sibling(jj).wait_recv()
            return own_half(acc.at[jj % 2]) + rbuf[jj]

        def send_block(jj):
            sbuf[jj] = pair_sum(jj).astype(BF16)
            to_owner(jj).start()

        @pl.when(s == 0)
        def _():
            gather_start()

        @pl.when((blk == 3) & (k == 0))
        def _():
            gather_relay()

        @pl.when((blk == 0) & (k == 0))
        def _():
            acc_o[...] = jnp.zeros((D, D), F32)

        @pl.when(blk == 0)
        def _():
            acc_o[...] += _tn(ao_ref[...], bo_ref[...])

        @pl.when((blk == 0) & last)
        def _():
            out_to_sibling().start()

        @pl.when((blk >= 3) & (k == 0))
        def _():
            to_sibling(j - 2).wait_send()

        @pl.when((blk >= 1) & (k == 0))
        def _():
            acc[j % 2] = jnp.zeros((D, QW), F32)

        @pl.when(blk >= 1)
        def _():
            acc[j % 2] += _tn(a_ref[...], b_ref[...])

        @pl.when((blk >= 1) & last)
        def _():
            to_sibling(j).start()

        @pl.when((blk == 1) & last)
        def _():
            out_to_sibling().wait_recv()
            total_o[...] = own_half(acc_o) + rbuf_o[...]
            for r in range(3):
                q = jnp.bitwise_xor(me, r + 1)
                sbuf_o[r] = total_o[pl.ds(pl.multiple_of(q * qr, qr), qr), :].astype(BF16)
                out_to_owner(r).start()

        @pl.when((blk == 2) & last)
        def _():
            send_block(0)

        @pl.when((blk == 3) & last)
        def _():
            send_block(1)
            send_block(2)

        @pl.when((blk == 4) & last)
        def _():
            g_in = pair_sum(3)
            g_out = total_o[pl.ds(pl.multiple_of(me * qr, qr), qr), :]
            to_sibling(2).wait_send()
            to_sibling(3).wait_send()
            out_to_sibling().wait_send()
            for r in range(3):
                to_owner(r).wait()
                out_to_owner(r).wait()
            for r in range(3):
                g_in = g_in + arr[r].astype(F32)
                g_out = g_out + arr_o[r].astype(F32)
            mine[...] = g_in
            mine_o[...] = g_out
            copies = []
            for idx, (src, dst) in enumerate([(mine, gin_ref), (mine_o, gout_ref)]):
                copies.append(pltpu.make_async_remote_copy(
                    src_ref=src, dst_ref=dst.at[c], send_sem=share_send.at[idx], recv_sem=share_recv.at[idx],
                    device_id=sibling, device_id_type=MESH))
                copies.append(pltpu.make_async_copy(src, dst.at[c], local_sems.at[idx]))
            for cp in copies:
                cp.start()
            for cp in copies:
                cp.wait()
            gather_finish()

    any_spec = pl.BlockSpec(memory_space=pl.ANY)
    grid_spec = pltpu.PrefetchScalarGridSpec(
        num_scalar_prefetch=1, grid=(nko + 4 * nki,),
        in_specs=[pl.BlockSpec((TK, D), out_index), pl.BlockSpec((TK, D), out_index),
                  pl.BlockSpec((TK_IN, D), a_in_index), pl.BlockSpec((TK_IN, QW), b_in_index), any_spec],
        out_specs=[any_spec, any_spec, any_spec],
        scratch_shapes=[pltpu.VMEM((2, D, QW), F32), pltpu.VMEM((4, hm, QW), F32), pltpu.VMEM((3, hm, QW), BF16),
                        pltpu.VMEM((3, hm, QW), BF16), pltpu.VMEM((hm, QW), F32),
                        pltpu.VMEM((D, D), F32), pltpu.VMEM((hm, D), F32), pltpu.VMEM((hm, D), F32), pltpu.VMEM((3, qr, D), BF16),
                        pltpu.VMEM((3, qr, D), BF16), pltpu.VMEM((qr, D), F32),
                        pltpu.VMEM(small.shape, F32),
                        pltpu.SemaphoreType.DMA((4,)), pltpu.SemaphoreType.DMA((4,)),
                        pltpu.SemaphoreType.DMA((3,)), pltpu.SemaphoreType.DMA((3,)),
                        pltpu.SemaphoreType.DMA, pltpu.SemaphoreType.DMA,
                        pltpu.SemaphoreType.DMA((3,)), pltpu.SemaphoreType.DMA((3,)),
                        pltpu.SemaphoreType.DMA((2,)), pltpu.SemaphoreType.DMA((2,)), pltpu.SemaphoreType.DMA((2,)),
                        pltpu.SemaphoreType.DMA((7,)), pltpu.SemaphoreType.DMA((7,)), pltpu.SemaphoreType.DMA])
    return pl.pallas_call(
        body, name=name, grid_spec=grid_spec,
        out_shape=[jax.ShapeDtypeStruct((2, qr, D), F32), jax.ShapeDtypeStruct((2, hm, QW), F32),
                   jax.ShapeDtypeStruct((8, *small.shape), F32)],
        compiler_params=pltpu.CompilerParams(dimension_semantics=("arbitrary",), vmem_limit_bytes=VMEM_LIMIT),
    )(pos, a_out, b_out, a_in, b_in, small)


def _chips(x, y):
    return [(1 - x, y), (x, 1 - y), (1 - x, 1 - y)]


def _gather_weights(parts, split):
    n = len(parts)

    def body(*refs):
        ins, outs = refs[:n], refs[n:2 * n]
        send_sems, recv_sems, local_sems = refs[2 * n:]
        x, y, c = _position()
        me = 2 * x + y
        chips = _chips(x, y)
        local = [pltpu.make_async_copy(ins[a], outs[a].at[me], local_sems.at[a]) for a in range(n)]
        for cp in local:
            cp.start()

        def half(a, ref, who):
            rows = parts[a].shape[0] // 2
            return ref.at[pl.ds(pl.multiple_of(who * rows, 8), rows), :] if split[a] else ref

        sends = []
        for a in range(n):
            for j, chip in enumerate(chips):
                sends.append(pltpu.make_async_remote_copy(
                    src_ref=half(a, ins[a], c), dst_ref=half(a, outs[a].at[me], c),
                    send_sem=send_sems.at[a, j], recv_sem=recv_sems.at[a, j], device_id=(*chip, c), device_id_type=MESH))
        for cp in sends:
            cp.start()
        for j, chip in enumerate(chips):
            src = 2 * chip[0] + chip[1]
            for a in range(n):
                pltpu.make_async_remote_copy(
                    src_ref=half(a, ins[a], c), dst_ref=half(a, outs[a].at[src], c),
                    send_sem=send_sems.at[a, j], recv_sem=recv_sems.at[a, j], device_id=(*chip, c), device_id_type=MESH).wait_recv()
                if split[a]:
                    fwd = pltpu.make_async_remote_copy(
                        src_ref=half(a, outs[a].at[src], c), dst_ref=half(a, outs[a].at[src], c),
                        send_sem=send_sems.at[a, 3 + j], recv_sem=recv_sems.at[a, 3 + j], device_id=(x, y, 1 - c), device_id_type=MESH)
                    fwd.start()
                    sends.append(fwd)
        for j, chip in enumerate(chips):
            src = 2 * chip[0] + chip[1]
            for a in range(n):
                if split[a]:
                    pltpu.make_async_remote_copy(
                        src_ref=half(a, outs[a].at[src], 1 - c), dst_ref=half(a, outs[a].at[src], 1 - c),
                        send_sem=send_sems.at[a, 3 + j], recv_sem=recv_sems.at[a, 3 + j], device_id=(x, y, 1 - c),
                        device_id_type=MESH).wait_recv()
        for cp in sends:
            cp.wait_send()
        for cp in local:
            cp.wait()

    any_spec = pl.BlockSpec(memory_space=pl.ANY)
    return pl.pallas_call(
        body, name="gather_weights",
        in_specs=[pl.BlockSpec(memory_space=pltpu.VMEM)] * n, out_specs=[any_spec] * n,
        out_shape=[jax.ShapeDtypeStruct((4, *p.shape), p.dtype) for p in parts],
        scratch_shapes=[pltpu.SemaphoreType.DMA((n, 6)), pltpu.SemaphoreType.DMA((n, 6)), pltpu.SemaphoreType.DMA((n,))],
    )(*parts)


def _adamw(w, g, m, v):
    m = ADAM_B1 * m + (1.0 - ADAM_B1) * g
    v = ADAM_B2 * v + (1.0 - ADAM_B2) * (g * g)
    m_hat = m / (1.0 - ADAM_B1 ** ADAM_STEP)
    v_hat = v / (1.0 - ADAM_B2 ** ADAM_STEP)
    delta = -ADAM_LR * (m_hat / (jnp.sqrt(v_hat) + ADAM_EPS) + ADAM_WD * w)
    return delta, m, v


def _adamw_big(w, g, m, v, name):
    rows, cols = w.shape
    tr = 256

    def body(w_ref, g_ref, m_ref, v_ref, go_ref, d_ref, mo_ref, vo_ref):
        gv = g_ref[...]
        go_ref[...] = gv
        d_ref[...], mo_ref[...], vo_ref[...] = _adamw(w_ref[...], gv, m_ref[...], v_ref[...])

    spec = pl.BlockSpec((tr, cols), lambda i: (i, 0))
    return pl.pallas_call(
        body, name=name, grid=(rows // tr,),
        in_specs=[spec] * 4, out_specs=[spec] * 4,
        out_shape=[jax.ShapeDtypeStruct((rows, cols), F32)] * 4,
        compiler_params=pltpu.CompilerParams(dimension_semantics=("arbitrary",)),
    )(w, g, m, v)


def _adamw_small(g0, g1, weights, moms, vels):
    names = ["pre", "post", "conv", "pw", "ps", "lng", "lnb", "ws", "bs"]
    shapes = [w.shape for w in weights]

    def body(*refs):
        me = 2 * lax.axis_index("x") + lax.axis_index("y")
        g0_ref, g1_ref = refs[0], refs[1]
        w_refs, m_refs, v_refs = refs[2:11], refs[11:20], refs[20:29]
        outs = refs[29:29 + 36]
        loss_ref = refs[65]
        t0_ref, t1_ref = refs[66], refs[67]
        t0 = g0_ref[0]
        t1 = g1_ref[0]
        for d in range(1, 8):
            t0 = t0 + g0_ref[d]
            t1 = t1 + g1_ref[d]
        t0_ref[...] = t0
        t1_ref[...] = t1
        loss_ref[...] = t1_ref[S1_LOSS:S1_LOSS + 1, 0:1]
        my_conv = pl.multiple_of(S0_CONV + 8 * me, 8)
        my_ln = pl.multiple_of(S1_LN + 8 * me, 8)

        def update(idx, piece, grad):
            go, do, mo, vo = outs[4 * idx:4 * idx + 4]
            go[piece] = grad
            do[piece], mo[piece], vo[piece] = _adamw(w_refs[idx][piece], grad, m_refs[idx][piece], v_refs[idx][piece])

        for layer in range(2):
            for k in range(D // GD):
                lanes = slice(k * GD, (k + 1) * GD)
                tref, pre0, post0 = (t0_ref, S0_PRE, S0_POST) if layer == 0 else (t1_ref, S1_PRE, S1_POST)
                update(0, (slice(layer, layer + 1), lanes), tref[pre0 + k:pre0 + k + 1, :])
                update(1, (slice(layer, layer + 1), lanes), tref[post0 + k:post0 + k + 1, :])
        conv_rows = t0_ref[pl.ds(my_conv, 8), :]
        update(2, (slice(0, 3), slice(None)), conv_rows[0:3, :])
        for g in range(4):
            update(3, (g,), t0_ref[S0_PW + g * GD:S0_PW + (g + 1) * GD, :])
            update(4, (slice(0, 1), slice(g * GD, (g + 1) * GD)), t0_ref[S0_PS + g:S0_PS + g + 1, :])
        ln_rows = t1_ref[pl.ds(my_ln, 8), :]
        for k in range(2):
            update(5, (slice(0, 1), slice(k * GD, (k + 1) * GD)), ln_rows[k:k + 1, :])
            update(6, (slice(0, 1), slice(k * GD, (k + 1) * GD)), ln_rows[2 + k:3 + k, :])
        for h in range(HEADS):
            update(7, (h,), t1_ref[S1_WS + h * CHUNK:S1_WS + (h + 1) * CHUNK, :])
        update(8, (slice(None), slice(None)), t1_ref[S1_BS:S1_BS + HEADS, :])

    vm = pl.BlockSpec(memory_space=pltpu.VMEM)
    out_shape = []
    for s in shapes:
        out_shape += [jax.ShapeDtypeStruct(s, F32)] * 4
    out_shape.append(jax.ShapeDtypeStruct((1, 1), F32))
    res = pl.pallas_call(
        body, name="adamw_small",
        in_specs=[vm] * 29, out_specs=[vm] * 37, out_shape=out_shape,
        scratch_shapes=[pltpu.VMEM((S0_ROWS, GD), F32), pltpu.VMEM((S1_ROWS, GD), F32)],
        compiler_params=pltpu.CompilerParams(vmem_limit_bytes=VMEM_LIMIT),
    )(g0, g1, *weights, *moms, *vels)
    per_weight = {nm: res[4 * i:4 * i + 4] for i, nm in enumerate(names)}
    return per_weight, res[36]


def _pad8(a):
    return jnp.pad(a, ((0, 8 - a.shape[0]), (0, 0)))


def kernel(x, pre_norm, post_norm, even_w_in, even_conv_w, even_pool_w, even_pool_scale, even_w_out, odd_w_in, odd_ln_g, odd_ln_b, odd_w_s, odd_b_s, odd_w_out, loss_target, m_pre_norm, m_post_norm, m_even_w_in, m_even_conv_w, m_even_pool_w, m_even_pool_scale, m_even_w_out, m_odd_w_in, m_odd_ln_g, m_odd_ln_b, m_odd_w_s, m_odd_b_s, m_odd_w_out, v_pre_norm, v_post_norm, v_even_w_in, v_even_conv_w, v_even_pool_w, v_even_pool_scale, v_even_w_out, v_odd_w_in, v_odd_ln_g, v_odd_ln_b, v_odd_w_s, v_odd_b_s, v_odd_w_out):
    xs = x[0]
    tgt = loss_target[0]

    small_shard = jnp.concatenate([_pad8(even_conv_w[0]), _pad8(odd_ln_g.reshape(2, GD)), _pad8(odd_ln_b.reshape(2, GD))], axis=0)
    win0, wout0, shard = _gather_weights([even_w_in[0].astype(BF16), even_w_out[0].astype(BF16), small_shard], [True, True, False])
    wout0 = wout0.reshape(D, D)
    px, py, pc = _position()
    pos = jnp.stack([pc, 2 * px + py]).astype(jnp.int32)
    conv_w = shard[:, 0:3, :].transpose(1, 0, 2).reshape(3, AW)
    ln_g = shard[:, 8:10, :].reshape(1, D)
    ln_b = shard[:, 16:18, :].reshape(1, D)
    pool_wb = even_pool_w[0].astype(BF16)
    ws_tril = jnp.tril(odd_w_s[0]).astype(BF16)
    ws_tril_t = jnp.swapaxes(ws_tril, 1, 2)
    bias = jnp.broadcast_to(odd_b_s[0][:, :, None], (HEADS, CHUNK, GD))
    pre0, pre1 = pre_norm[0:1], pre_norm[1:2]
    post0, post1 = post_norm[0:1], post_norm[1:2]

    x1, proj0, m0, hb0, mixp0, win1, wout1 = _even_fwd(
        xs, pre0, post0, win0, conv_w, pool_wb, even_pool_scale, wout0, [odd_w_in[0].astype(BF16), odd_w_out[0].astype(BF16)])
    wout1 = wout1.reshape(D, D)
    proj1, m1, hb1, yp1, dx2, loss_part = _odd_fwd(x1, tgt, pre1, post1, win1, ln_g, ln_b, ws_tril, bias, wout1)
    dx1, dproj1, dmb1, small1 = _odd_bwd(dx2, x1, proj1, m1, loss_part, pre1, post1, win1, ln_g, ln_b, ws_tril, ws_tril_t, bias, wout1)
    g_out1, g_in1, all1 = _wgrad_layer(yp1, dmb1, hb1, dproj1, small1, pos, "wgrad_odd")
    gx, dproj0, dmb0, small0 = _even_bwd(dx1, xs, proj0, m0, pre0, post0, win0, conv_w, pool_wb, even_pool_scale, wout0)
    g_out0, g_in0, all0 = _wgrad_layer(mixp0, dmb0, hb0, dproj0, small0, pos, "wgrad_even")

    grads = [g_in0, g_out0, g_in1, g_out1]
    tags = ["even_in", "even_out", "odd_in", "odd_out"]
    big_w = [even_w_in[0], even_w_out[0], odd_w_in[0], odd_w_out[0]]
    big_m = [m_even_w_in[0], m_even_w_out[0], m_odd_w_in[0], m_odd_w_out[0]]
    big_v = [v_even_w_in[0], v_even_w_out[0], v_odd_w_in[0], v_odd_w_out[0]]
    big = [_adamw_big(w, g.reshape(w.shape), m, v, "adamw_" + t) for g, w, m, v, t in zip(grads, big_w, big_m, big_v, tags)]

    small_w = [pre_norm, post_norm, even_conv_w[0], even_pool_w[0], even_pool_scale, odd_ln_g, odd_ln_b, odd_w_s[0], odd_b_s[0]]
    small_m = [m_pre_norm, m_post_norm, m_even_conv_w[0], m_even_pool_w[0], m_even_pool_scale, m_odd_ln_g, m_odd_ln_b, m_odd_w_s[0], m_odd_b_s[0]]
    small_v = [v_pre_norm, v_post_norm, v_even_conv_w[0], v_even_pool_w[0], v_even_pool_scale, v_odd_ln_g, v_odd_ln_b, v_odd_w_s[0], v_odd_b_s[0]]
    sm, loss = _adamw_small(all0, all1, small_w, small_m, small_v)

    def lead(a):
        return a[None]

    per = {
        "pre_norm": sm["pre"], "post_norm": sm["post"],
        "even_w_in": [lead(a) for a in big[0]], "even_conv_w": [lead(a) for a in sm["conv"]],
        "even_pool_w": [lead(a) for a in sm["pw"]], "even_pool_scale": sm["ps"],
        "even_w_out": [lead(a) for a in big[1]], "odd_w_in": [lead(a) for a in big[2]],
        "odd_ln_g": sm["lng"], "odd_ln_b": sm["lnb"],
        "odd_w_s": [lead(a) for a in sm["ws"]], "odd_b_s": [lead(a) for a in sm["bs"]],
        "odd_w_out": [lead(a) for a in big[3]],
    }
    order = ["pre_norm", "post_norm", "even_w_in", "even_conv_w", "even_pool_w", "even_pool_scale", "even_w_out", "odd_w_in",
             "odd_ln_g", "odd_ln_b", "odd_w_s", "odd_b_s", "odd_w_out"]
    outs = [loss.reshape(()), gx[None]]
    for kind in range(4):
        outs += [per[nm][kind] for nm in order]
    return tuple(outs)
```

```python
import functools

import jax
import jax.numpy as jnp
from jax import lax
from jax.experimental import pallas as pl
from jax.experimental.pallas import tpu as pltpu

F32 = jnp.float32
BF16 = jnp.bfloat16
MESH = pl.DeviceIdType.MESH

D = 1024
W3 = 3 * D
QW = W3 // 4
AW = 512
GD = 128
CHUNK = 128
HEADS = 8
HALO = 16
POOL_WINDOWS = (2, 4, 8, 16)
EPS = 1e-6
TM_FWD = 512
TM_BWD = 256
TK = 1024
TK_IN = 2048
VMEM_LIMIT = 56 * 1024 * 1024

ADAM_LR, ADAM_B1, ADAM_B2, ADAM_EPS, ADAM_WD, ADAM_STEP = 0.001, 0.9, 0.999, 1e-08, 0.01, 10

S0_PRE, S0_POST, S0_CONV, S0_PS, S0_PW, S0_ROWS = 0, 8, 16, 48, 56, 568
S1_PRE, S1_POST, S1_LN, S1_BS, S1_WS, S1_LOSS, S1_ROWS = 0, 8, 16, 48, 56, 1080, 1088


def _nn(a, b):
    return jnp.dot(a, b, preferred_element_type=F32)


def _nt(a, b):
    return lax.dot_general(a, b, (((1,), (1,)), ((), ())), preferred_element_type=F32)


def _tn(a, b):
    return lax.dot_general(a, b, (((0,), (0,)), ((), ())), preferred_element_type=F32)


def _sigmoid(z):
    return 1.0 / (1.0 + jnp.exp(-z))


def _rms_fwd(x, g):
    r = lax.rsqrt(jnp.mean(x * x, axis=-1, keepdims=True) + EPS)
    return x * r, r


def _rms_bwd(dy, xh, r, g):
    dn = dy * g
    dx = r * (dn - xh * jnp.mean(xh * dn, axis=-1, keepdims=True))
    return dx, jnp.sum(dy * xh, axis=0, keepdims=True)


def _full(shape):
    nd = len(shape)
    return pl.BlockSpec(shape, lambda i, _n=nd: (0,) * _n, pipeline_mode=pl.Buffered(1))


def _full_out(shape):
    nd = len(shape)
    return pl.BlockSpec(shape, lambda i, _n=nd: (0,) * _n)


def _rows(tm, width, index=None):
    return pl.BlockSpec((tm, width), (lambda i: (i, 0)) if index is None else index)


def _params():
    return pltpu.CompilerParams(dimension_semantics=("arbitrary",), vmem_limit_bytes=VMEM_LIMIT)


def _position():
    x, y, c = lax.axis_index("x"), lax.axis_index("y"), lax.axis_index("c")
    return x, y, c


def _even_mix(proj_ref, hc_ext, xp_ext, cw_ref, pw_ref, ps_ref, first_row):
    tm = proj_ref.shape[0]
    xa = proj_ref[:, 0:AW]
    gb = proj_ref[:, AW:2 * AW]
    gc = proj_ref[:, 2 * AW:3 * AW]
    za = proj_ref[:, 3 * AW:4 * AW]
    xp = proj_ref[:, 4 * AW:5 * AW]
    zp = proj_ref[:, 5 * AW:6 * AW]
    hc = gc * xa
    hc_ext[HALO:, :] = hc
    e = hc_ext[...]
    conv = cw_ref[2:3, :] * hc + cw_ref[1:2, :] * pltpu.roll(e, 1, 0)[HALO:] + cw_ref[0:1, :] * pltpu.roll(e, 2, 0)[HALO:]
    sa = _sigmoid(za)
    xp_ext[HALO:, :] = xp
    pos = first_row + lax.broadcasted_iota(jnp.int32, (tm, 1), 0)
    pooled, mixed, counts = [], [], []
    for g, w in enumerate(POOL_WINDOWS):
        cols = slice(g * GD, (g + 1) * GD)
        s = xp_ext[:, cols]
        for k in range(g + 1):
            s = s + pltpu.roll(s, 2 ** k, 0)
        count = jnp.minimum(pos + 1, w).astype(F32)
        pg = s[HALO:] / count - xp[:, cols]
        pooled.append(pg.astype(BF16))
        mixed.append(_nn(pooled[-1], pw_ref[g]))
        counts.append(count)
    mixed = jnp.concatenate(mixed, axis=-1)
    sb = _sigmoid(zp)
    return dict(xa=xa, gb=gb, gc=gc, za=za, zp=zp, hc=hc, conv=conv, sa=sa, sb=sb, pooled=pooled, mixed=mixed, counts=counts)


def _half_rows(ref, rows, who):
    return ref.at[pl.ds(pl.multiple_of(who * (rows // 2), 8), rows // 2), :]


def _store_permuted(ref, value):
    for ob in range(D // GD):
        nb = 4 * (ob % 2) + ob // 2
        ref[:, nb * GD:(nb + 1) * GD] = value[:, ob * GD:(ob + 1) * GD]


def _even_fwd(x, pre, post, win, cw, pwb, ps, wout, next_shards):
    S = x.shape[0]
    tm = TM_FWD
    nt = S // tm
    relay = (3 * nt) // 4
    n = len(next_shards)
    shard_rows = [p.shape[0] for p in next_shards]

    def body(x_ref, pre_ref, post_ref, win_ref, cw_ref, pw_ref, ps_ref, wout_ref, *rest):
        shard_refs, rest = rest[:n], rest[n:]
        x1_ref, proj_ref, m_ref, hb_ref, mixp_ref, conv_ref, mixed_ref, pooled_ref = rest[:8]
        full_refs, rest = rest[8:8 + n], rest[8 + n:]
        hc_ext, xp_ext, mix_sc = rest[:3]
        stage, rest = rest[3:3 + n], rest[3 + n:]
        send_sems, recv_sems, local_sems = rest
        i = pl.program_id(0)
        px, py, pc = _position()
        me = 2 * px + py
        chips = _chips(px, py)

        def ici(a, j):
            return pltpu.make_async_remote_copy(
                src_ref=_half_rows(shard_refs[a], shard_rows[a], pc), dst_ref=_half_rows(full_refs[a].at[me], shard_rows[a], pc),
                send_sem=send_sems.at[a, j], recv_sem=recv_sems.at[a, j], device_id=(*chips[j], pc), device_id_type=MESH)

        def ici_arrival(a, j):
            src = 2 * chips[j][0] + chips[j][1]
            return pltpu.make_async_remote_copy(
                src_ref=_half_rows(shard_refs[a], shard_rows[a], pc), dst_ref=_half_rows(full_refs[a].at[src], shard_rows[a], pc),
                send_sem=send_sems.at[a, j], recv_sem=recv_sems.at[a, j], device_id=(*chips[j], pc), device_id_type=MESH)

        def relay_copy(a, j, who):
            src = 2 * chips[j][0] + chips[j][1]
            region = _half_rows(full_refs[a].at[src], shard_rows[a], who)
            return pltpu.make_async_remote_copy(
                src_ref=region, dst_ref=region, send_sem=send_sems.at[a, 3 + j], recv_sem=recv_sems.at[a, 3 + j],
                device_id=(px, py, 1 - pc), device_id_type=MESH)

        def own_copy(a):
            return pltpu.make_async_copy(stage[a], full_refs[a].at[me], local_sems.at[a])

        @pl.when(i == 0)
        def _():
            hc_ext[0:HALO, :] = jnp.zeros((HALO, AW), F32)
            xp_ext[0:HALO, :] = jnp.zeros((HALO, AW), F32)
            for a in range(n):
                for j in range(3):
                    ici(a, j).start()
            for a in range(n):
                load = pltpu.make_async_copy(shard_refs[a], stage[a], local_sems.at[a])
                load.start()
                load.wait()
                own_copy(a).start()

        @pl.when(i == relay)
        def _():
            for j in range(3):
                for a in range(n):
                    ici_arrival(a, j).wait_recv()
                    relay_copy(a, j, pc).start()

        xv = x_ref[...]
        xh, _ = _rms_fwd(xv, None)
        hb = (xh * pre_ref[...]).astype(BF16)
        hb_ref[...] = hb
        for q in range(4):
            proj_ref[:, q * QW:(q + 1) * QW] = _nn(hb, win_ref[q])
        t = _even_mix(proj_ref, hc_ext, xp_ext, cw_ref, pw_ref, ps_ref, i * tm)
        conv_ref[...] = t["conv"]
        mixed_ref[...] = t["mixed"]
        for g in range(4):
            pooled_ref[:, g * GD:(g + 1) * GD] = t["pooled"][g]
        mix_sc[:, 0:AW] = (t["gb"] * t["conv"] * (t["za"] * t["sa"])).astype(BF16)
        mix_sc[:, AW:2 * AW] = (t["mixed"] * ps_ref[...] * (t["zp"] * t["sb"])).astype(BF16)
        mix = mix_sc[...]
        _store_permuted(mixp_ref, mix)
        m = _nn(mix, wout_ref[...])
        m_ref[...] = m
        mh, _ = _rms_fwd(m, None)
        x1_ref[...] = xv + mh * post_ref[...]
        hc_ext[0:HALO, :] = hc_ext[tm:tm + HALO, :]
        xp_ext[0:HALO, :] = xp_ext[tm:tm + HALO, :]

        @pl.when(i == nt - 1)
        def _():
            for j in range(3):
                for a in range(n):
                    relay_copy(a, j, 1 - pc).wait_recv()
            for a in range(n):
                for j in range(3):
                    ici(a, j).wait_send()
                    relay_copy(a, j, pc).wait_send()
                own_copy(a).wait()

    any_spec = pl.BlockSpec(memory_space=pl.ANY)
    return pl.pallas_call(
        body, name="even_fwd", grid=(nt,),
        in_specs=[_rows(tm, D), _full((1, D)), _full((1, D)), _full((4, D, QW)), _full((3, AW)), _full((4, GD, GD)),
                  _full((1, AW)), _full((D, D))] + [any_spec] * n,
        out_specs=[_rows(tm, D), _rows(tm, W3), _rows(tm, D), _rows(tm, D), _rows(tm, D), _rows(tm, AW), _rows(tm, AW), _rows(tm, AW)]
        + [any_spec] * n,
        out_shape=[jax.ShapeDtypeStruct((S, D), F32), jax.ShapeDtypeStruct((S, W3), F32), jax.ShapeDtypeStruct((S, D), F32),
                   jax.ShapeDtypeStruct((S, D), BF16), jax.ShapeDtypeStruct((S, D), BF16),
                   jax.ShapeDtypeStruct((S, AW), F32), jax.ShapeDtypeStruct((S, AW), F32), jax.ShapeDtypeStruct((S, AW), BF16)]
        + [jax.ShapeDtypeStruct((4, *p.shape), p.dtype) for p in next_shards],
        scratch_shapes=[pltpu.VMEM((tm + HALO, AW), F32), pltpu.VMEM((tm + HALO, AW), F32), pltpu.VMEM((tm, D), BF16)]
        + [pltpu.VMEM(p.shape, p.dtype) for p in next_shards]
        + [pltpu.SemaphoreType.DMA((n, 6)), pltpu.SemaphoreType.DMA((n, 6)), pltpu.SemaphoreType.DMA((n,))],
        compiler_params=_params(),
    )(x, pre, post, win, cw, pwb, ps, wout, *next_shards)


def _chunks_side_by_side(a, h):
    return jnp.concatenate([a[n * CHUNK:(n + 1) * CHUNK, h * GD:(h + 1) * GD] for n in range(a.shape[0] // CHUNK)], axis=1)


def _odd_mix(proj_ref, lng_ref, lnb_ref, ws_ref, bias_ref, sv_ref):
    tm = proj_ref.shape[0]
    u = proj_ref[:, 0:D]
    v = proj_ref[:, D:2 * D]
    z = proj_ref[:, 2 * D:3 * D]
    mu = jnp.mean(v, axis=-1, keepdims=True)
    vc = v - mu
    rs = lax.rsqrt(jnp.mean(vc * vc, axis=-1, keepdims=True) + EPS)
    vh = vc * rs
    vnb = (vh * lng_ref[...] + lnb_ref[...]).astype(BF16)
    for h in range(HEADS):
        sv = _nn(ws_ref[h], _chunks_side_by_side(vnb, h))
        for n in range(tm // CHUNK):
            sv_ref[n * CHUNK:(n + 1) * CHUNK, h * GD:(h + 1) * GD] = sv[:, n * GD:(n + 1) * GD] + bias_ref[h]
    return dict(u=u, z=z, vh=vh, rs=rs, vnb=vnb, sz=_sigmoid(z))


def _odd_fwd(x1, tgt, pre, post, win, lng, lnb, wsb, bias, wout):
    S = x1.shape[0]
    tm = TM_FWD
    nt = S // tm

    def body(x_ref, tgt_ref, pre_ref, post_ref, win_ref, lng_ref, lnb_ref, ws_ref, bias_ref, wout_ref,
             proj_ref, m_ref, hb_ref, yb_ref, dx2_ref, loss_ref, sv_ref):
        i = pl.program_id(0)

        @pl.when(i == 0)
        def _():
            loss_ref[...] = jnp.zeros((8, GD), F32)

        xv = x_ref[...]
        xh, _ = _rms_fwd(xv, None)
        hb = (xh * pre_ref[...]).astype(BF16)
        hb_ref[...] = hb
        for q in range(4):
            proj_ref[:, q * QW:(q + 1) * QW] = _nn(hb, win_ref[q])
        t = _odd_mix(proj_ref, lng_ref, lnb_ref, ws_ref, bias_ref, sv_ref)
        yb = (t["u"] * sv_ref[...] * (t["z"] * t["sz"])).astype(BF16)
        _store_permuted(yb_ref, yb)
        m = _nn(yb, wout_ref[...])
        m_ref[...] = m
        mh, _ = _rms_fwd(m, None)
        err = xv + mh * post_ref[...] - tgt_ref[...]
        dx2_ref[...] = err * (1.0 / D)
        part = 0.5 * jnp.sum(jnp.mean(err * err, axis=-1, keepdims=True), axis=0, keepdims=True)
        loss_ref[...] += jnp.broadcast_to(part, (8, GD))

    return pl.pallas_call(
        body, name="odd_fwd", grid=(nt,),
        in_specs=[_rows(tm, D), _rows(tm, D), _full((1, D)), _full((1, D)), _full((4, D, QW)), _full((1, D)), _full((1, D)),
                  _full((HEADS, CHUNK, CHUNK)), _full((HEADS, CHUNK, GD)), _full((D, D))],
        out_specs=[_rows(tm, W3), _rows(tm, D), _rows(tm, D), _rows(tm, D), _rows(tm, D), _full_out((8, GD))],
        out_shape=[jax.ShapeDtypeStruct((S, W3), F32), jax.ShapeDtypeStruct((S, D), F32), jax.ShapeDtypeStruct((S, D), BF16),
                   jax.ShapeDtypeStruct((S, D), BF16), jax.ShapeDtypeStruct((S, D), F32), jax.ShapeDtypeStruct((8, GD), F32)],
        scratch_shapes=[pltpu.VMEM((tm, D), F32)],
        compiler_params=_params(),
    )(x1, tgt, pre, post, win, lng, lnb, wsb, bias, wout)


def _store_rows(ref, row0, value):
    r, width = value.shape
    for a in range(r):
        for k in range(width // GD):
            ref[row0 + a * (width // GD) + k:row0 + a * (width // GD) + k + 1, :] = value[a:a + 1, k * GD:(k + 1) * GD]


def _proj_bwd(dproj, win_ref, x, dy, pre):
    dh = _nt(dproj[:, 0:QW], win_ref[0])
    for q in range(1, 4):
        dh += _nt(dproj[:, q * QW:(q + 1) * QW], win_ref[q])
    xh, r = _rms_fwd(x, None)
    dxn, dpre = _rms_bwd(dh, xh, r, pre)
    return dy + dxn, dpre


def _odd_bwd(dx2, x1, proj, m, loss, pre, post, win, lng, lnb, wsb, wsbt, bias, wout):
    S = x1.shape[0]
    tm = TM_BWD
    nt = S // tm

    def body(dy_ref, x_ref, proj_ref, m_ref, loss_ref, pre_ref, post_ref, win_ref, lng_ref, lnb_ref, ws_ref, wst_ref, bias_ref,
             wout_ref, dx_ref, dproj_ref, dmb_ref, small_ref, sv_ref, dvn_ref, acc1024, dws_acc, dbs_acc):
        i = pl.program_id(0)

        @pl.when(i == 0)
        def _():
            acc1024[...] = jnp.zeros_like(acc1024)
            dws_acc[...] = jnp.zeros_like(dws_acc)
            dbs_acc[...] = jnp.zeros_like(dbs_acc)

        dy = dy_ref[...]
        mh, rm = _rms_fwd(m_ref[...], None)
        dm, dpost = _rms_bwd(dy, mh, rm, post_ref[...])
        dmb = dm.astype(BF16)
        dmb_ref[...] = dmb
        dyv = _nt(dmb, wout_ref[...])
        t = _odd_mix(proj_ref, lng_ref, lnb_ref, ws_ref, bias_ref, sv_ref)
        u, z, sz, sv = t["u"], t["z"], t["sz"], sv_ref[...]
        dproj_ref[:, 0:D] = (dyv * sv * (z * sz)).astype(BF16)
        dproj_ref[:, 2 * D:3 * D] = (dyv * u * sv * (sz * (1.0 + z * (1.0 - sz)))).astype(BF16)
        dsv = dyv * u * (z * sz)
        dsvb = dsv.astype(BF16)
        for h in range(HEADS):
            dsv_h = _chunks_side_by_side(dsvb, h)
            dvn_h = _nn(wst_ref[h], dsv_h)
            dws_acc[h] += _nt(dsv_h, _chunks_side_by_side(t["vnb"], h))
            for n in range(tm // CHUNK):
                rows, cols = slice(n * CHUNK, (n + 1) * CHUNK), slice(h * GD, (h + 1) * GD)
                dvn_ref[rows, cols] = dvn_h[:, n * GD:(n + 1) * GD]
                dbs_acc[h] += dsv[rows, cols]
        dvn = dvn_ref[...]
        vh = t["vh"]
        dvh = dvn * lng_ref[...]
        dv = t["rs"] * (dvh - jnp.mean(dvh, axis=-1, keepdims=True) - vh * jnp.mean(dvh * vh, axis=-1, keepdims=True))
        dproj_ref[:, D:2 * D] = dv.astype(BF16)
        dx_ref[...], dpre = _proj_bwd(dproj_ref[...], win_ref, x_ref[...], dy, pre_ref[...])
        acc1024[0:1, :] += dpre
        acc1024[1:2, :] += dpost
        acc1024[2:3, :] += jnp.sum(dvn * vh, axis=0, keepdims=True)
        acc1024[3:4, :] += jnp.sum(dvn, axis=0, keepdims=True)

        @pl.when(i == nt - 1)
        def _():
            small_ref[...] = jnp.zeros_like(small_ref)
            _store_rows(small_ref, S1_PRE, acc1024[0:1, :])
            _store_rows(small_ref, S1_POST, acc1024[1:2, :])
            for q in range(4):
                _store_rows(small_ref, S1_LN + 8 * q, acc1024[2:3, 2 * q * GD:(2 * q + 2) * GD])
                _store_rows(small_ref, S1_LN + 8 * q + 2, acc1024[3:4, 2 * q * GD:(2 * q + 2) * GD])
            lower = lax.broadcasted_iota(jnp.int32, (CHUNK, CHUNK), 0) >= lax.broadcasted_iota(jnp.int32, (CHUNK, CHUNK), 1)
            for h in range(HEADS):
                small_ref[S1_WS + h * CHUNK:S1_WS + (h + 1) * CHUNK, :] = jnp.where(lower, dws_acc[h], 0.0)
                small_ref[S1_BS + h:S1_BS + h + 1, :] = jnp.sum(dbs_acc[h].T, axis=0, keepdims=True)
            small_ref[S1_LOSS:S1_LOSS + 8, :] = loss_ref[...]

    return pl.pallas_call(
        body, name="odd_bwd", grid=(nt,),
        in_specs=[_rows(tm, D), _rows(tm, D), _rows(tm, W3), _rows(tm, D), _full((8, GD)), _full((1, D)), _full((1, D)),
                  _full((4, D, QW)), _full((1, D)), _full((1, D)), _full((HEADS, CHUNK, CHUNK)), _full((HEADS, CHUNK, CHUNK)),
                  _full((HEADS, CHUNK, GD)), _full((D, D))],
        out_specs=[_rows(tm, D), _rows(tm, W3), _rows(tm, D), _full_out((S1_ROWS, GD))],
        out_shape=[jax.ShapeDtypeStruct((S, D), F32), jax.ShapeDtypeStruct((S, W3), BF16), jax.ShapeDtypeStruct((S, D), BF16),
                   jax.ShapeDtypeStruct((S1_ROWS, GD), F32)],
        scratch_shapes=[pltpu.VMEM((tm, D), F32), pltpu.VMEM((tm, D), F32), pltpu.VMEM((8, D), F32),
                        pltpu.VMEM((HEADS, CHUNK, CHUNK), F32), pltpu.VMEM((HEADS, CHUNK, GD), F32)],
        compiler_params=_params(),
    )(dx2, x1, proj, m, loss, pre, post, win, lng, lnb, wsb, wsbt, bias, wout)


def _even_bwd(dx1, x, proj, conv, mixed, pooled, m, pre, post, win, cw, pwb, ps, wout):
    S = x.shape[0]
    tm = TM_BWD
    nt = S // tm
    L = tm + HALO

    def rev(i):
        return (nt - 1 - i, 0)

    def body(dy_ref, x_ref, proj_ref, conv_ref, mixed_ref, pooled_ref, m_ref, pre_ref, post_ref, win_ref, cw_ref, pw_ref, ps_ref,
             wout_ref, dx_ref, dproj_ref, dmb_ref, small_ref, dconv_ext, q_ext, acc1024, acc512, dpw_acc):
        i = pl.program_id(0)
        tile = nt - 1 - i

        @pl.when(i == 0)
        def _():
            dconv_ext[tm:L, :] = jnp.zeros((HALO, AW), F32)
            q_ext[tm:L, :] = jnp.zeros((HALO, AW), F32)
            acc1024[...] = jnp.zeros_like(acc1024)
            acc512[...] = jnp.zeros_like(acc512)
            dpw_acc[...] = jnp.zeros_like(dpw_acc)

        dy = dy_ref[...]
        mh, rm = _rms_fwd(m_ref[...], None)
        dm, dpost = _rms_bwd(dy, mh, rm, post_ref[...])
        dmb = dm.astype(BF16)
        dmb_ref[...] = dmb
        dmix = _nt(dmb, wout_ref[...])
        dya, dyb = dmix[:, 0:AW], dmix[:, AW:2 * AW]
        xa, gb, gc, za = (proj_ref[:, k * AW:(k + 1) * AW] for k in range(4))
        zp = proj_ref[:, 5 * AW:6 * AW]
        hc = gc * xa
        conv = conv_ref[...]
        sa = _sigmoid(za)
        silu_a = za * sa
        dproj_ref[:, AW:2 * AW] = (dya * conv * silu_a).astype(BF16)
        dproj_ref[:, 3 * AW:4 * AW] = (dya * gb * conv * (sa * (1.0 + za * (1.0 - sa)))).astype(BF16)
        dconv = dya * gb * silu_a
        dconv_ext[0:tm, :] = dconv
        e = dconv_ext[...]
        dc1 = pltpu.roll(e, L - 1, 0)[0:tm]
        dc2 = pltpu.roll(e, L - 2, 0)[0:tm]
        dhc = cw_ref[2:3, :] * dconv + cw_ref[1:2, :] * dc1 + cw_ref[0:1, :] * dc2
        dproj_ref[:, 0:AW] = (dhc * gc).astype(BF16)
        dproj_ref[:, 2 * AW:3 * AW] = (dhc * xa).astype(BF16)
        acc512[0:1, :] += jnp.sum(dc2 * hc, axis=0, keepdims=True)
        acc512[1:2, :] += jnp.sum(dc1 * hc, axis=0, keepdims=True)
        acc512[2:3, :] += jnp.sum(dconv * hc, axis=0, keepdims=True)

        sb, mixed = _sigmoid(zp), mixed_ref[...]
        silu_b = zp * sb
        acc512[3:4, :] += jnp.sum(dyb * mixed * silu_b, axis=0, keepdims=True)
        dmixedb = (dyb * ps_ref[...] * silu_b).astype(BF16)
        dproj_ref[:, 5 * AW:6 * AW] = (dyb * mixed * ps_ref[...] * (sb * (1.0 + zp * (1.0 - sb)))).astype(BF16)
        pos = tile * tm + lax.broadcasted_iota(jnp.int32, (tm, 1), 0)
        for g, w in enumerate(POOL_WINDOWS):
            cols = slice(g * GD, (g + 1) * GD)
            dpw_acc[g] += _tn(pooled_ref[:, cols], dmixedb[:, cols])
            dpooled = _nt(dmixedb[:, cols], pw_ref[g])
            q_ext[0:tm, cols] = dpooled / jnp.minimum(pos + 1, w).astype(F32)
            s = q_ext[:, cols]
            for k in range(g + 1):
                s = s + pltpu.roll(s, L - 2 ** k, 0)
            dproj_ref[:, 4 * AW + g * GD:4 * AW + (g + 1) * GD] = (s[0:tm] - dpooled).astype(BF16)
        dconv_ext[tm:L, :] = dconv_ext[0:HALO, :]
        q_ext[tm:L, :] = q_ext[0:HALO, :]

        dx_ref[...], dpre = _proj_bwd(dproj_ref[...], win_ref, x_ref[...], dy, pre_ref[...])
        acc1024[0:1, :] += dpre
        acc1024[1:2, :] += dpost

        @pl.when(i == nt - 1)
        def _():
            small_ref[...] = jnp.zeros_like(small_ref)
            _store_rows(small_ref, S0_PRE, acc1024[0:1, :])
            _store_rows(small_ref, S0_POST, acc1024[1:2, :])
            for q in range(4):
                for k in range(3):
                    small_ref[S0_CONV + 8 * q + k:S0_CONV + 8 * q + k + 1, :] = acc512[k:k + 1, q * GD:(q + 1) * GD]
            _store_rows(small_ref, S0_PS, acc512[3:4, :])
            for g in range(4):
                small_ref[S0_PW + g * GD:S0_PW + (g + 1) * GD, :] = dpw_acc[g]

    return pl.pallas_call(
        body, name="even_bwd", grid=(nt,),
        in_specs=[_rows(tm, D, rev), _rows(tm, D, rev), _rows(tm, W3, rev), _rows(tm, AW, rev), _rows(tm, AW, rev), _rows(tm, AW, rev),
                  _rows(tm, D, rev),
                  _full((1, D)), _full((1, D)), _full((4, D, QW)), _full((3, AW)), _full((4, GD, GD)), _full((1, AW)), _full((D, D))],
        out_specs=[_rows(tm, D, rev), _rows(tm, W3, rev), _rows(tm, D, rev), _full_out((S0_ROWS, GD))],
        out_shape=[jax.ShapeDtypeStruct((S, D), F32), jax.ShapeDtypeStruct((S, W3), BF16), jax.ShapeDtypeStruct((S, D), BF16),
                   jax.ShapeDtypeStruct((S0_ROWS, GD), F32)],
        scratch_shapes=[pltpu.VMEM((L, AW), F32), pltpu.VMEM((L, AW), F32),
                        pltpu.VMEM((8, D), F32), pltpu.VMEM((8, AW), F32), pltpu.VMEM((4, GD, GD), F32)],
        compiler_params=_params(),
    )(dx1, x, proj, conv, mixed, pooled, m, pre, post, win, cw, pwb, ps, wout)


def _owner_id(me, relation, c):
    q = jnp.bitwise_xor(me, relation)
    return (q // 2, q % 2, c)


def _small_gather_steps(small_ref, all_ref, stage, send_sems, recv_sems, local_sem):
    x, y, c = _position()
    chips = _chips(x, y)

    def slot(chip, core):
        return 4 * chip[0] + 2 * chip[1] + core

    def copy(k, src, block, to):
        return pltpu.make_async_remote_copy(src_ref=src, dst_ref=all_ref.at[block], send_sem=send_sems.at[k],
                                            recv_sem=recv_sems.at[k], device_id=to, device_id_type=MESH)

    def own_copy():
        return pltpu.make_async_copy(stage, all_ref.at[slot((x, y), c)], local_sem)

    def first_sends():
        mine = slot((x, y), c)
        return [copy(0, small_ref, mine, (x, y, 1 - c))] + [copy(1 + j, small_ref, mine, (*chip, c)) for j, chip in enumerate(chips)]

    def relays():
        return [copy(4 + j, all_ref.at[slot(chip, c)], slot(chip, c), (x, y, 1 - c)) for j, chip in enumerate(chips)]

    def start():
        for cp in first_sends():
            cp.start()
        load = pltpu.make_async_copy(small_ref, stage, local_sem)
        load.start()
        load.wait()
        own_copy().start()

    def relay():
        for j, chip in enumerate(chips):
            copy(1 + j, small_ref, slot(chip, c), (*chip, c)).wait_recv()
        for cp in relays():
            cp.start()

    def finish():
        copy(0, small_ref, slot((x, y), 1 - c), (x, y, 1 - c)).wait_recv()
        for j, chip in enumerate(chips):
            copy(4 + j, small_ref, slot(chip, 1 - c), (x, y, 1 - c)).wait_recv()
        for cp in first_sends() + relays():
            cp.wait_send()
        own_copy().wait()

    return start, relay, finish


def _wgrad_layer(a_out, b_out, a_in, b_in, small, pos, name):
    S = a_in.shape[0]
    nko = S // TK
    nki = S // TK_IN
    hm = D // 2
    qr = hm // 4

    def out_index(s, pos_ref):
        return (jnp.minimum(s, nko - 1), 0)

    def a_in_index(s, pos_ref):
        return (jnp.where(s >= nko, (s - nko) % nki, 0), 0)

    def b_in_index(s, pos_ref):
        return (jnp.where(s >= nko, (s - nko) % nki, 0), jnp.bitwise_xor(pos_ref[1], 3 - jnp.maximum((s - nko) // nki, 0)))

    def body(pos_ref, ao_ref, bo_ref, a_ref, b_ref, small_ref, gout_ref, gin_ref, all_ref,
             acc, rbuf, sbuf, arr, mine, acc_o, rbuf_o, total_o, sbuf_o, arr_o, mine_o, stage,
             d2d_send, d2d_recv, ici_send, ici_recv, d2d_o_send, d2d_o_recv, ici_o_send, ici_o_recv,
             share_send, share_recv, local_sems, g_send, g_recv, g_local):
        s = pl.program_id(0)
        in_step = jnp.maximum(s - nko, 0)
        blk = jnp.where(s < nko, 0, 1 + in_step // nki)
        k = jnp.where(s < nko, s, in_step % nki)
        j = blk - 1
        x, y, c = _position()
        me = 2 * x + y
        sibling = (x, y, 1 - c)
        last = k == jnp.where(s < nko, nko - 1, nki - 1)
        gather_start, gather_relay, gather_finish = _small_gather_steps(small_ref, all_ref, stage, g_send, g_recv, g_local)

        def other_half(ref):
            return ref.at[pl.ds(pl.multiple_of((1 - c) * hm, hm), hm), :]

        def own_half(ref):
            return ref[pl.ds(pl.multiple_of(c * hm, hm), hm), :]

        def to_sibling(jj):
            return pltpu.make_async_remote_copy(
                src_ref=other_half(acc.at[jj % 2]), dst_ref=rbuf.at[jj], send_sem=d2d_send.at[jj], recv_sem=d2d_recv.at[jj],
                device_id=sibling, device_id_type=MESH)

        def to_owner(jj):
            return pltpu.make_async_remote_copy(
                src_ref=sbuf.at[jj], dst_ref=arr.at[2 - jj], send_sem=ici_send.at[jj], recv_sem=ici_recv.at[jj],
                device_id=_owner_id(me, 3 - jj, c), device_id_type=MESH)

        def out_to_sibling():
            return pltpu.make_async_remote_copy(
                src_ref=other_half(acc_o), dst_ref=rbuf_o, send_sem=d2d_o_send, recv_sem=d2d_o_recv,
                device_id=sibling, device_id_type=MESH)

        def out_to_owner(r):
            return pltpu.make_async_remote_copy(
                src_ref=sbuf_o.at[r], dst_ref=arr_o.at[r], send_sem=ici_o_send.at[r], recv_sem=ici_o_recv.at[r],
                device_id=_owner_id(me, r + 1, c), device_id_type=MESH)

        def pair_sum(jj):
            to_sibling(jj).wait_recv()
            return own_half(acc.at[jj % 2]) + rbuf[jj]

        def send_block(jj):
            sbuf[jj] = pair_sum(jj).astype(BF16)
            to_owner(jj).start()

        @pl.when(s == 0)
        def _():
            gather_start()

        @pl.when((blk == 3) & (k == 0))
        def _():
            gather_relay()

        @pl.when((blk == 0) & (k == 0))
        def _():
            acc_o[...] = jnp.zeros((D, D), F32)

        @pl.when(blk == 0)
        def _():
            acc_o[...] += _tn(ao_ref[...], bo_ref[...])

        @pl.when((blk == 0) & last)
        def _():
            out_to_sibling().start()

        @pl.when((blk >= 3) & (k == 0))
        def _():
            to_sibling(j - 2).wait_send()

        @pl.when((blk >= 1) & (k == 0))
        def _():
            acc[j % 2] = jnp.zeros((D, QW), F32)

        @pl.when(blk >= 1)
        def _():
            acc[j % 2] += _tn(a_ref[...], b_ref[...])

        @pl.when((blk >= 1) & last)
        def _():
            to_sibling(j).start()

        @pl.when((blk == 1) & last)
        def _():
            out_to_sibling().wait_recv()
            total_o[...] = own_half(acc_o) + rbuf_o[...]
            for r in range(3):
                q = jnp.bitwise_xor(me, r + 1)
                sbuf_o[r] = total_o[pl.ds(pl.multiple_of(q * qr, qr), qr), :].astype(BF16)
                out_to_owner(r).start()

        @pl.when((blk == 2) & last)
        def _():
            send_block(0)

        @pl.when((blk == 3) & last)
        def _():
            send_block(1)
            send_block(2)

        @pl.when((blk == 4) & last)
        def _():
            g_in = pair_sum(3)
            g_out = total_o[pl.ds(pl.multiple_of(me * qr, qr), qr), :]
            to_sibling(2).wait_send()
            to_sibling(3).wait_send()
            out_to_sibling().wait_send()
            for r in range(3):
                to_owner(r).wait()
                out_to_owner(r).wait()
            for r in range(3):
                g_in = g_in + arr[r].astype(F32)
                g_out = g_out + arr_o[r].astype(F32)
            mine[...] = g_in
            mine_o[...] = g_out
            copies = []
            for idx, (src, dst) in enumerate([(mine, gin_ref), (mine_o, gout_ref)]):
                copies.append(pltpu.make_async_remote_copy(
                    src_ref=src, dst_ref=dst.at[c], send_sem=share_send.at[idx], recv_sem=share_recv.at[idx],
                    device_id=sibling, device_id_type=MESH))
                copies.append(pltpu.make_async_copy(src, dst.at[c], local_sems.at[idx]))
            for cp in copies:
                cp.start()
            for cp in copies:
                cp.wait()
            gather_finish()

    any_spec = pl.BlockSpec(memory_space=pl.ANY)
    grid_spec = pltpu.PrefetchScalarGridSpec(
        num_scalar_prefetch=1, grid=(nko + 4 * nki,),
        in_specs=[pl.BlockSpec((TK, D), out_index), pl.BlockSpec((TK, D), out_index),
                  pl.BlockSpec((TK_IN, D), a_in_index), pl.BlockSpec((TK_IN, QW), b_in_index), any_spec],
        out_specs=[any_spec, any_spec, any_spec],
        scratch_shapes=[pltpu.VMEM((2, D, QW), F32), pltpu.VMEM((4, hm, QW), F32), pltpu.VMEM((3, hm, QW), BF16),
                        pltpu.VMEM((3, hm, QW), BF16), pltpu.VMEM((hm, QW), F32),
                        pltpu.VMEM((D, D), F32), pltpu.VMEM((hm, D), F32), pltpu.VMEM((hm, D), F32), pltpu.VMEM((3, qr, D), BF16),
                        pltpu.VMEM((3, qr, D), BF16), pltpu.VMEM((qr, D), F32),
                        pltpu.VMEM(small.shape, F32),
                        pltpu.SemaphoreType.DMA((4,)), pltpu.SemaphoreType.DMA((4,)),
                        pltpu.SemaphoreType.DMA((3,)), pltpu.SemaphoreType.DMA((3,)),
                        pltpu.SemaphoreType.DMA, pltpu.SemaphoreType.DMA,
                        pltpu.SemaphoreType.DMA((3,)), pltpu.SemaphoreType.DMA((3,)),
                        pltpu.SemaphoreType.DMA((2,)), pltpu.SemaphoreType.DMA((2,)), pltpu.SemaphoreType.DMA((2,)),
                        pltpu.SemaphoreType.DMA((7,)), pltpu.SemaphoreType.DMA((7,)), pltpu.SemaphoreType.DMA])
    return pl.pallas_call(
        body, name=name, grid_spec=grid_spec,
        out_shape=[jax.ShapeDtypeStruct((2, qr, D), F32), jax.ShapeDtypeStruct((2, hm, QW), F32),
                   jax.ShapeDtypeStruct((8, *small.shape), F32)],
        compiler_params=pltpu.CompilerParams(dimension_semantics=("arbitrary",), vmem_limit_bytes=VMEM_LIMIT),
    )(pos, a_out, b_out, a_in, b_in, small)


def _chips(x, y):
    return [(1 - x, y), (x, 1 - y), (1 - x, 1 - y)]


def _gather_weights(parts, split):
    n = len(parts)

    def body(*refs):
        ins, outs = refs[:n], refs[n:2 * n]
        send_sems, recv_sems, local_sems = refs[2 * n:]
        x, y, c = _position()
        me = 2 * x + y
        nbr_x, nbr_y, diag = _chips(x, y)
        id_x, id_y, id_d = (2 * chip[0] + chip[1] for chip in (nbr_x, nbr_y, diag))
        sibling = (x, y, 1 - c)

        def rows(a, ref, who, piece=None):
            r = parts[a].shape[0] // 2
            if piece is None:
                return ref.at[pl.ds(pl.multiple_of(who * r, 16), r), :]
            return ref.at[pl.ds(pl.multiple_of(who * r + piece * (r // 2), 16), r // 2), :]

        def copy(a, k, src, dst, to):
            return pltpu.make_async_remote_copy(src_ref=src, dst_ref=dst, send_sem=send_sems.at[a, k], recv_sem=recv_sems.at[a, k],
                                                device_id=to, device_id_type=MESH)

        def direct(a, k, chip):
            if split[a]:
                return copy(a, k, rows(a, ins[a], c), rows(a, outs[a].at[me], c), (*chip, c))
            return copy(a, k, ins[a], outs[a].at[me], (*chip, c))

        def arrival(a, k, src_id):
            if split[a]:
                return copy(a, k, rows(a, ins[a], c), rows(a, outs[a].at[src_id], c), (*nbr_x, c))
            return copy(a, k, ins[a], outs[a].at[src_id], (*nbr_x, c))

        def pass_on(a, k, src_id, piece, chip):
            region = rows(a, outs[a].at[src_id], c, piece)
            return copy(a, k, region, region, (*chip, c))

        def hand_over(a, k, src_id, who):
            region = rows(a, outs[a].at[src_id], who)
            return copy(a, k, region, region, sibling)

        local = [pltpu.make_async_copy(ins[a], outs[a].at[me], local_sems.at[a]) for a in range(n)]
        for cp in local:
            cp.start()
        sends = []
        for a in range(n):
            sends += [direct(a, 0, nbr_x), direct(a, 1, nbr_y)]
            if not split[a]:
                sends.append(direct(a, 2, diag))
        for cp in sends:
            cp.start()
        for a in range(n):
            arrival(a, 0, id_x).wait_recv()
            if split[a]:
                later = [pass_on(a, 3, id_x, 1, nbr_y), hand_over(a, 4, id_x, c)]
                for cp in later:
                    cp.start()
                sends += later
        for a in range(n):
            arrival(a, 1, id_y).wait_recv()
            if split[a]:
                later = [pass_on(a, 2, id_y, 0, nbr_x), hand_over(a, 5, id_y, c)]
                for cp in later:
                    cp.start()
                sends += later
        for a in range(n):
            if split[a]:
                pass_on(a, 2, id_d, 0, nbr_x).wait_recv()
                pass_on(a, 3, id_d, 1, nbr_y).wait_recv()
                cp = hand_over(a, 6, id_d, c)
                cp.start()
                sends.append(cp)
            else:
                arrival(a, 2, id_d).wait_recv()
        for a in range(n):
            if split[a]:
                for k, src_id in ((4, id_x), (5, id_y), (6, id_d)):
                    hand_over(a, k, src_id, 1 - c).wait_recv()
        for cp in sends:
            cp.wait_send()
        for cp in local:
            cp.wait()

    any_spec = pl.BlockSpec(memory_space=pl.ANY)
    return pl.pallas_call(
        body, name="gather_weights",
        in_specs=[pl.BlockSpec(memory_space=pltpu.VMEM)] * n, out_specs=[any_spec] * n,
        out_shape=[jax.ShapeDtypeStruct((4, *p.shape), p.dtype) for p in parts],
        scratch_shapes=[pltpu.SemaphoreType.DMA((n, 7)), pltpu.SemaphoreType.DMA((n, 7)), pltpu.SemaphoreType.DMA((n,))],
    )(*parts)


def _adamw(w, g, m, v):
    m = ADAM_B1 * m + (1.0 - ADAM_B1) * g
    v = ADAM_B2 * v + (1.0 - ADAM_B2) * (g * g)
    m_hat = m / (1.0 - ADAM_B1 ** ADAM_STEP)
    v_hat = v / (1.0 - ADAM_B2 ** ADAM_STEP)
    delta = -ADAM_LR * (m_hat / (jnp.sqrt(v_hat) + ADAM_EPS) + ADAM_WD * w)
    return delta, m, v


def _adamw_big(w, g, m, v, name):
    rows, cols = w.shape
    tr = 256

    def body(w_ref, g_ref, m_ref, v_ref, go_ref, d_ref, mo_ref, vo_ref):
        gv = g_ref[...]
        go_ref[...] = gv
        d_ref[...], mo_ref[...], vo_ref[...] = _adamw(w_ref[...], gv, m_ref[...], v_ref[...])

    spec = pl.BlockSpec((tr, cols), lambda i: (i, 0))
    return pl.pallas_call(
        body, name=name, grid=(rows // tr,),
        in_specs=[spec] * 4, out_specs=[spec] * 4,
        out_shape=[jax.ShapeDtypeStruct((rows, cols), F32)] * 4,
        compiler_params=pltpu.CompilerParams(dimension_semantics=("arbitrary",)),
    )(w, g, m, v)


def _adamw_small(g0, g1, weights, moms, vels):
    names = ["pre", "post", "conv", "pw", "ps", "lng", "lnb", "ws", "bs"]
    shapes = [w.shape for w in weights]

    def body(*refs):
        me = 2 * lax.axis_index("x") + lax.axis_index("y")
        g0_ref, g1_ref = refs[0], refs[1]
        w_refs, m_refs, v_refs = refs[2:11], refs[11:20], refs[20:29]
        outs = refs[29:29 + 36]
        loss_ref = refs[65]
        t0_ref, t1_ref = refs[66], refs[67]
        t0 = g0_ref[0]
        t1 = g1_ref[0]
        for d in range(1, 8):
            t0 = t0 + g0_ref[d]
            t1 = t1 + g1_ref[d]
        t0_ref[...] = t0
        t1_ref[...] = t1
        loss_ref[...] = t1_ref[S1_LOSS:S1_LOSS + 1, 0:1]
        my_conv = pl.multiple_of(S0_CONV + 8 * me, 8)
        my_ln = pl.multiple_of(S1_LN + 8 * me, 8)

        def update(idx, piece, grad):
            go, do, mo, vo = outs[4 * idx:4 * idx + 4]
            go[piece] = grad
            do[piece], mo[piece], vo[piece] = _adamw(w_refs[idx][piece], grad, m_refs[idx][piece], v_refs[idx][piece])

        for layer in range(2):
            for k in range(D // GD):
                lanes = slice(k * GD, (k + 1) * GD)
                tref, pre0, post0 = (t0_ref, S0_PRE, S0_POST) if layer == 0 else (t1_ref, S1_PRE, S1_POST)
                update(0, (slice(layer, layer + 1), lanes), tref[pre0 + k:pre0 + k + 1, :])
                update(1, (slice(layer, layer + 1), lanes), tref[post0 + k:post0 + k + 1, :])
        conv_rows = t0_ref[pl.ds(my_conv, 8), :]
        update(2, (slice(0, 3), slice(None)), conv_rows[0:3, :])
        for g in range(4):
            update(3, (g,), t0_ref[S0_PW + g * GD:S0_PW + (g + 1) * GD, :])
            update(4, (slice(0, 1), slice(g * GD, (g + 1) * GD)), t0_ref[S0_PS + g:S0_PS + g + 1, :])
        ln_rows = t1_ref[pl.ds(my_ln, 8), :]
        for k in range(2):
            update(5, (slice(0, 1), slice(k * GD, (k + 1) * GD)), ln_rows[k:k + 1, :])
            update(6, (slice(0, 1), slice(k * GD, (k + 1) * GD)), ln_rows[2 + k:3 + k, :])
        for h in range(HEADS):
            update(7, (h,), t1_ref[S1_WS + h * CHUNK:S1_WS + (h + 1) * CHUNK, :])
        update(8, (slice(None), slice(None)), t1_ref[S1_BS:S1_BS + HEADS, :])

    vm = pl.BlockSpec(memory_space=pltpu.VMEM)
    out_shape = []
    for s in shapes:
        out_shape += [jax.ShapeDtypeStruct(s, F32)] * 4
    out_shape.append(jax.ShapeDtypeStruct((1, 1), F32))
    res = pl.pallas_call(
        body, name="adamw_small",
        in_specs=[vm] * 29, out_specs=[vm] * 37, out_shape=out_shape,
        scratch_shapes=[pltpu.VMEM((S0_ROWS, GD), F32), pltpu.VMEM((S1_ROWS, GD), F32)],
        compiler_params=pltpu.CompilerParams(vmem_limit_bytes=VMEM_LIMIT),
    )(g0, g1, *weights, *moms, *vels)
    per_weight = {nm: res[4 * i:4 * i + 4] for i, nm in enumerate(names)}
    return per_weight, res[36]


def _pad8(a):
    return jnp.pad(a, ((0, 8 - a.shape[0]), (0, 0)))


def kernel(x, pre_norm, post_norm, even_w_in, even_conv_w, even_pool_w, even_pool_scale, even_w_out, odd_w_in, odd_ln_g, odd_ln_b, odd_w_s, odd_b_s, odd_w_out, loss_target, m_pre_norm, m_post_norm, m_even_w_in, m_even_conv_w, m_even_pool_w, m_even_pool_scale, m_even_w_out, m_odd_w_in, m_odd_ln_g, m_odd_ln_b, m_odd_w_s, m_odd_b_s, m_odd_w_out, v_pre_norm, v_post_norm, v_even_w_in, v_even_conv_w, v_even_pool_w, v_even_pool_scale, v_even_w_out, v_odd_w_in, v_odd_ln_g, v_odd_ln_b, v_odd_w_s, v_odd_b_s, v_odd_w_out):
    xs = x[0]
    tgt = loss_target[0]

    small_shard = jnp.concatenate([_pad8(even_conv_w[0]), _pad8(odd_ln_g.reshape(2, GD)), _pad8(odd_ln_b.reshape(2, GD))], axis=0)
    win0, wout0, shard = _gather_weights([even_w_in[0].astype(BF16), even_w_out[0].astype(BF16), small_shard], [True, True, False])
    wout0 = wout0.reshape(D, D)
    px, py, pc = _position()
    pos = jnp.stack([pc, 2 * px + py]).astype(jnp.int32)
    conv_w = shard[:, 0:3, :].transpose(1, 0, 2).reshape(3, AW)
    ln_g = shard[:, 8:10, :].reshape(1, D)
    ln_b = shard[:, 16:18, :].reshape(1, D)
    pool_wb = even_pool_w[0].astype(BF16)
    ws_tril = jnp.tril(odd_w_s[0]).astype(BF16)
    ws_tril_t = jnp.swapaxes(ws_tril, 1, 2)
    bias = jnp.broadcast_to(odd_b_s[0][:, :, None], (HEADS, CHUNK, GD))
    pre0, pre1 = pre_norm[0:1], pre_norm[1:2]
    post0, post1 = post_norm[0:1], post_norm[1:2]

    x1, proj0, m0, hb0, mixp0, conv0, mixed0, pooled0, win1, wout1 = _even_fwd(
        xs, pre0, post0, win0, conv_w, pool_wb, even_pool_scale, wout0, [odd_w_in[0].astype(BF16), odd_w_out[0].astype(BF16)])
    wout1 = wout1.reshape(D, D)
    proj1, m1, hb1, yp1, dx2, loss_part = _odd_fwd(x1, tgt, pre1, post1, win1, ln_g, ln_b, ws_tril, bias, wout1)
    dx1, dproj1, dmb1, small1 = _odd_bwd(dx2, x1, proj1, m1, loss_part, pre1, post1, win1, ln_g, ln_b, ws_tril, ws_tril_t, bias, wout1)
    g_out1, g_in1, all1 = _wgrad_layer(yp1, dmb1, hb1, dproj1, small1, pos, "wgrad_odd")
    gx, dproj0, dmb0, small0 = _even_bwd(dx1, xs, proj0, conv0, mixed0, pooled0, m0, pre0, post0, win0, conv_w, pool_wb,
                                         even_pool_scale, wout0)
    g_out0, g_in0, all0 = _wgrad_layer(mixp0, dmb0, hb0, dproj0, small0, pos, "wgrad_even")

    grads = [g_in0, g_out0, g_in1, g_out1]
    tags = ["even_in", "even_out", "odd_in", "odd_out"]
    big_w = [even_w_in[0], even_w_out[0], odd_w_in[0], odd_w_out[0]]
    big_m = [m_even_w_in[0], m_even_w_out[0], m_odd_w_in[0], m_odd_w_out[0]]
    big_v = [v_even_w_in[0], v_even_w_out[0], v_odd_w_in[0], v_odd_w_out[0]]
    big = [_adamw_big(w, g.reshape(w.shape), m, v, "adamw_" + t) for g, w, m, v, t in zip(grads, big_w, big_m, big_v, tags)]

    small_w = [pre_norm, post_norm, even_conv_w[0], even_pool_w[0], even_pool_scale, odd_ln_g, odd_ln_b, odd_w_s[0], odd_b_s[0]]
    small_m = [m_pre_norm, m_post_norm, m_even_conv_w[0], m_even_pool_w[0], m_even_pool_scale, m_odd_ln_g, m_odd_ln_b, m_odd_w_s[0], m_odd_b_s[0]]
    small_v = [v_pre_norm, v_post_norm, v_even_conv_w[0], v_even_pool_w[0], v_even_pool_scale, v_odd_ln_g, v_odd_ln_b, v_odd_w_s[0], v_odd_b_s[0]]
    sm, loss = _adamw_small(all0, all1, small_w, small_m, small_v)

    def lead(a):
        return a[None]

    per = {
        "pre_norm": sm["pre"], "post_norm": sm["post"],
        "even_w_in": [lead(a) for a in big[0]], "even_conv_w": [lead(a) for a in sm["conv"]],
        "even_pool_w": [lead(a) for a in sm["pw"]], "even_pool_scale": sm["ps"],
        "even_w_out": [lead(a) for a in big[1]], "odd_w_in": [lead(a) for a in big[2]],
        "odd_ln_g": sm["lng"], "odd_ln_b": sm["lnb"],
        "odd_w_s": [lead(a) for a in sm["ws"]], "odd_b_s": [lead(a) for a in sm["bs"]],
        "odd_w_out": [lead(a) for a in big[3]],
    }
    order = ["pre_norm", "post_norm", "even_w_in", "even_conv_w", "even_pool_w", "even_pool_scale", "even_w_out", "odd_w_in",
             "odd_ln_g", "odd_ln_b", "odd_w_s", "odd_b_s", "odd_w_out"]
    outs = [loss.reshape(()), gx[None]]
    for kind in range(4):
        outs += [per[nm][kind] for nm in order]
    return tuple(outs)
```

```python
import functools

import jax
import jax.numpy as jnp
from jax import lax
from jax.experimental import pallas as pl
from jax.experimental.pallas import tpu as pltpu

F32 = jnp.float32
BF16 = jnp.bfloat16
MESH = pl.DeviceIdType.MESH

D = 1024
W3 = 3 * D
QW = W3 // 4
AW = 512
GD = 128
CHUNK = 128
HEADS = 8
HALO = 16
POOL_WINDOWS = (2, 4, 8, 16)
EPS = 1e-6
TM_FWD = 512
TM_BWD = 256
TK = 1024
TK_IN = 2048
VMEM_LIMIT = 56 * 1024 * 1024

ADAM_LR, ADAM_B1, ADAM_B2, ADAM_EPS, ADAM_WD, ADAM_STEP = 0.001, 0.9, 0.999, 1e-08, 0.01, 10

S0_PRE, S0_POST, S0_CONV, S0_PS, S0_PW, S0_ROWS = 0, 8, 16, 48, 56, 568
S1_PRE, S1_POST, S1_LN, S1_BS, S1_WS, S1_LOSS, S1_ROWS = 0, 8, 16, 48, 56, 1080, 1088


def _nn(a, b):
    return jnp.dot(a, b, preferred_element_type=F32)


def _nt(a, b):
    return lax.dot_general(a, b, (((1,), (1,)), ((), ())), preferred_element_type=F32)


def _tn(a, b):
    return lax.dot_general(a, b, (((0,), (0,)), ((), ())), preferred_element_type=F32)


def _sigmoid(z):
    return 1.0 / (1.0 + jnp.exp(-z))


def _rms_fwd(x, g):
    r = lax.rsqrt(jnp.mean(x * x, axis=-1, keepdims=True) + EPS)
    return x * r, r


def _rms_bwd(dy, xh, r, g):
    dn = dy * g
    dx = r * (dn - xh * jnp.mean(xh * dn, axis=-1, keepdims=True))
    return dx, jnp.sum(dy * xh, axis=0, keepdims=True)


def _full(shape):
    nd = len(shape)
    return pl.BlockSpec(shape, lambda i, _n=nd: (0,) * _n, pipeline_mode=pl.Buffered(1))


def _full_out(shape):
    nd = len(shape)
    return pl.BlockSpec(shape, lambda i, _n=nd: (0,) * _n)


def _rows(tm, width, index=None):
    return pl.BlockSpec((tm, width), (lambda i: (i, 0)) if index is None else index)


def _params():
    return pltpu.CompilerParams(dimension_semantics=("arbitrary",), vmem_limit_bytes=VMEM_LIMIT)


def _position():
    x, y, c = lax.axis_index("x"), lax.axis_index("y"), lax.axis_index("c")
    return x, y, c


def _even_mix(proj_ref, hc_ext, xp_ext, cw_ref, pw_ref, ps_ref, first_row):
    tm = proj_ref.shape[0]
    xa = proj_ref[:, 0:AW]
    gb = proj_ref[:, AW:2 * AW]
    gc = proj_ref[:, 2 * AW:3 * AW]
    za = proj_ref[:, 3 * AW:4 * AW]
    xp = proj_ref[:, 4 * AW:5 * AW]
    zp = proj_ref[:, 5 * AW:6 * AW]
    hc = gc * xa
    hc_ext[HALO:, :] = hc
    e = hc_ext[...]
    conv = cw_ref[2:3, :] * hc + cw_ref[1:2, :] * pltpu.roll(e, 1, 0)[HALO:] + cw_ref[0:1, :] * pltpu.roll(e, 2, 0)[HALO:]
    sa = _sigmoid(za)
    xp_ext[HALO:, :] = xp
    pos = first_row + lax.broadcasted_iota(jnp.int32, (tm, 1), 0)
    pooled, mixed, counts = [], [], []
    for g, w in enumerate(POOL_WINDOWS):
        cols = slice(g * GD, (g + 1) * GD)
        s = xp_ext[:, cols]
        for k in range(g + 1):
            s = s + pltpu.roll(s, 2 ** k, 0)
        count = jnp.minimum(pos + 1, w).astype(F32)
        pg = s[HALO:] / count - xp[:, cols]
        pooled.append(pg.astype(BF16))
        mixed.append(_nn(pooled[-1], pw_ref[g]))
        counts.append(count)
    mixed = jnp.concatenate(mixed, axis=-1)
    sb = _sigmoid(zp)
    return dict(xa=xa, gb=gb, gc=gc, za=za, zp=zp, hc=hc, conv=conv, sa=sa, sb=sb, pooled=pooled, mixed=mixed, counts=counts)


def _half_rows(ref, rows, who):
    return ref.at[pl.ds(pl.multiple_of(who * (rows // 2), 8), rows // 2), :]


def _store_permuted(ref, value):
    for ob in range(D // GD):
        nb = 4 * (ob % 2) + ob // 2
        ref[:, nb * GD:(nb + 1) * GD] = value[:, ob * GD:(ob + 1) * GD]


def _even_fwd(x, pre, post, win, cw, pwb, ps, wout, next_shards):
    S = x.shape[0]
    tm = TM_FWD
    nt = S // tm
    relay = (3 * nt) // 4
    n = len(next_shards)
    shard_rows = [p.shape[0] for p in next_shards]

    def body(x_ref, pre_ref, post_ref, win_ref, cw_ref, pw_ref, ps_ref, wout_ref, *rest):
        shard_refs, rest = rest[:n], rest[n:]
        x1_ref, proj_ref, m_ref, hb_ref, mixp_ref, conv_ref, mixed_ref, pooled_ref = rest[:8]
        full_refs, rest = rest[8:8 + n], rest[8 + n:]
        hc_ext, xp_ext, mix_sc = rest[:3]
        stage, rest = rest[3:3 + n], rest[3 + n:]
        send_sems, recv_sems, local_sems = rest
        i = pl.program_id(0)
        px, py, pc = _position()
        me = 2 * px + py
        chips = _chips(px, py)

        def ici(a, j):
            return pltpu.make_async_remote_copy(
                src_ref=_half_rows(shard_refs[a], shard_rows[a], pc), dst_ref=_half_rows(full_refs[a].at[me], shard_rows[a], pc),
                send_sem=send_sems.at[a, j], recv_sem=recv_sems.at[a, j], device_id=(*chips[j], pc), device_id_type=MESH)

        def ici_arrival(a, j):
            src = 2 * chips[j][0] + chips[j][1]
            return pltpu.make_async_remote_copy(
                src_ref=_half_rows(shard_refs[a], shard_rows[a], pc), dst_ref=_half_rows(full_refs[a].at[src], shard_rows[a], pc),
                send_sem=send_sems.at[a, j], recv_sem=recv_sems.at[a, j], device_id=(*chips[j], pc), device_id_type=MESH)

        def relay_copy(a, j, who):
            src = 2 * chips[j][0] + chips[j][1]
            region = _half_rows(full_refs[a].at[src], shard_rows[a], who)
            return pltpu.make_async_remote_copy(
                src_ref=region, dst_ref=region, send_sem=send_sems.at[a, 3 + j], recv_sem=recv_sems.at[a, 3 + j],
                device_id=(px, py, 1 - pc), device_id_type=MESH)

        def own_copy(a):
            return pltpu.make_async_copy(stage[a], full_refs[a].at[me], local_sems.at[a])

        @pl.when(i == 0)
        def _():
            hc_ext[0:HALO, :] = jnp.zeros((HALO, AW), F32)
            xp_ext[0:HALO, :] = jnp.zeros((HALO, AW), F32)
            for a in range(n):
                for j in range(3):
                    ici(a, j).start()
            for a in range(n):
                load = pltpu.make_async_copy(shard_refs[a], stage[a], local_sems.at[a])
                load.start()
                load.wait()
                own_copy(a).start()

        @pl.when(i == relay)
        def _():
            for j in range(3):
                for a in range(n):
                    ici_arrival(a, j).wait_recv()
                    relay_copy(a, j, pc).start()

        xv = x_ref[...]
        xh, _ = _rms_fwd(xv, None)
        hb = (xh * pre_ref[...]).astype(BF16)
        hb_ref[...] = hb
        for q in range(4):
            proj_ref[:, q * QW:(q + 1) * QW] = _nn(hb, win_ref[q])
        t = _even_mix(proj_ref, hc_ext, xp_ext, cw_ref, pw_ref, ps_ref, i * tm)
        conv_ref[...] = t["conv"]
        mixed_ref[...] = t["mixed"]
        for g in range(4):
            pooled_ref[:, g * GD:(g + 1) * GD] = t["pooled"][g]
        mix_sc[:, 0:AW] = (t["gb"] * t["conv"] * (t["za"] * t["sa"])).astype(BF16)
        mix_sc[:, AW:2 * AW] = (t["mixed"] * ps_ref[...] * (t["zp"] * t["sb"])).astype(BF16)
        mix = mix_sc[...]
        _store_permuted(mixp_ref, mix)
        m = _nn(mix, wout_ref[...])
        m_ref[...] = m
        mh, _ = _rms_fwd(m, None)
        x1_ref[...] = xv + mh * post_ref[...]
        hc_ext[0:HALO, :] = hc_ext[tm:tm + HALO, :]
        xp_ext[0:HALO, :] = xp_ext[tm:tm + HALO, :]

        @pl.when(i == nt - 1)
        def _():
            for j in range(3):
                for a in range(n):
                    relay_copy(a, j, 1 - pc).wait_recv()
            for a in range(n):
                for j in range(3):
                    ici(a, j).wait_send()
                    relay_copy(a, j, pc).wait_send()
                own_copy(a).wait()

    any_spec = pl.BlockSpec(memory_space=pl.ANY)
    return pl.pallas_call(
        body, name="even_fwd", grid=(nt,),
        in_specs=[_rows(tm, D), _full((1, D)), _full((1, D)), _full((4, D, QW)), _full((3, AW)), _full((4, GD, GD)),
                  _full((1, AW)), _full((D, D))] + [any_spec] * n,
        out_specs=[_rows(tm, D), _rows(tm, W3), _rows(tm, D), _rows(tm, D), _rows(tm, D), _rows(tm, AW), _rows(tm, AW), _rows(tm, AW)]
        + [any_spec] * n,
        out_shape=[jax.ShapeDtypeStruct((S, D), F32), jax.ShapeDtypeStruct((S, W3), F32), jax.ShapeDtypeStruct((S, D), F32),
                   jax.ShapeDtypeStruct((S, D), BF16), jax.ShapeDtypeStruct((S, D), BF16),
                   jax.ShapeDtypeStruct((S, AW), F32), jax.ShapeDtypeStruct((S, AW), F32), jax.ShapeDtypeStruct((S, AW), BF16)]
        + [jax.ShapeDtypeStruct((4, *p.shape), p.dtype) for p in next_shards],
        scratch_shapes=[pltpu.VMEM((tm + HALO, AW), F32), pltpu.VMEM((tm + HALO, AW), F32), pltpu.VMEM((tm, D), BF16)]
        + [pltpu.VMEM(p.shape, p.dtype) for p in next_shards]
        + [pltpu.SemaphoreType.DMA((n, 6)), pltpu.SemaphoreType.DMA((n, 6)), pltpu.SemaphoreType.DMA((n,))],
        compiler_params=_params(),
    )(x, pre, post, win, cw, pwb, ps, wout, *next_shards)


def _chunks_side_by_side(a, h):
    return jnp.concatenate([a[n * CHUNK:(n + 1) * CHUNK, h * GD:(h + 1) * GD] for n in range(a.shape[0] // CHUNK)], axis=1)


def _odd_mix(proj_ref, lng_ref, lnb_ref, ws_ref, bias_ref, sv_ref):
    tm = proj_ref.shape[0]
    u = proj_ref[:, 0:D]
    v = proj_ref[:, D:2 * D]
    z = proj_ref[:, 2 * D:3 * D]
    mu = jnp.mean(v, axis=-1, keepdims=True)
    vc = v - mu
    rs = lax.rsqrt(jnp.mean(vc * vc, axis=-1, keepdims=True) + EPS)
    vh = vc * rs
    vnb = (vh * lng_ref[...] + lnb_ref[...]).astype(BF16)
    for h in range(HEADS):
        sv = _nn(ws_ref[h], _chunks_side_by_side(vnb, h))
        for n in range(tm // CHUNK):
            sv_ref[n * CHUNK:(n + 1) * CHUNK, h * GD:(h + 1) * GD] = sv[:, n * GD:(n + 1) * GD] + bias_ref[h]
    return dict(u=u, z=z, vh=vh, rs=rs, vnb=vnb, sz=_sigmoid(z))


def _odd_fwd(x1, tgt, pre, post, win, lng, lnb, wsb, bias, wout):
    S = x1.shape[0]
    tm = TM_FWD
    nt = S // tm

    def body(x_ref, tgt_ref, pre_ref, post_ref, win_ref, lng_ref, lnb_ref, ws_ref, bias_ref, wout_ref,
             proj_ref, m_ref, hb_ref, yb_ref, dx2_ref, loss_ref, sv_ref):
        i = pl.program_id(0)

        @pl.when(i == 0)
        def _():
            loss_ref[...] = jnp.zeros((8, GD), F32)

        xv = x_ref[...]
        xh, _ = _rms_fwd(xv, None)
        hb = (xh * pre_ref[...]).astype(BF16)
        hb_ref[...] = hb
        for q in range(4):
            proj_ref[:, q * QW:(q + 1) * QW] = _nn(hb, win_ref[q])
        t = _odd_mix(proj_ref, lng_ref, lnb_ref, ws_ref, bias_ref, sv_ref)
        yb = (t["u"] * sv_ref[...] * (t["z"] * t["sz"])).astype(BF16)
        _store_permuted(yb_ref, yb)
        m = _nn(yb, wout_ref[...])
        m_ref[...] = m
        mh, _ = _rms_fwd(m, None)
        err = xv + mh * post_ref[...] - tgt_ref[...]
        dx2_ref[...] = err * (1.0 / D)
        part = 0.5 * jnp.sum(jnp.mean(err * err, axis=-1, keepdims=True), axis=0, keepdims=True)
        loss_ref[...] += jnp.broadcast_to(part, (8, GD))

    return pl.pallas_call(
        body, name="odd_fwd", grid=(nt,),
        in_specs=[_rows(tm, D), _rows(tm, D), _full((1, D)), _full((1, D)), _full((4, D, QW)), _full((1, D)), _full((1, D)),
                  _full((HEADS, CHUNK, CHUNK)), _full((HEADS, CHUNK, GD)), _full((D, D))],
        out_specs=[_rows(tm, W3), _rows(tm, D), _rows(tm, D), _rows(tm, D), _rows(tm, D), _full_out((8, GD))],
        out_shape=[jax.ShapeDtypeStruct((S, W3), F32), jax.ShapeDtypeStruct((S, D), F32), jax.ShapeDtypeStruct((S, D), BF16),
                   jax.ShapeDtypeStruct((S, D), BF16), jax.ShapeDtypeStruct((S, D), F32), jax.ShapeDtypeStruct((8, GD), F32)],
        scratch_shapes=[pltpu.VMEM((tm, D), F32)],
        compiler_params=_params(),
    )(x1, tgt, pre, post, win, lng, lnb, wsb, bias, wout)


def _store_rows(ref, row0, value):
    r, width = value.shape
    for a in range(r):
        for k in range(width // GD):
            ref[row0 + a * (width // GD) + k:row0 + a * (width // GD) + k + 1, :] = value[a:a + 1, k * GD:(k + 1) * GD]


def _proj_bwd(dproj, win_ref, x, dy, pre):
    dh = _nt(dproj[:, 0:QW], win_ref[0])
    for q in range(1, 4):
        dh += _nt(dproj[:, q * QW:(q + 1) * QW], win_ref[q])
    xh, r = _rms_fwd(x, None)
    dxn, dpre = _rms_bwd(dh, xh, r, pre)
    return dy + dxn, dpre


def _odd_bwd(dx2, x1, proj, m, loss, pre, post, win, lng, lnb, wsb, wsbt, bias, wout):
    S = x1.shape[0]
    tm = TM_BWD
    nt = S // tm

    def body(dy_ref, x_ref, proj_ref, m_ref, loss_ref, pre_ref, post_ref, win_ref, lng_ref, lnb_ref, ws_ref, wst_ref, bias_ref,
             wout_ref, dx_ref, dproj_ref, dmb_ref, small_ref, sv_ref, dvn_ref, acc1024, dws_acc, dbs_acc):
        i = pl.program_id(0)

        @pl.when(i == 0)
        def _():
            acc1024[...] = jnp.zeros_like(acc1024)
            dws_acc[...] = jnp.zeros_like(dws_acc)
            dbs_acc[...] = jnp.zeros_like(dbs_acc)

        dy = dy_ref[...]
        mh, rm = _rms_fwd(m_ref[...], None)
        dm, dpost = _rms_bwd(dy, mh, rm, post_ref[...])
        dmb = dm.astype(BF16)
        dmb_ref[...] = dmb
        dyv = _nt(dmb, wout_ref[...])
        t = _odd_mix(proj_ref, lng_ref, lnb_ref, ws_ref, bias_ref, sv_ref)
        u, z, sz, sv = t["u"], t["z"], t["sz"], sv_ref[...]
        dproj_ref[:, 0:D] = (dyv * sv * (z * sz)).astype(BF16)
        dproj_ref[:, 2 * D:3 * D] = (dyv * u * sv * (sz * (1.0 + z * (1.0 - sz)))).astype(BF16)
        dsv = dyv * u * (z * sz)
        dsvb = dsv.astype(BF16)
        for h in range(HEADS):
            dsv_h = _chunks_side_by_side(dsvb, h)
            dvn_h = _nn(wst_ref[h], dsv_h)
            dws_acc[h] += _nt(dsv_h, _chunks_side_by_side(t["vnb"], h))
            for n in range(tm // CHUNK):
                rows, cols = slice(n * CHUNK, (n + 1) * CHUNK), slice(h * GD, (h + 1) * GD)
                dvn_ref[rows, cols] = dvn_h[:, n * GD:(n + 1) * GD]
                dbs_acc[h] += dsv[rows, cols]
        dvn = dvn_ref[...]
        vh = t["vh"]
        dvh = dvn * lng_ref[...]
        dv = t["rs"] * (dvh - jnp.mean(dvh, axis=-1, keepdims=True) - vh * jnp.mean(dvh * vh, axis=-1, keepdims=True))
        dproj_ref[:, D:2 * D] = dv.astype(BF16)
        dx_ref[...], dpre = _proj_bwd(dproj_ref[...], win_ref, x_ref[...], dy, pre_ref[...])
        acc1024[0:1, :] += dpre
        acc1024[1:2, :] += dpost
        acc1024[2:3, :] += jnp.sum(dvn * vh, axis=0, keepdims=True)
        acc1024[3:4, :] += jnp.sum(dvn, axis=0, keepdims=True)

        @pl.when(i == nt - 1)
        def _():
            small_ref[...] = jnp.zeros_like(small_ref)
            _store_rows(small_ref, S1_PRE, acc1024[0:1, :])
            _store_rows(small_ref, S1_POST, acc1024[1:2, :])
            for q in range(4):
                _store_rows(small_ref, S1_LN + 8 * q, acc1024[2:3, 2 * q * GD:(2 * q + 2) * GD])
                _store_rows(small_ref, S1_LN + 8 * q + 2, acc1024[3:4, 2 * q * GD:(2 * q + 2) * GD])
            lower = lax.broadcasted_iota(jnp.int32, (CHUNK, CHUNK), 0) >= lax.broadcasted_iota(jnp.int32, (CHUNK, CHUNK), 1)
            for h in range(HEADS):
                small_ref[S1_WS + h * CHUNK:S1_WS + (h + 1) * CHUNK, :] = jnp.where(lower, dws_acc[h], 0.0)
                small_ref[S1_BS + h:S1_BS + h + 1, :] = jnp.sum(dbs_acc[h].T, axis=0, keepdims=True)
            small_ref[S1_LOSS:S1_LOSS + 8, :] = loss_ref[...]

    return pl.pallas_call(
        body, name="odd_bwd", grid=(nt,),
        in_specs=[_rows(tm, D), _rows(tm, D), _rows(tm, W3), _rows(tm, D), _full((8, GD)), _full((1, D)), _full((1, D)),
                  _full((4, D, QW)), _full((1, D)), _full((1, D)), _full((HEADS, CHUNK, CHUNK)), _full((HEADS, CHUNK, CHUNK)),
                  _full((HEADS, CHUNK, GD)), _full((D, D))],
        out_specs=[_rows(tm, D), _rows(tm, W3), _rows(tm, D), _full_out((S1_ROWS, GD))],
        out_shape=[jax.ShapeDtypeStruct((S, D), F32), jax.ShapeDtypeStruct((S, W3), BF16), jax.ShapeDtypeStruct((S, D), BF16),
                   jax.ShapeDtypeStruct((S1_ROWS, GD), F32)],
        scratch_shapes=[pltpu.VMEM((tm, D), F32), pltpu.VMEM((tm, D), F32), pltpu.VMEM((8, D), F32),
                        pltpu.VMEM((HEADS, CHUNK, CHUNK), F32), pltpu.VMEM((HEADS, CHUNK, GD), F32)],
        compiler_params=_params(),
    )(dx2, x1, proj, m, loss, pre, post, win, lng, lnb, wsb, wsbt, bias, wout)


def _even_bwd(dx1, x, proj, conv, mixed, pooled, m, pre, post, win, cw, pwb, ps, wout):
    S = x.shape[0]
    tm = TM_BWD
    nt = S // tm
    L = tm + HALO

    def rev(i):
        return (nt - 1 - i, 0)

    def body(dy_ref, x_ref, proj_ref, conv_ref, mixed_ref, pooled_ref, m_ref, pre_ref, post_ref, win_ref, cw_ref, pw_ref, ps_ref,
             wout_ref, dx_ref, dproj_ref, dmb_ref, small_ref, dconv_ext, q_ext, acc1024, acc512, dpw_acc):
        i = pl.program_id(0)
        tile = nt - 1 - i

        @pl.when(i == 0)
        def _():
            dconv_ext[tm:L, :] = jnp.zeros((HALO, AW), F32)
            q_ext[tm:L, :] = jnp.zeros((HALO, AW), F32)
            acc1024[...] = jnp.zeros_like(acc1024)
            acc512[...] = jnp.zeros_like(acc512)
            dpw_acc[...] = jnp.zeros_like(dpw_acc)

        dy = dy_ref[...]
        mh, rm = _rms_fwd(m_ref[...], None)
        dm, dpost = _rms_bwd(dy, mh, rm, post_ref[...])
        dmb = dm.astype(BF16)
        dmb_ref[...] = dmb
        dmix = _nt(dmb, wout_ref[...])
        dya, dyb = dmix[:, 0:AW], dmix[:, AW:2 * AW]
        xa, gb, gc, za = (proj_ref[:, k * AW:(k + 1) * AW] for k in range(4))
        zp = proj_ref[:, 5 * AW:6 * AW]
        hc = gc * xa
        conv = conv_ref[...]
        sa = _sigmoid(za)
        silu_a = za * sa
        dproj_ref[:, AW:2 * AW] = (dya * conv * silu_a).astype(BF16)
        dproj_ref[:, 3 * AW:4 * AW] = (dya * gb * conv * (sa * (1.0 + za * (1.0 - sa)))).astype(BF16)
        dconv = dya * gb * silu_a
        dconv_ext[0:tm, :] = dconv
        e = dconv_ext[...]
        dc1 = pltpu.roll(e, L - 1, 0)[0:tm]
        dc2 = pltpu.roll(e, L - 2, 0)[0:tm]
        dhc = cw_ref[2:3, :] * dconv + cw_ref[1:2, :] * dc1 + cw_ref[0:1, :] * dc2
        dproj_ref[:, 0:AW] = (dhc * gc).astype(BF16)
        dproj_ref[:, 2 * AW:3 * AW] = (dhc * xa).astype(BF16)
        acc512[0:1, :] += jnp.sum(dc2 * hc, axis=0, keepdims=True)
        acc512[1:2, :] += jnp.sum(dc1 * hc, axis=0, keepdims=True)
        acc512[2:3, :] += jnp.sum(dconv * hc, axis=0, keepdims=True)

        sb, mixed = _sigmoid(zp), mixed_ref[...]
        silu_b = zp * sb
        acc512[3:4, :] += jnp.sum(dyb * mixed * silu_b, axis=0, keepdims=True)
        dmixedb = (dyb * ps_ref[...] * silu_b).astype(BF16)
        dproj_ref[:, 5 * AW:6 * AW] = (dyb * mixed * ps_ref[...] * (sb * (1.0 + zp * (1.0 - sb)))).astype(BF16)
        pos = tile * tm + lax.broadcasted_iota(jnp.int32, (tm, 1), 0)
        for g, w in enumerate(POOL_WINDOWS):
            cols = slice(g * GD, (g + 1) * GD)
            dpw_acc[g] += _tn(pooled_ref[:, cols], dmixedb[:, cols])
            dpooled = _nt(dmixedb[:, cols], pw_ref[g])
            q_ext[0:tm, cols] = dpooled / jnp.minimum(pos + 1, w).astype(F32)
            s = q_ext[:, cols]
            for k in range(g + 1):
                s = s + pltpu.roll(s, L - 2 ** k, 0)
            dproj_ref[:, 4 * AW + g * GD:4 * AW + (g + 1) * GD] = (s[0:tm] - dpooled).astype(BF16)
        dconv_ext[tm:L, :] = dconv_ext[0:HALO, :]
        q_ext[tm:L, :] = q_ext[0:HALO, :]

        dx_ref[...], dpre = _proj_bwd(dproj_ref[...], win_ref, x_ref[...], dy, pre_ref[...])
        acc1024[0:1, :] += dpre
        acc1024[1:2, :] += dpost

        @pl.when(i == nt - 1)
        def _():
            small_ref[...] = jnp.zeros_like(small_ref)
            _store_rows(small_ref, S0_PRE, acc1024[0:1, :])
            _store_rows(small_ref, S0_POST, acc1024[1:2, :])
            for q in range(4):
                for k in range(3):
                    small_ref[S0_CONV + 8 * q + k:S0_CONV + 8 * q + k + 1, :] = acc512[k:k + 1, q * GD:(q + 1) * GD]
            _store_rows(small_ref, S0_PS, acc512[3:4, :])
            for g in range(4):
                small_ref[S0_PW + g * GD:S0_PW + (g + 1) * GD, :] = dpw_acc[g]

    return pl.pallas_call(
        body, name="even_bwd", grid=(nt,),
        in_specs=[_rows(tm, D, rev), _rows(tm, D, rev), _rows(tm, W3, rev), _rows(tm, AW, rev), _rows(tm, AW, rev), _rows(tm, AW, rev),
                  _rows(tm, D, rev),
                  _full((1, D)), _full((1, D)), _full((4, D, QW)), _full((3, AW)), _full((4, GD, GD)), _full((1, AW)), _full((D, D))],
        out_specs=[_rows(tm, D, rev), _rows(tm, W3, rev), _rows(tm, D, rev), _full_out((S0_ROWS, GD))],
        out_shape=[jax.ShapeDtypeStruct((S, D), F32), jax.ShapeDtypeStruct((S, W3), BF16), jax.ShapeDtypeStruct((S, D), BF16),
                   jax.ShapeDtypeStruct((S0_ROWS, GD), F32)],
        scratch_shapes=[pltpu.VMEM((L, AW), F32), pltpu.VMEM((L, AW), F32),
                        pltpu.VMEM((8, D), F32), pltpu.VMEM((8, AW), F32), pltpu.VMEM((4, GD, GD), F32)],
        compiler_params=_params(),
    )(dx1, x, proj, conv, mixed, pooled, m, pre, post, win, cw, pwb, ps, wout)


def _owner_id(me, relation, c):
    q = jnp.bitwise_xor(me, relation)
    return (q // 2, q % 2, c)


def _small_gather_steps(small_ref, all_ref, stage, send_sems, recv_sems, local_sem):
    x, y, c = _position()
    chips = _chips(x, y)

    def slot(chip, core):
        return 4 * chip[0] + 2 * chip[1] + core

    def copy(k, src, block, to):
        return pltpu.make_async_remote_copy(src_ref=src, dst_ref=all_ref.at[block], send_sem=send_sems.at[k],
                                            recv_sem=recv_sems.at[k], device_id=to, device_id_type=MESH)

    def own_copy():
        return pltpu.make_async_copy(stage, all_ref.at[slot((x, y), c)], local_sem)

    def first_sends():
        mine = slot((x, y), c)
        return [copy(0, small_ref, mine, (x, y, 1 - c))] + [copy(1 + j, small_ref, mine, (*chip, c)) for j, chip in enumerate(chips)]

    def relays():
        return [copy(4 + j, all_ref.at[slot(chip, c)], slot(chip, c), (x, y, 1 - c)) for j, chip in enumerate(chips)]

    def start():
        for cp in first_sends():
            cp.start()
        load = pltpu.make_async_copy(small_ref, stage, local_sem)
        load.start()
        load.wait()
        own_copy().start()

    def relay():
        for j, chip in enumerate(chips):
            copy(1 + j, small_ref, slot(chip, c), (*chip, c)).wait_recv()
        for cp in relays():
            cp.start()

    def finish():
        copy(0, small_ref, slot((x, y), 1 - c), (x, y, 1 - c)).wait_recv()
        for j, chip in enumerate(chips):
            copy(4 + j, small_ref, slot(chip, 1 - c), (x, y, 1 - c)).wait_recv()
        for cp in first_sends() + relays():
            cp.wait_send()
        own_copy().wait()

    return start, relay, finish


def _wgrad_layer(a_out, b_out, a_in, b_in, small, pos, name):
    S = a_in.shape[0]
    nko = S // TK
    nki = S // TK_IN
    hm = D // 2
    qr = hm // 4

    def out_index(s, pos_ref):
        return (jnp.minimum(s, nko - 1), 0)

    def a_in_index(s, pos_ref):
        return (jnp.where(s >= nko, (s - nko) % nki, 0), 0)

    def b_in_index(s, pos_ref):
        return (jnp.where(s >= nko, (s - nko) % nki, 0), jnp.bitwise_xor(pos_ref[1], 3 - jnp.maximum((s - nko) // nki, 0)))

    def body(pos_ref, ao_ref, bo_ref, a_ref, b_ref, small_ref, gout_ref, gin_ref, all_ref,
             acc, rbuf, sbuf, arr, mine, acc_o, rbuf_o, total_o, sbuf_o, arr_o, mine_o, stage,
             d2d_send, d2d_recv, ici_send, ici_recv, d2d_o_send, d2d_o_recv, ici_o_send, ici_o_recv,
             share_send, share_recv, local_sems, g_send, g_recv, g_local):
        s = pl.program_id(0)
        in_step = jnp.maximum(s - nko, 0)
        blk = jnp.where(s < nko, 0, 1 + in_step // nki)
        k = jnp.where(s < nko, s, in_step % nki)
        j = blk - 1
        x, y, c = _position()
        me = 2 * x + y
        sibling = (x, y, 1 - c)
        last = k == jnp.where(s < nko, nko - 1, nki - 1)
        gather_start, gather_relay, gather_finish = _small_gather_steps(small_ref, all_ref, stage, g_send, g_recv, g_local)

        def other_half(ref):
            return ref.at[pl.ds(pl.multiple_of((1 - c) * hm, hm), hm), :]

        def own_half(ref):
            return ref[pl.ds(pl.multiple_of(c * hm, hm), hm), :]

        def to_sibling(jj):
            return pltpu.make_async_remote_copy(
                src_ref=other_half(acc.at[jj % 2]), dst_ref=rbuf.at[jj], send_sem=d2d_send.at[jj], recv_sem=d2d_recv.at[jj],
                device_id=sibling, device_id_type=MESH)

        def to_owner(jj):
            return pltpu.make_async_remote_copy(
                src_ref=sbuf.at[jj], dst_ref=arr.at[2 - jj], send_sem=ici_send.at[jj], recv_sem=ici_recv.at[jj],
                device_id=_owner_id(me, 3 - jj, c), device_id_type=MESH)

        def out_to_sibling():
            return pltpu.make_async_remote_copy(
                src_ref=other_half(acc_o), dst_ref=rbuf_o, send_sem=d2d_o_send, recv_sem=d2d_o_recv,
                device_id=sibling, device_id_type=MESH)

        def out_to_owner(r):
            return pltpu.make_async_remote_copy(
                src_ref=sbuf_o.at[r], dst_ref=arr_o.at[r], send_sem=ici_o_send.at[r], recv_sem=ici_o_recv.at[r],
                device_id=_owner_id(me, r + 1, c), device_id_type=MESH)

        def pair_sum(jj):
            to_sibling(jj).wait_recv()
            return own_half(acc.at[jj % 2]) + rbuf[jj]

        def send_block(jj):
            sbuf[jj] = pair_sum(jj).astype(BF16)
            to_owner(jj).start()

        @pl.when(s == 0)
        def _():
            gather_start()

        @pl.when((blk == 3) & (k == 0))
        def _():
            gather_relay()

        @pl.when((blk == 0) & (k == 0))
        def _():
            acc_o[...] = jnp.zeros((D, D), F32)

        @pl.when(blk == 0)
        def _():
            acc_o[...] += _tn(ao_ref[...], bo_ref[...])

        @pl.when((blk == 0) & last)
        def _():
            out_to_sibling().start()

        @pl.when((blk >= 3) & (k == 0))
        def _():
            to_sibling(j - 2).wait_send()

        @pl.when((blk >= 1) & (k == 0))
        def _():
            acc[j % 2] = jnp.zeros((D, QW), F32)

        @pl.when(blk >= 1)
        def _():
            acc[j % 2] += _tn(a_ref[...], b_ref[...])

        @pl.when((blk >= 1) & last)
        def _():
            to_sibling(j).start()

        @pl.when((blk == 1) & last)
        def _():
            out_to_sibling().wait_recv()
            total_o[...] = own_half(acc_o) + rbuf_o[...]
            for r in range(3):
                q = jnp.bitwise_xor(me, r + 1)
                sbuf_o[r] = total_o[pl.ds(pl.multiple_of(q * qr, qr), qr), :].astype(BF16)
                out_to_owner(r).start()

        @pl.when((blk == 2) & last)
        def _():
            send_block(0)

        @pl.when((blk == 3) & last)
        def _():
            send_block(1)
            send_block(2)

        @pl.when((blk == 4) & last)
        def _():
            g_in = pair_sum(3)
            g_out = total_o[pl.ds(pl.multiple_of(me * qr, qr), qr), :]
            to_sibling(2).wait_send()
            to_sibling(3).wait_send()
            out_to_sibling().wait_send()
            for r in range(3):
                to_owner(r).wait()
                out_to_owner(r).wait()
            for r in range(3):
                g_in = g_in + arr[r].astype(F32)
                g_out = g_out + arr_o[r].astype(F32)
            mine[...] = g_in
            mine_o[...] = g_out
            copies = []
            for idx, (src, dst) in enumerate([(mine, gin_ref), (mine_o, gout_ref)]):
                copies.append(pltpu.make_async_remote_copy(
                    src_ref=src, dst_ref=dst.at[c], send_sem=share_send.at[idx], recv_sem=share_recv.at[idx],
                    device_id=sibling, device_id_type=MESH))
                copies.append(pltpu.make_async_copy(src, dst.at[c], local_sems.at[idx]))
            for cp in copies:
                cp.start()
            for cp in copies:
                cp.wait()
            gather_finish()

    any_spec = pl.BlockSpec(memory_space=pl.ANY)
    grid_spec = pltpu.PrefetchScalarGridSpec(
        num_scalar_prefetch=1, grid=(nko + 4 * nki,),
        in_specs=[pl.BlockSpec((TK, D), out_index), pl.BlockSpec((TK, D), out_index),
                  pl.BlockSpec((TK_IN, D), a_in_index), pl.BlockSpec((TK_IN, QW), b_in_index), any_spec],
        out_specs=[any_spec, any_spec, any_spec],
        scratch_shapes=[pltpu.VMEM((2, D, QW), F32), pltpu.VMEM((4, hm, QW), F32), pltpu.VMEM((3, hm, QW), BF16),
                        pltpu.VMEM((3, hm, QW), BF16), pltpu.VMEM((hm, QW), F32),
                        pltpu.VMEM((D, D), F32), pltpu.VMEM((hm, D), F32), pltpu.VMEM((hm, D), F32), pltpu.VMEM((3, qr, D), BF16),
                        pltpu.VMEM((3, qr, D), BF16), pltpu.VMEM((qr, D), F32),
                        pltpu.VMEM(small.shape, F32),
                        pltpu.SemaphoreType.DMA((4,)), pltpu.SemaphoreType.DMA((4,)),
                        pltpu.SemaphoreType.DMA((3,)), pltpu.SemaphoreType.DMA((3,)),
                        pltpu.SemaphoreType.DMA, pltpu.SemaphoreType.DMA,
                        pltpu.SemaphoreType.DMA((3,)), pltpu.SemaphoreType.DMA((3,)),
                        pltpu.SemaphoreType.DMA((2,)), pltpu.SemaphoreType.DMA((2,)), pltpu.SemaphoreType.DMA((2,)),
                        pltpu.SemaphoreType.DMA((7,)), pltpu.SemaphoreType.DMA((7,)), pltpu.SemaphoreType.DMA])
    return pl.pallas_call(
        body, name=name, grid_spec=grid_spec,
        out_shape=[jax.ShapeDtypeStruct((2, qr, D), F32), jax.ShapeDtypeStruct((2, hm, QW), F32),
                   jax.ShapeDtypeStruct((8, *small.shape), F32)],
        compiler_params=pltpu.CompilerParams(dimension_semantics=("arbitrary",), vmem_limit_bytes=VMEM_LIMIT),
    )(pos, a_out, b_out, a_in, b_in, small)


def _chips(x, y):
    return [(1 - x, y), (x, 1 - y), (1 - x, 1 - y)]


def _gather_weights(parts, split):
    n = len(parts)

    def body(*refs):
        ins, outs = refs[:n], refs[n:2 * n]
        send_sems, recv_sems, local_sems = refs[2 * n:]
        x, y, c = _position()
        me = 2 * x + y
        nbr_x, nbr_y, diag = _chips(x, y)
        id_x, id_y, id_d = (2 * chip[0] + chip[1] for chip in (nbr_x, nbr_y, diag))
        sibling = (x, y, 1 - c)

        def rows(a, ref, who, piece=None):
            r = parts[a].shape[0] // 2
            if piece is None:
                return ref.at[pl.ds(pl.multiple_of(who * r, 16), r), :]
            return ref.at[pl.ds(pl.multiple_of(who * r + piece * (r // 2), 16), r // 2), :]

        def copy(a, k, src, dst, to):
            return pltpu.make_async_remote_copy(src_ref=src, dst_ref=dst, send_sem=send_sems.at[a, k], recv_sem=recv_sems.at[a, k],
                                                device_id=to, device_id_type=MESH)

        def direct(a, k, chip):
            if split[a]:
                return copy(a, k, rows(a, ins[a], c), rows(a, outs[a].at[me], c), (*chip, c))
            return copy(a, k, ins[a], outs[a].at[me], (*chip, c))

        def arrival(a, k, src_id):
            if split[a]:
                return copy(a, k, rows(a, ins[a], c), rows(a, outs[a].at[src_id], c), (*nbr_x, c))
            return copy(a, k, ins[a], outs[a].at[src_id], (*nbr_x, c))

        def pass_on(a, k, src_id, piece, chip):
            region = rows(a, outs[a].at[src_id], c, piece)
            return copy(a, k, region, region, (*chip, c))

        def hand_over(a, k, src_id, who):
            region = rows(a, outs[a].at[src_id], who)
            return copy(a, k, region, region, sibling)

        local = [pltpu.make_async_copy(ins[a], outs[a].at[me], local_sems.at[a]) for a in range(n)]
        for cp in local:
            cp.start()
        sends = []
        for a in range(n):
            sends += [direct(a, 0, nbr_x), direct(a, 1, nbr_y)]
            if not split[a]:
                sends.append(direct(a, 2, diag))
        for cp in sends:
            cp.start()
        for a in range(n):
            arrival(a, 0, id_x).wait_recv()
            if split[a]:
                later = [pass_on(a, 3, id_x, 1, nbr_y), hand_over(a, 4, id_x, c)]
                for cp in later:
                    cp.start()
                sends += later
        for a in range(n):
            arrival(a, 1, id_y).wait_recv()
            if split[a]:
                later = [pass_on(a, 2, id_y, 0, nbr_x), hand_over(a, 5, id_y, c)]
                for cp in later:
                    cp.start()
                sends += later
        for a in range(n):
            if split[a]:
                pass_on(a, 2, id_d, 0, nbr_x).wait_recv()
                pass_on(a, 3, id_d, 1, nbr_y).wait_recv()
                cp = hand_over(a, 6, id_d, c)
                cp.start()
                sends.append(cp)
            else:
                arrival(a, 2, id_d).wait_recv()
        for a in range(n):
            if split[a]:
                for k, src_id in ((4, id_x), (5, id_y), (6, id_d)):
                    hand_over(a, k, src_id, 1 - c).wait_recv()
        for cp in sends:
            cp.wait_send()
        for cp in local:
            cp.wait()

    any_spec = pl.BlockSpec(memory_space=pl.ANY)
    return pl.pallas_call(
        body, name="gather_weights",
        in_specs=[pl.BlockSpec(memory_space=pltpu.VMEM)] * n, out_specs=[any_spec] * n,
        out_shape=[jax.ShapeDtypeStruct((4, *p.shape), p.dtype) for p in parts],
        scratch_shapes=[pltpu.SemaphoreType.DMA((n, 7)), pltpu.SemaphoreType.DMA((n, 7)), pltpu.SemaphoreType.DMA((n,))],
    )(*parts)


def _adamw(w, g, m, v):
    m = ADAM_B1 * m + (1.0 - ADAM_B1) * g
    v = ADAM_B2 * v + (1.0 - ADAM_B2) * (g * g)
    m_hat = m / (1.0 - ADAM_B1 ** ADAM_STEP)
    v_hat = v / (1.0 - ADAM_B2 ** ADAM_STEP)
    delta = -ADAM_LR * (m_hat / (jnp.sqrt(v_hat) + ADAM_EPS) + ADAM_WD * w)
    return delta, m, v


def _adamw_big(ws, gs, ms, vs):
    steps = 4
    n = len(ws)

    def body(*refs):
        ins, outs = refs[:4 * n], refs[4 * n:]
        for a in range(n):
            w_ref, g_ref, m_ref, v_ref = ins[4 * a:4 * a + 4]
            go_ref, d_ref, mo_ref, vo_ref = outs[4 * a:4 * a + 4]
            gv = g_ref[...]
            go_ref[...] = gv
            d_ref[...], mo_ref[...], vo_ref[...] = _adamw(w_ref[...], gv, m_ref[...], v_ref[...])

    specs, shapes, operands = [], [], []
    for w, g, m, v in zip(ws, gs, ms, vs):
        rows, cols = w.shape
        specs += [pl.BlockSpec((rows // steps, cols), lambda i: (i, 0))] * 4
        shapes += [jax.ShapeDtypeStruct((rows, cols), F32)] * 4
        operands += [w, g, m, v]
    res = pl.pallas_call(
        body, name="adamw_big", grid=(steps,),
        in_specs=specs, out_specs=specs, out_shape=shapes,
        compiler_params=pltpu.CompilerParams(dimension_semantics=("arbitrary",), vmem_limit_bytes=VMEM_LIMIT),
    )(*operands)
    return [res[4 * a:4 * a + 4] for a in range(n)]


def _adamw_small(g0, g1, weights, moms, vels):
    names = ["pre", "post", "conv", "pw", "ps", "lng", "lnb", "ws", "bs"]
    shapes = [w.shape for w in weights]

    def body(*refs):
        me = 2 * lax.axis_index("x") + lax.axis_index("y")
        g0_ref, g1_ref = refs[0], refs[1]
        w_refs, m_refs, v_refs = refs[2:11], refs[11:20], refs[20:29]
        outs = refs[29:29 + 36]
        loss_ref = refs[65]
        t0_ref, t1_ref = refs[66], refs[67]
        t0 = g0_ref[0]
        t1 = g1_ref[0]
        for d in range(1, 8):
            t0 = t0 + g0_ref[d]
            t1 = t1 + g1_ref[d]
        t0_ref[...] = t0
        t1_ref[...] = t1
        loss_ref[...] = t1_ref[S1_LOSS:S1_LOSS + 1, 0:1]
        my_conv = pl.multiple_of(S0_CONV + 8 * me, 8)
        my_ln = pl.multiple_of(S1_LN + 8 * me, 8)

        def update(idx, piece, grad):
            go, do, mo, vo = outs[4 * idx:4 * idx + 4]
            go[piece] = grad
            do[piece], mo[piece], vo[piece] = _adamw(w_refs[idx][piece], grad, m_refs[idx][piece], v_refs[idx][piece])

        for layer in range(2):
            for k in range(D // GD):
                lanes = slice(k * GD, (k + 1) * GD)
                tref, pre0, post0 = (t0_ref, S0_PRE, S0_POST) if layer == 0 else (t1_ref, S1_PRE, S1_POST)
                update(0, (slice(layer, layer + 1), lanes), tref[pre0 + k:pre0 + k + 1, :])
                update(1, (slice(layer, layer + 1), lanes), tref[post0 + k:post0 + k + 1, :])
        conv_rows = t0_ref[pl.ds(my_conv, 8), :]
        update(2, (slice(0, 3), 0, slice(None)), conv_rows[0:3, :])
        for g in range(4):
            update(3, (g,), t0_ref[S0_PW + g * GD:S0_PW + (g + 1) * GD, :])
            update(4, (slice(0, 1), slice(g * GD, (g + 1) * GD)), t0_ref[S0_PS + g:S0_PS + g + 1, :])
        ln_rows = t1_ref[pl.ds(my_ln, 8), :]
        for k in range(2):
            update(5, (slice(0, 1), slice(k * GD, (k + 1) * GD)), ln_rows[k:k + 1, :])
            update(6, (slice(0, 1), slice(k * GD, (k + 1) * GD)), ln_rows[2 + k:3 + k, :])
        for h in range(HEADS):
            update(7, (h,), t1_ref[S1_WS + h * CHUNK:S1_WS + (h + 1) * CHUNK, :])
        update(8, (slice(None), slice(None)), t1_ref[S1_BS:S1_BS + HEADS, :])

    vm = pl.BlockSpec(memory_space=pltpu.VMEM)
    out_shape = []
    for s in shapes:
        out_shape += [jax.ShapeDtypeStruct(s, F32)] * 4
    out_shape.append(jax.ShapeDtypeStruct((1, 1), F32))
    res = pl.pallas_call(
        body, name="adamw_small",
        in_specs=[vm] * 29, out_specs=[vm] * 37, out_shape=out_shape,
        scratch_shapes=[pltpu.VMEM((S0_ROWS, GD), F32), pltpu.VMEM((S1_ROWS, GD), F32)],
        compiler_params=pltpu.CompilerParams(vmem_limit_bytes=VMEM_LIMIT),
    )(g0, g1, *weights, *moms, *vels)
    per_weight = {nm: res[4 * i:4 * i + 4] for i, nm in enumerate(names)}
    return per_weight, res[36]


def _pad8(a):
    return jnp.pad(a, ((0, 8 - a.shape[0]), (0, 0)))


def kernel(x, pre_norm, post_norm, even_w_in, even_conv_w, even_pool_w, even_pool_scale, even_w_out, odd_w_in, odd_ln_g, odd_ln_b, odd_w_s, odd_b_s, odd_w_out, loss_target, m_pre_norm, m_post_norm, m_even_w_in, m_even_conv_w, m_even_pool_w, m_even_pool_scale, m_even_w_out, m_odd_w_in, m_odd_ln_g, m_odd_ln_b, m_odd_w_s, m_odd_b_s, m_odd_w_out, v_pre_norm, v_post_norm, v_even_w_in, v_even_conv_w, v_even_pool_w, v_even_pool_scale, v_even_w_out, v_odd_w_in, v_odd_ln_g, v_odd_ln_b, v_odd_w_s, v_odd_b_s, v_odd_w_out):
    xs = x[0]
    tgt = loss_target[0]

    small_shard = jnp.concatenate([_pad8(even_conv_w[0]), _pad8(odd_ln_g.reshape(2, GD)), _pad8(odd_ln_b.reshape(2, GD))], axis=0)
    win0, wout0, shard = _gather_weights([even_w_in[0].astype(BF16), even_w_out[0].astype(BF16), small_shard], [True, True, False])
    wout0 = wout0.reshape(D, D)
    px, py, pc = _position()
    pos = jnp.stack([pc, 2 * px + py]).astype(jnp.int32)
    conv_w = shard[:, 0:3, :].transpose(1, 0, 2).reshape(3, AW)
    ln_g = shard[:, 8:10, :].reshape(1, D)
    ln_b = shard[:, 16:18, :].reshape(1, D)
    pool_wb = even_pool_w[0].astype(BF16)
    ws_tril = jnp.tril(odd_w_s[0]).astype(BF16)
    ws_tril_t = jnp.swapaxes(ws_tril, 1, 2)
    bias = jnp.broadcast_to(odd_b_s[0][:, :, None], (HEADS, CHUNK, GD))
    pre0, pre1 = pre_norm[0:1], pre_norm[1:2]
    post0, post1 = post_norm[0:1], post_norm[1:2]

    x1, proj0, m0, hb0, mixp0, conv0, mixed0, pooled0, win1, wout1 = _even_fwd(
        xs, pre0, post0, win0, conv_w, pool_wb, even_pool_scale, wout0, [odd_w_in[0].astype(BF16), odd_w_out[0].astype(BF16)])
    wout1 = wout1.reshape(D, D)
    proj1, m1, hb1, yp1, dx2, loss_part = _odd_fwd(x1, tgt, pre1, post1, win1, ln_g, ln_b, ws_tril, bias, wout1)
    dx1, dproj1, dmb1, small1 = _odd_bwd(dx2, x1, proj1, m1, loss_part, pre1, post1, win1, ln_g, ln_b, ws_tril, ws_tril_t, bias, wout1)
    g_out1, g_in1, all1 = _wgrad_layer(yp1, dmb1, hb1, dproj1, small1, pos, "wgrad_odd")
    gx, dproj0, dmb0, small0 = _even_bwd(dx1, xs, proj0, conv0, mixed0, pooled0, m0, pre0, post0, win0, conv_w, pool_wb,
                                         even_pool_scale, wout0)
    g_out0, g_in0, all0 = _wgrad_layer(mixp0, dmb0, hb0, dproj0, small0, pos, "wgrad_even")

    big_w = [even_w_in[0], even_w_out[0], odd_w_in[0], odd_w_out[0]]
    big_g = [g.reshape(w.shape) for g, w in zip([g_in0, g_out0, g_in1, g_out1], big_w)]
    big_m = [m_even_w_in[0], m_even_w_out[0], m_odd_w_in[0], m_odd_w_out[0]]
    big_v = [v_even_w_in[0], v_even_w_out[0], v_odd_w_in[0], v_odd_w_out[0]]
    big = _adamw_big(big_w, big_g, big_m, big_v)

    def taps_first(a):
        return jnp.swapaxes(a, 0, 1)

    small_w = [pre_norm, post_norm, taps_first(even_conv_w), even_pool_w[0], even_pool_scale, odd_ln_g, odd_ln_b, odd_w_s[0], odd_b_s[0]]
    small_m = [m_pre_norm, m_post_norm, taps_first(m_even_conv_w), m_even_pool_w[0], m_even_pool_scale, m_odd_ln_g, m_odd_ln_b,
               m_odd_w_s[0], m_odd_b_s[0]]
    small_v = [v_pre_norm, v_post_norm, taps_first(v_even_conv_w), v_even_pool_w[0], v_even_pool_scale, v_odd_ln_g, v_odd_ln_b,
               v_odd_w_s[0], v_odd_b_s[0]]
    sm, loss = _adamw_small(all0, all1, small_w, small_m, small_v)

    def lead(a):
        return a[None]

    per = {
        "pre_norm": sm["pre"], "post_norm": sm["post"],
        "even_w_in": [lead(a) for a in big[0]], "even_conv_w": [taps_first(a) for a in sm["conv"]],
        "even_pool_w": [lead(a) for a in sm["pw"]], "even_pool_scale": sm["ps"],
        "even_w_out": [lead(a) for a in big[1]], "odd_w_in": [lead(a) for a in big[2]],
        "odd_ln_g": sm["lng"], "odd_ln_b": sm["lnb"],
        "odd_w_s": [lead(a) for a in sm["ws"]], "odd_b_s": [lead(a) for a in sm["bs"]],
        "odd_w_out": [lead(a) for a in big[3]],
    }
    order = ["pre_norm", "post_norm", "even_w_in", "even_conv_w", "even_pool_w", "even_pool_scale", "even_w_out", "odd_w_in",
             "odd_ln_g", "odd_ln_b", "odd_w_s", "odd_b_s", "odd_w_out"]
    outs = [loss.reshape(()), gx[None]]
    for kind in range(4):
        outs += [per[nm][kind] for nm in order]
    return tuple(outs)
```

```python
import functools

import jax
import jax.numpy as jnp
from jax import lax
from jax.experimental import pallas as pl
from jax.experimental.pallas import tpu as pltpu

F32 = jnp.float32
BF16 = jnp.bfloat16
MESH = pl.DeviceIdType.MESH

D = 1024
W3 = 3 * D
QW = W3 // 4
AW = 512
GD = 128
CHUNK = 128
HEADS = 8
HALO = 16
POOL_WINDOWS = (2, 4, 8, 16)
EPS = 1e-6
TM_FWD = 512
TM_BWD = 256
TK = 1024
TK_IN = 2048
VMEM_LIMIT = 56 * 1024 * 1024

ADAM_LR, ADAM_B1, ADAM_B2, ADAM_EPS, ADAM_WD, ADAM_STEP = 0.001, 0.9, 0.999, 1e-08, 0.01, 10

S0_PRE, S0_POST, S0_CONV, S0_PS, S0_PW, S0_ROWS = 0, 8, 16, 48, 56, 568
S1_PRE, S1_POST, S1_LN, S1_BS, S1_WS, S1_LOSS, S1_ROWS = 0, 8, 16, 48, 56, 1080, 1088


def _nn(a, b):
    return jnp.dot(a, b, preferred_element_type=F32)


def _nt(a, b):
    return lax.dot_general(a, b, (((1,), (1,)), ((), ())), preferred_element_type=F32)


def _tn(a, b):
    return lax.dot_general(a, b, (((0,), (0,)), ((), ())), preferred_element_type=F32)


def _sigmoid(z):
    return 1.0 / (1.0 + jnp.exp(-z))


def _rms_fwd(x, g):
    r = lax.rsqrt(jnp.mean(x * x, axis=-1, keepdims=True) + EPS)
    return x * r, r


def _rms_bwd(dy, xh, r, g):
    dn = dy * g
    dx = r * (dn - xh * jnp.mean(xh * dn, axis=-1, keepdims=True))
    return dx, jnp.sum(dy * xh, axis=0, keepdims=True)


def _full(shape):
    nd = len(shape)
    return pl.BlockSpec(shape, lambda i, _n=nd: (0,) * _n, pipeline_mode=pl.Buffered(1))


def _full_out(shape):
    nd = len(shape)
    return pl.BlockSpec(shape, lambda i, _n=nd: (0,) * _n)


def _rows(tm, width, index=None):
    return pl.BlockSpec((tm, width), (lambda i: (i, 0)) if index is None else index)


def _params():
    return pltpu.CompilerParams(dimension_semantics=("arbitrary",), vmem_limit_bytes=VMEM_LIMIT)


def _position():
    x, y, c = lax.axis_index("x"), lax.axis_index("y"), lax.axis_index("c")
    return x, y, c


def _even_mix(proj_ref, hc_ext, xp_ext, cw_ref, pw_ref, ps_ref, first_row):
    tm = proj_ref.shape[0]
    xa = proj_ref[:, 0:AW]
    gb = proj_ref[:, AW:2 * AW]
    gc = proj_ref[:, 2 * AW:3 * AW]
    za = proj_ref[:, 3 * AW:4 * AW]
    xp = proj_ref[:, 4 * AW:5 * AW]
    zp = proj_ref[:, 5 * AW:6 * AW]
    hc = gc * xa
    hc_ext[HALO:, :] = hc
    e = hc_ext[...]
    conv = cw_ref[2:3, :] * hc + cw_ref[1:2, :] * pltpu.roll(e, 1, 0)[HALO:] + cw_ref[0:1, :] * pltpu.roll(e, 2, 0)[HALO:]
    sa = _sigmoid(za)
    xp_ext[HALO:, :] = xp
    pos = first_row + lax.broadcasted_iota(jnp.int32, (tm, 1), 0)
    pooled, mixed, counts = [], [], []
    for g, w in enumerate(POOL_WINDOWS):
        cols = slice(g * GD, (g + 1) * GD)
        s = xp_ext[:, cols]
        for k in range(g + 1):
            s = s + pltpu.roll(s, 2 ** k, 0)
        count = jnp.minimum(pos + 1, w).astype(F32)
        pg = s[HALO:] / count - xp[:, cols]
        pooled.append(pg.astype(BF16))
        mixed.append(_nn(pooled[-1], pw_ref[g]))
        counts.append(count)
    mixed = jnp.concatenate(mixed, axis=-1)
    sb = _sigmoid(zp)
    return dict(xa=xa, gb=gb, gc=gc, za=za, zp=zp, hc=hc, conv=conv, sa=sa, sb=sb, pooled=pooled, mixed=mixed, counts=counts)


def _half_rows(ref, rows, who):
    return ref.at[pl.ds(pl.multiple_of(who * (rows // 2), 8), rows // 2), :]


def _store_permuted(ref, value):
    for ob in range(D // GD):
        nb = 4 * (ob % 2) + ob // 2
        ref[:, nb * GD:(nb + 1) * GD] = value[:, ob * GD:(ob + 1) * GD]


def _even_fwd(x, hb, post, win, cw, pwb, ps, wout, next_shards):
    S = x.shape[0]
    tm = TM_FWD
    nt = S // tm
    relay = (3 * nt) // 4
    n = len(next_shards)
    shard_rows = [p.shape[0] for p in next_shards]

    def body(x_ref, hb_ref, post_ref, win_ref, cw_ref, pw_ref, ps_ref, wout_ref, *rest):
        shard_refs, rest = rest[:n], rest[n:]
        x1_ref, proj_ref, m_ref, mixp_ref, conv_ref, mixed_ref, pooled_ref = rest[:7]
        full_refs, rest = rest[7:7 + n], rest[7 + n:]
        hc_ext, xp_ext, mix_sc = rest[:3]
        stage, rest = rest[3:3 + n], rest[3 + n:]
        send_sems, recv_sems, local_sems = rest
        i = pl.program_id(0)
        px, py, pc = _position()
        me = 2 * px + py
        chips = _chips(px, py)

        def ici(a, j):
            return pltpu.make_async_remote_copy(
                src_ref=_half_rows(shard_refs[a], shard_rows[a], pc), dst_ref=_half_rows(full_refs[a].at[me], shard_rows[a], pc),
                send_sem=send_sems.at[a, j], recv_sem=recv_sems.at[a, j], device_id=(*chips[j], pc), device_id_type=MESH)

        def ici_arrival(a, j):
            src = 2 * chips[j][0] + chips[j][1]
            return pltpu.make_async_remote_copy(
                src_ref=_half_rows(shard_refs[a], shard_rows[a], pc), dst_ref=_half_rows(full_refs[a].at[src], shard_rows[a], pc),
                send_sem=send_sems.at[a, j], recv_sem=recv_sems.at[a, j], device_id=(*chips[j], pc), device_id_type=MESH)

        def relay_copy(a, j, who):
            src = 2 * chips[j][0] + chips[j][1]
            region = _half_rows(full_refs[a].at[src], shard_rows[a], who)
            return pltpu.make_async_remote_copy(
                src_ref=region, dst_ref=region, send_sem=send_sems.at[a, 3 + j], recv_sem=recv_sems.at[a, 3 + j],
                device_id=(px, py, 1 - pc), device_id_type=MESH)

        def own_copy(a):
            return pltpu.make_async_copy(stage[a], full_refs[a].at[me], local_sems.at[a])

        @pl.when(i == 0)
        def _():
            hc_ext[0:HALO, :] = jnp.zeros((HALO, AW), F32)
            xp_ext[0:HALO, :] = jnp.zeros((HALO, AW), F32)
            for a in range(n):
                for j in range(3):
                    ici(a, j).start()
            for a in range(n):
                load = pltpu.make_async_copy(shard_refs[a], stage[a], local_sems.at[a])
                load.start()
                load.wait()
                own_copy(a).start()

        @pl.when(i == relay)
        def _():
            for j in range(3):
                for a in range(n):
                    ici_arrival(a, j).wait_recv()
                    relay_copy(a, j, pc).start()

        hb = hb_ref[...]
        for q in range(4):
            proj_ref[:, q * QW:(q + 1) * QW] = _nn(hb, win_ref[q])
        t = _even_mix(proj_ref, hc_ext, xp_ext, cw_ref, pw_ref, ps_ref, i * tm)
        conv_ref[...] = t["conv"]
        mixed_ref[...] = t["mixed"]
        for g in range(4):
            pooled_ref[:, g * GD:(g + 1) * GD] = t["pooled"][g]
        mix_sc[:, 0:AW] = (t["gb"] * t["conv"] * (t["za"] * t["sa"])).astype(BF16)
        mix_sc[:, AW:2 * AW] = (t["mixed"] * ps_ref[...] * (t["zp"] * t["sb"])).astype(BF16)
        mix = mix_sc[...]
        _store_permuted(mixp_ref, mix)
        m = _nn(mix, wout_ref[...])
        m_ref[...] = m
        mh, _ = _rms_fwd(m, None)
        x1_ref[...] = x_ref[...] + mh * post_ref[...]
        hc_ext[0:HALO, :] = hc_ext[tm:tm + HALO, :]
        xp_ext[0:HALO, :] = xp_ext[tm:tm + HALO, :]

        @pl.when(i == nt - 1)
        def _():
            for j in range(3):
                for a in range(n):
                    relay_copy(a, j, 1 - pc).wait_recv()
            for a in range(n):
                for j in range(3):
                    ici(a, j).wait_send()
                    relay_copy(a, j, pc).wait_send()
                own_copy(a).wait()

    any_spec = pl.BlockSpec(memory_space=pl.ANY)
    return pl.pallas_call(
        body, name="even_fwd", grid=(nt,),
        in_specs=[_rows(tm, D), _rows(tm, D), _full((1, D)), _full((4, D, QW)), _full((3, AW)), _full((4, GD, GD)),
                  _full((1, AW)), _full((D, D))] + [any_spec] * n,
        out_specs=[_rows(tm, D), _rows(tm, W3), _rows(tm, D), _rows(tm, D), _rows(tm, AW), _rows(tm, AW), _rows(tm, AW)]
        + [any_spec] * n,
        out_shape=[jax.ShapeDtypeStruct((S, D), F32), jax.ShapeDtypeStruct((S, W3), F32), jax.ShapeDtypeStruct((S, D), F32),
                   jax.ShapeDtypeStruct((S, D), BF16),
                   jax.ShapeDtypeStruct((S, AW), F32), jax.ShapeDtypeStruct((S, AW), F32), jax.ShapeDtypeStruct((S, AW), BF16)]
        + [jax.ShapeDtypeStruct((4, *p.shape), p.dtype) for p in next_shards],
        scratch_shapes=[pltpu.VMEM((tm + HALO, AW), F32), pltpu.VMEM((tm + HALO, AW), F32), pltpu.VMEM((tm, D), BF16)]
        + [pltpu.VMEM(p.shape, p.dtype) for p in next_shards]
        + [pltpu.SemaphoreType.DMA((n, 6)), pltpu.SemaphoreType.DMA((n, 6)), pltpu.SemaphoreType.DMA((n,))],
        compiler_params=_params(),
    )(x, hb, post, win, cw, pwb, ps, wout, *next_shards)


def _chunks_side_by_side(a, h):
    return jnp.concatenate([a[n * CHUNK:(n + 1) * CHUNK, h * GD:(h + 1) * GD] for n in range(a.shape[0] // CHUNK)], axis=1)


def _odd_mix(proj_ref, lng_ref, lnb_ref, ws_ref, bias_ref, sv_ref):
    tm = proj_ref.shape[0]
    u = proj_ref[:, 0:D]
    v = proj_ref[:, D:2 * D]
    z = proj_ref[:, 2 * D:3 * D]
    mu = jnp.mean(v, axis=-1, keepdims=True)
    vc = v - mu
    rs = lax.rsqrt(jnp.mean(vc * vc, axis=-1, keepdims=True) + EPS)
    vh = vc * rs
    vnb = (vh * lng_ref[...] + lnb_ref[...]).astype(BF16)
    for h in range(HEADS):
        sv = _nn(ws_ref[h], _chunks_side_by_side(vnb, h))
        for n in range(tm // CHUNK):
            sv_ref[n * CHUNK:(n + 1) * CHUNK, h * GD:(h + 1) * GD] = sv[:, n * GD:(n + 1) * GD] + bias_ref[h]
    return dict(u=u, z=z, vh=vh, rs=rs, vnb=vnb, sz=_sigmoid(z))


def _odd_fwd(x1, tgt, pre, post, win, lng, lnb, wsb, bias, wout):
    S = x1.shape[0]
    tm = TM_FWD
    nt = S // tm

    def body(x_ref, tgt_ref, pre_ref, post_ref, win_ref, lng_ref, lnb_ref, ws_ref, bias_ref, wout_ref,
             proj_ref, m_ref, hb_ref, yb_ref, dx2_ref, loss_ref, sv_ref):
        i = pl.program_id(0)

        @pl.when(i == 0)
        def _():
            loss_ref[...] = jnp.zeros((8, GD), F32)

        xv = x_ref[...]
        xh, _ = _rms_fwd(xv, None)
        hb = (xh * pre_ref[...]).astype(BF16)
        hb_ref[...] = hb
        for q in range(4):
            proj_ref[:, q * QW:(q + 1) * QW] = _nn(hb, win_ref[q])
        t = _odd_mix(proj_ref, lng_ref, lnb_ref, ws_ref, bias_ref, sv_ref)
        yb = (t["u"] * sv_ref[...] * (t["z"] * t["sz"])).astype(BF16)
        _store_permuted(yb_ref, yb)
        m = _nn(yb, wout_ref[...])
        m_ref[...] = m
        mh, _ = _rms_fwd(m, None)
        err = xv + mh * post_ref[...] - tgt_ref[...]
        dx2_ref[...] = err * (1.0 / D)
        part = 0.5 * jnp.sum(jnp.mean(err * err, axis=-1, keepdims=True), axis=0, keepdims=True)
        loss_ref[...] += jnp.broadcast_to(part, (8, GD))

    return pl.pallas_call(
        body, name="odd_fwd", grid=(nt,),
        in_specs=[_rows(tm, D), _rows(tm, D), _full((1, D)), _full((1, D)), _full((4, D, QW)), _full((1, D)), _full((1, D)),
                  _full((HEADS, CHUNK, CHUNK)), _full((HEADS, CHUNK, GD)), _full((D, D))],
        out_specs=[_rows(tm, W3), _rows(tm, D), _rows(tm, D), _rows(tm, D), _rows(tm, D), _full_out((8, GD))],
        out_shape=[jax.ShapeDtypeStruct((S, W3), F32), jax.ShapeDtypeStruct((S, D), F32), jax.ShapeDtypeStruct((S, D), BF16),
                   jax.ShapeDtypeStruct((S, D), BF16), jax.ShapeDtypeStruct((S, D), F32), jax.ShapeDtypeStruct((8, GD), F32)],
        scratch_shapes=[pltpu.VMEM((tm, D), F32)],
        compiler_params=_params(),
    )(x1, tgt, pre, post, win, lng, lnb, wsb, bias, wout)


def _store_rows(ref, row0, value):
    r, width = value.shape
    for a in range(r):
        for k in range(width // GD):
            ref[row0 + a * (width // GD) + k:row0 + a * (width // GD) + k + 1, :] = value[a:a + 1, k * GD:(k + 1) * GD]


def _proj_bwd(dproj, win_ref, x, dy, pre):
    dh = _nt(dproj[:, 0:QW], win_ref[0])
    for q in range(1, 4):
        dh += _nt(dproj[:, q * QW:(q + 1) * QW], win_ref[q])
    xh, r = _rms_fwd(x, None)
    dxn, dpre = _rms_bwd(dh, xh, r, pre)
    return dy + dxn, dpre


def _odd_bwd(dx2, x1, proj, m, loss, pre, post, win, lng, lnb, wsb, wsbt, bias, wout):
    S = x1.shape[0]
    tm = TM_BWD
    nt = S // tm

    def body(dy_ref, x_ref, proj_ref, m_ref, loss_ref, pre_ref, post_ref, win_ref, lng_ref, lnb_ref, ws_ref, wst_ref, bias_ref,
             wout_ref, dx_ref, dproj_ref, dmb_ref, small_ref, sv_ref, dvn_ref, acc1024, dws_acc, dbs_acc):
        i = pl.program_id(0)

        @pl.when(i == 0)
        def _():
            acc1024[...] = jnp.zeros_like(acc1024)
            dws_acc[...] = jnp.zeros_like(dws_acc)
            dbs_acc[...] = jnp.zeros_like(dbs_acc)

        dy = dy_ref[...]
        mh, rm = _rms_fwd(m_ref[...], None)
        dm, dpost = _rms_bwd(dy, mh, rm, post_ref[...])
        dmb = dm.astype(BF16)
        dmb_ref[...] = dmb
        dyv = _nt(dmb, wout_ref[...])
        t = _odd_mix(proj_ref, lng_ref, lnb_ref, ws_ref, bias_ref, sv_ref)
        u, z, sz, sv = t["u"], t["z"], t["sz"], sv_ref[...]
        dproj_ref[:, 0:D] = (dyv * sv * (z * sz)).astype(BF16)
        dproj_ref[:, 2 * D:3 * D] = (dyv * u * sv * (sz * (1.0 + z * (1.0 - sz)))).astype(BF16)
        dsv = dyv * u * (z * sz)
        dsvb = dsv.astype(BF16)
        for h in range(HEADS):
            dsv_h = _chunks_side_by_side(dsvb, h)
            dvn_h = _nn(wst_ref[h], dsv_h)
            dws_acc[h] += _nt(dsv_h, _chunks_side_by_side(t["vnb"], h))
            for n in range(tm // CHUNK):
                rows, cols = slice(n * CHUNK, (n + 1) * CHUNK), slice(h * GD, (h + 1) * GD)
                dvn_ref[rows, cols] = dvn_h[:, n * GD:(n + 1) * GD]
                dbs_acc[h] += dsv[rows, cols]
        dvn = dvn_ref[...]
        vh = t["vh"]
        dvh = dvn * lng_ref[...]
        dv = t["rs"] * (dvh - jnp.mean(dvh, axis=-1, keepdims=True) - vh * jnp.mean(dvh * vh, axis=-1, keepdims=True))
        dproj_ref[:, D:2 * D] = dv.astype(BF16)
        dx_ref[...], dpre = _proj_bwd(dproj_ref[...], win_ref, x_ref[...], dy, pre_ref[...])
        acc1024[0:1, :] += dpre
        acc1024[1:2, :] += dpost
        acc1024[2:3, :] += jnp.sum(dvn * vh, axis=0, keepdims=True)
        acc1024[3:4, :] += jnp.sum(dvn, axis=0, keepdims=True)

        @pl.when(i == nt - 1)
        def _():
            small_ref[...] = jnp.zeros_like(small_ref)
            _store_rows(small_ref, S1_PRE, acc1024[0:1, :])
            _store_rows(small_ref, S1_POST, acc1024[1:2, :])
            for q in range(4):
                _store_rows(small_ref, S1_LN + 8 * q, acc1024[2:3, 2 * q * GD:(2 * q + 2) * GD])
                _store_rows(small_ref, S1_LN + 8 * q + 2, acc1024[3:4, 2 * q * GD:(2 * q + 2) * GD])
            lower = lax.broadcasted_iota(jnp.int32, (CHUNK, CHUNK), 0) >= lax.broadcasted_iota(jnp.int32, (CHUNK, CHUNK), 1)
            for h in range(HEADS):
                small_ref[S1_WS + h * CHUNK:S1_WS + (h + 1) * CHUNK, :] = jnp.where(lower, dws_acc[h], 0.0)
                small_ref[S1_BS + h:S1_BS + h + 1, :] = jnp.sum(dbs_acc[h].T, axis=0, keepdims=True)
            small_ref[S1_LOSS:S1_LOSS + 8, :] = loss_ref[...]

    return pl.pallas_call(
        body, name="odd_bwd", grid=(nt,),
        in_specs=[_rows(tm, D), _rows(tm, D), _rows(tm, W3), _rows(tm, D), _full((8, GD)), _full((1, D)), _full((1, D)),
                  _full((4, D, QW)), _full((1, D)), _full((1, D)), _full((HEADS, CHUNK, CHUNK)), _full((HEADS, CHUNK, CHUNK)),
                  _full((HEADS, CHUNK, GD)), _full((D, D))],
        out_specs=[_rows(tm, D), _rows(tm, W3), _rows(tm, D), _full_out((S1_ROWS, GD))],
        out_shape=[jax.ShapeDtypeStruct((S, D), F32), jax.ShapeDtypeStruct((S, W3), BF16), jax.ShapeDtypeStruct((S, D), BF16),
                   jax.ShapeDtypeStruct((S1_ROWS, GD), F32)],
        scratch_shapes=[pltpu.VMEM((tm, D), F32), pltpu.VMEM((tm, D), F32), pltpu.VMEM((8, D), F32),
                        pltpu.VMEM((HEADS, CHUNK, CHUNK), F32), pltpu.VMEM((HEADS, CHUNK, GD), F32)],
        compiler_params=_params(),
    )(dx2, x1, proj, m, loss, pre, post, win, lng, lnb, wsb, wsbt, bias, wout)


def _even_bwd(dx1, x, proj, conv, mixed, pooled, m, pre, post, win, cw, pwb, ps, wout):
    S = x.shape[0]
    tm = TM_BWD
    nt = S // tm
    L = tm + HALO

    def rev(i):
        return (nt - 1 - i, 0)

    def body(dy_ref, x_ref, proj_ref, conv_ref, mixed_ref, pooled_ref, m_ref, pre_ref, post_ref, win_ref, cw_ref, pw_ref, ps_ref,
             wout_ref, dx_ref, dproj_ref, dmb_ref, small_ref, dconv_ext, q_ext, acc1024, acc512, dpw_acc):
        i = pl.program_id(0)
        tile = nt - 1 - i

        @pl.when(i == 0)
        def _():
            dconv_ext[tm:L, :] = jnp.zeros((HALO, AW), F32)
            q_ext[tm:L, :] = jnp.zeros((HALO, AW), F32)
            acc1024[...] = jnp.zeros_like(acc1024)
            acc512[...] = jnp.zeros_like(acc512)
            dpw_acc[...] = jnp.zeros_like(dpw_acc)

        dy = dy_ref[...]
        mh, rm = _rms_fwd(m_ref[...], None)
        dm, dpost = _rms_bwd(dy, mh, rm, post_ref[...])
        dmb = dm.astype(BF16)
        dmb_ref[...] = dmb
        dmix = _nt(dmb, wout_ref[...])
        dya, dyb = dmix[:, 0:AW], dmix[:, AW:2 * AW]
        xa, gb, gc, za = (proj_ref[:, k * AW:(k + 1) * AW] for k in range(4))
        zp = proj_ref[:, 5 * AW:6 * AW]
        hc = gc * xa
        conv = conv_ref[...]
        sa = _sigmoid(za)
        silu_a = za * sa
        dproj_ref[:, AW:2 * AW] = (dya * conv * silu_a).astype(BF16)
        dproj_ref[:, 3 * AW:4 * AW] = (dya * gb * conv * (sa * (1.0 + za * (1.0 - sa)))).astype(BF16)
        dconv = dya * gb * silu_a
        dconv_ext[0:tm, :] = dconv
        e = dconv_ext[...]
        dc1 = pltpu.roll(e, L - 1, 0)[0:tm]
        dc2 = pltpu.roll(e, L - 2, 0)[0:tm]
        dhc = cw_ref[2:3, :] * dconv + cw_ref[1:2, :] * dc1 + cw_ref[0:1, :] * dc2
        dproj_ref[:, 0:AW] = (dhc * gc).astype(BF16)
        dproj_ref[:, 2 * AW:3 * AW] = (dhc * xa).astype(BF16)
        acc512[0:1, :] += jnp.sum(dc2 * hc, axis=0, keepdims=True)
        acc512[1:2, :] += jnp.sum(dc1 * hc, axis=0, keepdims=True)
        acc512[2:3, :] += jnp.sum(dconv * hc, axis=0, keepdims=True)

        sb, mixed = _sigmoid(zp), mixed_ref[...]
        silu_b = zp * sb
        acc512[3:4, :] += jnp.sum(dyb * mixed * silu_b, axis=0, keepdims=True)
        dmixedb = (dyb * ps_ref[...] * silu_b).astype(BF16)
        dproj_ref[:, 5 * AW:6 * AW] = (dyb * mixed * ps_ref[...] * (sb * (1.0 + zp * (1.0 - sb)))).astype(BF16)
        pos = tile * tm + lax.broadcasted_iota(jnp.int32, (tm, 1), 0)
        for g, w in enumerate(POOL_WINDOWS):
            cols = slice(g * GD, (g + 1) * GD)
            dpw_acc[g] += _tn(pooled_ref[:, cols], dmixedb[:, cols])
            dpooled = _nt(dmixedb[:, cols], pw_ref[g])
            q_ext[0:tm, cols] = dpooled / jnp.minimum(pos + 1, w).astype(F32)
            s = q_ext[:, cols]
            for k in range(g + 1):
                s = s + pltpu.roll(s, L - 2 ** k, 0)
            dproj_ref[:, 4 * AW + g * GD:4 * AW + (g + 1) * GD] = (s[0:tm] - dpooled).astype(BF16)
        dconv_ext[tm:L, :] = dconv_ext[0:HALO, :]
        q_ext[tm:L, :] = q_ext[0:HALO, :]

        dx_ref[...], dpre = _proj_bwd(dproj_ref[...], win_ref, x_ref[...], dy, pre_ref[...])
        acc1024[0:1, :] += dpre
        acc1024[1:2, :] += dpost

        @pl.when(i == nt - 1)
        def _():
            small_ref[...] = jnp.zeros_like(small_ref)
            _store_rows(small_ref, S0_PRE, acc1024[0:1, :])
            _store_rows(small_ref, S0_POST, acc1024[1:2, :])
            for q in range(4):
                for k in range(3):
                    small_ref[S0_CONV + 8 * q + k:S0_CONV + 8 * q + k + 1, :] = acc512[k:k + 1, q * GD:(q + 1) * GD]
            _store_rows(small_ref, S0_PS, acc512[3:4, :])
            for g in range(4):
                small_ref[S0_PW + g * GD:S0_PW + (g + 1) * GD, :] = dpw_acc[g]

    return pl.pallas_call(
        body, name="even_bwd", grid=(nt,),
        in_specs=[_rows(tm, D, rev), _rows(tm, D, rev), _rows(tm, W3, rev), _rows(tm, AW, rev), _rows(tm, AW, rev), _rows(tm, AW, rev),
                  _rows(tm, D, rev),
                  _full((1, D)), _full((1, D)), _full((4, D, QW)), _full((3, AW)), _full((4, GD, GD)), _full((1, AW)), _full((D, D))],
        out_specs=[_rows(tm, D, rev), _rows(tm, W3, rev), _rows(tm, D, rev), _full_out((S0_ROWS, GD))],
        out_shape=[jax.ShapeDtypeStruct((S, D), F32), jax.ShapeDtypeStruct((S, W3), BF16), jax.ShapeDtypeStruct((S, D), BF16),
                   jax.ShapeDtypeStruct((S0_ROWS, GD), F32)],
        scratch_shapes=[pltpu.VMEM((L, AW), F32), pltpu.VMEM((L, AW), F32),
                        pltpu.VMEM((8, D), F32), pltpu.VMEM((8, AW), F32), pltpu.VMEM((4, GD, GD), F32)],
        compiler_params=_params(),
    )(dx1, x, proj, conv, mixed, pooled, m, pre, post, win, cw, pwb, ps, wout)


def _owner_id(me, relation, c):
    q = jnp.bitwise_xor(me, relation)
    return (q // 2, q % 2, c)


def _small_gather_steps(small_ref, all_ref, stage, send_sems, recv_sems, local_sem):
    x, y, c = _position()
    chips = _chips(x, y)

    def slot(chip, core):
        return 4 * chip[0] + 2 * chip[1] + core

    def copy(k, src, block, to):
        return pltpu.make_async_remote_copy(src_ref=src, dst_ref=all_ref.at[block], send_sem=send_sems.at[k],
                                            recv_sem=recv_sems.at[k], device_id=to, device_id_type=MESH)

    def own_copy():
        return pltpu.make_async_copy(stage, all_ref.at[slot((x, y), c)], local_sem)

    def first_sends():
        mine = slot((x, y), c)
        return [copy(0, small_ref, mine, (x, y, 1 - c))] + [copy(1 + j, small_ref, mine, (*chip, c)) for j, chip in enumerate(chips)]

    def relays():
        return [copy(4 + j, all_ref.at[slot(chip, c)], slot(chip, c), (x, y, 1 - c)) for j, chip in enumerate(chips)]

    def start():
        for cp in first_sends():
            cp.start()
        load = pltpu.make_async_copy(small_ref, stage, local_sem)
        load.start()
        load.wait()
        own_copy().start()

    def relay():
        for j, chip in enumerate(chips):
            copy(1 + j, small_ref, slot(chip, c), (*chip, c)).wait_recv()
        for cp in relays():
            cp.start()

    def finish():
        copy(0, small_ref, slot((x, y), 1 - c), (x, y, 1 - c)).wait_recv()
        for j, chip in enumerate(chips):
            copy(4 + j, small_ref, slot(chip, 1 - c), (x, y, 1 - c)).wait_recv()
        for cp in first_sends() + relays():
            cp.wait_send()
        own_copy().wait()

    return start, relay, finish


def _wgrad_layer(a_out, b_out, a_in, b_in, small, pos, name):
    S = a_in.shape[0]
    nko = S // TK
    nki = S // TK_IN
    hm = D // 2
    qr = hm // 4

    def out_index(s, pos_ref):
        return (jnp.minimum(s, nko - 1), 0)

    def a_in_index(s, pos_ref):
        return (jnp.where(s >= nko, (s - nko) % nki, 0), 0)

    def b_in_index(s, pos_ref):
        return (jnp.where(s >= nko, (s - nko) % nki, 0), jnp.bitwise_xor(pos_ref[1], 3 - jnp.maximum((s - nko) // nki, 0)))

    def body(pos_ref, ao_ref, bo_ref, a_ref, b_ref, small_ref, gout_ref, gin_ref, all_ref,
             acc, rbuf, sbuf, arr, mine, acc_o, rbuf_o, total_o, sbuf_o, arr_o, mine_o, stage,
             d2d_send, d2d_recv, ici_send, ici_recv, d2d_o_send, d2d_o_recv, ici_o_send, ici_o_recv,
             share_send, share_recv, local_sems, g_send, g_recv, g_local):
        s = pl.program_id(0)
        in_step = jnp.maximum(s - nko, 0)
        blk = jnp.where(s < nko, 0, 1 + in_step // nki)
        k = jnp.where(s < nko, s, in_step % nki)
        j = blk - 1
        x, y, c = _position()
        me = 2 * x + y
        sibling = (x, y, 1 - c)
        last = k == jnp.where(s < nko, nko - 1, nki - 1)
        gather_start, gather_relay, gather_finish = _small_gather_steps(small_ref, all_ref, stage, g_send, g_recv, g_local)

        def other_half(ref):
            return ref.at[pl.ds(pl.multiple_of((1 - c) * hm, hm), hm), :]

        def own_half(ref):
            return ref[pl.ds(pl.multiple_of(c * hm, hm), hm), :]

        def to_sibling(jj):
            return pltpu.make_async_remote_copy(
                src_ref=other_half(acc.at[jj % 2]), dst_ref=rbuf.at[jj], send_sem=d2d_send.at[jj], recv_sem=d2d_recv.at[jj],
                device_id=sibling, device_id_type=MESH)

        def to_owner(jj):
            return pltpu.make_async_remote_copy(
                src_ref=sbuf.at[jj], dst_ref=arr.at[2 - jj], send_sem=ici_send.at[jj], recv_sem=ici_recv.at[jj],
                device_id=_owner_id(me, 3 - jj, c), device_id_type=MESH)

        def out_to_sibling():
            return pltpu.make_async_remote_copy(
                src_ref=other_half(acc_o), dst_ref=rbuf_o, send_sem=d2d_o_send, recv_sem=d2d_o_recv,
                device_id=sibling, device_id_type=MESH)

        def out_to_owner(r):
            return pltpu.make_async_remote_copy(
                src_ref=sbuf_o.at[r], dst_ref=arr_o.at[r], send_sem=ici_o_send.at[r], recv_sem=ici_o_recv.at[r],
                device_id=_owner_id(me, r + 1, c), device_id_type=MESH)

        def pair_sum(jj):
            to_sibling(jj).wait_recv()
            return own_half(acc.at[jj % 2]) + rbuf[jj]

        def send_block(jj):
            sbuf[jj] = pair_sum(jj).astype(BF16)
            to_owner(jj).start()

        @pl.when(s == 0)
        def _():
            gather_start()

        @pl.when((blk == 3) & (k == 0))
        def _():
            gather_relay()

        @pl.when((blk == 0) & (k == 0))
        def _():
            acc_o[...] = jnp.zeros((D, D), F32)

        @pl.when(blk == 0)
        def _():
            acc_o[...] += _tn(ao_ref[...], bo_ref[...])

        @pl.when((blk == 0) & last)
        def _():
            out_to_sibling().start()

        @pl.when((blk >= 3) & (k == 0))
        def _():
            to_sibling(j - 2).wait_send()

        @pl.when((blk >= 1) & (k == 0))
        def _():
            acc[j % 2] = jnp.zeros((D, QW), F32)

        @pl.when(blk >= 1)
        def _():
            acc[j % 2] += _tn(a_ref[...], b_ref[...])

        @pl.when((blk >= 1) & last)
        def _():
            to_sibling(j).start()

        @pl.when((blk == 1) & last)
        def _():
            out_to_sibling().wait_recv()
            total_o[...] = own_half(acc_o) + rbuf_o[...]
            for r in range(3):
                q = jnp.bitwise_xor(me, r + 1)
                sbuf_o[r] = total_o[pl.ds(pl.multiple_of(q * qr, qr), qr), :].astype(BF16)
                out_to_owner(r).start()

        @pl.when((blk == 2) & last)
        def _():
            send_block(0)

        @pl.when((blk == 3) & last)
        def _():
            send_block(1)
            send_block(2)

        @pl.when((blk == 4) & last)
        def _():
            g_in = pair_sum(3)
            g_out = total_o[pl.ds(pl.multiple_of(me * qr, qr), qr), :]
            to_sibling(2).wait_send()
            to_sibling(3).wait_send()
            out_to_sibling().wait_send()
            for r in range(3):
                to_owner(r).wait()
                out_to_owner(r).wait()
            for r in range(3):
                g_in = g_in + arr[r].astype(F32)
                g_out = g_out + arr_o[r].astype(F32)
            mine[...] = g_in
            mine_o[...] = g_out
            copies = []
            for idx, (src, dst) in enumerate([(mine, gin_ref), (mine_o, gout_ref)]):
                copies.append(pltpu.make_async_remote_copy(
                    src_ref=src, dst_ref=dst.at[c], send_sem=share_send.at[idx], recv_sem=share_recv.at[idx],
                    device_id=sibling, device_id_type=MESH))
                copies.append(pltpu.make_async_copy(src, dst.at[c], local_sems.at[idx]))
            for cp in copies:
                cp.start()
            for cp in copies:
                cp.wait()
            gather_finish()

    any_spec = pl.BlockSpec(memory_space=pl.ANY)
    grid_spec = pltpu.PrefetchScalarGridSpec(
        num_scalar_prefetch=1, grid=(nko + 4 * nki,),
        in_specs=[pl.BlockSpec((TK, D), out_index), pl.BlockSpec((TK, D), out_index),
                  pl.BlockSpec((TK_IN, D), a_in_index), pl.BlockSpec((TK_IN, QW), b_in_index), any_spec],
        out_specs=[any_spec, any_spec, any_spec],
        scratch_shapes=[pltpu.VMEM((2, D, QW), F32), pltpu.VMEM((4, hm, QW), F32), pltpu.VMEM((3, hm, QW), BF16),
                        pltpu.VMEM((3, hm, QW), BF16), pltpu.VMEM((hm, QW), F32),
                        pltpu.VMEM((D, D), F32), pltpu.VMEM((hm, D), F32), pltpu.VMEM((hm, D), F32), pltpu.VMEM((3, qr, D), BF16),
                        pltpu.VMEM((3, qr, D), BF16), pltpu.VMEM((qr, D), F32),
                        pltpu.VMEM(small.shape, F32),
                        pltpu.SemaphoreType.DMA((4,)), pltpu.SemaphoreType.DMA((4,)),
                        pltpu.SemaphoreType.DMA((3,)), pltpu.SemaphoreType.DMA((3,)),
                        pltpu.SemaphoreType.DMA, pltpu.SemaphoreType.DMA,
                        pltpu.SemaphoreType.DMA((3,)), pltpu.SemaphoreType.DMA((3,)),
                        pltpu.SemaphoreType.DMA((2,)), pltpu.SemaphoreType.DMA((2,)), pltpu.SemaphoreType.DMA((2,)),
                        pltpu.SemaphoreType.DMA((7,)), pltpu.SemaphoreType.DMA((7,)), pltpu.SemaphoreType.DMA])
    return pl.pallas_call(
        body, name=name, grid_spec=grid_spec,
        out_shape=[jax.ShapeDtypeStruct((2, qr, D), F32), jax.ShapeDtypeStruct((2, hm, QW), F32),
                   jax.ShapeDtypeStruct((8, *small.shape), F32)],
        compiler_params=pltpu.CompilerParams(dimension_semantics=("arbitrary",), vmem_limit_bytes=VMEM_LIMIT),
    )(pos, a_out, b_out, a_in, b_in, small)


def _chips(x, y):
    return [(1 - x, y), (x, 1 - y), (1 - x, 1 - y)]


def _gather_weights(parts, split, x, pre):
    n = len(parts)
    S = x.shape[0]
    tm = TM_FWD
    nt = S // tm

    def body(x_ref, pre_ref, *refs):
        ins, hb_ref, outs = refs[:n], refs[n], refs[n + 1:2 * n + 1]
        send_sems, recv_sems, local_sems = refs[2 * n + 1:]
        i = pl.program_id(0)
        x, y, c = _position()
        me = 2 * x + y
        nbr_x, nbr_y, diag = _chips(x, y)
        id_x, id_y, id_d = (2 * chip[0] + chip[1] for chip in (nbr_x, nbr_y, diag))
        sibling = (x, y, 1 - c)

        def rows(a, ref, who, piece=None):
            r = parts[a].shape[0] // 2
            if piece is None:
                return ref.at[pl.ds(pl.multiple_of(who * r, 16), r), :]
            return ref.at[pl.ds(pl.multiple_of(who * r + piece * (r // 2), 16), r // 2), :]

        def copy(a, k, src, dst, to):
            return pltpu.make_async_remote_copy(src_ref=src, dst_ref=dst, send_sem=send_sems.at[a, k], recv_sem=recv_sems.at[a, k],
                                                device_id=to, device_id_type=MESH)

        def direct(a, k, chip):
            if split[a]:
                return copy(a, k, rows(a, ins[a], c), rows(a, outs[a].at[me], c), (*chip, c))
            return copy(a, k, ins[a], outs[a].at[me], (*chip, c))

        def arrival(a, k, src_id):
            if split[a]:
                return copy(a, k, rows(a, ins[a], c), rows(a, outs[a].at[src_id], c), (*nbr_x, c))
            return copy(a, k, ins[a], outs[a].at[src_id], (*nbr_x, c))

        def pass_on(a, k, src_id, piece, chip):
            region = rows(a, outs[a].at[src_id], c, piece)
            return copy(a, k, region, region, (*chip, c))

        def hand_over(a, k, src_id, who):
            region = rows(a, outs[a].at[src_id], who)
            return copy(a, k, region, region, sibling)

        def local(a):
            return pltpu.make_async_copy(ins[a], outs[a].at[me], local_sems.at[a])

        def first_sends(a):
            return [direct(a, 0, nbr_x), direct(a, 1, nbr_y)] + ([] if split[a] else [direct(a, 2, diag)])

        def after_x(a):
            return [pass_on(a, 3, id_x, 1, nbr_y), hand_over(a, 4, id_x, c)] if split[a] else []

        def after_y(a):
            return [pass_on(a, 2, id_y, 0, nbr_x), hand_over(a, 5, id_y, c)] if split[a] else []

        def after_diag(a):
            return [hand_over(a, 6, id_d, c)] if split[a] else []

        @pl.when(i == 0)
        def _():
            for a in range(n):
                local(a).start()
                for cp in first_sends(a):
                    cp.start()

        xh, _ = _rms_fwd(x_ref[...], None)
        hb_ref[...] = (xh * pre_ref[...]).astype(BF16)

        @pl.when(i == nt - 2)
        def _():
            for a in range(n):
                arrival(a, 0, id_x).wait_recv()
                for cp in after_x(a):
                    cp.start()
            for a in range(n):
                arrival(a, 1, id_y).wait_recv()
                for cp in after_y(a):
                    cp.start()

        @pl.when(i == nt - 1)
        def _():
            for a in range(n):
                if split[a]:
                    pass_on(a, 2, id_d, 0, nbr_x).wait_recv()
                    pass_on(a, 3, id_d, 1, nbr_y).wait_recv()
                    for cp in after_diag(a):
                        cp.start()
                else:
                    arrival(a, 2, id_d).wait_recv()
            for a in range(n):
                if split[a]:
                    for k, src_id in ((4, id_x), (5, id_y), (6, id_d)):
                        hand_over(a, k, src_id, 1 - c).wait_recv()
            for a in range(n):
                for cp in first_sends(a) + after_x(a) + after_y(a) + after_diag(a):
                    cp.wait_send()
                local(a).wait()

    any_spec = pl.BlockSpec(memory_space=pl.ANY)
    res = pl.pallas_call(
        body, name="gather_weights", grid=(nt,),
        in_specs=[_rows(tm, D), _full((1, D))] + [pl.BlockSpec(memory_space=pltpu.VMEM)] * n,
        out_specs=[_rows(tm, D)] + [any_spec] * n,
        out_shape=[jax.ShapeDtypeStruct((S, D), BF16)] + [jax.ShapeDtypeStruct((4, *p.shape), p.dtype) for p in parts],
        scratch_shapes=[pltpu.SemaphoreType.DMA((n, 7)), pltpu.SemaphoreType.DMA((n, 7)), pltpu.SemaphoreType.DMA((n,))],
        compiler_params=_params(),
    )(x, pre, *parts)
    return res


def _adamw(w, g, m, v):
    m = ADAM_B1 * m + (1.0 - ADAM_B1) * g
    v = ADAM_B2 * v + (1.0 - ADAM_B2) * (g * g)
    m_hat = m / (1.0 - ADAM_B1 ** ADAM_STEP)
    v_hat = v / (1.0 - ADAM_B2 ** ADAM_STEP)
    delta = -ADAM_LR * (m_hat / (jnp.sqrt(v_hat) + ADAM_EPS) + ADAM_WD * w)
    return delta, m, v


def _adamw_big(ws, gs, ms, vs):
    steps = 4
    n = len(ws)

    def body(*refs):
        ins, outs = refs[:4 * n], refs[4 * n:]
        for a in range(n):
            w_ref, g_ref, m_ref, v_ref = ins[4 * a:4 * a + 4]
            go_ref, d_ref, mo_ref, vo_ref = outs[4 * a:4 * a + 4]
            gv = g_ref[...]
            go_ref[...] = gv
            d_ref[...], mo_ref[...], vo_ref[...] = _adamw(w_ref[...], gv, m_ref[...], v_ref[...])

    specs, shapes, operands = [], [], []
    for w, g, m, v in zip(ws, gs, ms, vs):
        rows, cols = w.shape
        specs += [pl.BlockSpec((rows // steps, cols), lambda i: (i, 0))] * 4
        shapes += [jax.ShapeDtypeStruct((rows, cols), F32)] * 4
        operands += [w, g, m, v]
    res = pl.pallas_call(
        body, name="adamw_big", grid=(steps,),
        in_specs=specs, out_specs=specs, out_shape=shapes,
        compiler_params=pltpu.CompilerParams(dimension_semantics=("arbitrary",), vmem_limit_bytes=VMEM_LIMIT),
    )(*operands)
    return [res[4 * a:4 * a + 4] for a in range(n)]


def _adamw_small(g0, g1, weights, moms, vels):
    names = ["pre", "post", "conv", "pw", "ps", "lng", "lnb", "ws", "bs"]
    shapes = [w.shape for w in weights]

    def body(*refs):
        me = 2 * lax.axis_index("x") + lax.axis_index("y")
        g0_ref, g1_ref = refs[0], refs[1]
        w_refs, m_refs, v_refs = refs[2:11], refs[11:20], refs[20:29]
        outs = refs[29:29 + 36]
        loss_ref = refs[65]
        t0_ref, t1_ref = refs[66], refs[67]
        t0 = g0_ref[0]
        t1 = g1_ref[0]
        for d in range(1, 8):
            t0 = t0 + g0_ref[d]
            t1 = t1 + g1_ref[d]
        t0_ref[...] = t0
        t1_ref[...] = t1
        loss_ref[...] = t1_ref[S1_LOSS:S1_LOSS + 1, 0:1]
        my_conv = pl.multiple_of(S0_CONV + 8 * me, 8)
        my_ln = pl.multiple_of(S1_LN + 8 * me, 8)

        def update(idx, piece, grad):
            go, do, mo, vo = outs[4 * idx:4 * idx + 4]
            go[piece] = grad
            do[piece], mo[piece], vo[piece] = _adamw(w_refs[idx][piece], grad, m_refs[idx][piece], v_refs[idx][piece])

        for layer in range(2):
            for k in range(D // GD):
                lanes = slice(k * GD, (k + 1) * GD)
                tref, pre0, post0 = (t0_ref, S0_PRE, S0_POST) if layer == 0 else (t1_ref, S1_PRE, S1_POST)
                update(0, (slice(layer, layer + 1), lanes), tref[pre0 + k:pre0 + k + 1, :])
                update(1, (slice(layer, layer + 1), lanes), tref[post0 + k:post0 + k + 1, :])
        conv_rows = t0_ref[pl.ds(my_conv, 8), :]
        update(2, (slice(0, 3), 0, slice(None)), conv_rows[0:3, :])
        for g in range(4):
            update(3, (g,), t0_ref[S0_PW + g * GD:S0_PW + (g + 1) * GD, :])
            update(4, (slice(0, 1), slice(g * GD, (g + 1) * GD)), t0_ref[S0_PS + g:S0_PS + g + 1, :])
        ln_rows = t1_ref[pl.ds(my_ln, 8), :]
        for k in range(2):
            update(5, (slice(0, 1), slice(k * GD, (k + 1) * GD)), ln_rows[k:k + 1, :])
            update(6, (slice(0, 1), slice(k * GD, (k + 1) * GD)), ln_rows[2 + k:3 + k, :])
        for h in range(HEADS):
            update(7, (h,), t1_ref[S1_WS + h * CHUNK:S1_WS + (h + 1) * CHUNK, :])
        update(8, (slice(None), slice(None)), t1_ref[S1_BS:S1_BS + HEADS, :])

    vm = pl.BlockSpec(memory_space=pltpu.VMEM)
    out_shape = []
    for s in shapes:
        out_shape += [jax.ShapeDtypeStruct(s, F32)] * 4
    out_shape.append(jax.ShapeDtypeStruct((1, 1), F32))
    res = pl.pallas_call(
        body, name="adamw_small",
        in_specs=[vm] * 29, out_specs=[vm] * 37, out_shape=out_shape,
        scratch_shapes=[pltpu.VMEM((S0_ROWS, GD), F32), pltpu.VMEM((S1_ROWS, GD), F32)],
        compiler_params=pltpu.CompilerParams(vmem_limit_bytes=VMEM_LIMIT),
    )(g0, g1, *weights, *moms, *vels)
    per_weight = {nm: res[4 * i:4 * i + 4] for i, nm in enumerate(names)}
    return per_weight, res[36]


def _pad8(a):
    return jnp.pad(a, ((0, 8 - a.shape[0]), (0, 0)))


def kernel(x, pre_norm, post_norm, even_w_in, even_conv_w, even_pool_w, even_pool_scale, even_w_out, odd_w_in, odd_ln_g, odd_ln_b, odd_w_s, odd_b_s, odd_w_out, loss_target, m_pre_norm, m_post_norm, m_even_w_in, m_even_conv_w, m_even_pool_w, m_even_pool_scale, m_even_w_out, m_odd_w_in, m_odd_ln_g, m_odd_ln_b, m_odd_w_s, m_odd_b_s, m_odd_w_out, v_pre_norm, v_post_norm, v_even_w_in, v_even_conv_w, v_even_pool_w, v_even_pool_scale, v_even_w_out, v_odd_w_in, v_odd_ln_g, v_odd_ln_b, v_odd_w_s, v_odd_b_s, v_odd_w_out):
    xs = x[0]
    tgt = loss_target[0]

    small_shard = jnp.concatenate([_pad8(even_conv_w[0]), _pad8(odd_ln_g.reshape(2, GD)), _pad8(odd_ln_b.reshape(2, GD))], axis=0)
    pre0, pre1 = pre_norm[0:1], pre_norm[1:2]
    post0, post1 = post_norm[0:1], post_norm[1:2]
    hb0, win0, wout0, shard = _gather_weights([even_w_in[0].astype(BF16), even_w_out[0].astype(BF16), small_shard],
                                              [True, True, False], xs, pre0)
    wout0 = wout0.reshape(D, D)
    px, py, pc = _position()
    pos = jnp.stack([pc, 2 * px + py]).astype(jnp.int32)
    conv_w = shard[:, 0:3, :].transpose(1, 0, 2).reshape(3, AW)
    ln_g = shard[:, 8:10, :].reshape(1, D)
    ln_b = shard[:, 16:18, :].reshape(1, D)
    pool_wb = even_pool_w[0].astype(BF16)
    ws_tril = jnp.tril(odd_w_s[0]).astype(BF16)
    ws_tril_t = jnp.swapaxes(ws_tril, 1, 2)
    bias = jnp.broadcast_to(odd_b_s[0][:, :, None], (HEADS, CHUNK, GD))

    x1, proj0, m0, mixp0, conv0, mixed0, pooled0, win1, wout1 = _even_fwd(
        xs, hb0, post0, win0, conv_w, pool_wb, even_pool_scale, wout0, [odd_w_in[0].astype(BF16), odd_w_out[0].astype(BF16)])
    wout1 = wout1.reshape(D, D)
    proj1, m1, hb1, yp1, dx2, loss_part = _odd_fwd(x1, tgt, pre1, post1, win1, ln_g, ln_b, ws_tril, bias, wout1)
    dx1, dproj1, dmb1, small1 = _odd_bwd(dx2, x1, proj1, m1, loss_part, pre1, post1, win1, ln_g, ln_b, ws_tril, ws_tril_t, bias, wout1)
    g_out1, g_in1, all1 = _wgrad_layer(yp1, dmb1, hb1, dproj1, small1, pos, "wgrad_odd")
    gx, dproj0, dmb0, small0 = _even_bwd(dx1, xs, proj0, conv0, mixed0, pooled0, m0, pre0, post0, win0, conv_w, pool_wb,
                                         even_pool_scale, wout0)
    g_out0, g_in0, all0 = _wgrad_layer(mixp0, dmb0, hb0, dproj0, small0, pos, "wgrad_even")

    big_w = [even_w_in[0], even_w_out[0], odd_w_in[0], odd_w_out[0]]
    big_g = [g.reshape(w.shape) for g, w in zip([g_in0, g_out0, g_in1, g_out1], big_w)]
    big_m = [m_even_w_in[0], m_even_w_out[0], m_odd_w_in[0], m_odd_w_out[0]]
    big_v = [v_even_w_in[0], v_even_w_out[0], v_odd_w_in[0], v_odd_w_out[0]]
    big = _adamw_big(big_w, big_g, big_m, big_v)

    def taps_first(a):
        return jnp.swapaxes(a, 0, 1)

    small_w = [pre_norm, post_norm, taps_first(even_conv_w), even_pool_w[0], even_pool_scale, odd_ln_g, odd_ln_b, odd_w_s[0], odd_b_s[0]]
    small_m = [m_pre_norm, m_post_norm, taps_first(m_even_conv_w), m_even_pool_w[0], m_even_pool_scale, m_odd_ln_g, m_odd_ln_b,
               m_odd_w_s[0], m_odd_b_s[0]]
    small_v = [v_pre_norm, v_post_norm, taps_first(v_even_conv_w), v_even_pool_w[0], v_even_pool_scale, v_odd_ln_g, v_odd_ln_b,
               v_odd_w_s[0], v_odd_b_s[0]]
    sm, loss = _adamw_small(all0, all1, small_w, small_m, small_v)

    def lead(a):
        return a[None]

    per = {
        "pre_norm": sm["pre"], "post_norm": sm["post"],
        "even_w_in": [lead(a) for a in big[0]], "even_conv_w": [taps_first(a) for a in sm["conv"]],
        "even_pool_w": [lead(a) for a in sm["pw"]], "even_pool_scale": sm["ps"],
        "even_w_out": [lead(a) for a in big[1]], "odd_w_in": [lead(a) for a in big[2]],
        "odd_ln_g": sm["lng"], "odd_ln_b": sm["lnb"],
        "odd_w_s": [lead(a) for a in sm["ws"]], "odd_b_s": [lead(a) for a in sm["bs"]],
        "odd_w_out": [lead(a) for a in big[3]],
    }
    order = ["pre_norm", "post_norm", "even_w_in", "even_conv_w", "even_pool_w", "even_pool_scale", "even_w_out", "odd_w_in",
             "odd_ln_g", "odd_ln_b", "odd_w_s", "odd_b_s", "odd_w_out"]
    outs = [loss.reshape(()), gx[None]]
    for kind in range(4):
        outs += [per[nm][kind] for nm in order]
    return tuple(outs)
```

```python
import functools

import jax
import jax.numpy as jnp
from jax import lax
from jax.experimental import pallas as pl
from jax.experimental.pallas import tpu as pltpu

F32 = jnp.float32
BF16 = jnp.bfloat16
MESH = pl.DeviceIdType.MESH

D = 1024
W3 = 3 * D
QW = W3 // 4
AW = 512
GD = 128
CHUNK = 128
HEADS = 8
HALO = 16
POOL_WINDOWS = (2, 4, 8, 16)
EPS = 1e-6
TM_FWD = 512
TM_BWD = 256
TK = 1024
TK_IN = 2048
VMEM_LIMIT = 56 * 1024 * 1024

ADAM_LR, ADAM_B1, ADAM_B2, ADAM_EPS, ADAM_WD, ADAM_STEP = 0.001, 0.9, 0.999, 1e-08, 0.01, 10

S0_PRE, S0_POST, S0_CONV, S0_PS, S0_PW, S0_ROWS = 0, 8, 16, 48, 56, 568
S1_PRE, S1_POST, S1_LN, S1_BS, S1_WS, S1_LOSS, S1_ROWS = 0, 8, 16, 48, 56, 1080, 1088


def _nn(a, b):
    return jnp.dot(a, b, preferred_element_type=F32)


def _nt(a, b):
    return lax.dot_general(a, b, (((1,), (1,)), ((), ())), preferred_element_type=F32)


def _tn(a, b):
    return lax.dot_general(a, b, (((0,), (0,)), ((), ())), preferred_element_type=F32)


def _sigmoid(z):
    return 1.0 / (1.0 + jnp.exp(-z))


def _rms_fwd(x, g):
    r = lax.rsqrt(jnp.mean(x * x, axis=-1, keepdims=True) + EPS)
    return x * r, r


def _rms_bwd(dy, xh, r, g):
    dn = dy * g
    dx = r * (dn - xh * jnp.mean(xh * dn, axis=-1, keepdims=True))
    return dx, jnp.sum(dy * xh, axis=0, keepdims=True)


def _full(shape):
    nd = len(shape)
    return pl.BlockSpec(shape, lambda i, _n=nd: (0,) * _n, pipeline_mode=pl.Buffered(1))


def _full_out(shape):
    nd = len(shape)
    return pl.BlockSpec(shape, lambda i, _n=nd: (0,) * _n)


def _rows(tm, width, index=None):
    return pl.BlockSpec((tm, width), (lambda i: (i, 0)) if index is None else index)


def _params():
    return pltpu.CompilerParams(dimension_semantics=("arbitrary",), vmem_limit_bytes=VMEM_LIMIT)


def _position():
    x, y, c = lax.axis_index("x"), lax.axis_index("y"), lax.axis_index("c")
    return x, y, c


def _even_mix(proj_ref, hc_ext, xp_ext, cw_ref, pw_ref, ps_ref, first_row):
    tm = proj_ref.shape[0]
    xa = proj_ref[:, 0:AW]
    gb = proj_ref[:, AW:2 * AW]
    gc = proj_ref[:, 2 * AW:3 * AW]
    za = proj_ref[:, 3 * AW:4 * AW]
    xp = proj_ref[:, 4 * AW:5 * AW]
    zp = proj_ref[:, 5 * AW:6 * AW]
    hc = gc * xa
    hc_ext[HALO:, :] = hc
    e = hc_ext[...]
    conv = cw_ref[2:3, :] * hc + cw_ref[1:2, :] * pltpu.roll(e, 1, 0)[HALO:] + cw_ref[0:1, :] * pltpu.roll(e, 2, 0)[HALO:]
    sa = _sigmoid(za)
    xp_ext[HALO:, :] = xp
    pos = first_row + lax.broadcasted_iota(jnp.int32, (tm, 1), 0)
    pooled, mixed, counts = [], [], []
    for g, w in enumerate(POOL_WINDOWS):
        cols = slice(g * GD, (g + 1) * GD)
        s = xp_ext[:, cols]
        for k in range(g + 1):
            s = s + pltpu.roll(s, 2 ** k, 0)
        count = jnp.minimum(pos + 1, w).astype(F32)
        pg = s[HALO:] / count - xp[:, cols]
        pooled.append(pg.astype(BF16))
        mixed.append(_nn(pooled[-1], pw_ref[g]))
        counts.append(count)
    mixed = jnp.concatenate(mixed, axis=-1)
    sb = _sigmoid(zp)
    return dict(xa=xa, gb=gb, gc=gc, za=za, zp=zp, hc=hc, conv=conv, sa=sa, sb=sb, pooled=pooled, mixed=mixed, counts=counts)


def _half_rows(ref, rows, who):
    return ref.at[pl.ds(pl.multiple_of(who * (rows // 2), 8), rows // 2), :]


def _store_permuted(ref, value):
    for ob in range(D // GD):
        nb = 4 * (ob % 2) + ob // 2
        ref[:, nb * GD:(nb + 1) * GD] = value[:, ob * GD:(ob + 1) * GD]


def _even_fwd(x, hb, post, win, cw, pwb, ps, wout, next_shards):
    S = x.shape[0]
    tm = TM_FWD
    nt = S // tm
    relay = (3 * nt) // 4
    n = len(next_shards)
    shard_rows = [p.shape[0] for p in next_shards]

    def body(x_ref, hb_ref, post_ref, win_ref, cw_ref, pw_ref, ps_ref, wout_ref, *rest):
        shard_refs, rest = rest[:n], rest[n:]
        x1_ref, proj_ref, m_ref, mixp_ref, conv_ref, mixed_ref, pooled_ref = rest[:7]
        full_refs, rest = rest[7:7 + n], rest[7 + n:]
        hc_ext, xp_ext, mix_sc = rest[:3]
        stage, rest = rest[3:3 + n], rest[3 + n:]
        send_sems, recv_sems, local_sems = rest
        i = pl.program_id(0)
        px, py, pc = _position()
        me = 2 * px + py
        chips = _chips(px, py)

        def ici(a, j):
            return pltpu.make_async_remote_copy(
                src_ref=_half_rows(shard_refs[a], shard_rows[a], pc), dst_ref=_half_rows(full_refs[a].at[me], shard_rows[a], pc),
                send_sem=send_sems.at[a, j], recv_sem=recv_sems.at[a, j], device_id=(*chips[j], pc), device_id_type=MESH)

        def ici_arrival(a, j):
            src = 2 * chips[j][0] + chips[j][1]
            return pltpu.make_async_remote_copy(
                src_ref=_half_rows(shard_refs[a], shard_rows[a], pc), dst_ref=_half_rows(full_refs[a].at[src], shard_rows[a], pc),
                send_sem=send_sems.at[a, j], recv_sem=recv_sems.at[a, j], device_id=(*chips[j], pc), device_id_type=MESH)

        def relay_copy(a, j, who):
            src = 2 * chips[j][0] + chips[j][1]
            region = _half_rows(full_refs[a].at[src], shard_rows[a], who)
            return pltpu.make_async_remote_copy(
                src_ref=region, dst_ref=region, send_sem=send_sems.at[a, 3 + j], recv_sem=recv_sems.at[a, 3 + j],
                device_id=(px, py, 1 - pc), device_id_type=MESH)

        def own_copy(a):
            return pltpu.make_async_copy(stage[a], full_refs[a].at[me], local_sems.at[a])

        @pl.when(i == 0)
        def _():
            hc_ext[0:HALO, :] = jnp.zeros((HALO, AW), F32)
            xp_ext[0:HALO, :] = jnp.zeros((HALO, AW), F32)
            for a in range(n):
                for j in range(3):
                    ici(a, j).start()
            for a in range(n):
                load = pltpu.make_async_copy(shard_refs[a], stage[a], local_sems.at[a])
                load.start()
                load.wait()
                own_copy(a).start()

        @pl.when(i == relay)
        def _():
            for j in range(3):
                for a in range(n):
                    ici_arrival(a, j).wait_recv()
                    relay_copy(a, j, pc).start()

        hb = hb_ref[...]
        for q in range(4):
            proj_ref[:, q * QW:(q + 1) * QW] = _nn(hb, win_ref[q])
        t = _even_mix(proj_ref, hc_ext, xp_ext, cw_ref, pw_ref, ps_ref, i * tm)
        conv_ref[...] = t["conv"]
        mixed_ref[...] = t["mixed"]
        for g in range(4):
            pooled_ref[:, g * GD:(g + 1) * GD] = t["pooled"][g]
        mix_sc[:, 0:AW] = (t["gb"] * t["conv"] * (t["za"] * t["sa"])).astype(BF16)
        mix_sc[:, AW:2 * AW] = (t["mixed"] * ps_ref[...] * (t["zp"] * t["sb"])).astype(BF16)
        mix = mix_sc[...]
        _store_permuted(mixp_ref, mix)
        m = _nn(mix, wout_ref[...])
        m_ref[...] = m
        mh, _ = _rms_fwd(m, None)
        x1_ref[...] = x_ref[...] + mh * post_ref[...]
        hc_ext[0:HALO, :] = hc_ext[tm:tm + HALO, :]
        xp_ext[0:HALO, :] = xp_ext[tm:tm + HALO, :]

        @pl.when(i == nt - 1)
        def _():
            for j in range(3):
                for a in range(n):
                    relay_copy(a, j, 1 - pc).wait_recv()
            for a in range(n):
                for j in range(3):
                    ici(a, j).wait_send()
                    relay_copy(a, j, pc).wait_send()
                own_copy(a).wait()

    any_spec = pl.BlockSpec(memory_space=pl.ANY)
    return pl.pallas_call(
        body, name="even_fwd", grid=(nt,),
        in_specs=[_rows(tm, D), _rows(tm, D), _full((1, D)), _full((4, D, QW)), _full((3, AW)), _full((4, GD, GD)),
                  _full((1, AW)), _full((D, D))] + [any_spec] * n,
        out_specs=[_rows(tm, D), _rows(tm, W3), _rows(tm, D), _rows(tm, D), _rows(tm, AW), _rows(tm, AW), _rows(tm, AW)]
        + [any_spec] * n,
        out_shape=[jax.ShapeDtypeStruct((S, D), F32), jax.ShapeDtypeStruct((S, W3), F32), jax.ShapeDtypeStruct((S, D), F32),
                   jax.ShapeDtypeStruct((S, D), BF16),
                   jax.ShapeDtypeStruct((S, AW), F32), jax.ShapeDtypeStruct((S, AW), F32), jax.ShapeDtypeStruct((S, AW), BF16)]
        + [jax.ShapeDtypeStruct((4, *p.shape), p.dtype) for p in next_shards],
        scratch_shapes=[pltpu.VMEM((tm + HALO, AW), F32), pltpu.VMEM((tm + HALO, AW), F32), pltpu.VMEM((tm, D), BF16)]
        + [pltpu.VMEM(p.shape, p.dtype) for p in next_shards]
        + [pltpu.SemaphoreType.DMA((n, 6)), pltpu.SemaphoreType.DMA((n, 6)), pltpu.SemaphoreType.DMA((n,))],
        compiler_params=_params(),
    )(x, hb, post, win, cw, pwb, ps, wout, *next_shards)


def _chunks_side_by_side(a, h):
    return jnp.concatenate([a[n * CHUNK:(n + 1) * CHUNK, h * GD:(h + 1) * GD] for n in range(a.shape[0] // CHUNK)], axis=1)


def _odd_mix(proj_ref, lng_ref, lnb_ref, ws_ref, bias_ref, sv_ref):
    tm = proj_ref.shape[0]
    u = proj_ref[:, 0:D]
    v = proj_ref[:, D:2 * D]
    z = proj_ref[:, 2 * D:3 * D]
    mu = jnp.mean(v, axis=-1, keepdims=True)
    vc = v - mu
    rs = lax.rsqrt(jnp.mean(vc * vc, axis=-1, keepdims=True) + EPS)
    vh = vc * rs
    vnb = (vh * lng_ref[...] + lnb_ref[...]).astype(BF16)
    for h in range(HEADS):
        sv = _nn(ws_ref[h], _chunks_side_by_side(vnb, h))
        for n in range(tm // CHUNK):
            sv_ref[n * CHUNK:(n + 1) * CHUNK, h * GD:(h + 1) * GD] = sv[:, n * GD:(n + 1) * GD] + bias_ref[h]
    return dict(u=u, z=z, vh=vh, rs=rs, vnb=vnb, sz=_sigmoid(z))


def _odd_fwd(x1, tgt, pre, post, win, lng, lnb, wsb, bias, wout):
    S = x1.shape[0]
    tm = TM_FWD
    nt = S // tm

    def body(x_ref, tgt_ref, pre_ref, post_ref, win_ref, lng_ref, lnb_ref, ws_ref, bias_ref, wout_ref,
             proj_ref, m_ref, hb_ref, yb_ref, dx2_ref, loss_ref, sv_ref):
        i = pl.program_id(0)

        @pl.when(i == 0)
        def _():
            loss_ref[...] = jnp.zeros((8, GD), F32)

        xv = x_ref[...]
        xh, _ = _rms_fwd(xv, None)
        hb = (xh * pre_ref[...]).astype(BF16)
        hb_ref[...] = hb
        for q in range(4):
            proj_ref[:, q * QW:(q + 1) * QW] = _nn(hb, win_ref[q])
        t = _odd_mix(proj_ref, lng_ref, lnb_ref, ws_ref, bias_ref, sv_ref)
        yb = (t["u"] * sv_ref[...] * (t["z"] * t["sz"])).astype(BF16)
        _store_permuted(yb_ref, yb)
        m = _nn(yb, wout_ref[...])
        m_ref[...] = m
        mh, _ = _rms_fwd(m, None)
        err = xv + mh * post_ref[...] - tgt_ref[...]
        dx2_ref[...] = err * (1.0 / D)
        part = 0.5 * jnp.sum(jnp.mean(err * err, axis=-1, keepdims=True), axis=0, keepdims=True)
        loss_ref[...] += jnp.broadcast_to(part, (8, GD))

    return pl.pallas_call(
        body, name="odd_fwd", grid=(nt,),
        in_specs=[_rows(tm, D), _rows(tm, D), _full((1, D)), _full((1, D)), _full((4, D, QW)), _full((1, D)), _full((1, D)),
                  _full((HEADS, CHUNK, CHUNK)), _full((HEADS, CHUNK, GD)), _full((D, D))],
        out_specs=[_rows(tm, W3), _rows(tm, D), _rows(tm, D), _rows(tm, D), _rows(tm, D), _full_out((8, GD))],
        out_shape=[jax.ShapeDtypeStruct((S, W3), F32), jax.ShapeDtypeStruct((S, D), F32), jax.ShapeDtypeStruct((S, D), BF16),
                   jax.ShapeDtypeStruct((S, D), BF16), jax.ShapeDtypeStruct((S, D), F32), jax.ShapeDtypeStruct((8, GD), F32)],
        scratch_shapes=[pltpu.VMEM((tm, D), F32)],
        compiler_params=_params(),
    )(x1, tgt, pre, post, win, lng, lnb, wsb, bias, wout)


def _store_rows(ref, row0, value):
    r, width = value.shape
    for a in range(r):
        for k in range(width // GD):
            ref[row0 + a * (width // GD) + k:row0 + a * (width // GD) + k + 1, :] = value[a:a + 1, k * GD:(k + 1) * GD]


def _proj_bwd(dproj, win_ref, x, dy, pre):
    dh = _nt(dproj[:, 0:QW], win_ref[0])
    for q in range(1, 4):
        dh += _nt(dproj[:, q * QW:(q + 1) * QW], win_ref[q])
    xh, r = _rms_fwd(x, None)
    dxn, dpre = _rms_bwd(dh, xh, r, pre)
    return dy + dxn, dpre


def _odd_bwd(dx2, x1, proj, m, loss, pre, post, win, lng, lnb, wsb, wsbt, bias, wout):
    S = x1.shape[0]
    tm = TM_BWD
    nt = S // tm

    def body(dy_ref, x_ref, proj_ref, m_ref, loss_ref, pre_ref, post_ref, win_ref, lng_ref, lnb_ref, ws_ref, wst_ref, bias_ref,
             wout_ref, dx_ref, dproj_ref, dmb_ref, small_ref, sv_ref, dvn_ref, acc1024, dws_acc, dbs_acc):
        i = pl.program_id(0)

        @pl.when(i == 0)
        def _():
            acc1024[...] = jnp.zeros_like(acc1024)
            dws_acc[...] = jnp.zeros_like(dws_acc)
            dbs_acc[...] = jnp.zeros_like(dbs_acc)

        dy = dy_ref[...]
        mh, rm = _rms_fwd(m_ref[...], None)
        dm, dpost = _rms_bwd(dy, mh, rm, post_ref[...])
        dmb = dm.astype(BF16)
        dmb_ref[...] = dmb
        dyv = _nt(dmb, wout_ref[...])
        t = _odd_mix(proj_ref, lng_ref, lnb_ref, ws_ref, bias_ref, sv_ref)
        u, z, sz, sv = t["u"], t["z"], t["sz"], sv_ref[...]
        dproj_ref[:, 0:D] = (dyv * sv * (z * sz)).astype(BF16)
        dproj_ref[:, 2 * D:3 * D] = (dyv * u * sv * (sz * (1.0 + z * (1.0 - sz)))).astype(BF16)
        dsv = dyv * u * (z * sz)
        dsvb = dsv.astype(BF16)
        for h in range(HEADS):
            dsv_h = _chunks_side_by_side(dsvb, h)
            dvn_h = _nn(wst_ref[h], dsv_h)
            dws_acc[h] += _nt(dsv_h, _chunks_side_by_side(t["vnb"], h))
            for n in range(tm // CHUNK):
                rows, cols = slice(n * CHUNK, (n + 1) * CHUNK), slice(h * GD, (h + 1) * GD)
                dvn_ref[rows, cols] = dvn_h[:, n * GD:(n + 1) * GD]
                dbs_acc[h] += dsv[rows, cols]
        dvn = dvn_ref[...]
        vh = t["vh"]
        dvh = dvn * lng_ref[...]
        dv = t["rs"] * (dvh - jnp.mean(dvh, axis=-1, keepdims=True) - vh * jnp.mean(dvh * vh, axis=-1, keepdims=True))
        dproj_ref[:, D:2 * D] = dv.astype(BF16)
        dx_ref[...], dpre = _proj_bwd(dproj_ref[...], win_ref, x_ref[...], dy, pre_ref[...])
        acc1024[0:1, :] += dpre
        acc1024[1:2, :] += dpost
        acc1024[2:3, :] += jnp.sum(dvn * vh, axis=0, keepdims=True)
        acc1024[3:4, :] += jnp.sum(dvn, axis=0, keepdims=True)

        @pl.when(i == nt - 1)
        def _():
            small_ref[...] = jnp.zeros_like(small_ref)
            _store_rows(small_ref, S1_PRE, acc1024[0:1, :])
            _store_rows(small_ref, S1_POST, acc1024[1:2, :])
            for q in range(4):
                _store_rows(small_ref, S1_LN + 8 * q, acc1024[2:3, 2 * q * GD:(2 * q + 2) * GD])
                _store_rows(small_ref, S1_LN + 8 * q + 2, acc1024[3:4, 2 * q * GD:(2 * q + 2) * GD])
            lower = lax.broadcasted_iota(jnp.int32, (CHUNK, CHUNK), 0) >= lax.broadcasted_iota(jnp.int32, (CHUNK, CHUNK), 1)
            for h in range(HEADS):
                small_ref[S1_WS + h * CHUNK:S1_WS + (h + 1) * CHUNK, :] = jnp.where(lower, dws_acc[h], 0.0)
                small_ref[S1_BS + h:S1_BS + h + 1, :] = jnp.sum(dbs_acc[h].T, axis=0, keepdims=True)
            small_ref[S1_LOSS:S1_LOSS + 8, :] = loss_ref[...]

    return pl.pallas_call(
        body, name="odd_bwd", grid=(nt,),
        in_specs=[_rows(tm, D), _rows(tm, D), _rows(tm, W3), _rows(tm, D), _full((8, GD)), _full((1, D)), _full((1, D)),
                  _full((4, D, QW)), _full((1, D)), _full((1, D)), _full((HEADS, CHUNK, CHUNK)), _full((HEADS, CHUNK, CHUNK)),
                  _full((HEADS, CHUNK, GD)), _full((D, D))],
        out_specs=[_rows(tm, D), _rows(tm, W3), _rows(tm, D), _full_out((S1_ROWS, GD))],
        out_shape=[jax.ShapeDtypeStruct((S, D), F32), jax.ShapeDtypeStruct((S, W3), BF16), jax.ShapeDtypeStruct((S, D), BF16),
                   jax.ShapeDtypeStruct((S1_ROWS, GD), F32)],
        scratch_shapes=[pltpu.VMEM((tm, D), F32), pltpu.VMEM((tm, D), F32), pltpu.VMEM((8, D), F32),
                        pltpu.VMEM((HEADS, CHUNK, CHUNK), F32), pltpu.VMEM((HEADS, CHUNK, GD), F32)],
        compiler_params=_params(),
    )(dx2, x1, proj, m, loss, pre, post, win, lng, lnb, wsb, wsbt, bias, wout)


def _even_bwd(dx1, x, proj, conv, mixed, pooled, m, pre, post, win, cw, pwb, ps, wout):
    S = x.shape[0]
    tm = TM_BWD
    nt = S // tm
    L = tm + HALO

    def rev(i):
        return (nt - 1 - i, 0)

    def body(dy_ref, x_ref, proj_ref, conv_ref, mixed_ref, pooled_ref, m_ref, pre_ref, post_ref, win_ref, cw_ref, pw_ref, ps_ref,
             wout_ref, dx_ref, dproj_ref, dmb_ref, small_ref, dconv_ext, q_ext, acc1024, acc512, dpw_acc):
        i = pl.program_id(0)
        tile = nt - 1 - i

        @pl.when(i == 0)
        def _():
            dconv_ext[tm:L, :] = jnp.zeros((HALO, AW), F32)
            q_ext[tm:L, :] = jnp.zeros((HALO, AW), F32)
            acc1024[...] = jnp.zeros_like(acc1024)
            acc512[...] = jnp.zeros_like(acc512)
            dpw_acc[...] = jnp.zeros_like(dpw_acc)

        dy = dy_ref[...]
        mh, rm = _rms_fwd(m_ref[...], None)
        dm, dpost = _rms_bwd(dy, mh, rm, post_ref[...])
        dmb = dm.astype(BF16)
        dmb_ref[...] = dmb
        dmix = _nt(dmb, wout_ref[...])
        dya, dyb = dmix[:, 0:AW], dmix[:, AW:2 * AW]
        xa, gb, gc, za = (proj_ref[:, k * AW:(k + 1) * AW] for k in range(4))
        zp = proj_ref[:, 5 * AW:6 * AW]
        hc = gc * xa
        conv = conv_ref[...]
        sa = _sigmoid(za)
        silu_a = za * sa
        dproj_ref[:, AW:2 * AW] = (dya * conv * silu_a).astype(BF16)
        dproj_ref[:, 3 * AW:4 * AW] = (dya * gb * conv * (sa * (1.0 + za * (1.0 - sa)))).astype(BF16)
        dconv = dya * gb * silu_a
        dconv_ext[0:tm, :] = dconv
        e = dconv_ext[...]
        dc1 = pltpu.roll(e, L - 1, 0)[0:tm]
        dc2 = pltpu.roll(e, L - 2, 0)[0:tm]
        dhc = cw_ref[2:3, :] * dconv + cw_ref[1:2, :] * dc1 + cw_ref[0:1, :] * dc2
        dproj_ref[:, 0:AW] = (dhc * gc).astype(BF16)
        dproj_ref[:, 2 * AW:3 * AW] = (dhc * xa).astype(BF16)
        acc512[0:1, :] += jnp.sum(dc2 * hc, axis=0, keepdims=True)
        acc512[1:2, :] += jnp.sum(dc1 * hc, axis=0, keepdims=True)
        acc512[2:3, :] += jnp.sum(dconv * hc, axis=0, keepdims=True)

        sb, mixed = _sigmoid(zp), mixed_ref[...]
        silu_b = zp * sb
        acc512[3:4, :] += jnp.sum(dyb * mixed * silu_b, axis=0, keepdims=True)
        dmixedb = (dyb * ps_ref[...] * silu_b).astype(BF16)
        dproj_ref[:, 5 * AW:6 * AW] = (dyb * mixed * ps_ref[...] * (sb * (1.0 + zp * (1.0 - sb)))).astype(BF16)
        pos = tile * tm + lax.broadcasted_iota(jnp.int32, (tm, 1), 0)
        for g, w in enumerate(POOL_WINDOWS):
            cols = slice(g * GD, (g + 1) * GD)
            dpw_acc[g] += _tn(pooled_ref[:, cols], dmixedb[:, cols])
            dpooled = _nt(dmixedb[:, cols], pw_ref[g])
            q_ext[0:tm, cols] = dpooled / jnp.minimum(pos + 1, w).astype(F32)
            s = q_ext[:, cols]
            for k in range(g + 1):
                s = s + pltpu.roll(s, L - 2 ** k, 0)
            dproj_ref[:, 4 * AW + g * GD:4 * AW + (g + 1) * GD] = (s[0:tm] - dpooled).astype(BF16)
        dconv_ext[tm:L, :] = dconv_ext[0:HALO, :]
        q_ext[tm:L, :] = q_ext[0:HALO, :]

        dx_ref[...], dpre = _proj_bwd(dproj_ref[...], win_ref, x_ref[...], dy, pre_ref[...])
        acc1024[0:1, :] += dpre
        acc1024[1:2, :] += dpost

        @pl.when(i == nt - 1)
        def _():
            small_ref[...] = jnp.zeros_like(small_ref)
            _store_rows(small_ref, S0_PRE, acc1024[0:1, :])
            _store_rows(small_ref, S0_POST, acc1024[1:2, :])
            for q in range(4):
                for k in range(3):
                    small_ref[S0_CONV + 8 * q + k:S0_CONV + 8 * q + k + 1, :] = acc512[k:k + 1, q * GD:(q + 1) * GD]
            _store_rows(small_ref, S0_PS, acc512[3:4, :])
            for g in range(4):
                small_ref[S0_PW + g * GD:S0_PW + (g + 1) * GD, :] = dpw_acc[g]

    return pl.pallas_call(
        body, name="even_bwd", grid=(nt,),
        in_specs=[_rows(tm, D, rev), _rows(tm, D, rev), _rows(tm, W3, rev), _rows(tm, AW, rev), _rows(tm, AW, rev), _rows(tm, AW, rev),
                  _rows(tm, D, rev),
                  _full((1, D)), _full((1, D)), _full((4, D, QW)), _full((3, AW)), _full((4, GD, GD)), _full((1, AW)), _full((D, D))],
        out_specs=[_rows(tm, D, rev), _rows(tm, W3, rev), _rows(tm, D, rev), _full_out((S0_ROWS, GD))],
        out_shape=[jax.ShapeDtypeStruct((S, D), F32), jax.ShapeDtypeStruct((S, W3), BF16), jax.ShapeDtypeStruct((S, D), BF16),
                   jax.ShapeDtypeStruct((S0_ROWS, GD), F32)],
        scratch_shapes=[pltpu.VMEM((L, AW), F32), pltpu.VMEM((L, AW), F32),
                        pltpu.VMEM((8, D), F32), pltpu.VMEM((8, AW), F32), pltpu.VMEM((4, GD, GD), F32)],
        compiler_params=_params(),
    )(dx1, x, proj, conv, mixed, pooled, m, pre, post, win, cw, pwb, ps, wout)


def _owner_id(me, relation, c):
    q = jnp.bitwise_xor(me, relation)
    return (q // 2, q % 2, c)


def _small_gather_steps(small_ref, all_ref, stage, send_sems, recv_sems, local_sem):
    x, y, c = _position()
    chips = _chips(x, y)

    def slot(chip, core):
        return 4 * chip[0] + 2 * chip[1] + core

    def copy(k, src, block, to):
        return pltpu.make_async_remote_copy(src_ref=src, dst_ref=all_ref.at[block], send_sem=send_sems.at[k],
                                            recv_sem=recv_sems.at[k], device_id=to, device_id_type=MESH)

    def own_copy():
        return pltpu.make_async_copy(stage, all_ref.at[slot((x, y), c)], local_sem)

    def first_sends():
        mine = slot((x, y), c)
        return [copy(0, small_ref, mine, (x, y, 1 - c))] + [copy(1 + j, small_ref, mine, (*chip, c)) for j, chip in enumerate(chips)]

    def relays():
        return [copy(4 + j, all_ref.at[slot(chip, c)], slot(chip, c), (x, y, 1 - c)) for j, chip in enumerate(chips)]

    def start():
        for cp in first_sends():
            cp.start()
        load = pltpu.make_async_copy(small_ref, stage, local_sem)
        load.start()
        load.wait()
        own_copy().start()

    def relay():
        for j, chip in enumerate(chips):
            copy(1 + j, small_ref, slot(chip, c), (*chip, c)).wait_recv()
        for cp in relays():
            cp.start()

    def finish():
        copy(0, small_ref, slot((x, y), 1 - c), (x, y, 1 - c)).wait_recv()
        for j, chip in enumerate(chips):
            copy(4 + j, small_ref, slot(chip, 1 - c), (x, y, 1 - c)).wait_recv()
        for cp in first_sends() + relays():
            cp.wait_send()
        own_copy().wait()

    return start, relay, finish


def _wgrad_layer(a_out, b_out, a_in, b_in, small, pos, name):
    S = a_in.shape[0]
    nko = S // TK
    nki = S // TK_IN
    hm = D // 2
    qr = hm // 4

    def out_index(s, pos_ref):
        return (jnp.minimum(s, nko - 1), 0)

    def a_in_index(s, pos_ref):
        return (jnp.where(s >= nko, (s - nko) % nki, 0), 0)

    def b_in_index(s, pos_ref):
        return (jnp.where(s >= nko, (s - nko) % nki, 0), jnp.bitwise_xor(pos_ref[1], 3 - jnp.maximum((s - nko) // nki, 0)))

    def body(pos_ref, ao_ref, bo_ref, a_ref, b_ref, small_ref, gout_ref, gin_ref, all_ref,
             acc, rbuf, sbuf, arr, mine, acc_o, rbuf_o, total_o, sbuf_o, arr_o, mine_o, stage,
             d2d_send, d2d_recv, ici_send, ici_recv, d2d_o_send, d2d_o_recv, ici_o_send, ici_o_recv,
             share_send, share_recv, local_sems, g_send, g_recv, g_local):
        s = pl.program_id(0)
        in_step = jnp.maximum(s - nko, 0)
        blk = jnp.where(s < nko, 0, 1 + in_step // nki)
        k = jnp.where(s < nko, s, in_step % nki)
        j = blk - 1
        x, y, c = _position()
        me = 2 * x + y
        sibling = (x, y, 1 - c)
        last = k == jnp.where(s < nko, nko - 1, nki - 1)
        gather_start, gather_relay, gather_finish = _small_gather_steps(small_ref, all_ref, stage, g_send, g_recv, g_local)

        def other_half(ref):
            return ref.at[pl.ds(pl.multiple_of((1 - c) * hm, hm), hm), :]

        def own_half(ref):
            return ref[pl.ds(pl.multiple_of(c * hm, hm), hm), :]

        def to_sibling(jj):
            return pltpu.make_async_remote_copy(
                src_ref=other_half(acc.at[jj % 2]), dst_ref=rbuf.at[jj], send_sem=d2d_send.at[jj], recv_sem=d2d_recv.at[jj],
                device_id=sibling, device_id_type=MESH)

        def to_owner(jj):
            return pltpu.make_async_remote_copy(
                src_ref=sbuf.at[jj], dst_ref=arr.at[2 - jj], send_sem=ici_send.at[jj], recv_sem=ici_recv.at[jj],
                device_id=_owner_id(me, 3 - jj, c), device_id_type=MESH)

        def out_to_sibling():
            return pltpu.make_async_remote_copy(
                src_ref=other_half(acc_o), dst_ref=rbuf_o, send_sem=d2d_o_send, recv_sem=d2d_o_recv,
                device_id=sibling, device_id_type=MESH)

        def out_to_owner(r):
            return pltpu.make_async_remote_copy(
                src_ref=sbuf_o.at[r], dst_ref=arr_o.at[r], send_sem=ici_o_send.at[r], recv_sem=ici_o_recv.at[r],
                device_id=_owner_id(me, r + 1, c), device_id_type=MESH)

        def pair_sum(jj):
            to_sibling(jj).wait_recv()
            return own_half(acc.at[jj % 2]) + rbuf[jj]

        def send_block(jj):
            sbuf[jj] = pair_sum(jj).astype(BF16)
            to_owner(jj).start()

        @pl.when(s == 0)
        def _():
            gather_start()

        @pl.when((blk == 3) & (k == 0))
        def _():
            gather_relay()

        @pl.when((blk == 0) & (k == 0))
        def _():
            acc_o[...] = jnp.zeros((D, D), F32)

        @pl.when(blk == 0)
        def _():
            acc_o[...] += _tn(ao_ref[...], bo_ref[...])

        @pl.when((blk == 0) & last)
        def _():
            out_to_sibling().start()

        @pl.when((blk >= 3) & (k == 0))
        def _():
            to_sibling(j - 2).wait_send()

        @pl.when((blk >= 1) & (k == 0))
        def _():
            acc[j % 2] = jnp.zeros((D, QW), F32)

        @pl.when(blk >= 1)
        def _():
            acc[j % 2] += _tn(a_ref[...], b_ref[...])

        @pl.when((blk >= 1) & last)
        def _():
            to_sibling(j).start()

        @pl.when((blk == 1) & last)
        def _():
            out_to_sibling().wait_recv()
            total_o[...] = own_half(acc_o) + rbuf_o[...]
            for r in range(3):
                q = jnp.bitwise_xor(me, r + 1)
                sbuf_o[r] = total_o[pl.ds(pl.multiple_of(q * qr, qr), qr), :].astype(BF16)
                out_to_owner(r).start()

        @pl.when((blk == 2) & last)
        def _():
            send_block(0)

        @pl.when((blk == 3) & last)
        def _():
            send_block(1)
            send_block(2)

        @pl.when((blk == 4) & last)
        def _():
            g_in = pair_sum(3)
            g_out = total_o[pl.ds(pl.multiple_of(me * qr, qr), qr), :]
            to_sibling(2).wait_send()
            to_sibling(3).wait_send()
            out_to_sibling().wait_send()
            for r in range(3):
                to_owner(r).wait()
                out_to_owner(r).wait()
            for r in range(3):
                g_in = g_in + arr[r].astype(F32)
                g_out = g_out + arr_o[r].astype(F32)
            mine[...] = g_in
            mine_o[...] = g_out
            copies = []
            for idx, (src, dst) in enumerate([(mine, gin_ref), (mine_o, gout_ref)]):
                copies.append(pltpu.make_async_remote_copy(
                    src_ref=src, dst_ref=dst.at[c], send_sem=share_send.at[idx], recv_sem=share_recv.at[idx],
                    device_id=sibling, device_id_type=MESH))
                copies.append(pltpu.make_async_copy(src, dst.at[c], local_sems.at[idx]))
            for cp in copies:
                cp.start()
            for cp in copies:
                cp.wait()
            gather_finish()

    any_spec = pl.BlockSpec(memory_space=pl.ANY)
    grid_spec = pltpu.PrefetchScalarGridSpec(
        num_scalar_prefetch=1, grid=(nko + 4 * nki,),
        in_specs=[pl.BlockSpec((TK, D), out_index), pl.BlockSpec((TK, D), out_index),
                  pl.BlockSpec((TK_IN, D), a_in_index), pl.BlockSpec((TK_IN, QW), b_in_index), any_spec],
        out_specs=[any_spec, any_spec, any_spec],
        scratch_shapes=[pltpu.VMEM((2, D, QW), F32), pltpu.VMEM((4, hm, QW), F32), pltpu.VMEM((3, hm, QW), BF16),
                        pltpu.VMEM((3, hm, QW), BF16), pltpu.VMEM((hm, QW), F32),
                        pltpu.VMEM((D, D), F32), pltpu.VMEM((hm, D), F32), pltpu.VMEM((hm, D), F32), pltpu.VMEM((3, qr, D), BF16),
                        pltpu.VMEM((3, qr, D), BF16), pltpu.VMEM((qr, D), F32),
                        pltpu.VMEM(small.shape, F32),
                        pltpu.SemaphoreType.DMA((4,)), pltpu.SemaphoreType.DMA((4,)),
                        pltpu.SemaphoreType.DMA((3,)), pltpu.SemaphoreType.DMA((3,)),
                        pltpu.SemaphoreType.DMA, pltpu.SemaphoreType.DMA,
                        pltpu.SemaphoreType.DMA((3,)), pltpu.SemaphoreType.DMA((3,)),
                        pltpu.SemaphoreType.DMA((2,)), pltpu.SemaphoreType.DMA((2,)), pltpu.SemaphoreType.DMA((2,)),
                        pltpu.SemaphoreType.DMA((7,)), pltpu.SemaphoreType.DMA((7,)), pltpu.SemaphoreType.DMA])
    return pl.pallas_call(
        body, name=name, grid_spec=grid_spec,
        out_shape=[jax.ShapeDtypeStruct((2, qr, D), F32), jax.ShapeDtypeStruct((2, hm, QW), F32),
                   jax.ShapeDtypeStruct((8, *small.shape), F32)],
        compiler_params=pltpu.CompilerParams(dimension_semantics=("arbitrary",), vmem_limit_bytes=VMEM_LIMIT),
    )(pos, a_out, b_out, a_in, b_in, small)


def _chips(x, y):
    return [(1 - x, y), (x, 1 - y), (1 - x, 1 - y)]


def _gather_weights(parts, split, x, pre, next_parts, pool_w, w_s):
    n = len(parts)
    nn = len(next_parts)
    S = x.shape[0]
    tm = TM_FWD
    nt = S // tm
    staged = [a for a in range(n) if split[a]]

    def body(x_ref, pre_ref, *refs):
        raw, refs = refs[:n], refs[n:]
        next_raw, pool_ref, ws_ref, hb_ref = refs[:nn], refs[nn], refs[nn + 1], refs[nn + 2]
        refs = refs[nn + 3:]
        outs, refs = refs[:n], refs[n:]
        next_outs, poolb_ref, tril_ref, trilt_ref = refs[:nn], refs[nn], refs[nn + 1], refs[nn + 2]
        refs = refs[nn + 3:]
        stage, (send_sems, recv_sems, local_sems) = refs[:len(staged)], refs[len(staged):]
        ins = [stage[staged.index(a)] if split[a] else raw[a] for a in range(n)]
        i = pl.program_id(0)
        x, y, c = _position()
        me = 2 * x + y
        nbr_x, nbr_y, diag = _chips(x, y)
        id_x, id_y, id_d = (2 * chip[0] + chip[1] for chip in (nbr_x, nbr_y, diag))
        sibling = (x, y, 1 - c)

        def rows(a, ref, who, piece=None):
            r = parts[a].shape[0] // 2
            if piece is None:
                return ref.at[pl.ds(pl.multiple_of(who * r, 16), r), :]
            return ref.at[pl.ds(pl.multiple_of(who * r + piece * (r // 2), 16), r // 2), :]

        def copy(a, k, src, dst, to):
            return pltpu.make_async_remote_copy(src_ref=src, dst_ref=dst, send_sem=send_sems.at[a, k], recv_sem=recv_sems.at[a, k],
                                                device_id=to, device_id_type=MESH)

        def direct(a, k, chip):
            if split[a]:
                return copy(a, k, rows(a, ins[a], c), rows(a, outs[a].at[me], c), (*chip, c))
            return copy(a, k, ins[a], outs[a].at[me], (*chip, c))

        def arrival(a, k, src_id):
            if split[a]:
                return copy(a, k, rows(a, ins[a], c), rows(a, outs[a].at[src_id], c), (*nbr_x, c))
            return copy(a, k, ins[a], outs[a].at[src_id], (*nbr_x, c))

        def pass_on(a, k, src_id, piece, chip):
            region = rows(a, outs[a].at[src_id], c, piece)
            return copy(a, k, region, region, (*chip, c))

        def hand_over(a, k, src_id, who):
            region = rows(a, outs[a].at[src_id], who)
            return copy(a, k, region, region, sibling)

        def local(a):
            return pltpu.make_async_copy(ins[a], outs[a].at[me], local_sems.at[a])

        def first_sends(a):
            return [direct(a, 0, nbr_x), direct(a, 1, nbr_y)] + ([] if split[a] else [direct(a, 2, diag)])

        def after_x(a):
            return [pass_on(a, 3, id_x, 1, nbr_y), hand_over(a, 4, id_x, c)] if split[a] else []

        def after_y(a):
            return [pass_on(a, 2, id_y, 0, nbr_x), hand_over(a, 5, id_y, c)] if split[a] else []

        def after_diag(a):
            return [hand_over(a, 6, id_d, c)] if split[a] else []

        @pl.when(i == 0)
        def _():
            for a in staged:
                ins[a][...] = raw[a][...].astype(BF16)
            for a in range(n):
                local(a).start()
                for cp in first_sends(a):
                    cp.start()

        @pl.when(i == 1)
        def _():
            for a in range(nn):
                next_outs[a][...] = next_raw[a][...].astype(BF16)
            poolb_ref[...] = pool_ref[...].astype(BF16)
            lower = lax.broadcasted_iota(jnp.int32, (CHUNK, CHUNK), 0) >= lax.broadcasted_iota(jnp.int32, (CHUNK, CHUNK), 1)
            for h in range(HEADS):
                tril = jnp.where(lower, ws_ref[h], 0.0)
                tril_ref[h] = tril.astype(BF16)
                trilt_ref[h] = tril.T.astype(BF16)

        xh, _ = _rms_fwd(x_ref[...], None)
        hb_ref[...] = (xh * pre_ref[...]).astype(BF16)

        @pl.when(i == nt - 2)
        def _():
            for a in range(n):
                arrival(a, 0, id_x).wait_recv()
                for cp in after_x(a):
                    cp.start()
            for a in range(n):
                arrival(a, 1, id_y).wait_recv()
                for cp in after_y(a):
                    cp.start()

        @pl.when(i == nt - 1)
        def _():
            for a in range(n):
                if split[a]:
                    pass_on(a, 2, id_d, 0, nbr_x).wait_recv()
                    pass_on(a, 3, id_d, 1, nbr_y).wait_recv()
                    for cp in after_diag(a):
                        cp.start()
                else:
                    arrival(a, 2, id_d).wait_recv()
            for a in range(n):
                if split[a]:
                    for k, src_id in ((4, id_x), (5, id_y), (6, id_d)):
                        hand_over(a, k, src_id, 1 - c).wait_recv()
            for a in range(n):
                for cp in first_sends(a) + after_x(a) + after_y(a) + after_diag(a):
                    cp.wait_send()
                local(a).wait()

    any_spec = pl.BlockSpec(memory_space=pl.ANY)
    vmem = pl.BlockSpec(memory_space=pltpu.VMEM)
    sent = [BF16 if split[a] else parts[a].dtype for a in range(n)]
    return pl.pallas_call(
        body, name="gather_weights", grid=(nt,),
        in_specs=[_rows(tm, D), _full((1, D))] + [vmem] * (n + nn + 2),
        out_specs=[_rows(tm, D)] + [any_spec] * n + [vmem] * (nn + 3),
        out_shape=[jax.ShapeDtypeStruct((S, D), BF16)] + [jax.ShapeDtypeStruct((4, *p.shape), t) for p, t in zip(parts, sent)]
        + [jax.ShapeDtypeStruct(p.shape, BF16) for p in next_parts]
        + [jax.ShapeDtypeStruct(pool_w.shape, BF16), jax.ShapeDtypeStruct(w_s.shape, BF16), jax.ShapeDtypeStruct(w_s.shape, BF16)],
        scratch_shapes=[pltpu.VMEM(parts[a].shape, BF16) for a in staged]
        + [pltpu.SemaphoreType.DMA((n, 7)), pltpu.SemaphoreType.DMA((n, 7)), pltpu.SemaphoreType.DMA((n,))],
        compiler_params=_params(),
    )(x, pre, *parts, *next_parts, pool_w, w_s)


def _adamw(w, g, m, v):
    m = ADAM_B1 * m + (1.0 - ADAM_B1) * g
    v = ADAM_B2 * v + (1.0 - ADAM_B2) * (g * g)
    m_hat = m / (1.0 - ADAM_B1 ** ADAM_STEP)
    v_hat = v / (1.0 - ADAM_B2 ** ADAM_STEP)
    delta = -ADAM_LR * (m_hat / (jnp.sqrt(v_hat) + ADAM_EPS) + ADAM_WD * w)
    return delta, m, v


def _adamw_big(ws, gs, ms, vs):
    steps = 4
    n = len(ws)

    def body(*refs):
        ins, outs = refs[:4 * n], refs[4 * n:]
        for a in range(n):
            w_ref, g_ref, m_ref, v_ref = ins[4 * a:4 * a + 4]
            go_ref, d_ref, mo_ref, vo_ref = outs[4 * a:4 * a + 4]
            gv = g_ref[...]
            go_ref[...] = gv
            d_ref[...], mo_ref[...], vo_ref[...] = _adamw(w_ref[...], gv, m_ref[...], v_ref[...])

    specs, shapes, operands = [], [], []
    for w, g, m, v in zip(ws, gs, ms, vs):
        rows, cols = w.shape
        specs += [pl.BlockSpec((rows // steps, cols), lambda i: (i, 0))] * 4
        shapes += [jax.ShapeDtypeStruct((rows, cols), F32)] * 4
        operands += [w, g, m, v]
    res = pl.pallas_call(
        body, name="adamw_big", grid=(steps,),
        in_specs=specs, out_specs=specs, out_shape=shapes,
        compiler_params=pltpu.CompilerParams(dimension_semantics=("arbitrary",), vmem_limit_bytes=VMEM_LIMIT),
    )(*operands)
    return [res[4 * a:4 * a + 4] for a in range(n)]


def _adamw_small(g0, g1, weights, moms, vels):
    names = ["pre", "post", "conv", "pw", "ps", "lng", "lnb", "ws", "bs"]
    shapes = [w.shape for w in weights]

    def body(*refs):
        me = 2 * lax.axis_index("x") + lax.axis_index("y")
        g0_ref, g1_ref = refs[0], refs[1]
        w_refs, m_refs, v_refs = refs[2:11], refs[11:20], refs[20:29]
        outs = refs[29:29 + 36]
        loss_ref = refs[65]
        t0_ref, t1_ref = refs[66], refs[67]
        t0 = g0_ref[0]
        t1 = g1_ref[0]
        for d in range(1, 8):
            t0 = t0 + g0_ref[d]
            t1 = t1 + g1_ref[d]
        t0_ref[...] = t0
        t1_ref[...] = t1
        loss_ref[...] = t1_ref[S1_LOSS:S1_LOSS + 1, 0:1]
        my_conv = pl.multiple_of(S0_CONV + 8 * me, 8)
        my_ln = pl.multiple_of(S1_LN + 8 * me, 8)

        def update(idx, piece, grad):
            go, do, mo, vo = outs[4 * idx:4 * idx + 4]
            go[piece] = grad
            do[piece], mo[piece], vo[piece] = _adamw(w_refs[idx][piece], grad, m_refs[idx][piece], v_refs[idx][piece])

        for layer in range(2):
            for k in range(D // GD):
                lanes = slice(k * GD, (k + 1) * GD)
                tref, pre0, post0 = (t0_ref, S0_PRE, S0_POST) if layer == 0 else (t1_ref, S1_PRE, S1_POST)
                update(0, (slice(layer, layer + 1), lanes), tref[pre0 + k:pre0 + k + 1, :])
                update(1, (slice(layer, layer + 1), lanes), tref[post0 + k:post0 + k + 1, :])
        conv_rows = t0_ref[pl.ds(my_conv, 8), :]
        update(2, (slice(0, 3), 0, slice(None)), conv_rows[0:3, :])
        for g in range(4):
            update(3, (g,), t0_ref[S0_PW + g * GD:S0_PW + (g + 1) * GD, :])
            update(4, (slice(0, 1), slice(g * GD, (g + 1) * GD)), t0_ref[S0_PS + g:S0_PS + g + 1, :])
        ln_rows = t1_ref[pl.ds(my_ln, 8), :]
        for k in range(2):
            update(5, (slice(0, 1), slice(k * GD, (k + 1) * GD)), ln_rows[k:k + 1, :])
            update(6, (slice(0, 1), slice(k * GD, (k + 1) * GD)), ln_rows[2 + k:3 + k, :])
        for h in range(HEADS):
            update(7, (h,), t1_ref[S1_WS + h * CHUNK:S1_WS + (h + 1) * CHUNK, :])
        update(8, (slice(None), slice(None)), t1_ref[S1_BS:S1_BS + HEADS, :])

    vm = pl.BlockSpec(memory_space=pltpu.VMEM)
    out_shape = []
    for s in shapes:
        out_shape += [jax.ShapeDtypeStruct(s, F32)] * 4
    out_shape.append(jax.ShapeDtypeStruct((1, 1), F32))
    res = pl.pallas_call(
        body, name="adamw_small",
        in_specs=[vm] * 29, out_specs=[vm] * 37, out_shape=out_shape,
        scratch_shapes=[pltpu.VMEM((S0_ROWS, GD), F32), pltpu.VMEM((S1_ROWS, GD), F32)],
        compiler_params=pltpu.CompilerParams(vmem_limit_bytes=VMEM_LIMIT),
    )(g0, g1, *weights, *moms, *vels)
    per_weight = {nm: res[4 * i:4 * i + 4] for i, nm in enumerate(names)}
    return per_weight, res[36]


def _pad8(a):
    return jnp.pad(a, ((0, 8 - a.shape[0]), (0, 0)))


def kernel(x, pre_norm, post_norm, even_w_in, even_conv_w, even_pool_w, even_pool_scale, even_w_out, odd_w_in, odd_ln_g, odd_ln_b, odd_w_s, odd_b_s, odd_w_out, loss_target, m_pre_norm, m_post_norm, m_even_w_in, m_even_conv_w, m_even_pool_w, m_even_pool_scale, m_even_w_out, m_odd_w_in, m_odd_ln_g, m_odd_ln_b, m_odd_w_s, m_odd_b_s, m_odd_w_out, v_pre_norm, v_post_norm, v_even_w_in, v_even_conv_w, v_even_pool_w, v_even_pool_scale, v_even_w_out, v_odd_w_in, v_odd_ln_g, v_odd_ln_b, v_odd_w_s, v_odd_b_s, v_odd_w_out):
    xs = x[0]
    tgt = loss_target[0]

    small_shard = jnp.concatenate([_pad8(even_conv_w[0]), _pad8(odd_ln_g.reshape(2, GD)), _pad8(odd_ln_b.reshape(2, GD))], axis=0)
    pre0, pre1 = pre_norm[0:1], pre_norm[1:2]
    post0, post1 = post_norm[0:1], post_norm[1:2]
    hb0, win0, wout0, shard, win1_shard, wout1_shard, pool_wb, ws_tril, ws_tril_t = _gather_weights(
        [even_w_in[0], even_w_out[0], small_shard], [True, True, False], xs, pre0, [odd_w_in[0], odd_w_out[0]],
        even_pool_w[0], odd_w_s[0])
    wout0 = wout0.reshape(D, D)
    px, py, pc = _position()
    pos = jnp.stack([pc, 2 * px + py]).astype(jnp.int32)
    conv_w = shard[:, 0:3, :].transpose(1, 0, 2).reshape(3, AW)
    ln_g = shard[:, 8:10, :].reshape(1, D)
    ln_b = shard[:, 16:18, :].reshape(1, D)
    bias = jnp.broadcast_to(odd_b_s[0][:, :, None], (HEADS, CHUNK, GD))

    x1, proj0, m0, mixp0, conv0, mixed0, pooled0, win1, wout1 = _even_fwd(
        xs, hb0, post0, win0, conv_w, pool_wb, even_pool_scale, wout0, [win1_shard, wout1_shard])
    wout1 = wout1.reshape(D, D)
    proj1, m1, hb1, yp1, dx2, loss_part = _odd_fwd(x1, tgt, pre1, post1, win1, ln_g, ln_b, ws_tril, bias, wout1)
    dx1, dproj1, dmb1, small1 = _odd_bwd(dx2, x1, proj1, m1, loss_part, pre1, post1, win1, ln_g, ln_b, ws_tril, ws_tril_t, bias, wout1)
    g_out1, g_in1, all1 = _wgrad_layer(yp1, dmb1, hb1, dproj1, small1, pos, "wgrad_odd")
    gx, dproj0, dmb0, small0 = _even_bwd(dx1, xs, proj0, conv0, mixed0, pooled0, m0, pre0, post0, win0, conv_w, pool_wb,
                                         even_pool_scale, wout0)
    g_out0, g_in0, all0 = _wgrad_layer(mixp0, dmb0, hb0, dproj0, small0, pos, "wgrad_even")

    big_w = [even_w_in[0], even_w_out[0], odd_w_in[0], odd_w_out[0]]
    big_g = [g.reshape(w.shape) for g, w in zip([g_in0, g_out0, g_in1, g_out1], big_w)]
    big_m = [m_even_w_in[0], m_even_w_out[0], m_odd_w_in[0], m_odd_w_out[0]]
    big_v = [v_even_w_in[0], v_even_w_out[0], v_odd_w_in[0], v_odd_w_out[0]]
    big = _adamw_big(big_w, big_g, big_m, big_v)

    def taps_first(a):
        return jnp.swapaxes(a, 0, 1)

    small_w = [pre_norm, post_norm, taps_first(even_conv_w), even_pool_w[0], even_pool_scale, odd_ln_g, odd_ln_b, odd_w_s[0], odd_b_s[0]]
    small_m = [m_pre_norm, m_post_norm, taps_first(m_even_conv_w), m_even_pool_w[0], m_even_pool_scale, m_odd_ln_g, m_odd_ln_b,
               m_odd_w_s[0], m_odd_b_s[0]]
    small_v = [v_pre_norm, v_post_norm, taps_first(v_even_conv_w), v_even_pool_w[0], v_even_pool_scale, v_odd_ln_g, v_odd_ln_b,
               v_odd_w_s[0], v_odd_b_s[0]]
    sm, loss = _adamw_small(all0, all1, small_w, small_m, small_v)

    def lead(a):
        return a[None]

    per = {
        "pre_norm": sm["pre"], "post_norm": sm["post"],
        "even_w_in": [lead(a) for a in big[0]], "even_conv_w": [taps_first(a) for a in sm["conv"]],
        "even_pool_w": [lead(a) for a in sm["pw"]], "even_pool_scale": sm["ps"],
        "even_w_out": [lead(a) for a in big[1]], "odd_w_in": [lead(a) for a in big[2]],
        "odd_ln_g": sm["lng"], "odd_ln_b": sm["lnb"],
        "odd_w_s": [lead(a) for a in sm["ws"]], "odd_b_s": [lead(a) for a in sm["bs"]],
        "odd_w_out": [lead(a) for a in big[3]],
    }
    order = ["pre_norm", "post_norm", "even_w_in", "even_conv_w", "even_pool_w", "even_pool_scale", "even_w_out", "odd_w_in",
             "odd_ln_g", "odd_ln_b", "odd_w_s", "odd_b_s", "odd_w_out"]
    outs = [loss.reshape(()), gx[None]]
    for kind in range(4):
        outs += [per[nm][kind] for nm in order]
    return tuple(outs)
```

```python
import jax
import jax.numpy as jnp
from jax import lax
from jax.experimental import pallas as pl
from jax.experimental.pallas import tpu as pltpu

F32 = jnp.float32
BF16 = jnp.bfloat16
MESH = pl.DeviceIdType.MESH

D = 1024
W3 = 3 * D
QW = W3 // 4
AW = 512
GD = 128
CHUNK = 128
HEADS = 8
HALO = 16
POOL_WINDOWS = (2, 4, 8, 16)
EPS = 1e-6
TM_FWD = 512
TM_BWD = 256
TK = 1024
TK_IN = 2048
VMEM_LIMIT = 56 * 1024 * 1024

ADAM_LR, ADAM_B1, ADAM_B2, ADAM_EPS, ADAM_WD, ADAM_STEP = 0.001, 0.9, 0.999, 1e-08, 0.01, 10

S0_PRE, S0_POST, S0_CONV, S0_PS, S0_PW, S0_ROWS = 0, 8, 16, 48, 56, 568
S1_PRE, S1_POST, S1_LN, S1_BS, S1_WS, S1_LOSS, S1_ROWS = 0, 8, 16, 48, 56, 1080, 1088


def _nn(a, b):
    return jnp.dot(a, b, preferred_element_type=F32)


def _nt(a, b):
    return lax.dot_general(a, b, (((1,), (1,)), ((), ())), preferred_element_type=F32)


def _tn(a, b):
    return lax.dot_general(a, b, (((0,), (0,)), ((), ())), preferred_element_type=F32)


def _sigmoid(z):
    return 1.0 / (1.0 + jnp.exp(-z))


def _rms_fwd(x):
    r = lax.rsqrt(jnp.mean(x * x, axis=-1, keepdims=True) + EPS)
    return x * r, r


def _rms_bwd(dy, xh, r, g):
    dn = dy * g
    dx = r * (dn - xh * jnp.mean(xh * dn, axis=-1, keepdims=True))
    return dx, jnp.sum(dy * xh, axis=0, keepdims=True)


def _full(shape):
    nd = len(shape)
    return pl.BlockSpec(shape, lambda i, _n=nd: (0,) * _n, pipeline_mode=pl.Buffered(1))


def _full_out(shape):
    nd = len(shape)
    return pl.BlockSpec(shape, lambda i, _n=nd: (0,) * _n)


def _rows(tm, width, index=None):
    return pl.BlockSpec((tm, width), (lambda i: (i, 0)) if index is None else index)


def _params():
    return pltpu.CompilerParams(dimension_semantics=("arbitrary",), vmem_limit_bytes=VMEM_LIMIT)


def _position():
    x, y, c = lax.axis_index("x"), lax.axis_index("y"), lax.axis_index("c")
    return x, y, c


def _even_mix(proj_ref, hc_ext, xp_ext, cw_ref, pw_ref, ps_ref, first_row):
    tm = proj_ref.shape[0]
    xa = proj_ref[:, 0:AW]
    gb = proj_ref[:, AW:2 * AW]
    gc = proj_ref[:, 2 * AW:3 * AW]
    za = proj_ref[:, 3 * AW:4 * AW]
    xp = proj_ref[:, 4 * AW:5 * AW]
    zp = proj_ref[:, 5 * AW:6 * AW]
    hc = gc * xa
    hc_ext[HALO:, :] = hc
    e = hc_ext[...]
    conv = cw_ref[2:3, :] * hc + cw_ref[1:2, :] * pltpu.roll(e, 1, 0)[HALO:] + cw_ref[0:1, :] * pltpu.roll(e, 2, 0)[HALO:]
    sa = _sigmoid(za)
    xp_ext[HALO:, :] = xp
    pos = first_row + lax.broadcasted_iota(jnp.int32, (tm, 1), 0)
    pooled, mixed, counts = [], [], []
    for g, w in enumerate(POOL_WINDOWS):
        cols = slice(g * GD, (g + 1) * GD)
        s = xp_ext[:, cols]
        for k in range(g + 1):
            s = s + pltpu.roll(s, 2 ** k, 0)
        count = jnp.minimum(pos + 1, w).astype(F32)
        pg = s[HALO:] / count - xp[:, cols]
        pooled.append(pg.astype(BF16))
        mixed.append(_nn(pooled[-1], pw_ref[g]))
        counts.append(count)
    mixed = jnp.concatenate(mixed, axis=-1)
    sb = _sigmoid(zp)
    return dict(xa=xa, gb=gb, gc=gc, za=za, zp=zp, hc=hc, conv=conv, sa=sa, sb=sb, pooled=pooled, mixed=mixed, counts=counts)


def _half_rows(ref, rows, who):
    return ref.at[pl.ds(pl.multiple_of(who * (rows // 2), 8), rows // 2), :]


def _store_permuted(ref, value):
    for ob in range(D // GD):
        nb = 4 * (ob % 2) + ob // 2
        ref[:, nb * GD:(nb + 1) * GD] = value[:, ob * GD:(ob + 1) * GD]


def _even_fwd(x, hb, post, win, cw, pwb, ps, wout, next_shards):
    S = x.shape[0]
    tm = TM_FWD
    nt = S // tm
    relay = (3 * nt) // 4
    n = len(next_shards)
    shard_rows = [p.shape[0] for p in next_shards]

    def body(x_ref, hb_ref, post_ref, win_ref, cw_ref, pw_ref, ps_ref, wout_ref, *rest):
        shard_refs, rest = rest[:n], rest[n:]
        x1_ref, proj_ref, m_ref, mixp_ref, conv_ref, mixed_ref, pooled_ref = rest[:7]
        full_refs, rest = rest[7:7 + n], rest[7 + n:]
        hc_ext, xp_ext, mix_sc = rest[:3]
        stage, rest = rest[3:3 + n], rest[3 + n:]
        send_sems, recv_sems, local_sems = rest
        i = pl.program_id(0)
        px, py, pc = _position()
        me = 2 * px + py
        chips = _chips(px, py)

        def ici(a, j):
            return pltpu.make_async_remote_copy(
                src_ref=_half_rows(shard_refs[a], shard_rows[a], pc), dst_ref=_half_rows(full_refs[a].at[me], shard_rows[a], pc),
                send_sem=send_sems.at[a, j], recv_sem=recv_sems.at[a, j], device_id=(*chips[j], pc), device_id_type=MESH)

        def ici_arrival(a, j):
            src = 2 * chips[j][0] + chips[j][1]
            return pltpu.make_async_remote_copy(
                src_ref=_half_rows(shard_refs[a], shard_rows[a], pc), dst_ref=_half_rows(full_refs[a].at[src], shard_rows[a], pc),
                send_sem=send_sems.at[a, j], recv_sem=recv_sems.at[a, j], device_id=(*chips[j], pc), device_id_type=MESH)

        def relay_copy(a, j, who):
            src = 2 * chips[j][0] + chips[j][1]
            region = _half_rows(full_refs[a].at[src], shard_rows[a], who)
            return pltpu.make_async_remote_copy(
                src_ref=region, dst_ref=region, send_sem=send_sems.at[a, 3 + j], recv_sem=recv_sems.at[a, 3 + j],
                device_id=(px, py, 1 - pc), device_id_type=MESH)

        def own_copy(a):
            return pltpu.make_async_copy(stage[a], full_refs[a].at[me], local_sems.at[a])

        @pl.when(i == 0)
        def _():
            hc_ext[0:HALO, :] = jnp.zeros((HALO, AW), F32)
            xp_ext[0:HALO, :] = jnp.zeros((HALO, AW), F32)
            for a in range(n):
                for j in range(3):
                    ici(a, j).start()
            for a in range(n):
                load = pltpu.make_async_copy(shard_refs[a], stage[a], local_sems.at[a])
                load.start()
                load.wait()
                own_copy(a).start()

        @pl.when(i == relay)
        def _():
            for j in range(3):
                for a in range(n):
                    ici_arrival(a, j).wait_recv()
                    relay_copy(a, j, pc).start()

        hb = hb_ref[...]
        for q in range(4):
            proj_ref[:, q * QW:(q + 1) * QW] = _nn(hb, win_ref[q])
        t = _even_mix(proj_ref, hc_ext, xp_ext, cw_ref, pw_ref, ps_ref, i * tm)
        conv_ref[...] = t["conv"]
        mixed_ref[...] = t["mixed"]
        for g in range(4):
            pooled_ref[:, g * GD:(g + 1) * GD] = t["pooled"][g]
        mix_sc[:, 0:AW] = (t["gb"] * t["conv"] * (t["za"] * t["sa"])).astype(BF16)
        mix_sc[:, AW:2 * AW] = (t["mixed"] * ps_ref[...] * (t["zp"] * t["sb"])).astype(BF16)
        mix = mix_sc[...]
        _store_permuted(mixp_ref, mix)
        m = _nn(mix, wout_ref[...])
        m_ref[...] = m
        mh, _ = _rms_fwd(m)
        x1_ref[...] = x_ref[...] + mh * post_ref[...]
        hc_ext[0:HALO, :] = hc_ext[tm:tm + HALO, :]
        xp_ext[0:HALO, :] = xp_ext[tm:tm + HALO, :]

        @pl.when(i == nt - 1)
        def _():
            for j in range(3):
                for a in range(n):
                    relay_copy(a, j, 1 - pc).wait_recv()
            for a in range(n):
                for j in range(3):
                    ici(a, j).wait_send()
                    relay_copy(a, j, pc).wait_send()
                own_copy(a).wait()

    any_spec = pl.BlockSpec(memory_space=pl.ANY)
    return pl.pallas_call(
        body, name="even_fwd", grid=(nt,),
        in_specs=[_rows(tm, D), _rows(tm, D), _full((1, D)), _full((4, D, QW)), _full((3, AW)), _full((4, GD, GD)),
                  _full((1, AW)), _full((D, D))] + [any_spec] * n,
        out_specs=[_rows(tm, D), _rows(tm, W3), _rows(tm, D), _rows(tm, D), _rows(tm, AW), _rows(tm, AW), _rows(tm, AW)]
        + [any_spec] * n,
        out_shape=[jax.ShapeDtypeStruct((S, D), F32), jax.ShapeDtypeStruct((S, W3), F32), jax.ShapeDtypeStruct((S, D), F32),
                   jax.ShapeDtypeStruct((S, D), BF16),
                   jax.ShapeDtypeStruct((S, AW), F32), jax.ShapeDtypeStruct((S, AW), F32), jax.ShapeDtypeStruct((S, AW), BF16)]
        + [jax.ShapeDtypeStruct((4, *p.shape), p.dtype) for p in next_shards],
        scratch_shapes=[pltpu.VMEM((tm + HALO, AW), F32), pltpu.VMEM((tm + HALO, AW), F32), pltpu.VMEM((tm, D), BF16)]
        + [pltpu.VMEM(p.shape, p.dtype) for p in next_shards]
        + [pltpu.SemaphoreType.DMA((n, 6)), pltpu.SemaphoreType.DMA((n, 6)), pltpu.SemaphoreType.DMA((n,))],
        compiler_params=_params(),
    )(x, hb, post, win, cw, pwb, ps, wout, *next_shards)


def _chunks_side_by_side(a, h):
    return jnp.concatenate([a[n * CHUNK:(n + 1) * CHUNK, h * GD:(h + 1) * GD] for n in range(a.shape[0] // CHUNK)], axis=1)


def _odd_mix(proj_ref, lng_ref, lnb_ref, ws_ref, bias_ref, sv_ref):
    tm = proj_ref.shape[0]
    u = proj_ref[:, 0:D]
    v = proj_ref[:, D:2 * D]
    z = proj_ref[:, 2 * D:3 * D]
    mu = jnp.mean(v, axis=-1, keepdims=True)
    vc = v - mu
    rs = lax.rsqrt(jnp.mean(vc * vc, axis=-1, keepdims=True) + EPS)
    vh = vc * rs
    vnb = (vh * lng_ref[...] + lnb_ref[...]).astype(BF16)
    for h in range(HEADS):
        sv = _nn(ws_ref[h], _chunks_side_by_side(vnb, h))
        for n in range(tm // CHUNK):
            sv_ref[n * CHUNK:(n + 1) * CHUNK, h * GD:(h + 1) * GD] = sv[:, n * GD:(n + 1) * GD] + bias_ref[h]
    return dict(u=u, z=z, vh=vh, rs=rs, vnb=vnb, sz=_sigmoid(z))


def _odd_fwd(x1, tgt, pre, post, win, lng, lnb, wsb, bias, wout):
    S = x1.shape[0]
    tm = TM_FWD
    nt = S // tm

    def body(x_ref, tgt_ref, pre_ref, post_ref, win_ref, lng_ref, lnb_ref, ws_ref, bias_ref, wout_ref,
             proj_ref, m_ref, hb_ref, yb_ref, dx2_ref, loss_ref, sv_ref):
        i = pl.program_id(0)

        @pl.when(i == 0)
        def _():
            loss_ref[...] = jnp.zeros((8, GD), F32)

        xv = x_ref[...]
        xh, _ = _rms_fwd(xv)
        hb = (xh * pre_ref[...]).astype(BF16)
        hb_ref[...] = hb
        for q in range(4):
            proj_ref[:, q * QW:(q + 1) * QW] = _nn(hb, win_ref[q])
        t = _odd_mix(proj_ref, lng_ref, lnb_ref, ws_ref, bias_ref, sv_ref)
        yb = (t["u"] * sv_ref[...] * (t["z"] * t["sz"])).astype(BF16)
        _store_permuted(yb_ref, yb)
        m = _nn(yb, wout_ref[...])
        m_ref[...] = m
        mh, _ = _rms_fwd(m)
        err = xv + mh * post_ref[...] - tgt_ref[...]
        dx2_ref[...] = err * (1.0 / D)
        part = 0.5 * jnp.sum(jnp.mean(err * err, axis=-1, keepdims=True), axis=0, keepdims=True)
        loss_ref[...] += jnp.broadcast_to(part, (8, GD))

    return pl.pallas_call(
        body, name="odd_fwd", grid=(nt,),
        in_specs=[_rows(tm, D), _rows(tm, D), _full((1, D)), _full((1, D)), _full((4, D, QW)), _full((1, D)), _full((1, D)),
                  _full((HEADS, CHUNK, CHUNK)), _full((HEADS, CHUNK, GD)), _full((D, D))],
        out_specs=[_rows(tm, W3), _rows(tm, D), _rows(tm, D), _rows(tm, D), _rows(tm, D), _full_out((8, GD))],
        out_shape=[jax.ShapeDtypeStruct((S, W3), F32), jax.ShapeDtypeStruct((S, D), F32), jax.ShapeDtypeStruct((S, D), BF16),
                   jax.ShapeDtypeStruct((S, D), BF16), jax.ShapeDtypeStruct((S, D), F32), jax.ShapeDtypeStruct((8, GD), F32)],
        scratch_shapes=[pltpu.VMEM((tm, D), F32)],
        compiler_params=_params(),
    )(x1, tgt, pre, post, win, lng, lnb, wsb, bias, wout)


def _store_rows(ref, row0, value):
    r, width = value.shape
    for a in range(r):
        for k in range(width // GD):
            ref[row0 + a * (width // GD) + k:row0 + a * (width // GD) + k + 1, :] = value[a:a + 1, k * GD:(k + 1) * GD]


def _proj_bwd(dproj, win_ref, x, dy, pre):
    dh = _nt(dproj[:, 0:QW], win_ref[0])
    for q in range(1, 4):
        dh += _nt(dproj[:, q * QW:(q + 1) * QW], win_ref[q])
    xh, r = _rms_fwd(x)
    dxn, dpre = _rms_bwd(dh, xh, r, pre)
    return dy + dxn, dpre


def _odd_bwd(dx2, x1, proj, m, loss, pre, post, win, lng, lnb, wsb, wsbt, bias, wout):
    S = x1.shape[0]
    tm = TM_BWD
    nt = S // tm

    def body(dy_ref, x_ref, proj_ref, m_ref, loss_ref, pre_ref, post_ref, win_ref, lng_ref, lnb_ref, ws_ref, wst_ref, bias_ref,
             wout_ref, dx_ref, dproj_ref, dmb_ref, small_ref, sv_ref, dvn_ref, acc1024, dws_acc, dbs_acc):
        i = pl.program_id(0)

        @pl.when(i == 0)
        def _():
            acc1024[...] = jnp.zeros_like(acc1024)
            dws_acc[...] = jnp.zeros_like(dws_acc)
            dbs_acc[...] = jnp.zeros_like(dbs_acc)

        dy = dy_ref[...]
        mh, rm = _rms_fwd(m_ref[...])
        dm, dpost = _rms_bwd(dy, mh, rm, post_ref[...])
        dmb = dm.astype(BF16)
        dmb_ref[...] = dmb
        dyv = _nt(dmb, wout_ref[...])
        t = _odd_mix(proj_ref, lng_ref, lnb_ref, ws_ref, bias_ref, sv_ref)
        u, z, sz, sv = t["u"], t["z"], t["sz"], sv_ref[...]
        dproj_ref[:, 0:D] = (dyv * sv * (z * sz)).astype(BF16)
        dproj_ref[:, 2 * D:3 * D] = (dyv * u * sv * (sz * (1.0 + z * (1.0 - sz)))).astype(BF16)
        dsv = dyv * u * (z * sz)
        dsvb = dsv.astype(BF16)
        for h in range(HEADS):
            dsv_h = _chunks_side_by_side(dsvb, h)
            dvn_h = _nn(wst_ref[h], dsv_h)
            dws_acc[h] += _nt(dsv_h, _chunks_side_by_side(t["vnb"], h))
            for n in range(tm // CHUNK):
                rows, cols = slice(n * CHUNK, (n + 1) * CHUNK), slice(h * GD, (h + 1) * GD)
                dvn_ref[rows, cols] = dvn_h[:, n * GD:(n + 1) * GD]
                dbs_acc[h] += dsv[rows, cols]
        dvn = dvn_ref[...]
        vh = t["vh"]
        dvh = dvn * lng_ref[...]
        dv = t["rs"] * (dvh - jnp.mean(dvh, axis=-1, keepdims=True) - vh * jnp.mean(dvh * vh, axis=-1, keepdims=True))
        dproj_ref[:, D:2 * D] = dv.astype(BF16)
        dx_ref[...], dpre = _proj_bwd(dproj_ref[...], win_ref, x_ref[...], dy, pre_ref[...])
        acc1024[0:1, :] += dpre
        acc1024[1:2, :] += dpost
        acc1024[2:3, :] += jnp.sum(dvn * vh, axis=0, keepdims=True)
        acc1024[3:4, :] += jnp.sum(dvn, axis=0, keepdims=True)

        @pl.when(i == nt - 1)
        def _():
            small_ref[...] = jnp.zeros_like(small_ref)
            _store_rows(small_ref, S1_PRE, acc1024[0:1, :])
            _store_rows(small_ref, S1_POST, acc1024[1:2, :])
            for q in range(4):
                _store_rows(small_ref, S1_LN + 8 * q, acc1024[2:3, 2 * q * GD:(2 * q + 2) * GD])
                _store_rows(small_ref, S1_LN + 8 * q + 2, acc1024[3:4, 2 * q * GD:(2 * q + 2) * GD])
            lower = lax.broadcasted_iota(jnp.int32, (CHUNK, CHUNK), 0) >= lax.broadcasted_iota(jnp.int32, (CHUNK, CHUNK), 1)
            for h in range(HEADS):
                small_ref[S1_WS + h * CHUNK:S1_WS + (h + 1) * CHUNK, :] = jnp.where(lower, dws_acc[h], 0.0)
                small_ref[S1_BS + h:S1_BS + h + 1, :] = jnp.sum(dbs_acc[h].T, axis=0, keepdims=True)
            small_ref[S1_LOSS:S1_LOSS + 8, :] = loss_ref[...]

    return pl.pallas_call(
        body, name="odd_bwd", grid=(nt,),
        in_specs=[_rows(tm, D), _rows(tm, D), _rows(tm, W3), _rows(tm, D), _full((8, GD)), _full((1, D)), _full((1, D)),
                  _full((4, D, QW)), _full((1, D)), _full((1, D)), _full((HEADS, CHUNK, CHUNK)), _full((HEADS, CHUNK, CHUNK)),
                  _full((HEADS, CHUNK, GD)), _full((D, D))],
        out_specs=[_rows(tm, D), _rows(tm, W3), _rows(tm, D), _full_out((S1_ROWS, GD))],
        out_shape=[jax.ShapeDtypeStruct((S, D), F32), jax.ShapeDtypeStruct((S, W3), BF16), jax.ShapeDtypeStruct((S, D), BF16),
                   jax.ShapeDtypeStruct((S1_ROWS, GD), F32)],
        scratch_shapes=[pltpu.VMEM((tm, D), F32), pltpu.VMEM((tm, D), F32), pltpu.VMEM((8, D), F32),
                        pltpu.VMEM((HEADS, CHUNK, CHUNK), F32), pltpu.VMEM((HEADS, CHUNK, GD), F32)],
        compiler_params=_params(),
    )(dx2, x1, proj, m, loss, pre, post, win, lng, lnb, wsb, wsbt, bias, wout)


def _even_bwd(dx1, x, proj, conv, mixed, pooled, m, pre, post, win, cw, pwb, ps, wout):
    S = x.shape[0]
    tm = TM_BWD
    nt = S // tm
    L = tm + HALO

    def rev(i):
        return (nt - 1 - i, 0)

    def body(dy_ref, x_ref, proj_ref, conv_ref, mixed_ref, pooled_ref, m_ref, pre_ref, post_ref, win_ref, cw_ref, pw_ref, ps_ref,
             wout_ref, dx_ref, dproj_ref, dmb_ref, small_ref, dconv_ext, q_ext, acc1024, acc512, dpw_acc):
        i = pl.program_id(0)
        tile = nt - 1 - i

        @pl.when(i == 0)
        def _():
            dconv_ext[tm:L, :] = jnp.zeros((HALO, AW), F32)
            q_ext[tm:L, :] = jnp.zeros((HALO, AW), F32)
            acc1024[...] = jnp.zeros_like(acc1024)
            acc512[...] = jnp.zeros_like(acc512)
            dpw_acc[...] = jnp.zeros_like(dpw_acc)

        dy = dy_ref[...]
        mh, rm = _rms_fwd(m_ref[...])
        dm, dpost = _rms_bwd(dy, mh, rm, post_ref[...])
        dmb = dm.astype(BF16)
        dmb_ref[...] = dmb
        dmix = _nt(dmb, wout_ref[...])
        dya, dyb = dmix[:, 0:AW], dmix[:, AW:2 * AW]
        xa, gb, gc, za = (proj_ref[:, k * AW:(k + 1) * AW] for k in range(4))
        zp = proj_ref[:, 5 * AW:6 * AW]
        hc = gc * xa
        conv = conv_ref[...]
        sa = _sigmoid(za)
        silu_a = za * sa
        dproj_ref[:, AW:2 * AW] = (dya * conv * silu_a).astype(BF16)
        dproj_ref[:, 3 * AW:4 * AW] = (dya * gb * conv * (sa * (1.0 + za * (1.0 - sa)))).astype(BF16)
        dconv = dya * gb * silu_a
        dconv_ext[0:tm, :] = dconv
        e = dconv_ext[...]
        dc1 = pltpu.roll(e, L - 1, 0)[0:tm]
        dc2 = pltpu.roll(e, L - 2, 0)[0:tm]
        dhc = cw_ref[2:3, :] * dconv + cw_ref[1:2, :] * dc1 + cw_ref[0:1, :] * dc2
        dproj_ref[:, 0:AW] = (dhc * gc).astype(BF16)
        dproj_ref[:, 2 * AW:3 * AW] = (dhc * xa).astype(BF16)
        acc512[0:1, :] += jnp.sum(dc2 * hc, axis=0, keepdims=True)
        acc512[1:2, :] += jnp.sum(dc1 * hc, axis=0, keepdims=True)
        acc512[2:3, :] += jnp.sum(dconv * hc, axis=0, keepdims=True)

        sb, mixed = _sigmoid(zp), mixed_ref[...]
        silu_b = zp * sb
        acc512[3:4, :] += jnp.sum(dyb * mixed * silu_b, axis=0, keepdims=True)
        dmixedb = (dyb * ps_ref[...] * silu_b).astype(BF16)
        dproj_ref[:, 5 * AW:6 * AW] = (dyb * mixed * ps_ref[...] * (sb * (1.0 + zp * (1.0 - sb)))).astype(BF16)
        pos = tile * tm + lax.broadcasted_iota(jnp.int32, (tm, 1), 0)
        for g, w in enumerate(POOL_WINDOWS):
            cols = slice(g * GD, (g + 1) * GD)
            dpw_acc[g] += _tn(pooled_ref[:, cols], dmixedb[:, cols])
            dpooled = _nt(dmixedb[:, cols], pw_ref[g])
            q_ext[0:tm, cols] = dpooled / jnp.minimum(pos + 1, w).astype(F32)
            s = q_ext[:, cols]
            for k in range(g + 1):
                s = s + pltpu.roll(s, L - 2 ** k, 0)
            dproj_ref[:, 4 * AW + g * GD:4 * AW + (g + 1) * GD] = (s[0:tm] - dpooled).astype(BF16)
        dconv_ext[tm:L, :] = dconv_ext[0:HALO, :]
        q_ext[tm:L, :] = q_ext[0:HALO, :]

        dx_ref[...], dpre = _proj_bwd(dproj_ref[...], win_ref, x_ref[...], dy, pre_ref[...])
        acc1024[0:1, :] += dpre
        acc1024[1:2, :] += dpost

        @pl.when(i == nt - 1)
        def _():
            small_ref[...] = jnp.zeros_like(small_ref)
            _store_rows(small_ref, S0_PRE, acc1024[0:1, :])
            _store_rows(small_ref, S0_POST, acc1024[1:2, :])
            for q in range(4):
                for k in range(3):
                    small_ref[S0_CONV + 8 * q + k:S0_CONV + 8 * q + k + 1, :] = acc512[k:k + 1, q * GD:(q + 1) * GD]
            _store_rows(small_ref, S0_PS, acc512[3:4, :])
            for g in range(4):
                small_ref[S0_PW + g * GD:S0_PW + (g + 1) * GD, :] = dpw_acc[g]

    return pl.pallas_call(
        body, name="even_bwd", grid=(nt,),
        in_specs=[_rows(tm, D, rev), _rows(tm, D, rev), _rows(tm, W3, rev), _rows(tm, AW, rev), _rows(tm, AW, rev), _rows(tm, AW, rev),
                  _rows(tm, D, rev),
                  _full((1, D)), _full((1, D)), _full((4, D, QW)), _full((3, AW)), _full((4, GD, GD)), _full((1, AW)), _full((D, D))],
        out_specs=[_rows(tm, D, rev), _rows(tm, W3, rev), _rows(tm, D, rev), _full_out((S0_ROWS, GD))],
        out_shape=[jax.ShapeDtypeStruct((S, D), F32), jax.ShapeDtypeStruct((S, W3), BF16), jax.ShapeDtypeStruct((S, D), BF16),
                   jax.ShapeDtypeStruct((S0_ROWS, GD), F32)],
        scratch_shapes=[pltpu.VMEM((L, AW), F32), pltpu.VMEM((L, AW), F32),
                        pltpu.VMEM((8, D), F32), pltpu.VMEM((8, AW), F32), pltpu.VMEM((4, GD, GD), F32)],
        compiler_params=_params(),
    )(dx1, x, proj, conv, mixed, pooled, m, pre, post, win, cw, pwb, ps, wout)


def _owner_id(me, relation, c):
    q = jnp.bitwise_xor(me, relation)
    return (q // 2, q % 2, c)


def _small_gather_steps(small_ref, all_ref, stage, send_sems, recv_sems, local_sem):
    x, y, c = _position()
    chips = _chips(x, y)

    def slot(chip, core):
        return 4 * chip[0] + 2 * chip[1] + core

    def copy(k, src, block, to):
        return pltpu.make_async_remote_copy(src_ref=src, dst_ref=all_ref.at[block], send_sem=send_sems.at[k],
                                            recv_sem=recv_sems.at[k], device_id=to, device_id_type=MESH)

    def own_copy():
        return pltpu.make_async_copy(stage, all_ref.at[slot((x, y), c)], local_sem)

    def first_sends():
        mine = slot((x, y), c)
        return [copy(0, small_ref, mine, (x, y, 1 - c))] + [copy(1 + j, small_ref, mine, (*chip, c)) for j, chip in enumerate(chips)]

    def relays():
        return [copy(4 + j, all_ref.at[slot(chip, c)], slot(chip, c), (x, y, 1 - c)) for j, chip in enumerate(chips)]

    def start():
        for cp in first_sends():
            cp.start()
        load = pltpu.make_async_copy(small_ref, stage, local_sem)
        load.start()
        load.wait()
        own_copy().start()

    def relay():
        for j, chip in enumerate(chips):
            copy(1 + j, small_ref, slot(chip, c), (*chip, c)).wait_recv()
        for cp in relays():
            cp.start()

    def finish():
        copy(0, small_ref, slot((x, y), 1 - c), (x, y, 1 - c)).wait_recv()
        for j, chip in enumerate(chips):
            copy(4 + j, small_ref, slot(chip, 1 - c), (x, y, 1 - c)).wait_recv()
        for cp in first_sends() + relays():
            cp.wait_send()
        own_copy().wait()

    return start, relay, finish


def _wgrad_layer(a_out, b_out, a_in, b_in, small, pos, name):
    S = a_in.shape[0]
    nko = S // TK
    nki = S // TK_IN
    hm = D // 2
    qr = hm // 4

    def out_index(s, pos_ref):
        return (jnp.minimum(s, nko - 1), 0)

    def a_in_index(s, pos_ref):
        return (jnp.where(s >= nko, (s - nko) % nki, 0), 0)

    def b_in_index(s, pos_ref):
        return (jnp.where(s >= nko, (s - nko) % nki, 0), jnp.bitwise_xor(pos_ref[1], 3 - jnp.maximum((s - nko) // nki, 0)))

    def body(pos_ref, ao_ref, bo_ref, a_ref, b_ref, small_ref, gout_ref, gin_ref, all_ref,
             acc, rbuf, sbuf, arr, mine, acc_o, rbuf_o, total_o, sbuf_o, arr_o, mine_o, stage,
             d2d_send, d2d_recv, ici_send, ici_recv, d2d_o_send, d2d_o_recv, ici_o_send, ici_o_recv,
             share_send, share_recv, local_sems, g_send, g_recv, g_local):
        s = pl.program_id(0)
        in_step = jnp.maximum(s - nko, 0)
        blk = jnp.where(s < nko, 0, 1 + in_step // nki)
        k = jnp.where(s < nko, s, in_step % nki)
        j = blk - 1
        x, y, c = _position()
        me = 2 * x + y
        sibling = (x, y, 1 - c)
        last = k == jnp.where(s < nko, nko - 1, nki - 1)
        gather_start, gather_relay, gather_finish = _small_gather_steps(small_ref, all_ref, stage, g_send, g_recv, g_local)

        def other_half(ref):
            return ref.at[pl.ds(pl.multiple_of((1 - c) * hm, hm), hm), :]

        def own_half(ref):
            return ref[pl.ds(pl.multiple_of(c * hm, hm), hm), :]

        def to_sibling(jj):
            return pltpu.make_async_remote_copy(
                src_ref=other_half(acc.at[jj % 2]), dst_ref=rbuf.at[jj], send_sem=d2d_send.at[jj], recv_sem=d2d_recv.at[jj],
                device_id=sibling, device_id_type=MESH)

        def to_owner(jj):
            return pltpu.make_async_remote_copy(
                src_ref=sbuf.at[jj], dst_ref=arr.at[2 - jj], send_sem=ici_send.at[jj], recv_sem=ici_recv.at[jj],
                device_id=_owner_id(me, 3 - jj, c), device_id_type=MESH)

        def out_to_sibling():
            return pltpu.make_async_remote_copy(
                src_ref=other_half(acc_o), dst_ref=rbuf_o, send_sem=d2d_o_send, recv_sem=d2d_o_recv,
                device_id=sibling, device_id_type=MESH)

        def out_to_owner(r):
            return pltpu.make_async_remote_copy(
                src_ref=sbuf_o.at[r], dst_ref=arr_o.at[r], send_sem=ici_o_send.at[r], recv_sem=ici_o_recv.at[r],
                device_id=_owner_id(me, r + 1, c), device_id_type=MESH)

        def pair_sum(jj):
            to_sibling(jj).wait_recv()
            return own_half(acc.at[jj % 2]) + rbuf[jj]

        def send_block(jj):
            sbuf[jj] = pair_sum(jj).astype(BF16)
            to_owner(jj).start()

        @pl.when(s == 0)
        def _():
            gather_start()

        @pl.when((blk == 3) & (k == 0))
        def _():
            gather_relay()

        @pl.when((blk == 0) & (k == 0))
        def _():
            acc_o[...] = jnp.zeros((D, D), F32)

        @pl.when(blk == 0)
        def _():
            acc_o[...] += _tn(ao_ref[...], bo_ref[...])

        @pl.when((blk == 0) & last)
        def _():
            out_to_sibling().start()

        @pl.when((blk >= 3) & (k == 0))
        def _():
            to_sibling(j - 2).wait_send()

        @pl.when((blk >= 1) & (k == 0))
        def _():
            acc[j % 2] = jnp.zeros((D, QW), F32)

        @pl.when(blk >= 1)
        def _():
            acc[j % 2] += _tn(a_ref[...], b_ref[...])

        @pl.when((blk >= 1) & last)
        def _():
            to_sibling(j).start()

        @pl.when((blk == 1) & last)
        def _():
            out_to_sibling().wait_recv()
            total_o[...] = own_half(acc_o) + rbuf_o[...]
            for r in range(3):
                q = jnp.bitwise_xor(me, r + 1)
                sbuf_o[r] = total_o[pl.ds(pl.multiple_of(q * qr, qr), qr), :].astype(BF16)
                out_to_owner(r).start()

        @pl.when((blk == 2) & last)
        def _():
            send_block(0)

        @pl.when((blk == 3) & last)
        def _():
            send_block(1)
            send_block(2)

        @pl.when((blk == 4) & last)
        def _():
            g_in = pair_sum(3)
            g_out = total_o[pl.ds(pl.multiple_of(me * qr, qr), qr), :]
            to_sibling(2).wait_send()
            to_sibling(3).wait_send()
            out_to_sibling().wait_send()
            for r in range(3):
                to_owner(r).wait()
                out_to_owner(r).wait()
            for r in range(3):
                g_in = g_in + arr[r].astype(F32)
                g_out = g_out + arr_o[r].astype(F32)
            mine[...] = g_in
            mine_o[...] = g_out
            copies = []
            for idx, (src, dst) in enumerate([(mine, gin_ref), (mine_o, gout_ref)]):
                copies.append(pltpu.make_async_remote_copy(
                    src_ref=src, dst_ref=dst.at[c], send_sem=share_send.at[idx], recv_sem=share_recv.at[idx],
                    device_id=sibling, device_id_type=MESH))
                copies.append(pltpu.make_async_copy(src, dst.at[c], local_sems.at[idx]))
            for cp in copies:
                cp.start()
            for cp in copies:
                cp.wait()
            gather_finish()

    any_spec = pl.BlockSpec(memory_space=pl.ANY)
    grid_spec = pltpu.PrefetchScalarGridSpec(
        num_scalar_prefetch=1, grid=(nko + 4 * nki,),
        in_specs=[pl.BlockSpec((TK, D), out_index), pl.BlockSpec((TK, D), out_index),
                  pl.BlockSpec((TK_IN, D), a_in_index), pl.BlockSpec((TK_IN, QW), b_in_index), any_spec],
        out_specs=[any_spec, any_spec, any_spec],
        scratch_shapes=[pltpu.VMEM((2, D, QW), F32), pltpu.VMEM((4, hm, QW), F32), pltpu.VMEM((3, hm, QW), BF16),
                        pltpu.VMEM((3, hm, QW), BF16), pltpu.VMEM((hm, QW), F32),
                        pltpu.VMEM((D, D), F32), pltpu.VMEM((hm, D), F32), pltpu.VMEM((hm, D), F32), pltpu.VMEM((3, qr, D), BF16),
                        pltpu.VMEM((3, qr, D), BF16), pltpu.VMEM((qr, D), F32),
                        pltpu.VMEM(small.shape, F32),
                        pltpu.SemaphoreType.DMA((4,)), pltpu.SemaphoreType.DMA((4,)),
                        pltpu.SemaphoreType.DMA((3,)), pltpu.SemaphoreType.DMA((3,)),
                        pltpu.SemaphoreType.DMA, pltpu.SemaphoreType.DMA,
                        pltpu.SemaphoreType.DMA((3,)), pltpu.SemaphoreType.DMA((3,)),
                        pltpu.SemaphoreType.DMA((2,)), pltpu.SemaphoreType.DMA((2,)), pltpu.SemaphoreType.DMA((2,)),
                        pltpu.SemaphoreType.DMA((7,)), pltpu.SemaphoreType.DMA((7,)), pltpu.SemaphoreType.DMA])
    return pl.pallas_call(
        body, name=name, grid_spec=grid_spec,
        out_shape=[jax.ShapeDtypeStruct((2, qr, D), F32), jax.ShapeDtypeStruct((2, hm, QW), F32),
                   jax.ShapeDtypeStruct((8, *small.shape), F32)],
        compiler_params=pltpu.CompilerParams(dimension_semantics=("arbitrary",), vmem_limit_bytes=VMEM_LIMIT),
    )(pos, a_out, b_out, a_in, b_in, small)


def _chips(x, y):
    return [(1 - x, y), (x, 1 - y), (1 - x, 1 - y)]


def _gather_weights(parts, split, x, pre, next_parts, pool_w, w_s):
    n = len(parts)
    nn = len(next_parts)
    S = x.shape[0]
    tm = TM_FWD
    nt = S // tm
    staged = [a for a in range(n) if split[a]]

    def body(x_ref, pre_ref, *refs):
        raw, refs = refs[:n], refs[n:]
        side_hbm, hb_ref = refs[:nn + 2], refs[nn + 2]
        refs = refs[nn + 3:]
        outs, refs = refs[:n], refs[n:]
        next_outs, poolb_ref, tril_ref, trilt_ref = refs[:nn], refs[nn], refs[nn + 1], refs[nn + 2]
        refs = refs[nn + 3:]
        stage, refs = refs[:len(staged)], refs[len(staged):]
        side, (send_sems, recv_sems, local_sems, side_sems) = refs[:nn + 2], refs[nn + 2:]
        next_raw, pool_ref, ws_ref = side[:nn], side[nn], side[nn + 1]
        ins = [stage[staged.index(a)] if split[a] else raw[a] for a in range(n)]
        i = pl.program_id(0)

        def fetch(k):
            return pltpu.make_async_copy(side_hbm[k], side[k], side_sems.at[k])
        x, y, c = _position()
        me = 2 * x + y
        nbr_x, nbr_y, diag = _chips(x, y)
        id_x, id_y, id_d = (2 * chip[0] + chip[1] for chip in (nbr_x, nbr_y, diag))
        sibling = (x, y, 1 - c)

        def rows(a, ref, who, piece=None):
            r = parts[a].shape[0] // 2
            if piece is None:
                return ref.at[pl.ds(pl.multiple_of(who * r, 16), r), :]
            return ref.at[pl.ds(pl.multiple_of(who * r + piece * (r // 2), 16), r // 2), :]

        def copy(a, k, src, dst, to):
            return pltpu.make_async_remote_copy(src_ref=src, dst_ref=dst, send_sem=send_sems.at[a, k], recv_sem=recv_sems.at[a, k],
                                                device_id=to, device_id_type=MESH)

        def direct(a, k, chip):
            if split[a]:
                return copy(a, k, rows(a, ins[a], c), rows(a, outs[a].at[me], c), (*chip, c))
            return copy(a, k, ins[a], outs[a].at[me], (*chip, c))

        def arrival(a, k, src_id):
            if split[a]:
                return copy(a, k, rows(a, ins[a], c), rows(a, outs[a].at[src_id], c), (*nbr_x, c))
            return copy(a, k, ins[a], outs[a].at[src_id], (*nbr_x, c))

        def pass_on(a, k, src_id, piece, chip):
            region = rows(a, outs[a].at[src_id], c, piece)
            return copy(a, k, region, region, (*chip, c))

        def hand_over(a, k, src_id, who):
            region = rows(a, outs[a].at[src_id], who)
            return copy(a, k, region, region, sibling)

        def local(a):
            return pltpu.make_async_copy(ins[a], outs[a].at[me], local_sems.at[a])

        def first_sends(a):
            return [direct(a, 0, nbr_x), direct(a, 1, nbr_y)] + ([] if split[a] else [direct(a, 2, diag)])

        def after_x(a):
            return [pass_on(a, 3, id_x, 1, nbr_y), hand_over(a, 4, id_x, c)] if split[a] else []

        def after_y(a):
            return [pass_on(a, 2, id_y, 0, nbr_x), hand_over(a, 5, id_y, c)] if split[a] else []

        def after_diag(a):
            return [hand_over(a, 6, id_d, c)] if split[a] else []

        @pl.when(i == 0)
        def _():
            for a in staged:
                ins[a][...] = raw[a][...].astype(BF16)
            for a in range(n):
                local(a).start()
                for cp in first_sends(a):
                    cp.start()
            for k in range(nn + 2):
                fetch(k).start()

        @pl.when(i == 1)
        def _():
            for k in range(nn + 2):
                fetch(k).wait()
            for a in range(nn):
                next_outs[a][...] = next_raw[a][...].astype(BF16)
            poolb_ref[...] = pool_ref[...].astype(BF16)
            lower = lax.broadcasted_iota(jnp.int32, (CHUNK, CHUNK), 0) >= lax.broadcasted_iota(jnp.int32, (CHUNK, CHUNK), 1)
            for h in range(HEADS):
                tril = jnp.where(lower, ws_ref[h], 0.0)
                tril_ref[h] = tril.astype(BF16)
                trilt_ref[h] = tril.T.astype(BF16)

        xh, _ = _rms_fwd(x_ref[...])
        hb_ref[...] = (xh * pre_ref[...]).astype(BF16)

        @pl.when(i == nt - 2)
        def _():
            for a in range(n):
                arrival(a, 0, id_x).wait_recv()
                for cp in after_x(a):
                    cp.start()
            for a in range(n):
                arrival(a, 1, id_y).wait_recv()
                for cp in after_y(a):
                    cp.start()

        @pl.when(i == nt - 1)
        def _():
            for a in range(n):
                if split[a]:
                    pass_on(a, 2, id_d, 0, nbr_x).wait_recv()
                    pass_on(a, 3, id_d, 1, nbr_y).wait_recv()
                    for cp in after_diag(a):
                        cp.start()
                else:
                    arrival(a, 2, id_d).wait_recv()
            for a in range(n):
                if split[a]:
                    for k, src_id in ((4, id_x), (5, id_y), (6, id_d)):
                        hand_over(a, k, src_id, 1 - c).wait_recv()
            for a in range(n):
                for cp in first_sends(a) + after_x(a) + after_y(a) + after_diag(a):
                    cp.wait_send()
                local(a).wait()

    any_spec = pl.BlockSpec(memory_space=pl.ANY)
    vmem = pl.BlockSpec(memory_space=pltpu.VMEM)
    sent = [BF16 if split[a] else parts[a].dtype for a in range(n)]
    return pl.pallas_call(
        body, name="gather_weights", grid=(nt,),
        in_specs=[_rows(tm, D), _full((1, D))] + [vmem] * n + [any_spec] * (nn + 2),
        out_specs=[_rows(tm, D)] + [any_spec] * n + [vmem] * (nn + 3),
        out_shape=[jax.ShapeDtypeStruct((S, D), BF16)] + [jax.ShapeDtypeStruct((4, *p.shape), t) for p, t in zip(parts, sent)]
        + [jax.ShapeDtypeStruct(p.shape, BF16) for p in next_parts]
        + [jax.ShapeDtypeStruct(pool_w.shape, BF16), jax.ShapeDtypeStruct(w_s.shape, BF16), jax.ShapeDtypeStruct(w_s.shape, BF16)],
        scratch_shapes=[pltpu.VMEM(parts[a].shape, BF16) for a in staged]
        + [pltpu.VMEM(p.shape, F32) for p in (*next_parts, pool_w, w_s)]
        + [pltpu.SemaphoreType.DMA((n, 7)), pltpu.SemaphoreType.DMA((n, 7)), pltpu.SemaphoreType.DMA((n,)),
           pltpu.SemaphoreType.DMA((nn + 2,))],
        compiler_params=_params(),
    )(x, pre, *parts, *next_parts, pool_w, w_s)


def _adamw(w, g, m, v):
    m = ADAM_B1 * m + (1.0 - ADAM_B1) * g
    v = ADAM_B2 * v + (1.0 - ADAM_B2) * (g * g)
    m_hat = m / (1.0 - ADAM_B1 ** ADAM_STEP)
    v_hat = v / (1.0 - ADAM_B2 ** ADAM_STEP)
    delta = -ADAM_LR * (m_hat / (jnp.sqrt(v_hat) + ADAM_EPS) + ADAM_WD * w)
    return delta, m, v


def _adamw_big(ws, gs, ms, vs):
    steps = 4
    n = len(ws)

    def body(*refs):
        ins, outs = refs[:4 * n], refs[4 * n:]
        for a in range(n):
            w_ref, g_ref, m_ref, v_ref = ins[4 * a:4 * a + 4]
            go_ref, d_ref, mo_ref, vo_ref = outs[4 * a:4 * a + 4]
            gv = g_ref[...]
            go_ref[...] = gv
            d_ref[...], mo_ref[...], vo_ref[...] = _adamw(w_ref[...], gv, m_ref[...], v_ref[...])

    specs, shapes, operands = [], [], []
    for w, g, m, v in zip(ws, gs, ms, vs):
        rows, cols = w.shape
        specs += [pl.BlockSpec((rows // steps, cols), lambda i: (i, 0))] * 4
        shapes += [jax.ShapeDtypeStruct((rows, cols), F32)] * 4
        operands += [w, g, m, v]
    res = pl.pallas_call(
        body, name="adamw_big", grid=(steps,),
        in_specs=specs, out_specs=specs, out_shape=shapes,
        compiler_params=pltpu.CompilerParams(dimension_semantics=("arbitrary",), vmem_limit_bytes=VMEM_LIMIT),
    )(*operands)
    return [res[4 * a:4 * a + 4] for a in range(n)]


def _adamw_small(g0, g1, weights, moms, vels):
    names = ["pre", "post", "conv", "pw", "ps", "lng", "lnb", "ws", "bs"]
    shapes = [w.shape for w in weights]

    def body(*refs):
        me = 2 * lax.axis_index("x") + lax.axis_index("y")
        g0_ref, g1_ref = refs[0], refs[1]
        w_refs, m_refs, v_refs = refs[2:11], refs[11:20], refs[20:29]
        outs = refs[29:29 + 36]
        loss_ref = refs[65]
        t0_ref, t1_ref = refs[66], refs[67]
        t0 = g0_ref[0]
        t1 = g1_ref[0]
        for d in range(1, 8):
            t0 = t0 + g0_ref[d]
            t1 = t1 + g1_ref[d]
        t0_ref[...] = t0
        t1_ref[...] = t1
        loss_ref[...] = t1_ref[S1_LOSS:S1_LOSS + 1, 0:1]
        my_conv = pl.multiple_of(S0_CONV + 8 * me, 8)
        my_ln = pl.multiple_of(S1_LN + 8 * me, 8)

        def update(idx, piece, grad):
            go, do, mo, vo = outs[4 * idx:4 * idx + 4]
            go[piece] = grad
            do[piece], mo[piece], vo[piece] = _adamw(w_refs[idx][piece], grad, m_refs[idx][piece], v_refs[idx][piece])

        for layer in range(2):
            for k in range(D // GD):
                lanes = slice(k * GD, (k + 1) * GD)
                tref, pre0, post0 = (t0_ref, S0_PRE, S0_POST) if layer == 0 else (t1_ref, S1_PRE, S1_POST)
                update(0, (slice(layer, layer + 1), lanes), tref[pre0 + k:pre0 + k + 1, :])
                update(1, (slice(layer, layer + 1), lanes), tref[post0 + k:post0 + k + 1, :])
        conv_rows = t0_ref[pl.ds(my_conv, 8), :]
        update(2, (slice(0, 3), 0, slice(None)), conv_rows[0:3, :])
        for g in range(4):
            update(3, (g,), t0_ref[S0_PW + g * GD:S0_PW + (g + 1) * GD, :])
            update(4, (slice(0, 1), slice(g * GD, (g + 1) * GD)), t0_ref[S0_PS + g:S0_PS + g + 1, :])
        ln_rows = t1_ref[pl.ds(my_ln, 8), :]
        for k in range(2):
            update(5, (slice(0, 1), slice(k * GD, (k + 1) * GD)), ln_rows[k:k + 1, :])
            update(6, (slice(0, 1), slice(k * GD, (k + 1) * GD)), ln_rows[2 + k:3 + k, :])
        for h in range(HEADS):
            update(7, (h,), t1_ref[S1_WS + h * CHUNK:S1_WS + (h + 1) * CHUNK, :])
        update(8, (slice(None), slice(None)), t1_ref[S1_BS:S1_BS + HEADS, :])

    vm = pl.BlockSpec(memory_space=pltpu.VMEM)
    out_shape = []
    for s in shapes:
        out_shape += [jax.ShapeDtypeStruct(s, F32)] * 4
    out_shape.append(jax.ShapeDtypeStruct((1, 1), F32))
    res = pl.pallas_call(
        body, name="adamw_small",
        in_specs=[vm] * 29, out_specs=[vm] * 37, out_shape=out_shape,
        scratch_shapes=[pltpu.VMEM((S0_ROWS, GD), F32), pltpu.VMEM((S1_ROWS, GD), F32)],
        compiler_params=pltpu.CompilerParams(vmem_limit_bytes=VMEM_LIMIT),
    )(g0, g1, *weights, *moms, *vels)
    per_weight = {nm: res[4 * i:4 * i + 4] for i, nm in enumerate(names)}
    return per_weight, res[36]


def _pad8(a):
    return jnp.pad(a, ((0, 8 - a.shape[0]), (0, 0)))


def kernel(x, pre_norm, post_norm, even_w_in, even_conv_w, even_pool_w, even_pool_scale, even_w_out, odd_w_in, odd_ln_g, odd_ln_b, odd_w_s, odd_b_s, odd_w_out, loss_target, m_pre_norm, m_post_norm, m_even_w_in, m_even_conv_w, m_even_pool_w, m_even_pool_scale, m_even_w_out, m_odd_w_in, m_odd_ln_g, m_odd_ln_b, m_odd_w_s, m_odd_b_s, m_odd_w_out, v_pre_norm, v_post_norm, v_even_w_in, v_even_conv_w, v_even_pool_w, v_even_pool_scale, v_even_w_out, v_odd_w_in, v_odd_ln_g, v_odd_ln_b, v_odd_w_s, v_odd_b_s, v_odd_w_out):
    xs = x[0]
    tgt = loss_target[0]

    small_shard = jnp.concatenate([_pad8(even_conv_w[0]), _pad8(odd_ln_g.reshape(2, GD)), _pad8(odd_ln_b.reshape(2, GD))], axis=0)
    pre0, pre1 = pre_norm[0:1], pre_norm[1:2]
    post0, post1 = post_norm[0:1], post_norm[1:2]
    hb0, win0, wout0, shard, win1_shard, wout1_shard, pool_wb, ws_tril, ws_tril_t = _gather_weights(
        [even_w_in[0], even_w_out[0], small_shard], [True, True, False], xs, pre0, [odd_w_in[0], odd_w_out[0]],
        even_pool_w[0], odd_w_s[0])
    wout0 = wout0.reshape(D, D)
    px, py, pc = _position()
    pos = jnp.stack([pc, 2 * px + py]).astype(jnp.int32)
    conv_w = shard[:, 0:3, :].transpose(1, 0, 2).reshape(3, AW)
    ln_g = shard[:, 8:10, :].reshape(1, D)
    ln_b = shard[:, 16:18, :].reshape(1, D)
    bias = jnp.broadcast_to(odd_b_s[0][:, :, None], (HEADS, CHUNK, GD))

    x1, proj0, m0, mixp0, conv0, mixed0, pooled0, win1, wout1 = _even_fwd(
        xs, hb0, post0, win0, conv_w, pool_wb, even_pool_scale, wout0, [win1_shard, wout1_shard])
    wout1 = wout1.reshape(D, D)
    proj1, m1, hb1, yp1, dx2, loss_part = _odd_fwd(x1, tgt, pre1, post1, win1, ln_g, ln_b, ws_tril, bias, wout1)
    dx1, dproj1, dmb1, small1 = _odd_bwd(dx2, x1, proj1, m1, loss_part, pre1, post1, win1, ln_g, ln_b, ws_tril, ws_tril_t, bias, wout1)
    g_out1, g_in1, all1 = _wgrad_layer(yp1, dmb1, hb1, dproj1, small1, pos, "wgrad_odd")
    gx, dproj0, dmb0, small0 = _even_bwd(dx1, xs, proj0, conv0, mixed0, pooled0, m0, pre0, post0, win0, conv_w, pool_wb,
                                         even_pool_scale, wout0)
    g_out0, g_in0, all0 = _wgrad_layer(mixp0, dmb0, hb0, dproj0, small0, pos, "wgrad_even")

    big_w = [even_w_in[0], even_w_out[0], odd_w_in[0], odd_w_out[0]]
    big_g = [g.reshape(w.shape) for g, w in zip([g_in0, g_out0, g_in1, g_out1], big_w)]
    big_m = [m_even_w_in[0], m_even_w_out[0], m_odd_w_in[0], m_odd_w_out[0]]
    big_v = [v_even_w_in[0], v_even_w_out[0], v_odd_w_in[0], v_odd_w_out[0]]
    big = _adamw_big(big_w, big_g, big_m, big_v)

    def taps_first(a):
        return jnp.swapaxes(a, 0, 1)

    small_w = [pre_norm, post_norm, taps_first(even_conv_w), even_pool_w[0], even_pool_scale, odd_ln_g, odd_ln_b, odd_w_s[0], odd_b_s[0]]
    small_m = [m_pre_norm, m_post_norm, taps_first(m_even_conv_w), m_even_pool_w[0], m_even_pool_scale, m_odd_ln_g, m_odd_ln_b,
               m_odd_w_s[0], m_odd_b_s[0]]
    small_v = [v_pre_norm, v_post_norm, taps_first(v_even_conv_w), v_even_pool_w[0], v_even_pool_scale, v_odd_ln_g, v_odd_ln_b,
               v_odd_w_s[0], v_odd_b_s[0]]
    sm, loss = _adamw_small(all0, all1, small_w, small_m, small_v)

    def lead(a):
        return a[None]

    per = {
        "pre_norm": sm["pre"], "post_norm": sm["post"],
        "even_w_in": [lead(a) for a in big[0]], "even_conv_w": [taps_first(a) for a in sm["conv"]],
        "even_pool_w": [lead(a) for a in sm["pw"]], "even_pool_scale": sm["ps"],
        "even_w_out": [lead(a) for a in big[1]], "odd_w_in": [lead(a) for a in big[2]],
        "odd_ln_g": sm["lng"], "odd_ln_b": sm["lnb"],
        "odd_w_s": [lead(a) for a in sm["ws"]], "odd_b_s": [lead(a) for a in sm["bs"]],
        "odd_w_out": [lead(a) for a in big[3]],
    }
    order = ["pre_norm", "post_norm", "even_w_in", "even_conv_w", "even_pool_w", "even_pool_scale", "even_w_out", "odd_w_in",
             "odd_ln_g", "odd_ln_b", "odd_w_s", "odd_b_s", "odd_w_out"]
    outs = [loss.reshape(()), gx[None]]
    for kind in range(4):
        outs += [per[nm][kind] for nm in order]
    return tuple(outs)
```

```python
import jax
import jax.numpy as jnp
from jax import lax
from jax.experimental import pallas as pl
from jax.experimental.pallas import tpu as pltpu

F32 = jnp.float32
BF16 = jnp.bfloat16
MESH = pl.DeviceIdType.MESH

D = 1024
W3 = 3 * D
QW = W3 // 4
AW = 512
GD = 128
CHUNK = 128
HEADS = 8
HALO = 16
POOL_WINDOWS = (2, 4, 8, 16)
EPS = 1e-6
TM_FWD = 512
TM_BWD = 256
TK = 1024
TK_IN = 2048
VMEM_LIMIT = 56 * 1024 * 1024

ADAM_LR, ADAM_B1, ADAM_B2, ADAM_EPS, ADAM_WD, ADAM_STEP = 0.001, 0.9, 0.999, 1e-08, 0.01, 10

S0_PRE, S0_POST, S0_CONV, S0_PS, S0_PW, S0_ROWS = 0, 8, 16, 48, 56, 568
S1_PRE, S1_POST, S1_LN, S1_BS, S1_WS, S1_LOSS, S1_ROWS = 0, 8, 16, 48, 56, 1080, 1088


def _nn(a, b):
    return jnp.dot(a, b, preferred_element_type=F32)


def _nt(a, b):
    return lax.dot_general(a, b, (((1,), (1,)), ((), ())), preferred_element_type=F32)


def _tn(a, b):
    return lax.dot_general(a, b, (((0,), (0,)), ((), ())), preferred_element_type=F32)


def _sigmoid(z):
    return 1.0 / (1.0 + jnp.exp(-z))


def _rms_fwd(x):
    r = lax.rsqrt(jnp.mean(x * x, axis=-1, keepdims=True) + EPS)
    return x * r, r


def _rms_bwd(dy, xh, r, g):
    dn = dy * g
    dx = r * (dn - xh * jnp.mean(xh * dn, axis=-1, keepdims=True))
    return dx, jnp.sum(dy * xh, axis=0, keepdims=True)


def _full(shape):
    nd = len(shape)
    return pl.BlockSpec(shape, lambda i, _n=nd: (0,) * _n, pipeline_mode=pl.Buffered(1))


def _full_out(shape):
    nd = len(shape)
    return pl.BlockSpec(shape, lambda i, _n=nd: (0,) * _n)


def _rows(tm, width, index=None):
    return pl.BlockSpec((tm, width), (lambda i: (i, 0)) if index is None else index)


def _params():
    return pltpu.CompilerParams(dimension_semantics=("arbitrary",), vmem_limit_bytes=VMEM_LIMIT)


def _position():
    x, y, c = lax.axis_index("x"), lax.axis_index("y"), lax.axis_index("c")
    return x, y, c


SHARD_ROWS = 24


def _shard_row(shard_ref, row):
    return jnp.concatenate([shard_ref[q, row:row + 1, :] for q in range(4)], axis=1)


def _shard_vector(shard_ref, first_row):
    return jnp.concatenate([shard_ref[q, first_row + k:first_row + k + 1, :] for q in range(4) for k in range(2)], axis=1)


def _even_mix(proj_ref, hc_ext, xp_ext, taps, pw_ref, ps_ref, first_row):
    tm = proj_ref.shape[0]
    xa = proj_ref[:, 0:AW]
    gb = proj_ref[:, AW:2 * AW]
    gc = proj_ref[:, 2 * AW:3 * AW]
    za = proj_ref[:, 3 * AW:4 * AW]
    xp = proj_ref[:, 4 * AW:5 * AW]
    zp = proj_ref[:, 5 * AW:6 * AW]
    hc = gc * xa
    hc_ext[HALO:, :] = hc
    e = hc_ext[...]
    conv = taps[2] * hc + taps[1] * pltpu.roll(e, 1, 0)[HALO:] + taps[0] * pltpu.roll(e, 2, 0)[HALO:]
    sa = _sigmoid(za)
    xp_ext[HALO:, :] = xp
    pos = first_row + lax.broadcasted_iota(jnp.int32, (tm, 1), 0)
    pooled, mixed, counts = [], [], []
    for g, w in enumerate(POOL_WINDOWS):
        cols = slice(g * GD, (g + 1) * GD)
        s = xp_ext[:, cols]
        for k in range(g + 1):
            s = s + pltpu.roll(s, 2 ** k, 0)
        count = jnp.minimum(pos + 1, w).astype(F32)
        pg = s[HALO:] / count - xp[:, cols]
        pooled.append(pg.astype(BF16))
        mixed.append(_nn(pooled[-1], pw_ref[g]))
        counts.append(count)
    mixed = jnp.concatenate(mixed, axis=-1)
    sb = _sigmoid(zp)
    return dict(xa=xa, gb=gb, gc=gc, za=za, zp=zp, hc=hc, conv=conv, sa=sa, sb=sb, pooled=pooled, mixed=mixed, counts=counts)


def _half_rows(ref, rows, who):
    return ref.at[pl.ds(pl.multiple_of(who * (rows // 2), 8), rows // 2), :]


def _store_permuted(ref, value):
    for ob in range(D // GD):
        nb = 4 * (ob % 2) + ob // 2
        ref[:, nb * GD:(nb + 1) * GD] = value[:, ob * GD:(ob + 1) * GD]


def _even_fwd(x, hb, post, win, shard, pwb, ps, wout, next_shards):
    S = x.shape[0]
    tm = TM_FWD
    nt = S // tm
    relay = (3 * nt) // 4
    n = len(next_shards)
    shard_rows = [p.shape[0] for p in next_shards]

    def body(x_ref, hb_ref, post_ref, win_ref, shard_ref, pw_ref, ps_ref, wout_ref, *rest):
        shard_refs, rest = rest[:n], rest[n:]
        x1_ref, proj_ref, m_ref, mixp_ref, conv_ref, mixed_ref, pooled_ref = rest[:7]
        full_refs, rest = rest[7:7 + n], rest[7 + n:]
        hc_ext, xp_ext, mix_sc = rest[:3]
        stage, rest = rest[3:3 + n], rest[3 + n:]
        send_sems, recv_sems, local_sems = rest
        i = pl.program_id(0)
        px, py, pc = _position()
        me = 2 * px + py
        chips = _chips(px, py)

        def ici(a, j):
            return pltpu.make_async_remote_copy(
                src_ref=_half_rows(shard_refs[a], shard_rows[a], pc), dst_ref=_half_rows(full_refs[a].at[me], shard_rows[a], pc),
                send_sem=send_sems.at[a, j], recv_sem=recv_sems.at[a, j], device_id=(*chips[j], pc), device_id_type=MESH)

        def ici_arrival(a, j):
            src = 2 * chips[j][0] + chips[j][1]
            return pltpu.make_async_remote_copy(
                src_ref=_half_rows(shard_refs[a], shard_rows[a], pc), dst_ref=_half_rows(full_refs[a].at[src], shard_rows[a], pc),
                send_sem=send_sems.at[a, j], recv_sem=recv_sems.at[a, j], device_id=(*chips[j], pc), device_id_type=MESH)

        def relay_copy(a, j, who):
            src = 2 * chips[j][0] + chips[j][1]
            region = _half_rows(full_refs[a].at[src], shard_rows[a], who)
            return pltpu.make_async_remote_copy(
                src_ref=region, dst_ref=region, send_sem=send_sems.at[a, 3 + j], recv_sem=recv_sems.at[a, 3 + j],
                device_id=(px, py, 1 - pc), device_id_type=MESH)

        def own_copy(a):
            return pltpu.make_async_copy(stage[a], full_refs[a].at[me], local_sems.at[a])

        @pl.when(i == 0)
        def _():
            hc_ext[0:HALO, :] = jnp.zeros((HALO, AW), F32)
            xp_ext[0:HALO, :] = jnp.zeros((HALO, AW), F32)
            for a in range(n):
                for j in range(3):
                    ici(a, j).start()
            for a in range(n):
                load = pltpu.make_async_copy(shard_refs[a], stage[a], local_sems.at[a])
                load.start()
                load.wait()
                own_copy(a).start()

        @pl.when(i == relay)
        def _():
            for j in range(3):
                for a in range(n):
                    ici_arrival(a, j).wait_recv()
                    relay_copy(a, j, pc).start()

        hb = hb_ref[...]
        for q in range(4):
            proj_ref[:, q * QW:(q + 1) * QW] = _nn(hb, win_ref[q])
        t = _even_mix(proj_ref, hc_ext, xp_ext, [_shard_row(shard_ref, k) for k in range(3)], pw_ref, ps_ref, i * tm)
        conv_ref[...] = t["conv"]
        mixed_ref[...] = t["mixed"]
        for g in range(4):
            pooled_ref[:, g * GD:(g + 1) * GD] = t["pooled"][g]
        mix_sc[:, 0:AW] = (t["gb"] * t["conv"] * (t["za"] * t["sa"])).astype(BF16)
        mix_sc[:, AW:2 * AW] = (t["mixed"] * ps_ref[...] * (t["zp"] * t["sb"])).astype(BF16)
        mix = mix_sc[...]
        _store_permuted(mixp_ref, mix)
        m = _nn(mix, wout_ref[...])
        m_ref[...] = m
        mh, _ = _rms_fwd(m)
        x1_ref[...] = x_ref[...] + mh * post_ref[...]
        hc_ext[0:HALO, :] = hc_ext[tm:tm + HALO, :]
        xp_ext[0:HALO, :] = xp_ext[tm:tm + HALO, :]

        @pl.when(i == nt - 1)
        def _():
            for j in range(3):
                for a in range(n):
                    relay_copy(a, j, 1 - pc).wait_recv()
            for a in range(n):
                for j in range(3):
                    ici(a, j).wait_send()
                    relay_copy(a, j, pc).wait_send()
                own_copy(a).wait()

    any_spec = pl.BlockSpec(memory_space=pl.ANY)
    return pl.pallas_call(
        body, name="even_fwd", grid=(nt,),
        in_specs=[_rows(tm, D), _rows(tm, D), _full((1, D)), _full((4, D, QW)), _full((4, SHARD_ROWS, GD)), _full((4, GD, GD)),
                  _full((1, AW)), _full((D, D))] + [any_spec] * n,
        out_specs=[_rows(tm, D), _rows(tm, W3), _rows(tm, D), _rows(tm, D), _rows(tm, AW), _rows(tm, AW), _rows(tm, AW)]
        + [any_spec] * n,
        out_shape=[jax.ShapeDtypeStruct((S, D), F32), jax.ShapeDtypeStruct((S, W3), F32), jax.ShapeDtypeStruct((S, D), F32),
                   jax.ShapeDtypeStruct((S, D), BF16),
                   jax.ShapeDtypeStruct((S, AW), F32), jax.ShapeDtypeStruct((S, AW), F32), jax.ShapeDtypeStruct((S, AW), BF16)]
        + [jax.ShapeDtypeStruct((4, *p.shape), p.dtype) for p in next_shards],
        scratch_shapes=[pltpu.VMEM((tm + HALO, AW), F32), pltpu.VMEM((tm + HALO, AW), F32), pltpu.VMEM((tm, D), BF16)]
        + [pltpu.VMEM(p.shape, p.dtype) for p in next_shards]
        + [pltpu.SemaphoreType.DMA((n, 6)), pltpu.SemaphoreType.DMA((n, 6)), pltpu.SemaphoreType.DMA((n,))],
        compiler_params=_params(),
    )(x, hb, post, win, shard, pwb, ps, wout, *next_shards)


def _chunks_side_by_side(a, h):
    return jnp.concatenate([a[n * CHUNK:(n + 1) * CHUNK, h * GD:(h + 1) * GD] for n in range(a.shape[0] // CHUNK)], axis=1)


def _odd_mix(proj_ref, lng, lnb, ws_ref, bias_ref, sv_ref):
    tm = proj_ref.shape[0]
    u = proj_ref[:, 0:D]
    v = proj_ref[:, D:2 * D]
    z = proj_ref[:, 2 * D:3 * D]
    mu = jnp.mean(v, axis=-1, keepdims=True)
    vc = v - mu
    rs = lax.rsqrt(jnp.mean(vc * vc, axis=-1, keepdims=True) + EPS)
    vh = vc * rs
    vnb = (vh * lng + lnb).astype(BF16)
    for h in range(HEADS):
        sv = _nn(ws_ref[h], _chunks_side_by_side(vnb, h))
        for n in range(tm // CHUNK):
            sv_ref[n * CHUNK:(n + 1) * CHUNK, h * GD:(h + 1) * GD] = sv[:, n * GD:(n + 1) * GD] + bias_ref[h]
    return dict(u=u, z=z, vh=vh, rs=rs, vnb=vnb, sz=_sigmoid(z))


def _odd_fwd(x1, tgt, pre, post, win, shard, wsb, bias, wout):
    S = x1.shape[0]
    tm = TM_FWD
    nt = S // tm

    def body(x_ref, tgt_ref, pre_ref, post_ref, win_ref, shard_ref, ws_ref, bias_ref, wout_ref,
             proj_ref, m_ref, hb_ref, yb_ref, dx2_ref, loss_ref, sv_ref):
        i = pl.program_id(0)

        @pl.when(i == 0)
        def _():
            loss_ref[...] = jnp.zeros((8, GD), F32)

        xv = x_ref[...]
        xh, _ = _rms_fwd(xv)
        hb = (xh * pre_ref[...]).astype(BF16)
        hb_ref[...] = hb
        for q in range(4):
            proj_ref[:, q * QW:(q + 1) * QW] = _nn(hb, win_ref[q])
        t = _odd_mix(proj_ref, _shard_vector(shard_ref, 8), _shard_vector(shard_ref, 16), ws_ref, bias_ref, sv_ref)
        yb = (t["u"] * sv_ref[...] * (t["z"] * t["sz"])).astype(BF16)
        _store_permuted(yb_ref, yb)
        m = _nn(yb, wout_ref[...])
        m_ref[...] = m
        mh, _ = _rms_fwd(m)
        err = xv + mh * post_ref[...] - tgt_ref[...]
        dx2_ref[...] = err * (1.0 / D)
        part = 0.5 * jnp.sum(jnp.mean(err * err, axis=-1, keepdims=True), axis=0, keepdims=True)
        loss_ref[...] += jnp.broadcast_to(part, (8, GD))

    return pl.pallas_call(
        body, name="odd_fwd", grid=(nt,),
        in_specs=[_rows(tm, D), _rows(tm, D), _full((1, D)), _full((1, D)), _full((4, D, QW)), _full((4, SHARD_ROWS, GD)),
                  _full((HEADS, CHUNK, CHUNK)), _full((HEADS, CHUNK, GD)), _full((D, D))],
        out_specs=[_rows(tm, W3), _rows(tm, D), _rows(tm, D), _rows(tm, D), _rows(tm, D), _full_out((8, GD))],
        out_shape=[jax.ShapeDtypeStruct((S, W3), F32), jax.ShapeDtypeStruct((S, D), F32), jax.ShapeDtypeStruct((S, D), BF16),
                   jax.ShapeDtypeStruct((S, D), BF16), jax.ShapeDtypeStruct((S, D), F32), jax.ShapeDtypeStruct((8, GD), F32)],
        scratch_shapes=[pltpu.VMEM((tm, D), F32)],
        compiler_params=_params(),
    )(x1, tgt, pre, post, win, shard, wsb, bias, wout)


def _store_rows(ref, row0, value):
    r, width = value.shape
    for a in range(r):
        for k in range(width // GD):
            ref[row0 + a * (width // GD) + k:row0 + a * (width // GD) + k + 1, :] = value[a:a + 1, k * GD:(k + 1) * GD]


def _proj_bwd(dproj, win_ref, x, dy, pre):
    dh = _nt(dproj[:, 0:QW], win_ref[0])
    for q in range(1, 4):
        dh += _nt(dproj[:, q * QW:(q + 1) * QW], win_ref[q])
    xh, r = _rms_fwd(x)
    dxn, dpre = _rms_bwd(dh, xh, r, pre)
    return dy + dxn, dpre


def _odd_bwd(dx2, x1, proj, m, loss, pre, post, win, shard, wsb, wsbt, bias, wout):
    S = x1.shape[0]
    tm = TM_BWD
    nt = S // tm

    def body(dy_ref, x_ref, proj_ref, m_ref, loss_ref, pre_ref, post_ref, win_ref, shard_ref, ws_ref, wst_ref, bias_ref,
             wout_ref, dx_ref, dproj_ref, dmb_ref, small_ref, sv_ref, dvn_ref, acc1024, dws_acc, dbs_acc):
        i = pl.program_id(0)
        lng = _shard_vector(shard_ref, 8)

        @pl.when(i == 0)
        def _():
            acc1024[...] = jnp.zeros_like(acc1024)
            dws_acc[...] = jnp.zeros_like(dws_acc)
            dbs_acc[...] = jnp.zeros_like(dbs_acc)

        dy = dy_ref[...]
        mh, rm = _rms_fwd(m_ref[...])
        dm, dpost = _rms_bwd(dy, mh, rm, post_ref[...])
        dmb = dm.astype(BF16)
        dmb_ref[...] = dmb
        dyv = _nt(dmb, wout_ref[...])
        t = _odd_mix(proj_ref, lng, _shard_vector(shard_ref, 16), ws_ref, bias_ref, sv_ref)
        u, z, sz, sv = t["u"], t["z"], t["sz"], sv_ref[...]
        dproj_ref[:, 0:D] = (dyv * sv * (z * sz)).astype(BF16)
        dproj_ref[:, 2 * D:3 * D] = (dyv * u * sv * (sz * (1.0 + z * (1.0 - sz)))).astype(BF16)
        dsv = dyv * u * (z * sz)
        dsvb = dsv.astype(BF16)
        for h in range(HEADS):
            dsv_h = _chunks_side_by_side(dsvb, h)
            dvn_h = _nn(wst_ref[h], dsv_h)
            dws_acc[h] += _nt(dsv_h, _chunks_side_by_side(t["vnb"], h))
            for n in range(tm // CHUNK):
                rows, cols = slice(n * CHUNK, (n + 1) * CHUNK), slice(h * GD, (h + 1) * GD)
                dvn_ref[rows, cols] = dvn_h[:, n * GD:(n + 1) * GD]
                dbs_acc[h] += dsv[rows, cols]
        dvn = dvn_ref[...]
        vh = t["vh"]
        dvh = dvn * lng
        dv = t["rs"] * (dvh - jnp.mean(dvh, axis=-1, keepdims=True) - vh * jnp.mean(dvh * vh, axis=-1, keepdims=True))
        dproj_ref[:, D:2 * D] = dv.astype(BF16)
        dx_ref[...], dpre = _proj_bwd(dproj_ref[...], win_ref, x_ref[...], dy, pre_ref[...])
        acc1024[0:1, :] += dpre
        acc1024[1:2, :] += dpost
        acc1024[2:3, :] += jnp.sum(dvn * vh, axis=0, keepdims=True)
        acc1024[3:4, :] += jnp.sum(dvn, axis=0, keepdims=True)

        @pl.when(i == nt - 1)
        def _():
            small_ref[...] = jnp.zeros_like(small_ref)
            _store_rows(small_ref, S1_PRE, acc1024[0:1, :])
            _store_rows(small_ref, S1_POST, acc1024[1:2, :])
            for q in range(4):
                _store_rows(small_ref, S1_LN + 8 * q, acc1024[2:3, 2 * q * GD:(2 * q + 2) * GD])
                _store_rows(small_ref, S1_LN + 8 * q + 2, acc1024[3:4, 2 * q * GD:(2 * q + 2) * GD])
            lower = lax.broadcasted_iota(jnp.int32, (CHUNK, CHUNK), 0) >= lax.broadcasted_iota(jnp.int32, (CHUNK, CHUNK), 1)
            for h in range(HEADS):
                small_ref[S1_WS + h * CHUNK:S1_WS + (h + 1) * CHUNK, :] = jnp.where(lower, dws_acc[h], 0.0)
                small_ref[S1_BS + h:S1_BS + h + 1, :] = jnp.sum(dbs_acc[h].T, axis=0, keepdims=True)
            small_ref[S1_LOSS:S1_LOSS + 8, :] = loss_ref[...]

    return pl.pallas_call(
        body, name="odd_bwd", grid=(nt,),
        in_specs=[_rows(tm, D), _rows(tm, D), _rows(tm, W3), _rows(tm, D), _full((8, GD)), _full((1, D)), _full((1, D)),
                  _full((4, D, QW)), _full((4, SHARD_ROWS, GD)), _full((HEADS, CHUNK, CHUNK)), _full((HEADS, CHUNK, CHUNK)),
                  _full((HEADS, CHUNK, GD)), _full((D, D))],
        out_specs=[_rows(tm, D), _rows(tm, W3), _rows(tm, D), _full_out((S1_ROWS, GD))],
        out_shape=[jax.ShapeDtypeStruct((S, D), F32), jax.ShapeDtypeStruct((S, W3), BF16), jax.ShapeDtypeStruct((S, D), BF16),
                   jax.ShapeDtypeStruct((S1_ROWS, GD), F32)],
        scratch_shapes=[pltpu.VMEM((tm, D), F32), pltpu.VMEM((tm, D), F32), pltpu.VMEM((8, D), F32),
                        pltpu.VMEM((HEADS, CHUNK, CHUNK), F32), pltpu.VMEM((HEADS, CHUNK, GD), F32)],
        compiler_params=_params(),
    )(dx2, x1, proj, m, loss, pre, post, win, shard, wsb, wsbt, bias, wout)


def _even_bwd(dx1, x, proj, conv, mixed, pooled, m, pre, post, win, shard, pwb, ps, wout):
    S = x.shape[0]
    tm = TM_BWD
    nt = S // tm
    L = tm + HALO

    def rev(i):
        return (nt - 1 - i, 0)

    def body(dy_ref, x_ref, proj_ref, conv_ref, mixed_ref, pooled_ref, m_ref, pre_ref, post_ref, win_ref, shard_ref, pw_ref, ps_ref,
             wout_ref, dx_ref, dproj_ref, dmb_ref, small_ref, dconv_ext, q_ext, acc1024, acc512, dpw_acc):
        i = pl.program_id(0)
        tile = nt - 1 - i

        @pl.when(i == 0)
        def _():
            dconv_ext[tm:L, :] = jnp.zeros((HALO, AW), F32)
            q_ext[tm:L, :] = jnp.zeros((HALO, AW), F32)
            acc1024[...] = jnp.zeros_like(acc1024)
            acc512[...] = jnp.zeros_like(acc512)
            dpw_acc[...] = jnp.zeros_like(dpw_acc)

        dy = dy_ref[...]
        mh, rm = _rms_fwd(m_ref[...])
        dm, dpost = _rms_bwd(dy, mh, rm, post_ref[...])
        dmb = dm.astype(BF16)
        dmb_ref[...] = dmb
        dmix = _nt(dmb, wout_ref[...])
        dya, dyb = dmix[:, 0:AW], dmix[:, AW:2 * AW]
        xa, gb, gc, za = (proj_ref[:, k * AW:(k + 1) * AW] for k in range(4))
        zp = proj_ref[:, 5 * AW:6 * AW]
        hc = gc * xa
        conv = conv_ref[...]
        sa = _sigmoid(za)
        silu_a = za * sa
        dproj_ref[:, AW:2 * AW] = (dya * conv * silu_a).astype(BF16)
        dproj_ref[:, 3 * AW:4 * AW] = (dya * gb * conv * (sa * (1.0 + za * (1.0 - sa)))).astype(BF16)
        dconv = dya * gb * silu_a
        dconv_ext[0:tm, :] = dconv
        e = dconv_ext[...]
        dc1 = pltpu.roll(e, L - 1, 0)[0:tm]
        dc2 = pltpu.roll(e, L - 2, 0)[0:tm]
        taps = [_shard_row(shard_ref, k) for k in range(3)]
        dhc = taps[2] * dconv + taps[1] * dc1 + taps[0] * dc2
        dproj_ref[:, 0:AW] = (dhc * gc).astype(BF16)
        dproj_ref[:, 2 * AW:3 * AW] = (dhc * xa).astype(BF16)
        acc512[0:1, :] += jnp.sum(dc2 * hc, axis=0, keepdims=True)
        acc512[1:2, :] += jnp.sum(dc1 * hc, axis=0, keepdims=True)
        acc512[2:3, :] += jnp.sum(dconv * hc, axis=0, keepdims=True)

        sb, mixed = _sigmoid(zp), mixed_ref[...]
        silu_b = zp * sb
        acc512[3:4, :] += jnp.sum(dyb * mixed * silu_b, axis=0, keepdims=True)
        dmixedb = (dyb * ps_ref[...] * silu_b).astype(BF16)
        dproj_ref[:, 5 * AW:6 * AW] = (dyb * mixed * ps_ref[...] * (sb * (1.0 + zp * (1.0 - sb)))).astype(BF16)
        pos = tile * tm + lax.broadcasted_iota(jnp.int32, (tm, 1), 0)
        for g, w in enumerate(POOL_WINDOWS):
            cols = slice(g * GD, (g + 1) * GD)
            dpw_acc[g] += _tn(pooled_ref[:, cols], dmixedb[:, cols])
            dpooled = _nt(dmixedb[:, cols], pw_ref[g])
            q_ext[0:tm, cols] = dpooled / jnp.minimum(pos + 1, w).astype(F32)
            s = q_ext[:, cols]
            for k in range(g + 1):
                s = s + pltpu.roll(s, L - 2 ** k, 0)
            dproj_ref[:, 4 * AW + g * GD:4 * AW + (g + 1) * GD] = (s[0:tm] - dpooled).astype(BF16)
        dconv_ext[tm:L, :] = dconv_ext[0:HALO, :]
        q_ext[tm:L, :] = q_ext[0:HALO, :]

        dx_ref[...], dpre = _proj_bwd(dproj_ref[...], win_ref, x_ref[...], dy, pre_ref[...])
        acc1024[0:1, :] += dpre
        acc1024[1:2, :] += dpost

        @pl.when(i == nt - 1)
        def _():
            small_ref[...] = jnp.zeros_like(small_ref)
            _store_rows(small_ref, S0_PRE, acc1024[0:1, :])
            _store_rows(small_ref, S0_POST, acc1024[1:2, :])
            for q in range(4):
                for k in range(3):
                    small_ref[S0_CONV + 8 * q + k:S0_CONV + 8 * q + k + 1, :] = acc512[k:k + 1, q * GD:(q + 1) * GD]
            _store_rows(small_ref, S0_PS, acc512[3:4, :])
            for g in range(4):
                small_ref[S0_PW + g * GD:S0_PW + (g + 1) * GD, :] = dpw_acc[g]

    return pl.pallas_call(
        body, name="even_bwd", grid=(nt,),
        in_specs=[_rows(tm, D, rev), _rows(tm, D, rev), _rows(tm, W3, rev), _rows(tm, AW, rev), _rows(tm, AW, rev), _rows(tm, AW, rev),
                  _rows(tm, D, rev),
                  _full((1, D)), _full((1, D)), _full((4, D, QW)), _full((4, SHARD_ROWS, GD)), _full((4, GD, GD)), _full((1, AW)),
                  _full((D, D))],
        out_specs=[_rows(tm, D, rev), _rows(tm, W3, rev), _rows(tm, D, rev), _full_out((S0_ROWS, GD))],
        out_shape=[jax.ShapeDtypeStruct((S, D), F32), jax.ShapeDtypeStruct((S, W3), BF16), jax.ShapeDtypeStruct((S, D), BF16),
                   jax.ShapeDtypeStruct((S0_ROWS, GD), F32)],
        scratch_shapes=[pltpu.VMEM((L, AW), F32), pltpu.VMEM((L, AW), F32),
                        pltpu.VMEM((8, D), F32), pltpu.VMEM((8, AW), F32), pltpu.VMEM((4, GD, GD), F32)],
        compiler_params=_params(),
    )(dx1, x, proj, conv, mixed, pooled, m, pre, post, win, shard, pwb, ps, wout)


def _owner_id(me, relation, c):
    q = jnp.bitwise_xor(me, relation)
    return (q // 2, q % 2, c)


def _small_gather_steps(small_ref, all_ref, stage, send_sems, recv_sems, local_sem):
    x, y, c = _position()
    chips = _chips(x, y)

    def slot(chip, core):
        return 4 * chip[0] + 2 * chip[1] + core

    def copy(k, src, block, to):
        return pltpu.make_async_remote_copy(src_ref=src, dst_ref=all_ref.at[block], send_sem=send_sems.at[k],
                                            recv_sem=recv_sems.at[k], device_id=to, device_id_type=MESH)

    def own_copy():
        return pltpu.make_async_copy(stage, all_ref.at[slot((x, y), c)], local_sem)

    def first_sends():
        mine = slot((x, y), c)
        return [copy(0, small_ref, mine, (x, y, 1 - c))] + [copy(1 + j, small_ref, mine, (*chip, c)) for j, chip in enumerate(chips)]

    def relays():
        return [copy(4 + j, all_ref.at[slot(chip, c)], slot(chip, c), (x, y, 1 - c)) for j, chip in enumerate(chips)]

    def start():
        for cp in first_sends():
            cp.start()
        load = pltpu.make_async_copy(small_ref, stage, local_sem)
        load.start()
        load.wait()
        own_copy().start()

    def relay():
        for j, chip in enumerate(chips):
            copy(1 + j, small_ref, slot(chip, c), (*chip, c)).wait_recv()
        for cp in relays():
            cp.start()

    def finish():
        copy(0, small_ref, slot((x, y), 1 - c), (x, y, 1 - c)).wait_recv()
        for j, chip in enumerate(chips):
            copy(4 + j, small_ref, slot(chip, 1 - c), (x, y, 1 - c)).wait_recv()
        for cp in first_sends() + relays():
            cp.wait_send()
        own_copy().wait()

    return start, relay, finish


def _wgrad_layer(a_out, b_out, a_in, b_in, small, pos, name):
    S = a_in.shape[0]
    nko = S // TK
    nki = S // TK_IN
    hm = D // 2
    qr = hm // 4

    def out_index(s, pos_ref):
        return (jnp.minimum(s, nko - 1), 0)

    def a_in_index(s, pos_ref):
        return (jnp.where(s >= nko, (s - nko) % nki, 0), 0)

    def b_in_index(s, pos_ref):
        return (jnp.where(s >= nko, (s - nko) % nki, 0), jnp.bitwise_xor(pos_ref[1], 3 - jnp.maximum((s - nko) // nki, 0)))

    def body(pos_ref, ao_ref, bo_ref, a_ref, b_ref, small_ref, gout_ref, gin_ref, all_ref,
             acc, rbuf, sbuf, arr, mine, acc_o, rbuf_o, total_o, sbuf_o, arr_o, mine_o, stage,
             d2d_send, d2d_recv, ici_send, ici_recv, d2d_o_send, d2d_o_recv, ici_o_send, ici_o_recv,
             share_send, share_recv, local_sems, g_send, g_recv, g_local):
        s = pl.program_id(0)
        in_step = jnp.maximum(s - nko, 0)
        blk = jnp.where(s < nko, 0, 1 + in_step // nki)
        k = jnp.where(s < nko, s, in_step % nki)
        j = blk - 1
        x, y, c = _position()
        me = 2 * x + y
        sibling = (x, y, 1 - c)
        last = k == jnp.where(s < nko, nko - 1, nki - 1)
        gather_start, gather_relay, gather_finish = _small_gather_steps(small_ref, all_ref, stage, g_send, g_recv, g_local)

        def other_half(ref):
            return ref.at[pl.ds(pl.multiple_of((1 - c) * hm, hm), hm), :]

        def own_half(ref):
            return ref[pl.ds(pl.multiple_of(c * hm, hm), hm), :]

        def to_sibling(jj):
            return pltpu.make_async_remote_copy(
                src_ref=other_half(acc.at[jj % 2]), dst_ref=rbuf.at[jj], send_sem=d2d_send.at[jj], recv_sem=d2d_recv.at[jj],
                device_id=sibling, device_id_type=MESH)

        def to_owner(jj):
            return pltpu.make_async_remote_copy(
                src_ref=sbuf.at[jj], dst_ref=arr.at[2 - jj], send_sem=ici_send.at[jj], recv_sem=ici_recv.at[jj],
                device_id=_owner_id(me, 3 - jj, c), device_id_type=MESH)

        def out_to_sibling():
            return pltpu.make_async_remote_copy(
                src_ref=other_half(acc_o), dst_ref=rbuf_o, send_sem=d2d_o_send, recv_sem=d2d_o_recv,
                device_id=sibling, device_id_type=MESH)

        def out_to_owner(r):
            return pltpu.make_async_remote_copy(
                src_ref=sbuf_o.at[r], dst_ref=arr_o.at[r], send_sem=ici_o_send.at[r], recv_sem=ici_o_recv.at[r],
                device_id=_owner_id(me, r + 1, c), device_id_type=MESH)

        def pair_sum(jj):
            to_sibling(jj).wait_recv()
            return own_half(acc.at[jj % 2]) + rbuf[jj]

        def send_block(jj):
            sbuf[jj] = pair_sum(jj).astype(BF16)
            to_owner(jj).start()

        @pl.when(s == 0)
        def _():
            gather_start()

        @pl.when((blk == 3) & (k == 0))
        def _():
            gather_relay()

        @pl.when((blk == 0) & (k == 0))
        def _():
            acc_o[...] = jnp.zeros((D, D), F32)

        @pl.when(blk == 0)
        def _():
            acc_o[...] += _tn(ao_ref[...], bo_ref[...])

        @pl.when((blk == 0) & last)
        def _():
            out_to_sibling().start()

        @pl.when((blk >= 3) & (k == 0))
        def _():
            to_sibling(j - 2).wait_send()

        @pl.when((blk >= 1) & (k == 0))
        def _():
            acc[j % 2] = jnp.zeros((D, QW), F32)

        @pl.when(blk >= 1)
        def _():
            acc[j % 2] += _tn(a_ref[...], b_ref[...])

        @pl.when((blk >= 1) & last)
        def _():
            to_sibling(j).start()

        @pl.when((blk == 1) & last)
        def _():
            out_to_sibling().wait_recv()
            total_o[...] = own_half(acc_o) + rbuf_o[...]
            for r in range(3):
                q = jnp.bitwise_xor(me, r + 1)
                sbuf_o[r] = total_o[pl.ds(pl.multiple_of(q * qr, qr), qr), :].astype(BF16)
                out_to_owner(r).start()

        @pl.when((blk == 2) & last)
        def _():
            send_block(0)

        @pl.when((blk == 3) & last)
        def _():
            send_block(1)
            send_block(2)

        @pl.when((blk == 4) & last)
        def _():
            g_in = pair_sum(3)
            g_out = total_o[pl.ds(pl.multiple_of(me * qr, qr), qr), :]
            to_sibling(2).wait_send()
            to_sibling(3).wait_send()
            out_to_sibling().wait_send()
            for r in range(3):
                to_owner(r).wait()
                out_to_owner(r).wait()
            for r in range(3):
                g_in = g_in + arr[r].astype(F32)
                g_out = g_out + arr_o[r].astype(F32)
            mine[...] = g_in
            mine_o[...] = g_out
            copies = []
            for idx, (src, dst) in enumerate([(mine, gin_ref), (mine_o, gout_ref)]):
                copies.append(pltpu.make_async_remote_copy(
                    src_ref=src, dst_ref=dst.at[c], send_sem=share_send.at[idx], recv_sem=share_recv.at[idx],
                    device_id=sibling, device_id_type=MESH))
                copies.append(pltpu.make_async_copy(src, dst.at[c], local_sems.at[idx]))
            for cp in copies:
                cp.start()
            for cp in copies:
                cp.wait()
            gather_finish()

    any_spec = pl.BlockSpec(memory_space=pl.ANY)
    grid_spec = pltpu.PrefetchScalarGridSpec(
        num_scalar_prefetch=1, grid=(nko + 4 * nki,),
        in_specs=[pl.BlockSpec((TK, D), out_index), pl.BlockSpec((TK, D), out_index),
                  pl.BlockSpec((TK_IN, D), a_in_index), pl.BlockSpec((TK_IN, QW), b_in_index), any_spec],
        out_specs=[any_spec, any_spec, any_spec],
        scratch_shapes=[pltpu.VMEM((2, D, QW), F32), pltpu.VMEM((4, hm, QW), F32), pltpu.VMEM((3, hm, QW), BF16),
                        pltpu.VMEM((3, hm, QW), BF16), pltpu.VMEM((hm, QW), F32),
                        pltpu.VMEM((D, D), F32), pltpu.VMEM((hm, D), F32), pltpu.VMEM((hm, D), F32), pltpu.VMEM((3, qr, D), BF16),
                        pltpu.VMEM((3, qr, D), BF16), pltpu.VMEM((qr, D), F32),
                        pltpu.VMEM(small.shape, F32),
                        pltpu.SemaphoreType.DMA((4,)), pltpu.SemaphoreType.DMA((4,)),
                        pltpu.SemaphoreType.DMA((3,)), pltpu.SemaphoreType.DMA((3,)),
                        pltpu.SemaphoreType.DMA, pltpu.SemaphoreType.DMA,
                        pltpu.SemaphoreType.DMA((3,)), pltpu.SemaphoreType.DMA((3,)),
                        pltpu.SemaphoreType.DMA((2,)), pltpu.SemaphoreType.DMA((2,)), pltpu.SemaphoreType.DMA((2,)),
                        pltpu.SemaphoreType.DMA((7,)), pltpu.SemaphoreType.DMA((7,)), pltpu.SemaphoreType.DMA])
    return pl.pallas_call(
        body, name=name, grid_spec=grid_spec,
        out_shape=[jax.ShapeDtypeStruct((2, qr, D), F32), jax.ShapeDtypeStruct((2, hm, QW), F32),
                   jax.ShapeDtypeStruct((8, *small.shape), F32)],
        compiler_params=pltpu.CompilerParams(dimension_semantics=("arbitrary",), vmem_limit_bytes=VMEM_LIMIT),
    )(pos, a_out, b_out, a_in, b_in, small)


def _chips(x, y):
    return [(1 - x, y), (x, 1 - y), (1 - x, 1 - y)]


def _gather_weights(parts, split, x, pre, next_parts, pool_w, w_s):
    n = len(parts)
    nn = len(next_parts)
    S = x.shape[0]
    tm = TM_FWD
    nt = S // tm
    staged = [a for a in range(n) if split[a]]

    def body(x_ref, pre_ref, *refs):
        raw, refs = refs[:n], refs[n:]
        side_hbm, hb_ref = refs[:nn + 2], refs[nn + 2]
        refs = refs[nn + 3:]
        outs, refs = refs[:n], refs[n:]
        next_outs, poolb_ref, tril_ref, trilt_ref = refs[:nn], refs[nn], refs[nn + 1], refs[nn + 2]
        refs = refs[nn + 3:]
        stage, refs = refs[:len(staged)], refs[len(staged):]
        side, (send_sems, recv_sems, local_sems, side_sems) = refs[:nn + 2], refs[nn + 2:]
        next_raw, pool_ref, ws_ref = side[:nn], side[nn], side[nn + 1]
        ins = [stage[staged.index(a)] if split[a] else raw[a] for a in range(n)]
        i = pl.program_id(0)

        def fetch(k):
            return pltpu.make_async_copy(side_hbm[k], side[k], side_sems.at[k])
        x, y, c = _position()
        me = 2 * x + y
        nbr_x, nbr_y, diag = _chips(x, y)
        id_x, id_y, id_d = (2 * chip[0] + chip[1] for chip in (nbr_x, nbr_y, diag))
        sibling = (x, y, 1 - c)

        def rows(a, ref, who, piece=None):
            r = parts[a].shape[0] // 2
            if piece is None:
                return ref.at[pl.ds(pl.multiple_of(who * r, 16), r), :]
            return ref.at[pl.ds(pl.multiple_of(who * r + piece * (r // 2), 16), r // 2), :]

        def copy(a, k, src, dst, to):
            return pltpu.make_async_remote_copy(src_ref=src, dst_ref=dst, send_sem=send_sems.at[a, k], recv_sem=recv_sems.at[a, k],
                                                device_id=to, device_id_type=MESH)

        def direct(a, k, chip):
            if split[a]:
                return copy(a, k, rows(a, ins[a], c), rows(a, outs[a].at[me], c), (*chip, c))
            return copy(a, k, ins[a], outs[a].at[me], (*chip, c))

        def arrival(a, k, src_id):
            if split[a]:
                return copy(a, k, rows(a, ins[a], c), rows(a, outs[a].at[src_id], c), (*nbr_x, c))
            return copy(a, k, ins[a], outs[a].at[src_id], (*nbr_x, c))

        def pass_on(a, k, src_id, piece, chip):
            region = rows(a, outs[a].at[src_id], c, piece)
            return copy(a, k, region, region, (*chip, c))

        def hand_over(a, k, src_id, who):
            region = rows(a, outs[a].at[src_id], who)
            return copy(a, k, region, region, sibling)

        def local(a):
            return pltpu.make_async_copy(ins[a], outs[a].at[me], local_sems.at[a])

        def first_sends(a):
            return [direct(a, 0, nbr_x), direct(a, 1, nbr_y)] + ([] if split[a] else [direct(a, 2, diag)])

        def after_x(a):
            return [pass_on(a, 3, id_x, 1, nbr_y), hand_over(a, 4, id_x, c)] if split[a] else []

        def after_y(a):
            return [pass_on(a, 2, id_y, 0, nbr_x), hand_over(a, 5, id_y, c)] if split[a] else []

        def after_diag(a):
            return [hand_over(a, 6, id_d, c)] if split[a] else []

        @pl.when(i == 0)
        def _():
            for a in staged:
                ins[a][...] = raw[a][...].astype(BF16)
            for a in range(n):
                local(a).start()
                for cp in first_sends(a):
                    cp.start()
            for k in range(nn + 2):
                fetch(k).start()

        @pl.when(i == 1)
        def _():
            for k in range(nn + 2):
                fetch(k).wait()
            for a in range(nn):
                next_outs[a][...] = next_raw[a][...].astype(BF16)
            poolb_ref[...] = pool_ref[...].astype(BF16)
            lower = lax.broadcasted_iota(jnp.int32, (CHUNK, CHUNK), 0) >= lax.broadcasted_iota(jnp.int32, (CHUNK, CHUNK), 1)
            for h in range(HEADS):
                tril = jnp.where(lower, ws_ref[h], 0.0)
                tril_ref[h] = tril.astype(BF16)
                trilt_ref[h] = tril.T.astype(BF16)

        xh, _ = _rms_fwd(x_ref[...])
        hb_ref[...] = (xh * pre_ref[...]).astype(BF16)

        @pl.when(i == nt - 2)
        def _():
            for a in range(n):
                arrival(a, 0, id_x).wait_recv()
                for cp in after_x(a):
                    cp.start()
            for a in range(n):
                arrival(a, 1, id_y).wait_recv()
                for cp in after_y(a):
                    cp.start()

        @pl.when(i == nt - 1)
        def _():
            for a in range(n):
                if split[a]:
                    pass_on(a, 2, id_d, 0, nbr_x).wait_recv()
                    pass_on(a, 3, id_d, 1, nbr_y).wait_recv()
                    for cp in after_diag(a):
                        cp.start()
                else:
                    arrival(a, 2, id_d).wait_recv()
            for a in range(n):
                if split[a]:
                    for k, src_id in ((4, id_x), (5, id_y), (6, id_d)):
                        hand_over(a, k, src_id, 1 - c).wait_recv()
            for a in range(n):
                for cp in first_sends(a) + after_x(a) + after_y(a) + after_diag(a):
                    cp.wait_send()
                local(a).wait()

    any_spec = pl.BlockSpec(memory_space=pl.ANY)
    vmem = pl.BlockSpec(memory_space=pltpu.VMEM)
    sent = [BF16 if split[a] else parts[a].dtype for a in range(n)]
    return pl.pallas_call(
        body, name="gather_weights", grid=(nt,),
        in_specs=[_rows(tm, D), _full((1, D))] + [vmem] * n + [any_spec] * (nn + 2),
        out_specs=[_rows(tm, D)] + [any_spec] * n + [vmem] * (nn + 3),
        out_shape=[jax.ShapeDtypeStruct((S, D), BF16)] + [jax.ShapeDtypeStruct((4, *p.shape), t) for p, t in zip(parts, sent)]
        + [jax.ShapeDtypeStruct(p.shape, BF16) for p in next_parts]
        + [jax.ShapeDtypeStruct(pool_w.shape, BF16), jax.ShapeDtypeStruct(w_s.shape, BF16), jax.ShapeDtypeStruct(w_s.shape, BF16)],
        scratch_shapes=[pltpu.VMEM(parts[a].shape, BF16) for a in staged]
        + [pltpu.VMEM(p.shape, F32) for p in (*next_parts, pool_w, w_s)]
        + [pltpu.SemaphoreType.DMA((n, 7)), pltpu.SemaphoreType.DMA((n, 7)), pltpu.SemaphoreType.DMA((n,)),
           pltpu.SemaphoreType.DMA((nn + 2,))],
        compiler_params=_params(),
    )(x, pre, *parts, *next_parts, pool_w, w_s)


def _adamw(w, g, m, v):
    m = ADAM_B1 * m + (1.0 - ADAM_B1) * g
    v = ADAM_B2 * v + (1.0 - ADAM_B2) * (g * g)
    m_hat = m / (1.0 - ADAM_B1 ** ADAM_STEP)
    v_hat = v / (1.0 - ADAM_B2 ** ADAM_STEP)
    delta = -ADAM_LR * (m_hat / (jnp.sqrt(v_hat) + ADAM_EPS) + ADAM_WD * w)
    return delta, m, v


def _adamw_big(ws, gs, ms, vs):
    steps = 4
    n = len(ws)

    def body(*refs):
        ins, outs = refs[:4 * n], refs[4 * n:]
        for a in range(n):
            w_ref, g_ref, m_ref, v_ref = ins[4 * a:4 * a + 4]
            go_ref, d_ref, mo_ref, vo_ref = outs[4 * a:4 * a + 4]
            gv = g_ref[...]
            go_ref[...] = gv
            d_ref[...], mo_ref[...], vo_ref[...] = _adamw(w_ref[...], gv, m_ref[...], v_ref[...])

    specs, shapes, operands = [], [], []
    for w, g, m, v in zip(ws, gs, ms, vs):
        rows, cols = w.shape
        specs += [pl.BlockSpec((rows // steps, cols), lambda i: (i, 0))] * 4
        shapes += [jax.ShapeDtypeStruct((rows, cols), F32)] * 4
        operands += [w, g, m, v]
    res = pl.pallas_call(
        body, name="adamw_big", grid=(steps,),
        in_specs=specs, out_specs=specs, out_shape=shapes,
        compiler_params=pltpu.CompilerParams(dimension_semantics=("arbitrary",), vmem_limit_bytes=VMEM_LIMIT),
    )(*operands)
    return [res[4 * a:4 * a + 4] for a in range(n)]


def _adamw_small(g0, g1, weights, moms, vels):
    names = ["pre", "post", "conv", "pw", "ps", "lng", "lnb", "ws", "bs"]
    shapes = [w.shape for w in weights]

    def body(*refs):
        me = 2 * lax.axis_index("x") + lax.axis_index("y")
        g0_ref, g1_ref = refs[0], refs[1]
        w_refs, m_refs, v_refs = refs[2:11], refs[11:20], refs[20:29]
        outs = refs[29:29 + 36]
        loss_ref = refs[65]
        t0_ref, t1_ref = refs[66], refs[67]
        t0 = g0_ref[0]
        t1 = g1_ref[0]
        for d in range(1, 8):
            t0 = t0 + g0_ref[d]
            t1 = t1 + g1_ref[d]
        t0_ref[...] = t0
        t1_ref[...] = t1
        loss_ref[...] = t1_ref[S1_LOSS:S1_LOSS + 1, 0:1]
        my_conv = pl.multiple_of(S0_CONV + 8 * me, 8)
        my_ln = pl.multiple_of(S1_LN + 8 * me, 8)

        def update(idx, piece, grad):
            go, do, mo, vo = outs[4 * idx:4 * idx + 4]
            go[piece] = grad
            do[piece], mo[piece], vo[piece] = _adamw(w_refs[idx][piece], grad, m_refs[idx][piece], v_refs[idx][piece])

        for layer in range(2):
            for k in range(D // GD):
                lanes = slice(k * GD, (k + 1) * GD)
                tref, pre0, post0 = (t0_ref, S0_PRE, S0_POST) if layer == 0 else (t1_ref, S1_PRE, S1_POST)
                update(0, (slice(layer, layer + 1), lanes), tref[pre0 + k:pre0 + k + 1, :])
                update(1, (slice(layer, layer + 1), lanes), tref[post0 + k:post0 + k + 1, :])
        conv_rows = t0_ref[pl.ds(my_conv, 8), :]
        update(2, (slice(0, 3), 0, slice(None)), conv_rows[0:3, :])
        for g in range(4):
            update(3, (g,), t0_ref[S0_PW + g * GD:S0_PW + (g + 1) * GD, :])
            update(4, (slice(0, 1), slice(g * GD, (g + 1) * GD)), t0_ref[S0_PS + g:S0_PS + g + 1, :])
        ln_rows = t1_ref[pl.ds(my_ln, 8), :]
        for k in range(2):
            update(5, (slice(0, 1), slice(k * GD, (k + 1) * GD)), ln_rows[k:k + 1, :])
            update(6, (slice(0, 1), slice(k * GD, (k + 1) * GD)), ln_rows[2 + k:3 + k, :])
        for h in range(HEADS):
            update(7, (h,), t1_ref[S1_WS + h * CHUNK:S1_WS + (h + 1) * CHUNK, :])
        update(8, (slice(None), slice(None)), t1_ref[S1_BS:S1_BS + HEADS, :])

    vm = pl.BlockSpec(memory_space=pltpu.VMEM)
    out_shape = []
    for s in shapes:
        out_shape += [jax.ShapeDtypeStruct(s, F32)] * 4
    out_shape.append(jax.ShapeDtypeStruct((1, 1), F32))
    res = pl.pallas_call(
        body, name="adamw_small",
        in_specs=[vm] * 29, out_specs=[vm] * 37, out_shape=out_shape,
        scratch_shapes=[pltpu.VMEM((S0_ROWS, GD), F32), pltpu.VMEM((S1_ROWS, GD), F32)],
        compiler_params=pltpu.CompilerParams(vmem_limit_bytes=VMEM_LIMIT),
    )(g0, g1, *weights, *moms, *vels)
    per_weight = {nm: res[4 * i:4 * i + 4] for i, nm in enumerate(names)}
    return per_weight, res[36]


def _pad8(a):
    return jnp.pad(a, ((0, 8 - a.shape[0]), (0, 0)))


def kernel(x, pre_norm, post_norm, even_w_in, even_conv_w, even_pool_w, even_pool_scale, even_w_out, odd_w_in, odd_ln_g, odd_ln_b, odd_w_s, odd_b_s, odd_w_out, loss_target, m_pre_norm, m_post_norm, m_even_w_in, m_even_conv_w, m_even_pool_w, m_even_pool_scale, m_even_w_out, m_odd_w_in, m_odd_ln_g, m_odd_ln_b, m_odd_w_s, m_odd_b_s, m_odd_w_out, v_pre_norm, v_post_norm, v_even_w_in, v_even_conv_w, v_even_pool_w, v_even_pool_scale, v_even_w_out, v_odd_w_in, v_odd_ln_g, v_odd_ln_b, v_odd_w_s, v_odd_b_s, v_odd_w_out):
    xs = x[0]
    tgt = loss_target[0]

    small_shard = jnp.concatenate([_pad8(even_conv_w[0]), _pad8(odd_ln_g.reshape(2, GD)), _pad8(odd_ln_b.reshape(2, GD))], axis=0)
    pre0, pre1 = pre_norm[0:1], pre_norm[1:2]
    post0, post1 = post_norm[0:1], post_norm[1:2]
    hb0, win0, wout0, shard, win1_shard, wout1_shard, pool_wb, ws_tril, ws_tril_t = _gather_weights(
        [even_w_in[0], even_w_out[0], small_shard], [True, True, False], xs, pre0, [odd_w_in[0], odd_w_out[0]],
        even_pool_w[0], odd_w_s[0])
    wout0 = wout0.reshape(D, D)
    px, py, pc = _position()
    pos = jnp.stack([pc, 2 * px + py]).astype(jnp.int32)
    bias = jnp.broadcast_to(odd_b_s[0][:, :, None], (HEADS, CHUNK, GD))

    x1, proj0, m0, mixp0, conv0, mixed0, pooled0, win1, wout1 = _even_fwd(
        xs, hb0, post0, win0, shard, pool_wb, even_pool_scale, wout0, [win1_shard, wout1_shard])
    wout1 = wout1.reshape(D, D)
    proj1, m1, hb1, yp1, dx2, loss_part = _odd_fwd(x1, tgt, pre1, post1, win1, shard, ws_tril, bias, wout1)
    dx1, dproj1, dmb1, small1 = _odd_bwd(dx2, x1, proj1, m1, loss_part, pre1, post1, win1, shard, ws_tril, ws_tril_t, bias, wout1)
    g_out1, g_in1, all1 = _wgrad_layer(yp1, dmb1, hb1, dproj1, small1, pos, "wgrad_odd")
    gx, dproj0, dmb0, small0 = _even_bwd(dx1, xs, proj0, conv0, mixed0, pooled0, m0, pre0, post0, win0, shard, pool_wb,
                                         even_pool_scale, wout0)
    g_out0, g_in0, all0 = _wgrad_layer(mixp0, dmb0, hb0, dproj0, small0, pos, "wgrad_even")

    big_w = [even_w_in[0], even_w_out[0], odd_w_in[0], odd_w_out[0]]
    big_g = [g.reshape(w.shape) for g, w in zip([g_in0, g_out0, g_in1, g_out1], big_w)]
    big_m = [m_even_w_in[0], m_even_w_out[0], m_odd_w_in[0], m_odd_w_out[0]]
    big_v = [v_even_w_in[0], v_even_w_out[0], v_odd_w_in[0], v_odd_w_out[0]]
    big = _adamw_big(big_w, big_g, big_m, big_v)

    def taps_first(a):
        return jnp.swapaxes(a, 0, 1)

    small_w = [pre_norm, post_norm, taps_first(even_conv_w), even_pool_w[0], even_pool_scale, odd_ln_g, odd_ln_b, odd_w_s[0], odd_b_s[0]]
    small_m = [m_pre_norm, m_post_norm, taps_first(m_even_conv_w), m_even_pool_w[0], m_even_pool_scale, m_odd_ln_g, m_odd_ln_b,
               m_odd_w_s[0], m_odd_b_s[0]]
    small_v = [v_pre_norm, v_post_norm, taps_first(v_even_conv_w), v_even_pool_w[0], v_even_pool_scale, v_odd_ln_g, v_odd_ln_b,
               v_odd_w_s[0], v_odd_b_s[0]]
    sm, loss = _adamw_small(all0, all1, small_w, small_m, small_v)

    def lead(a):
        return a[None]

    per = {
        "pre_norm": sm["pre"], "post_norm": sm["post"],
        "even_w_in": [lead(a) for a in big[0]], "even_conv_w": [taps_first(a) for a in sm["conv"]],
        "even_pool_w": [lead(a) for a in sm["pw"]], "even_pool_scale": sm["ps"],
        "even_w_out": [lead(a) for a in big[1]], "odd_w_in": [lead(a) for a in big[2]],
        "odd_ln_g": sm["lng"], "odd_ln_b": sm["lnb"],
        "odd_w_s": [lead(a) for a in sm["ws"]], "odd_b_s": [lead(a) for a in sm["bs"]],
        "odd_w_out": [lead(a) for a in big[3]],
    }
    order = ["pre_norm", "post_norm", "even_w_in", "even_conv_w", "even_pool_w", "even_pool_scale", "even_w_out", "odd_w_in",
             "odd_ln_g", "odd_ln_b", "odd_w_s", "odd_b_s", "odd_w_out"]
    outs = [loss.reshape(()), gx[None]]
    for kind in range(4):
        outs += [per[nm][kind] for nm in order]
    return tuple(outs)
```

```python
import jax
import jax.numpy as jnp
from jax import lax
from jax.experimental import pallas as pl
from jax.experimental.pallas import tpu as pltpu

F32 = jnp.float32
BF16 = jnp.bfloat16
MESH = pl.DeviceIdType.MESH

D = 1024
W3 = 3 * D
QW = W3 // 4
AW = 512
GD = 128
CHUNK = 128
HEADS = 8
HALO = 16
POOL_WINDOWS = (2, 4, 8, 16)
EPS = 1e-6
TM_FWD = 512
TM_BWD = 256
TK = 1024
TK_IN = 2048
VMEM_LIMIT = 56 * 1024 * 1024

ADAM_LR, ADAM_B1, ADAM_B2, ADAM_EPS, ADAM_WD, ADAM_STEP = 0.001, 0.9, 0.999, 1e-08, 0.01, 10

S0_PRE, S0_POST, S0_CONV, S0_PS, S0_PW, S0_ROWS = 0, 8, 16, 48, 56, 568
S1_PRE, S1_POST, S1_LN, S1_BS, S1_WS, S1_LOSS, S1_ROWS = 0, 8, 16, 48, 56, 1080, 1088


def _nn(a, b):
    return jnp.dot(a, b, preferred_element_type=F32)


def _nt(a, b):
    return lax.dot_general(a, b, (((1,), (1,)), ((), ())), preferred_element_type=F32)


def _tn(a, b):
    return lax.dot_general(a, b, (((0,), (0,)), ((), ())), preferred_element_type=F32)


def _sigmoid(z):
    return 1.0 / (1.0 + jnp.exp(-z))


def _rms_fwd(x):
    r = lax.rsqrt(jnp.mean(x * x, axis=-1, keepdims=True) + EPS)
    return x * r, r


def _rms_bwd(dy, xh, r, g):
    dn = dy * g
    dx = r * (dn - xh * jnp.mean(xh * dn, axis=-1, keepdims=True))
    return dx, jnp.sum(dy * xh, axis=0, keepdims=True)


def _full(shape):
    nd = len(shape)
    return pl.BlockSpec(shape, lambda i, _n=nd: (0,) * _n, pipeline_mode=pl.Buffered(1))


def _full_out(shape):
    nd = len(shape)
    return pl.BlockSpec(shape, lambda i, _n=nd: (0,) * _n)


def _rows(tm, width, index=None):
    return pl.BlockSpec((tm, width), (lambda i: (i, 0)) if index is None else index)


def _params():
    return pltpu.CompilerParams(dimension_semantics=("arbitrary",), vmem_limit_bytes=VMEM_LIMIT)


def _position():
    x, y, c = lax.axis_index("x"), lax.axis_index("y"), lax.axis_index("c")
    return x, y, c


SHARD_ROWS = 24


def _shard_row(shard_ref, row):
    return jnp.concatenate([shard_ref[q, row:row + 1, :] for q in range(4)], axis=1)


def _shard_vector(shard_ref, first_row):
    return jnp.concatenate([shard_ref[q, first_row + k:first_row + k + 1, :] for q in range(4) for k in range(2)], axis=1)


def _even_mix(proj_ref, hc_ext, xp_ext, taps, pw_ref, ps_ref, first_row):
    tm = proj_ref.shape[0]
    xa = proj_ref[:, 0:AW]
    gb = proj_ref[:, AW:2 * AW]
    gc = proj_ref[:, 2 * AW:3 * AW]
    za = proj_ref[:, 3 * AW:4 * AW]
    xp = proj_ref[:, 4 * AW:5 * AW]
    zp = proj_ref[:, 5 * AW:6 * AW]
    hc = gc * xa
    hc_ext[HALO:, :] = hc
    e = hc_ext[...]
    conv = taps[2] * hc + taps[1] * pltpu.roll(e, 1, 0)[HALO:] + taps[0] * pltpu.roll(e, 2, 0)[HALO:]
    sa = _sigmoid(za)
    xp_ext[HALO:, :] = xp
    pos = first_row + lax.broadcasted_iota(jnp.int32, (tm, 1), 0)
    pooled, mixed, counts = [], [], []
    for g, w in enumerate(POOL_WINDOWS):
        cols = slice(g * GD, (g + 1) * GD)
        s = xp_ext[:, cols]
        for k in range(g + 1):
            s = s + pltpu.roll(s, 2 ** k, 0)
        count = jnp.minimum(pos + 1, w).astype(F32)
        pg = s[HALO:] / count - xp[:, cols]
        pooled.append(pg.astype(BF16))
        mixed.append(_nn(pooled[-1], pw_ref[g]))
        counts.append(count)
    mixed = jnp.concatenate(mixed, axis=-1)
    sb = _sigmoid(zp)
    return dict(xa=xa, gb=gb, gc=gc, za=za, zp=zp, hc=hc, conv=conv, sa=sa, sb=sb, pooled=pooled, mixed=mixed, counts=counts)


def _half_rows(ref, rows, who):
    return ref.at[pl.ds(pl.multiple_of(who * (rows // 2), 8), rows // 2), :]


def _store_permuted(ref, value):
    for ob in range(D // GD):
        nb = 4 * (ob % 2) + ob // 2
        ref[:, nb * GD:(nb + 1) * GD] = value[:, ob * GD:(ob + 1) * GD]


def _even_fwd(x, hb, post, win, shard, pwb, ps, wout, next_shards):
    S = x.shape[0]
    tm = TM_FWD
    nt = S // tm
    relay = (3 * nt) // 4
    n = len(next_shards)
    shard_rows = [p.shape[0] for p in next_shards]

    def body(x_ref, hb_ref, post_ref, win_ref, shard_ref, pw_ref, ps_ref, wout_ref, *rest):
        shard_refs, rest = rest[:n], rest[n:]
        x1_ref, proj_ref, m_ref, mixp_ref, conv_ref, mixed_ref, pooled_ref = rest[:7]
        full_refs, rest = rest[7:7 + n], rest[7 + n:]
        hc_ext, xp_ext, mix_sc = rest[:3]
        stage, rest = rest[3:3 + n], rest[3 + n:]
        send_sems, recv_sems, local_sems = rest
        i = pl.program_id(0)
        px, py, pc = _position()
        me = 2 * px + py
        chips = _chips(px, py)

        def ici(a, j):
            return pltpu.make_async_remote_copy(
                src_ref=_half_rows(shard_refs[a], shard_rows[a], pc), dst_ref=_half_rows(full_refs[a].at[me], shard_rows[a], pc),
                send_sem=send_sems.at[a, j], recv_sem=recv_sems.at[a, j], device_id=(*chips[j], pc), device_id_type=MESH)

        def ici_arrival(a, j):
            src = 2 * chips[j][0] + chips[j][1]
            return pltpu.make_async_remote_copy(
                src_ref=_half_rows(shard_refs[a], shard_rows[a], pc), dst_ref=_half_rows(full_refs[a].at[src], shard_rows[a], pc),
                send_sem=send_sems.at[a, j], recv_sem=recv_sems.at[a, j], device_id=(*chips[j], pc), device_id_type=MESH)

        def relay_copy(a, j, who):
            src = 2 * chips[j][0] + chips[j][1]
            region = _half_rows(full_refs[a].at[src], shard_rows[a], who)
            return pltpu.make_async_remote_copy(
                src_ref=region, dst_ref=region, send_sem=send_sems.at[a, 3 + j], recv_sem=recv_sems.at[a, 3 + j],
                device_id=(px, py, 1 - pc), device_id_type=MESH)

        def own_copy(a):
            return pltpu.make_async_copy(stage[a], full_refs[a].at[me], local_sems.at[a])

        @pl.when(i == 0)
        def _():
            hc_ext[0:HALO, :] = jnp.zeros((HALO, AW), F32)
            xp_ext[0:HALO, :] = jnp.zeros((HALO, AW), F32)
            for a in range(n):
                for j in range(3):
                    ici(a, j).start()
            for a in range(n):
                load = pltpu.make_async_copy(shard_refs[a], stage[a], local_sems.at[a])
                load.start()
                load.wait()
                own_copy(a).start()

        @pl.when(i == relay)
        def _():
            for j in range(3):
                for a in range(n):
                    ici_arrival(a, j).wait_recv()
                    relay_copy(a, j, pc).start()

        hb = hb_ref[...]
        for q in range(4):
            proj_ref[:, q * QW:(q + 1) * QW] = _nn(hb, win_ref[q])
        t = _even_mix(proj_ref, hc_ext, xp_ext, [_shard_row(shard_ref, k) for k in range(3)], pw_ref, ps_ref, i * tm)
        conv_ref[...] = t["conv"]
        mixed_ref[...] = t["mixed"]
        for g in range(4):
            pooled_ref[:, g * GD:(g + 1) * GD] = t["pooled"][g]
        mix_sc[:, 0:AW] = (t["gb"] * t["conv"] * (t["za"] * t["sa"])).astype(BF16)
        mix_sc[:, AW:2 * AW] = (t["mixed"] * ps_ref[...] * (t["zp"] * t["sb"])).astype(BF16)
        mix = mix_sc[...]
        _store_permuted(mixp_ref, mix)
        m = _nn(mix, wout_ref[...])
        m_ref[...] = m
        mh, _ = _rms_fwd(m)
        x1_ref[...] = x_ref[...] + mh * post_ref[0:1, :]
        hc_ext[0:HALO, :] = hc_ext[tm:tm + HALO, :]
        xp_ext[0:HALO, :] = xp_ext[tm:tm + HALO, :]

        @pl.when(i == nt - 1)
        def _():
            for j in range(3):
                for a in range(n):
                    relay_copy(a, j, 1 - pc).wait_recv()
            for a in range(n):
                for j in range(3):
                    ici(a, j).wait_send()
                    relay_copy(a, j, pc).wait_send()
                own_copy(a).wait()

    any_spec = pl.BlockSpec(memory_space=pl.ANY)
    return pl.pallas_call(
        body, name="even_fwd", grid=(nt,),
        in_specs=[_rows(tm, D), _rows(tm, D), _full((2, D)), _full((4, D, QW)), _full((4, SHARD_ROWS, GD)), _full((4, GD, GD)),
                  _full((1, AW)), _full((D, D))] + [any_spec] * n,
        out_specs=[_rows(tm, D), _rows(tm, W3), _rows(tm, D), _rows(tm, D), _rows(tm, AW), _rows(tm, AW), _rows(tm, AW)]
        + [any_spec] * n,
        out_shape=[jax.ShapeDtypeStruct((S, D), F32), jax.ShapeDtypeStruct((S, W3), F32), jax.ShapeDtypeStruct((S, D), F32),
                   jax.ShapeDtypeStruct((S, D), BF16),
                   jax.ShapeDtypeStruct((S, AW), F32), jax.ShapeDtypeStruct((S, AW), F32), jax.ShapeDtypeStruct((S, AW), BF16)]
        + [jax.ShapeDtypeStruct((4, *p.shape), p.dtype) for p in next_shards],
        scratch_shapes=[pltpu.VMEM((tm + HALO, AW), F32), pltpu.VMEM((tm + HALO, AW), F32), pltpu.VMEM((tm, D), BF16)]
        + [pltpu.VMEM(p.shape, p.dtype) for p in next_shards]
        + [pltpu.SemaphoreType.DMA((n, 6)), pltpu.SemaphoreType.DMA((n, 6)), pltpu.SemaphoreType.DMA((n,))],
        compiler_params=_params(),
    )(x, hb, post, win, shard, pwb, ps, wout, *next_shards)


def _chunks_side_by_side(a, h):
    return jnp.concatenate([a[n * CHUNK:(n + 1) * CHUNK, h * GD:(h + 1) * GD] for n in range(a.shape[0] // CHUNK)], axis=1)


def _odd_mix(proj_ref, lng, lnb, ws_ref, bias_ref, sv_ref):
    tm = proj_ref.shape[0]
    u = proj_ref[:, 0:D]
    v = proj_ref[:, D:2 * D]
    z = proj_ref[:, 2 * D:3 * D]
    mu = jnp.mean(v, axis=-1, keepdims=True)
    vc = v - mu
    rs = lax.rsqrt(jnp.mean(vc * vc, axis=-1, keepdims=True) + EPS)
    vh = vc * rs
    vnb = (vh * lng + lnb).astype(BF16)
    for h in range(HEADS):
        sv = _nn(ws_ref[h], _chunks_side_by_side(vnb, h))
        for n in range(tm // CHUNK):
            sv_ref[n * CHUNK:(n + 1) * CHUNK, h * GD:(h + 1) * GD] = sv[:, n * GD:(n + 1) * GD] + bias_ref[h]
    return dict(u=u, z=z, vh=vh, rs=rs, vnb=vnb, sz=_sigmoid(z))


def _odd_fwd(x1, tgt, pre, post, win, shard, wsb, bias, wout):
    S = x1.shape[0]
    tm = TM_FWD
    nt = S // tm

    def body(x_ref, tgt_ref, pre_ref, post_ref, win_ref, shard_ref, ws_ref, bias_ref, wout_ref,
             proj_ref, m_ref, hb_ref, yb_ref, dx2_ref, loss_ref, sv_ref):
        i = pl.program_id(0)

        @pl.when(i == 0)
        def _():
            loss_ref[...] = jnp.zeros((8, GD), F32)

        xv = x_ref[...]
        xh, _ = _rms_fwd(xv)
        hb = (xh * pre_ref[1:2, :]).astype(BF16)
        hb_ref[...] = hb
        for q in range(4):
            proj_ref[:, q * QW:(q + 1) * QW] = _nn(hb, win_ref[q])
        t = _odd_mix(proj_ref, _shard_vector(shard_ref, 8), _shard_vector(shard_ref, 16), ws_ref, bias_ref, sv_ref)
        yb = (t["u"] * sv_ref[...] * (t["z"] * t["sz"])).astype(BF16)
        _store_permuted(yb_ref, yb)
        m = _nn(yb, wout_ref[...])
        m_ref[...] = m
        mh, _ = _rms_fwd(m)
        err = xv + mh * post_ref[1:2, :] - tgt_ref[...]
        dx2_ref[...] = err * (1.0 / D)
        part = 0.5 * jnp.sum(jnp.mean(err * err, axis=-1, keepdims=True), axis=0, keepdims=True)
        loss_ref[...] += jnp.broadcast_to(part, (8, GD))

    return pl.pallas_call(
        body, name="odd_fwd", grid=(nt,),
        in_specs=[_rows(tm, D), _rows(tm, D), _full((2, D)), _full((2, D)), _full((4, D, QW)), _full((4, SHARD_ROWS, GD)),
                  _full((HEADS, CHUNK, CHUNK)), _full((HEADS, CHUNK, GD)), _full((D, D))],
        out_specs=[_rows(tm, W3), _rows(tm, D), _rows(tm, D), _rows(tm, D), _rows(tm, D), _full_out((8, GD))],
        out_shape=[jax.ShapeDtypeStruct((S, W3), F32), jax.ShapeDtypeStruct((S, D), F32), jax.ShapeDtypeStruct((S, D), BF16),
                   jax.ShapeDtypeStruct((S, D), BF16), jax.ShapeDtypeStruct((S, D), F32), jax.ShapeDtypeStruct((8, GD), F32)],
        scratch_shapes=[pltpu.VMEM((tm, D), F32)],
        compiler_params=_params(),
    )(x1, tgt, pre, post, win, shard, wsb, bias, wout)


def _store_rows(ref, row0, value):
    r, width = value.shape
    for a in range(r):
        for k in range(width // GD):
            ref[row0 + a * (width // GD) + k:row0 + a * (width // GD) + k + 1, :] = value[a:a + 1, k * GD:(k + 1) * GD]


def _proj_bwd(dproj, win_ref, x, dy, pre):
    dh = _nt(dproj[:, 0:QW], win_ref[0])
    for q in range(1, 4):
        dh += _nt(dproj[:, q * QW:(q + 1) * QW], win_ref[q])
    xh, r = _rms_fwd(x)
    dxn, dpre = _rms_bwd(dh, xh, r, pre)
    return dy + dxn, dpre


def _odd_bwd(dx2, x1, proj, m, loss, pre, post, win, shard, wsb, wsbt, bias, wout):
    S = x1.shape[0]
    tm = TM_BWD
    nt = S // tm

    def body(dy_ref, x_ref, proj_ref, m_ref, loss_ref, pre_ref, post_ref, win_ref, shard_ref, ws_ref, wst_ref, bias_ref,
             wout_ref, dx_ref, dproj_ref, dmb_ref, small_ref, sv_ref, dvn_ref, acc1024, dws_acc, dbs_acc):
        i = pl.program_id(0)
        lng = _shard_vector(shard_ref, 8)

        @pl.when(i == 0)
        def _():
            acc1024[...] = jnp.zeros_like(acc1024)
            dws_acc[...] = jnp.zeros_like(dws_acc)
            dbs_acc[...] = jnp.zeros_like(dbs_acc)

        dy = dy_ref[...]
        mh, rm = _rms_fwd(m_ref[...])
        dm, dpost = _rms_bwd(dy, mh, rm, post_ref[1:2, :])
        dmb = dm.astype(BF16)
        dmb_ref[...] = dmb
        dyv = _nt(dmb, wout_ref[...])
        t = _odd_mix(proj_ref, lng, _shard_vector(shard_ref, 16), ws_ref, bias_ref, sv_ref)
        u, z, sz, sv = t["u"], t["z"], t["sz"], sv_ref[...]
        dproj_ref[:, 0:D] = (dyv * sv * (z * sz)).astype(BF16)
        dproj_ref[:, 2 * D:3 * D] = (dyv * u * sv * (sz * (1.0 + z * (1.0 - sz)))).astype(BF16)
        dsv = dyv * u * (z * sz)
        dsvb = dsv.astype(BF16)
        for h in range(HEADS):
            dsv_h = _chunks_side_by_side(dsvb, h)
            dvn_h = _nn(wst_ref[h], dsv_h)
            dws_acc[h] += _nt(dsv_h, _chunks_side_by_side(t["vnb"], h))
            for n in range(tm // CHUNK):
                rows, cols = slice(n * CHUNK, (n + 1) * CHUNK), slice(h * GD, (h + 1) * GD)
                dvn_ref[rows, cols] = dvn_h[:, n * GD:(n + 1) * GD]
                dbs_acc[h] += dsv[rows, cols]
        dvn = dvn_ref[...]
        vh = t["vh"]
        dvh = dvn * lng
        dv = t["rs"] * (dvh - jnp.mean(dvh, axis=-1, keepdims=True) - vh * jnp.mean(dvh * vh, axis=-1, keepdims=True))
        dproj_ref[:, D:2 * D] = dv.astype(BF16)
        dx_ref[...], dpre = _proj_bwd(dproj_ref[...], win_ref, x_ref[...], dy, pre_ref[1:2, :])
        acc1024[0:1, :] += dpre
        acc1024[1:2, :] += dpost
        acc1024[2:3, :] += jnp.sum(dvn * vh, axis=0, keepdims=True)
        acc1024[3:4, :] += jnp.sum(dvn, axis=0, keepdims=True)

        @pl.when(i == nt - 1)
        def _():
            small_ref[...] = jnp.zeros_like(small_ref)
            _store_rows(small_ref, S1_PRE, acc1024[0:1, :])
            _store_rows(small_ref, S1_POST, acc1024[1:2, :])
            for q in range(4):
                _store_rows(small_ref, S1_LN + 8 * q, acc1024[2:3, 2 * q * GD:(2 * q + 2) * GD])
                _store_rows(small_ref, S1_LN + 8 * q + 2, acc1024[3:4, 2 * q * GD:(2 * q + 2) * GD])
            lower = lax.broadcasted_iota(jnp.int32, (CHUNK, CHUNK), 0) >= lax.broadcasted_iota(jnp.int32, (CHUNK, CHUNK), 1)
            for h in range(HEADS):
                small_ref[S1_WS + h * CHUNK:S1_WS + (h + 1) * CHUNK, :] = jnp.where(lower, dws_acc[h], 0.0)
                small_ref[S1_BS + h:S1_BS + h + 1, :] = jnp.sum(dbs_acc[h].T, axis=0, keepdims=True)
            small_ref[S1_LOSS:S1_LOSS + 8, :] = loss_ref[...]

    return pl.pallas_call(
        body, name="odd_bwd", grid=(nt,),
        in_specs=[_rows(tm, D), _rows(tm, D), _rows(tm, W3), _rows(tm, D), _full((8, GD)), _full((2, D)), _full((2, D)),
                  _full((4, D, QW)), _full((4, SHARD_ROWS, GD)), _full((HEADS, CHUNK, CHUNK)), _full((HEADS, CHUNK, CHUNK)),
                  _full((HEADS, CHUNK, GD)), _full((D, D))],
        out_specs=[_rows(tm, D), _rows(tm, W3), _rows(tm, D), _full_out((S1_ROWS, GD))],
        out_shape=[jax.ShapeDtypeStruct((S, D), F32), jax.ShapeDtypeStruct((S, W3), BF16), jax.ShapeDtypeStruct((S, D), BF16),
                   jax.ShapeDtypeStruct((S1_ROWS, GD), F32)],
        scratch_shapes=[pltpu.VMEM((tm, D), F32), pltpu.VMEM((tm, D), F32), pltpu.VMEM((8, D), F32),
                        pltpu.VMEM((HEADS, CHUNK, CHUNK), F32), pltpu.VMEM((HEADS, CHUNK, GD), F32)],
        compiler_params=_params(),
    )(dx2, x1, proj, m, loss, pre, post, win, shard, wsb, wsbt, bias, wout)


def _even_bwd(dx1, x, proj, conv, mixed, pooled, m, pre, post, win, shard, pwb, ps, wout):
    S = x.shape[0]
    tm = TM_BWD
    nt = S // tm
    L = tm + HALO

    def rev(i):
        return (nt - 1 - i, 0)

    def body(dy_ref, x_ref, proj_ref, conv_ref, mixed_ref, pooled_ref, m_ref, pre_ref, post_ref, win_ref, shard_ref, pw_ref, ps_ref,
             wout_ref, dx_ref, dproj_ref, dmb_ref, small_ref, dconv_ext, q_ext, acc1024, acc512, dpw_acc):
        i = pl.program_id(0)
        tile = nt - 1 - i

        @pl.when(i == 0)
        def _():
            dconv_ext[tm:L, :] = jnp.zeros((HALO, AW), F32)
            q_ext[tm:L, :] = jnp.zeros((HALO, AW), F32)
            acc1024[...] = jnp.zeros_like(acc1024)
            acc512[...] = jnp.zeros_like(acc512)
            dpw_acc[...] = jnp.zeros_like(dpw_acc)

        dy = dy_ref[...]
        mh, rm = _rms_fwd(m_ref[...])
        dm, dpost = _rms_bwd(dy, mh, rm, post_ref[0:1, :])
        dmb = dm.astype(BF16)
        dmb_ref[...] = dmb
        dmix = _nt(dmb, wout_ref[...])
        dya, dyb = dmix[:, 0:AW], dmix[:, AW:2 * AW]
        xa, gb, gc, za = (proj_ref[:, k * AW:(k + 1) * AW] for k in range(4))
        zp = proj_ref[:, 5 * AW:6 * AW]
        hc = gc * xa
        conv = conv_ref[...]
        sa = _sigmoid(za)
        silu_a = za * sa
        dproj_ref[:, AW:2 * AW] = (dya * conv * silu_a).astype(BF16)
        dproj_ref[:, 3 * AW:4 * AW] = (dya * gb * conv * (sa * (1.0 + za * (1.0 - sa)))).astype(BF16)
        dconv = dya * gb * silu_a
        dconv_ext[0:tm, :] = dconv
        e = dconv_ext[...]
        dc1 = pltpu.roll(e, L - 1, 0)[0:tm]
        dc2 = pltpu.roll(e, L - 2, 0)[0:tm]
        taps = [_shard_row(shard_ref, k) for k in range(3)]
        dhc = taps[2] * dconv + taps[1] * dc1 + taps[0] * dc2
        dproj_ref[:, 0:AW] = (dhc * gc).astype(BF16)
        dproj_ref[:, 2 * AW:3 * AW] = (dhc * xa).astype(BF16)
        acc512[0:1, :] += jnp.sum(dc2 * hc, axis=0, keepdims=True)
        acc512[1:2, :] += jnp.sum(dc1 * hc, axis=0, keepdims=True)
        acc512[2:3, :] += jnp.sum(dconv * hc, axis=0, keepdims=True)

        sb, mixed = _sigmoid(zp), mixed_ref[...]
        silu_b = zp * sb
        acc512[3:4, :] += jnp.sum(dyb * mixed * silu_b, axis=0, keepdims=True)
        dmixedb = (dyb * ps_ref[...] * silu_b).astype(BF16)
        dproj_ref[:, 5 * AW:6 * AW] = (dyb * mixed * ps_ref[...] * (sb * (1.0 + zp * (1.0 - sb)))).astype(BF16)
        pos = tile * tm + lax.broadcasted_iota(jnp.int32, (tm, 1), 0)
        for g, w in enumerate(POOL_WINDOWS):
            cols = slice(g * GD, (g + 1) * GD)
            dpw_acc[g] += _tn(pooled_ref[:, cols], dmixedb[:, cols])
            dpooled = _nt(dmixedb[:, cols], pw_ref[g])
            q_ext[0:tm, cols] = dpooled / jnp.minimum(pos + 1, w).astype(F32)
            s = q_ext[:, cols]
            for k in range(g + 1):
                s = s + pltpu.roll(s, L - 2 ** k, 0)
            dproj_ref[:, 4 * AW + g * GD:4 * AW + (g + 1) * GD] = (s[0:tm] - dpooled).astype(BF16)
        dconv_ext[tm:L, :] = dconv_ext[0:HALO, :]
        q_ext[tm:L, :] = q_ext[0:HALO, :]

        dx_ref[...], dpre = _proj_bwd(dproj_ref[...], win_ref, x_ref[...], dy, pre_ref[0:1, :])
        acc1024[0:1, :] += dpre
        acc1024[1:2, :] += dpost

        @pl.when(i == nt - 1)
        def _():
            small_ref[...] = jnp.zeros_like(small_ref)
            _store_rows(small_ref, S0_PRE, acc1024[0:1, :])
            _store_rows(small_ref, S0_POST, acc1024[1:2, :])
            for q in range(4):
                for k in range(3):
                    small_ref[S0_CONV + 8 * q + k:S0_CONV + 8 * q + k + 1, :] = acc512[k:k + 1, q * GD:(q + 1) * GD]
            _store_rows(small_ref, S0_PS, acc512[3:4, :])
            for g in range(4):
                small_ref[S0_PW + g * GD:S0_PW + (g + 1) * GD, :] = dpw_acc[g]

    return pl.pallas_call(
        body, name="even_bwd", grid=(nt,),
        in_specs=[_rows(tm, D, rev), _rows(tm, D, rev), _rows(tm, W3, rev), _rows(tm, AW, rev), _rows(tm, AW, rev), _rows(tm, AW, rev),
                  _rows(tm, D, rev),
                  _full((2, D)), _full((2, D)), _full((4, D, QW)), _full((4, SHARD_ROWS, GD)), _full((4, GD, GD)), _full((1, AW)),
                  _full((D, D))],
        out_specs=[_rows(tm, D, rev), _rows(tm, W3, rev), _rows(tm, D, rev), _full_out((S0_ROWS, GD))],
        out_shape=[jax.ShapeDtypeStruct((S, D), F32), jax.ShapeDtypeStruct((S, W3), BF16), jax.ShapeDtypeStruct((S, D), BF16),
                   jax.ShapeDtypeStruct((S0_ROWS, GD), F32)],
        scratch_shapes=[pltpu.VMEM((L, AW), F32), pltpu.VMEM((L, AW), F32),
                        pltpu.VMEM((8, D), F32), pltpu.VMEM((8, AW), F32), pltpu.VMEM((4, GD, GD), F32)],
        compiler_params=_params(),
    )(dx1, x, proj, conv, mixed, pooled, m, pre, post, win, shard, pwb, ps, wout)


def _owner_id(me, relation, c):
    q = jnp.bitwise_xor(me, relation)
    return (q // 2, q % 2, c)


def _small_gather_steps(small_ref, all_ref, stage, send_sems, recv_sems, local_sem):
    x, y, c = _position()
    chips = _chips(x, y)

    def slot(chip, core):
        return 4 * chip[0] + 2 * chip[1] + core

    def copy(k, src, block, to):
        return pltpu.make_async_remote_copy(src_ref=src, dst_ref=all_ref.at[block], send_sem=send_sems.at[k],
                                            recv_sem=recv_sems.at[k], device_id=to, device_id_type=MESH)

    def own_copy():
        return pltpu.make_async_copy(stage, all_ref.at[slot((x, y), c)], local_sem)

    def first_sends():
        mine = slot((x, y), c)
        return [copy(0, small_ref, mine, (x, y, 1 - c))] + [copy(1 + j, small_ref, mine, (*chip, c)) for j, chip in enumerate(chips)]

    def relays():
        return [copy(4 + j, all_ref.at[slot(chip, c)], slot(chip, c), (x, y, 1 - c)) for j, chip in enumerate(chips)]

    def start():
        for cp in first_sends():
            cp.start()
        load = pltpu.make_async_copy(small_ref, stage, local_sem)
        load.start()
        load.wait()
        own_copy().start()

    def relay():
        for j, chip in enumerate(chips):
            copy(1 + j, small_ref, slot(chip, c), (*chip, c)).wait_recv()
        for cp in relays():
            cp.start()

    def finish():
        copy(0, small_ref, slot((x, y), 1 - c), (x, y, 1 - c)).wait_recv()
        for j, chip in enumerate(chips):
            copy(4 + j, small_ref, slot(chip, 1 - c), (x, y, 1 - c)).wait_recv()
        for cp in first_sends() + relays():
            cp.wait_send()
        own_copy().wait()

    return start, relay, finish


def _wgrad_layer(a_out, b_out, a_in, b_in, small, pos, name):
    S = a_in.shape[0]
    nko = S // TK
    nki = S // TK_IN
    hm = D // 2
    qr = hm // 4

    def out_index(s, pos_ref):
        return (jnp.minimum(s, nko - 1), 0)

    def a_in_index(s, pos_ref):
        return (jnp.where(s >= nko, (s - nko) % nki, 0), 0)

    def b_in_index(s, pos_ref):
        return (jnp.where(s >= nko, (s - nko) % nki, 0), jnp.bitwise_xor(pos_ref[1], 3 - jnp.maximum((s - nko) // nki, 0)))

    def body(pos_ref, ao_ref, bo_ref, a_ref, b_ref, small_ref, gout_ref, gin_ref, all_ref,
             acc, rbuf, sbuf, arr, mine, acc_o, rbuf_o, total_o, sbuf_o, arr_o, mine_o, stage,
             d2d_send, d2d_recv, ici_send, ici_recv, d2d_o_send, d2d_o_recv, ici_o_send, ici_o_recv,
             share_send, share_recv, local_sems, g_send, g_recv, g_local):
        s = pl.program_id(0)
        in_step = jnp.maximum(s - nko, 0)
        blk = jnp.where(s < nko, 0, 1 + in_step // nki)
        k = jnp.where(s < nko, s, in_step % nki)
        j = blk - 1
        x, y, c = _position()
        me = 2 * x + y
        sibling = (x, y, 1 - c)
        last = k == jnp.where(s < nko, nko - 1, nki - 1)
        gather_start, gather_relay, gather_finish = _small_gather_steps(small_ref, all_ref, stage, g_send, g_recv, g_local)

        def other_half(ref):
            return ref.at[pl.ds(pl.multiple_of((1 - c) * hm, hm), hm), :]

        def own_half(ref):
            return ref[pl.ds(pl.multiple_of(c * hm, hm), hm), :]

        def to_sibling(jj):
            return pltpu.make_async_remote_copy(
                src_ref=other_half(acc.at[jj % 2]), dst_ref=rbuf.at[jj], send_sem=d2d_send.at[jj], recv_sem=d2d_recv.at[jj],
                device_id=sibling, device_id_type=MESH)

        def to_owner(jj):
            return pltpu.make_async_remote_copy(
                src_ref=sbuf.at[jj], dst_ref=arr.at[2 - jj], send_sem=ici_send.at[jj], recv_sem=ici_recv.at[jj],
                device_id=_owner_id(me, 3 - jj, c), device_id_type=MESH)

        def out_to_sibling():
            return pltpu.make_async_remote_copy(
                src_ref=other_half(acc_o), dst_ref=rbuf_o, send_sem=d2d_o_send, recv_sem=d2d_o_recv,
                device_id=sibling, device_id_type=MESH)

        def out_to_owner(r):
            return pltpu.make_async_remote_copy(
                src_ref=sbuf_o.at[r], dst_ref=arr_o.at[r], send_sem=ici_o_send.at[r], recv_sem=ici_o_recv.at[r],
                device_id=_owner_id(me, r + 1, c), device_id_type=MESH)

        def pair_sum(jj):
            to_sibling(jj).wait_recv()
            return own_half(acc.at[jj % 2]) + rbuf[jj]

        def send_block(jj):
            sbuf[jj] = pair_sum(jj).astype(BF16)
            to_owner(jj).start()

        @pl.when(s == 0)
        def _():
            gather_start()

        @pl.when((blk == 3) & (k == 0))
        def _():
            gather_relay()

        @pl.when((blk == 0) & (k == 0))
        def _():
            acc_o[...] = jnp.zeros((D, D), F32)

        @pl.when(blk == 0)
        def _():
            acc_o[...] += _tn(ao_ref[...], bo_ref[...])

        @pl.when((blk == 0) & last)
        def _():
            out_to_sibling().start()

        @pl.when((blk >= 3) & (k == 0))
        def _():
            to_sibling(j - 2).wait_send()

        @pl.when((blk >= 1) & (k == 0))
        def _():
            acc[j % 2] = jnp.zeros((D, QW), F32)

        @pl.when(blk >= 1)
        def _():
            acc[j % 2] += _tn(a_ref[...], b_ref[...])

        @pl.when((blk >= 1) & last)
        def _():
            to_sibling(j).start()

        @pl.when((blk == 1) & last)
        def _():
            out_to_sibling().wait_recv()
            total_o[...] = own_half(acc_o) + rbuf_o[...]
            for r in range(3):
                q = jnp.bitwise_xor(me, r + 1)
                sbuf_o[r] = total_o[pl.ds(pl.multiple_of(q * qr, qr), qr), :].astype(BF16)
                out_to_owner(r).start()

        @pl.when((blk == 2) & last)
        def _():
            send_block(0)

        @pl.when((blk == 3) & last)
        def _():
            send_block(1)
            send_block(2)

        @pl.when((blk == 4) & last)
        def _():
            g_in = pair_sum(3)
            g_out = total_o[pl.ds(pl.multiple_of(me * qr, qr), qr), :]
            to_sibling(2).wait_send()
            to_sibling(3).wait_send()
            out_to_sibling().wait_send()
            for r in range(3):
                to_owner(r).wait()
                out_to_owner(r).wait()
            for r in range(3):
                g_in = g_in + arr[r].astype(F32)
                g_out = g_out + arr_o[r].astype(F32)
            mine[...] = g_in
            mine_o[...] = g_out
            copies = []
            for idx, (src, dst) in enumerate([(mine, gin_ref), (mine_o, gout_ref)]):
                copies.append(pltpu.make_async_remote_copy(
                    src_ref=src, dst_ref=dst.at[c], send_sem=share_send.at[idx], recv_sem=share_recv.at[idx],
                    device_id=sibling, device_id_type=MESH))
                copies.append(pltpu.make_async_copy(src, dst.at[c], local_sems.at[idx]))
            for cp in copies:
                cp.start()
            for cp in copies:
                cp.wait()
            gather_finish()

    any_spec = pl.BlockSpec(memory_space=pl.ANY)
    grid_spec = pltpu.PrefetchScalarGridSpec(
        num_scalar_prefetch=1, grid=(nko + 4 * nki,),
        in_specs=[pl.BlockSpec((TK, D), out_index), pl.BlockSpec((TK, D), out_index),
                  pl.BlockSpec((TK_IN, D), a_in_index), pl.BlockSpec((TK_IN, QW), b_in_index), any_spec],
        out_specs=[any_spec, any_spec, any_spec],
        scratch_shapes=[pltpu.VMEM((2, D, QW), F32), pltpu.VMEM((4, hm, QW), F32), pltpu.VMEM((3, hm, QW), BF16),
                        pltpu.VMEM((3, hm, QW), BF16), pltpu.VMEM((hm, QW), F32),
                        pltpu.VMEM((D, D), F32), pltpu.VMEM((hm, D), F32), pltpu.VMEM((hm, D), F32), pltpu.VMEM((3, qr, D), BF16),
                        pltpu.VMEM((3, qr, D), BF16), pltpu.VMEM((qr, D), F32),
                        pltpu.VMEM(small.shape, F32),
                        pltpu.SemaphoreType.DMA((4,)), pltpu.SemaphoreType.DMA((4,)),
                        pltpu.SemaphoreType.DMA((3,)), pltpu.SemaphoreType.DMA((3,)),
                        pltpu.SemaphoreType.DMA, pltpu.SemaphoreType.DMA,
                        pltpu.SemaphoreType.DMA((3,)), pltpu.SemaphoreType.DMA((3,)),
                        pltpu.SemaphoreType.DMA((2,)), pltpu.SemaphoreType.DMA((2,)), pltpu.SemaphoreType.DMA((2,)),
                        pltpu.SemaphoreType.DMA((7,)), pltpu.SemaphoreType.DMA((7,)), pltpu.SemaphoreType.DMA])
    return pl.pallas_call(
        body, name=name, grid_spec=grid_spec,
        out_shape=[jax.ShapeDtypeStruct((2, qr, D), F32), jax.ShapeDtypeStruct((2, hm, QW), F32),
                   jax.ShapeDtypeStruct((8, *small.shape), F32)],
        compiler_params=pltpu.CompilerParams(dimension_semantics=("arbitrary",), vmem_limit_bytes=VMEM_LIMIT),
    )(pos, a_out, b_out, a_in, b_in, small)


def _chips(x, y):
    return [(1 - x, y), (x, 1 - y), (1 - x, 1 - y)]


def _gather_weights(parts, split, x, pre, next_parts, pool_w, w_s, b_s):
    n = len(parts)
    nn = len(next_parts)
    S = x.shape[0]
    tm = TM_FWD
    nt = S // tm
    staged = [a for a in range(n) if split[a]]

    ns = nn + 3

    def body(x_ref, pre_ref, *refs):
        raw, refs = refs[:n], refs[n:]
        side_hbm, hb_ref = refs[:ns], refs[ns]
        refs = refs[ns + 1:]
        outs, refs = refs[:n], refs[n:]
        next_outs, poolb_ref, tril_ref, trilt_ref, bias_ref = refs[:nn], refs[nn], refs[nn + 1], refs[nn + 2], refs[nn + 3]
        refs = refs[nn + 4:]
        stage, refs = refs[:len(staged)], refs[len(staged):]
        side, (send_sems, recv_sems, local_sems, side_sems) = refs[:ns], refs[ns:]
        next_raw, pool_ref, ws_ref, bs_ref = side[:nn], side[nn], side[nn + 1], side[nn + 2]
        ins = [stage[staged.index(a)] if split[a] else raw[a] for a in range(n)]
        i = pl.program_id(0)

        def fetch(k):
            return pltpu.make_async_copy(side_hbm[k], side[k], side_sems.at[k])

        x, y, c = _position()
        me = 2 * x + y
        nbr_x, nbr_y, diag = _chips(x, y)
        id_x, id_y, id_d = (2 * chip[0] + chip[1] for chip in (nbr_x, nbr_y, diag))
        sibling = (x, y, 1 - c)

        def rows(a, ref, who, piece=None):
            r = parts[a].shape[0] // 2
            if piece is None:
                return ref.at[pl.ds(pl.multiple_of(who * r, 16), r), :]
            return ref.at[pl.ds(pl.multiple_of(who * r + piece * (r // 2), 16), r // 2), :]

        def copy(a, k, src, dst, to):
            return pltpu.make_async_remote_copy(src_ref=src, dst_ref=dst, send_sem=send_sems.at[a, k], recv_sem=recv_sems.at[a, k],
                                                device_id=to, device_id_type=MESH)

        def direct(a, k, chip):
            if split[a]:
                return copy(a, k, rows(a, ins[a], c), rows(a, outs[a].at[me], c), (*chip, c))
            return copy(a, k, ins[a], outs[a].at[me], (*chip, c))

        def arrival(a, k, src_id):
            if split[a]:
                return copy(a, k, rows(a, ins[a], c), rows(a, outs[a].at[src_id], c), (*nbr_x, c))
            return copy(a, k, ins[a], outs[a].at[src_id], (*nbr_x, c))

        def pass_on(a, k, src_id, piece, chip):
            region = rows(a, outs[a].at[src_id], c, piece)
            return copy(a, k, region, region, (*chip, c))

        def hand_over(a, k, src_id, who):
            region = rows(a, outs[a].at[src_id], who)
            return copy(a, k, region, region, sibling)

        def local(a):
            return pltpu.make_async_copy(ins[a], outs[a].at[me], local_sems.at[a])

        def first_sends(a):
            return [direct(a, 0, nbr_x), direct(a, 1, nbr_y)] + ([] if split[a] else [direct(a, 2, diag)])

        def after_x(a):
            return [pass_on(a, 3, id_x, 1, nbr_y), hand_over(a, 4, id_x, c)] if split[a] else []

        def after_y(a):
            return [pass_on(a, 2, id_y, 0, nbr_x), hand_over(a, 5, id_y, c)] if split[a] else []

        def after_diag(a):
            return [hand_over(a, 6, id_d, c)] if split[a] else []

        @pl.when(i == 0)
        def _():
            for a in staged:
                ins[a][...] = raw[a][...].astype(BF16)
            for a in range(n):
                local(a).start()
                for cp in first_sends(a):
                    cp.start()
            for k in range(ns):
                fetch(k).start()

        @pl.when(i == 1)
        def _():
            for k in range(ns):
                fetch(k).wait()
            for a in range(nn):
                next_outs[a][...] = next_raw[a][...].astype(BF16)
            poolb_ref[...] = pool_ref[...].astype(BF16)
            lower = lax.broadcasted_iota(jnp.int32, (CHUNK, CHUNK), 0) >= lax.broadcasted_iota(jnp.int32, (CHUNK, CHUNK), 1)
            for h in range(HEADS):
                tril = jnp.where(lower, ws_ref[h], 0.0)
                tril_ref[h] = tril.astype(BF16)
                trilt_ref[h] = tril.T.astype(BF16)
                bias_ref[h] = jnp.broadcast_to(bs_ref[h:h + 1, :], (CHUNK, GD)).T

        xh, _ = _rms_fwd(x_ref[...])
        hb_ref[...] = (xh * pre_ref[0:1, :]).astype(BF16)

        @pl.when(i == nt - 2)
        def _():
            for a in range(n):
                arrival(a, 0, id_x).wait_recv()
                for cp in after_x(a):
                    cp.start()
            for a in range(n):
                arrival(a, 1, id_y).wait_recv()
                for cp in after_y(a):
                    cp.start()

        @pl.when(i == nt - 1)
        def _():
            for a in range(n):
                if split[a]:
                    pass_on(a, 2, id_d, 0, nbr_x).wait_recv()
                    pass_on(a, 3, id_d, 1, nbr_y).wait_recv()
                    for cp in after_diag(a):
                        cp.start()
                else:
                    arrival(a, 2, id_d).wait_recv()
            for a in range(n):
                if split[a]:
                    for k, src_id in ((4, id_x), (5, id_y), (6, id_d)):
                        hand_over(a, k, src_id, 1 - c).wait_recv()
            for a in range(n):
                for cp in first_sends(a) + after_x(a) + after_y(a) + after_diag(a):
                    cp.wait_send()
                local(a).wait()

    any_spec = pl.BlockSpec(memory_space=pl.ANY)
    vmem = pl.BlockSpec(memory_space=pltpu.VMEM)
    sent = [BF16 if split[a] else parts[a].dtype for a in range(n)]
    return pl.pallas_call(
        body, name="gather_weights", grid=(nt,),
        in_specs=[_rows(tm, D), _full((2, D))] + [vmem] * n + [any_spec] * ns,
        out_specs=[_rows(tm, D)] + [any_spec] * n + [vmem] * (nn + 4),
        out_shape=[jax.ShapeDtypeStruct((S, D), BF16)] + [jax.ShapeDtypeStruct((4, *p.shape), t) for p, t in zip(parts, sent)]
        + [jax.ShapeDtypeStruct(p.shape, BF16) for p in next_parts]
        + [jax.ShapeDtypeStruct(pool_w.shape, BF16), jax.ShapeDtypeStruct(w_s.shape, BF16), jax.ShapeDtypeStruct(w_s.shape, BF16),
           jax.ShapeDtypeStruct((HEADS, CHUNK, GD), F32)],
        scratch_shapes=[pltpu.VMEM(parts[a].shape, BF16) for a in staged]
        + [pltpu.VMEM(p.shape, F32) for p in (*next_parts, pool_w, w_s, b_s)]
        + [pltpu.SemaphoreType.DMA((n, 7)), pltpu.SemaphoreType.DMA((n, 7)), pltpu.SemaphoreType.DMA((n,)),
           pltpu.SemaphoreType.DMA((ns,))],
        compiler_params=_params(),
    )(x, pre, *parts, *next_parts, pool_w, w_s, b_s)


def _adamw(w, g, m, v):
    m = ADAM_B1 * m + (1.0 - ADAM_B1) * g
    v = ADAM_B2 * v + (1.0 - ADAM_B2) * (g * g)
    m_hat = m / (1.0 - ADAM_B1 ** ADAM_STEP)
    v_hat = v / (1.0 - ADAM_B2 ** ADAM_STEP)
    delta = -ADAM_LR * (m_hat / (jnp.sqrt(v_hat) + ADAM_EPS) + ADAM_WD * w)
    return delta, m, v


def _adamw_big(ws, gs, ms, vs):
    steps = 4
    n = len(ws)

    def body(*refs):
        ins, outs = refs[:4 * n], refs[4 * n:]
        for a in range(n):
            w_ref, g_ref, m_ref, v_ref = ins[4 * a:4 * a + 4]
            go_ref, d_ref, mo_ref, vo_ref = outs[4 * a:4 * a + 4]
            gv = g_ref[...]
            go_ref[...] = gv
            d_ref[...], mo_ref[...], vo_ref[...] = _adamw(w_ref[...], gv, m_ref[...], v_ref[...])

    specs, shapes, operands = [], [], []
    for w, g, m, v in zip(ws, gs, ms, vs):
        rows, cols = w.shape
        specs += [pl.BlockSpec((rows // steps, cols), lambda i: (i, 0))] * 4
        shapes += [jax.ShapeDtypeStruct((rows, cols), F32)] * 4
        operands += [w, g, m, v]
    res = pl.pallas_call(
        body, name="adamw_big", grid=(steps,),
        in_specs=specs, out_specs=specs, out_shape=shapes,
        compiler_params=pltpu.CompilerParams(dimension_semantics=("arbitrary",), vmem_limit_bytes=VMEM_LIMIT),
    )(*operands)
    return [res[4 * a:4 * a + 4] for a in range(n)]


def _adamw_small(g0, g1, weights, moms, vels):
    names = ["pre", "post", "conv", "pw", "ps", "lng", "lnb", "ws", "bs"]
    shapes = [w.shape for w in weights]

    def body(*refs):
        me = 2 * lax.axis_index("x") + lax.axis_index("y")
        g0_ref, g1_ref = refs[0], refs[1]
        w_refs, m_refs, v_refs = refs[2:11], refs[11:20], refs[20:29]
        outs = refs[29:29 + 36]
        loss_ref = refs[65]
        t0_ref, t1_ref = refs[66], refs[67]
        t0 = g0_ref[0]
        t1 = g1_ref[0]
        for d in range(1, 8):
            t0 = t0 + g0_ref[d]
            t1 = t1 + g1_ref[d]
        t0_ref[...] = t0
        t1_ref[...] = t1
        loss_ref[...] = t1_ref[S1_LOSS:S1_LOSS + 1, 0:1]
        my_conv = pl.multiple_of(S0_CONV + 8 * me, 8)
        my_ln = pl.multiple_of(S1_LN + 8 * me, 8)

        def update(idx, piece, grad):
            go, do, mo, vo = outs[4 * idx:4 * idx + 4]
            go[piece] = grad
            do[piece], mo[piece], vo[piece] = _adamw(w_refs[idx][piece], grad, m_refs[idx][piece], v_refs[idx][piece])

        for layer in range(2):
            for k in range(D // GD):
                lanes = slice(k * GD, (k + 1) * GD)
                tref, pre0, post0 = (t0_ref, S0_PRE, S0_POST) if layer == 0 else (t1_ref, S1_PRE, S1_POST)
                update(0, (slice(layer, layer + 1), lanes), tref[pre0 + k:pre0 + k + 1, :])
                update(1, (slice(layer, layer + 1), lanes), tref[post0 + k:post0 + k + 1, :])
        conv_rows = t0_ref[pl.ds(my_conv, 8), :]
        update(2, (slice(0, 3), 0, slice(None)), conv_rows[0:3, :])
        for g in range(4):
            update(3, (g,), t0_ref[S0_PW + g * GD:S0_PW + (g + 1) * GD, :])
            update(4, (slice(0, 1), slice(g * GD, (g + 1) * GD)), t0_ref[S0_PS + g:S0_PS + g + 1, :])
        ln_rows = t1_ref[pl.ds(my_ln, 8), :]
        for k in range(2):
            update(5, (slice(0, 1), slice(k * GD, (k + 1) * GD)), ln_rows[k:k + 1, :])
            update(6, (slice(0, 1), slice(k * GD, (k + 1) * GD)), ln_rows[2 + k:3 + k, :])
        for h in range(HEADS):
            update(7, (h,), t1_ref[S1_WS + h * CHUNK:S1_WS + (h + 1) * CHUNK, :])
        update(8, (slice(None), slice(None)), t1_ref[S1_BS:S1_BS + HEADS, :])

    vm = pl.BlockSpec(memory_space=pltpu.VMEM)
    out_shape = []
    for s in shapes:
        out_shape += [jax.ShapeDtypeStruct(s, F32)] * 4
    out_shape.append(jax.ShapeDtypeStruct((1, 1), F32))
    res = pl.pallas_call(
        body, name="adamw_small",
        in_specs=[vm] * 29, out_specs=[vm] * 37, out_shape=out_shape,
        scratch_shapes=[pltpu.VMEM((S0_ROWS, GD), F32), pltpu.VMEM((S1_ROWS, GD), F32)],
        compiler_params=pltpu.CompilerParams(vmem_limit_bytes=VMEM_LIMIT),
    )(g0, g1, *weights, *moms, *vels)
    per_weight = {nm: res[4 * i:4 * i + 4] for i, nm in enumerate(names)}
    return per_weight, res[36]


def _pad8(a):
    return jnp.pad(a, ((0, 8 - a.shape[0]), (0, 0)))


def kernel(x, pre_norm, post_norm, even_w_in, even_conv_w, even_pool_w, even_pool_scale, even_w_out, odd_w_in, odd_ln_g, odd_ln_b, odd_w_s, odd_b_s, odd_w_out, loss_target, m_pre_norm, m_post_norm, m_even_w_in, m_even_conv_w, m_even_pool_w, m_even_pool_scale, m_even_w_out, m_odd_w_in, m_odd_ln_g, m_odd_ln_b, m_odd_w_s, m_odd_b_s, m_odd_w_out, v_pre_norm, v_post_norm, v_even_w_in, v_even_conv_w, v_even_pool_w, v_even_pool_scale, v_even_w_out, v_odd_w_in, v_odd_ln_g, v_odd_ln_b, v_odd_w_s, v_odd_b_s, v_odd_w_out):
    xs = x[0]
    tgt = loss_target[0]

    small_shard = jnp.concatenate([_pad8(even_conv_w[0]), _pad8(odd_ln_g.reshape(2, GD)), _pad8(odd_ln_b.reshape(2, GD))], axis=0)
    hb0, win0, wout0, shard, win1_shard, wout1_shard, pool_wb, ws_tril, ws_tril_t, bias = _gather_weights(
        [even_w_in[0], even_w_out[0], small_shard], [True, True, False], xs, pre_norm, [odd_w_in[0], odd_w_out[0]],
        even_pool_w[0], odd_w_s[0], odd_b_s[0])
    wout0 = wout0.reshape(D, D)
    px, py, pc = _position()
    pos = jnp.stack([pc, 2 * px + py]).astype(jnp.int32)

    x1, proj0, m0, mixp0, conv0, mixed0, pooled0, win1, wout1 = _even_fwd(
        xs, hb0, post_norm, win0, shard, pool_wb, even_pool_scale, wout0, [win1_shard, wout1_shard])
    wout1 = wout1.reshape(D, D)
    proj1, m1, hb1, yp1, dx2, loss_part = _odd_fwd(x1, tgt, pre_norm, post_norm, win1, shard, ws_tril, bias, wout1)
    dx1, dproj1, dmb1, small1 = _odd_bwd(dx2, x1, proj1, m1, loss_part, pre_norm, post_norm, win1, shard, ws_tril, ws_tril_t, bias,
                                         wout1)
    g_out1, g_in1, all1 = _wgrad_layer(yp1, dmb1, hb1, dproj1, small1, pos, "wgrad_odd")
    gx, dproj0, dmb0, small0 = _even_bwd(dx1, xs, proj0, conv0, mixed0, pooled0, m0, pre_norm, post_norm, win0, shard, pool_wb,
                                         even_pool_scale, wout0)
    g_out0, g_in0, all0 = _wgrad_layer(mixp0, dmb0, hb0, dproj0, small0, pos, "wgrad_even")

    big_w = [even_w_in[0], even_w_out[0], odd_w_in[0], odd_w_out[0]]
    big_g = [g.reshape(w.shape) for g, w in zip([g_in0, g_out0, g_in1, g_out1], big_w)]
    big_m = [m_even_w_in[0], m_even_w_out[0], m_odd_w_in[0], m_odd_w_out[0]]
    big_v = [v_even_w_in[0], v_even_w_out[0], v_odd_w_in[0], v_odd_w_out[0]]
    big = _adamw_big(big_w, big_g, big_m, big_v)

    def taps_first(a):
        return jnp.swapaxes(a, 0, 1)

    small_w = [pre_norm, post_norm, taps_first(even_conv_w), even_pool_w[0], even_pool_scale, odd_ln_g, odd_ln_b, odd_w_s[0], odd_b_s[0]]
    small_m = [m_pre_norm, m_post_norm, taps_first(m_even_conv_w), m_even_pool_w[0], m_even_pool_scale, m_odd_ln_g, m_odd_ln_b,
               m_odd_w_s[0], m_odd_b_s[0]]
    small_v = [v_pre_norm, v_post_norm, taps_first(v_even_conv_w), v_even_pool_w[0], v_even_pool_scale, v_odd_ln_g, v_odd_ln_b,
               v_odd_w_s[0], v_odd_b_s[0]]
    sm, loss = _adamw_small(all0, all1, small_w, small_m, small_v)

    def lead(a):
        return a[None]

    per = {
        "pre_norm": sm["pre"], "post_norm": sm["post"],
        "even_w_in": [lead(a) for a in big[0]], "even_conv_w": [taps_first(a) for a in sm["conv"]],
        "even_pool_w": [lead(a) for a in sm["pw"]], "even_pool_scale": sm["ps"],
        "even_w_out": [lead(a) for a in big[1]], "odd_w_in": [lead(a) for a in big[2]],
        "odd_ln_g": sm["lng"], "odd_ln_b": sm["lnb"],
        "odd_w_s": [lead(a) for a in sm["ws"]], "odd_b_s": [lead(a) for a in sm["bs"]],
        "odd_w_out": [lead(a) for a in big[3]],
    }
    order = ["pre_norm", "post_norm", "even_w_in", "even_conv_w", "even_pool_w", "even_pool_scale", "even_w_out", "odd_w_in",
             "odd_ln_g", "odd_ln_b", "odd_w_s", "odd_b_s", "odd_w_out"]
    outs = [loss.reshape(()), gx[None]]
    for kind in range(4):
        outs += [per[nm][kind] for nm in order]
    return tuple(outs)
```

```python
import jax
import jax.numpy as jnp
from jax import lax
from jax.experimental import pallas as pl
from jax.experimental.pallas import tpu as pltpu

F32 = jnp.float32
BF16 = jnp.bfloat16
MESH = pl.DeviceIdType.MESH

D = 1024
W3 = 3 * D
QW = W3 // 4
AW = 512
GD = 128
CHUNK = 128
HEADS = 8
HALO = 16
POOL_WINDOWS = (2, 4, 8, 16)
EPS = 1e-6
TM_FWD = 512
TM_BWD = 256
TK = 1024
TK_IN = 2048
VMEM_LIMIT = 56 * 1024 * 1024

ADAM_LR, ADAM_B1, ADAM_B2, ADAM_EPS, ADAM_WD, ADAM_STEP = 0.001, 0.9, 0.999, 1e-08, 0.01, 10

S0_PRE, S0_POST, S0_CONV, S0_PS, S0_PW, S0_ROWS = 0, 8, 16, 48, 56, 568
S1_PRE, S1_POST, S1_LN, S1_BS, S1_WS, S1_LOSS, S1_ROWS = 0, 8, 16, 48, 56, 1080, 1088


def _nn(a, b):
    return jnp.dot(a, b, preferred_element_type=F32)


def _nt(a, b):
    return lax.dot_general(a, b, (((1,), (1,)), ((), ())), preferred_element_type=F32)


def _tn(a, b):
    return lax.dot_general(a, b, (((0,), (0,)), ((), ())), preferred_element_type=F32)


def _sigmoid(z):
    return 1.0 / (1.0 + jnp.exp(-z))


def _rms_fwd(x):
    r = lax.rsqrt(jnp.mean(x * x, axis=-1, keepdims=True) + EPS)
    return x * r, r


def _rms_bwd(dy, xh, r, g):
    dn = dy * g
    dx = r * (dn - xh * jnp.mean(xh * dn, axis=-1, keepdims=True))
    return dx, jnp.sum(dy * xh, axis=0, keepdims=True)


def _full(shape):
    nd = len(shape)
    return pl.BlockSpec(shape, lambda i, _n=nd: (0,) * _n, pipeline_mode=pl.Buffered(1))


def _full_out(shape):
    nd = len(shape)
    return pl.BlockSpec(shape, lambda i, _n=nd: (0,) * _n)


def _rows(tm, width, index=None):
    return pl.BlockSpec((tm, width), (lambda i: (i, 0)) if index is None else index)


def _params():
    return pltpu.CompilerParams(dimension_semantics=("arbitrary",), vmem_limit_bytes=VMEM_LIMIT)


def _position():
    x, y, c = lax.axis_index("x"), lax.axis_index("y"), lax.axis_index("c")
    return x, y, c


SHARD_ROWS = 24


def _shard_row(shard_ref, row):
    return jnp.concatenate([shard_ref[q, row:row + 1, :] for q in range(4)], axis=1)


def _shard_vector(shard_ref, first_row):
    return jnp.concatenate([shard_ref[q, first_row + k:first_row + k + 1, :] for q in range(4) for k in range(2)], axis=1)


def _even_mix(proj_ref, hc_ext, xp_ext, taps, pw_ref, ps_ref, first_row):
    tm = proj_ref.shape[0]
    xa = proj_ref[:, 0:AW]
    gb = proj_ref[:, AW:2 * AW]
    gc = proj_ref[:, 2 * AW:3 * AW]
    za = proj_ref[:, 3 * AW:4 * AW]
    xp = proj_ref[:, 4 * AW:5 * AW]
    zp = proj_ref[:, 5 * AW:6 * AW]
    hc = gc * xa
    hc_ext[HALO:, :] = hc
    e = hc_ext[...]
    conv = taps[2] * hc + taps[1] * pltpu.roll(e, 1, 0)[HALO:] + taps[0] * pltpu.roll(e, 2, 0)[HALO:]
    sa = _sigmoid(za)
    xp_ext[HALO:, :] = xp
    pos = first_row + lax.broadcasted_iota(jnp.int32, (tm, 1), 0)
    pooled, mixed, counts = [], [], []
    for g, w in enumerate(POOL_WINDOWS):
        cols = slice(g * GD, (g + 1) * GD)
        s = xp_ext[:, cols]
        for k in range(g + 1):
            s = s + pltpu.roll(s, 2 ** k, 0)
        count = jnp.minimum(pos + 1, w).astype(F32)
        pg = s[HALO:] / count - xp[:, cols]
        pooled.append(pg.astype(BF16))
        mixed.append(_nn(pooled[-1], pw_ref[g]))
        counts.append(count)
    mixed = jnp.concatenate(mixed, axis=-1)
    sb = _sigmoid(zp)
    return dict(xa=xa, gb=gb, gc=gc, za=za, zp=zp, hc=hc, conv=conv, sa=sa, sb=sb, pooled=pooled, mixed=mixed, counts=counts)


def _half_rows(ref, rows, who):
    return ref.at[pl.ds(pl.multiple_of(who * (rows // 2), 8), rows // 2), :]


def _store_permuted(ref, value):
    for ob in range(D // GD):
        nb = 4 * (ob % 2) + ob // 2
        ref[:, nb * GD:(nb + 1) * GD] = value[:, ob * GD:(ob + 1) * GD]


def _even_fwd(x, hb, post, win, shard, pwb, ps, wout, next_shards):
    S = x.shape[0]
    tm = TM_FWD
    nt = S // tm
    relay = (3 * nt) // 4
    n = len(next_shards)
    shard_rows = [p.shape[0] for p in next_shards]

    def body(x_ref, hb_ref, post_ref, win_ref, shard_ref, pw_ref, ps_ref, wout_ref, *rest):
        shard_refs, rest = rest[:n], rest[n:]
        x1_ref, proj_ref, m_ref, mixp_ref, conv_ref, mixed_ref, pooled_ref = rest[:7]
        full_refs, rest = rest[7:7 + n], rest[7 + n:]
        hc_ext, xp_ext, mix_sc = rest[:3]
        stage, rest = rest[3:3 + n], rest[3 + n:]
        send_sems, recv_sems, local_sems = rest
        i = pl.program_id(0)
        px, py, pc = _position()
        me = 2 * px + py
        chips = _chips(px, py)

        def ici(a, j):
            return pltpu.make_async_remote_copy(
                src_ref=_half_rows(shard_refs[a], shard_rows[a], pc), dst_ref=_half_rows(full_refs[a].at[me], shard_rows[a], pc),
                send_sem=send_sems.at[a, j], recv_sem=recv_sems.at[a, j], device_id=(*chips[j], pc), device_id_type=MESH)

        def ici_arrival(a, j):
            src = 2 * chips[j][0] + chips[j][1]
            return pltpu.make_async_remote_copy(
                src_ref=_half_rows(shard_refs[a], shard_rows[a], pc), dst_ref=_half_rows(full_refs[a].at[src], shard_rows[a], pc),
                send_sem=send_sems.at[a, j], recv_sem=recv_sems.at[a, j], device_id=(*chips[j], pc), device_id_type=MESH)

        def relay_copy(a, j, who):
            src = 2 * chips[j][0] + chips[j][1]
            region = _half_rows(full_refs[a].at[src], shard_rows[a], who)
            return pltpu.make_async_remote_copy(
                src_ref=region, dst_ref=region, send_sem=send_sems.at[a, 3 + j], recv_sem=recv_sems.at[a, 3 + j],
                device_id=(px, py, 1 - pc), device_id_type=MESH)

        def own_copy(a):
            return pltpu.make_async_copy(stage[a], full_refs[a].at[me], local_sems.at[a])

        @pl.when(i == 0)
        def _():
            hc_ext[0:HALO, :] = jnp.zeros((HALO, AW), F32)
            xp_ext[0:HALO, :] = jnp.zeros((HALO, AW), F32)
            for a in range(n):
                for j in range(3):
                    ici(a, j).start()
            for a in range(n):
                load = pltpu.make_async_copy(shard_refs[a], stage[a], local_sems.at[a])
                load.start()
                load.wait()
                own_copy(a).start()

        @pl.when(i == relay)
        def _():
            for j in range(3):
                for a in range(n):
                    ici_arrival(a, j).wait_recv()
                    relay_copy(a, j, pc).start()

        hb = hb_ref[...]
        for q in range(4):
            proj_ref[:, q * QW:(q + 1) * QW] = _nn(hb, win_ref[q])
        t = _even_mix(proj_ref, hc_ext, xp_ext, [_shard_row(shard_ref, k) for k in range(3)], pw_ref, ps_ref, i * tm)
        conv_ref[...] = t["conv"]
        mixed_ref[...] = t["mixed"]
        for g in range(4):
            pooled_ref[:, g * GD:(g + 1) * GD] = t["pooled"][g]
        mix_sc[:, 0:AW] = (t["gb"] * t["conv"] * (t["za"] * t["sa"])).astype(BF16)
        mix_sc[:, AW:2 * AW] = (t["mixed"] * ps_ref[...] * (t["zp"] * t["sb"])).astype(BF16)
        mix = mix_sc[...]
        _store_permuted(mixp_ref, mix)
        m = _nn(mix, wout_ref[...])
        m_ref[...] = m
        mh, _ = _rms_fwd(m)
        x1_ref[...] = x_ref[...] + mh * post_ref[0:1, :]
        hc_ext[0:HALO, :] = hc_ext[tm:tm + HALO, :]
        xp_ext[0:HALO, :] = xp_ext[tm:tm + HALO, :]

        @pl.when(i == nt - 1)
        def _():
            for j in range(3):
                for a in range(n):
                    relay_copy(a, j, 1 - pc).wait_recv()
            for a in range(n):
                for j in range(3):
                    ici(a, j).wait_send()
                    relay_copy(a, j, pc).wait_send()
                own_copy(a).wait()

    any_spec = pl.BlockSpec(memory_space=pl.ANY)
    return pl.pallas_call(
        body, name="even_fwd", grid=(nt,),
        in_specs=[_rows(tm, D), _rows(tm, D), _full((2, D)), _full((4, D, QW)), _full((4, SHARD_ROWS, GD)), _full((4, GD, GD)),
                  _full((1, AW)), _full((D, D))] + [any_spec] * n,
        out_specs=[_rows(tm, D), _rows(tm, W3), _rows(tm, D), _rows(tm, D), _rows(tm, AW), _rows(tm, AW), _rows(tm, AW)]
        + [any_spec] * n,
        out_shape=[jax.ShapeDtypeStruct((S, D), F32), jax.ShapeDtypeStruct((S, W3), F32), jax.ShapeDtypeStruct((S, D), F32),
                   jax.ShapeDtypeStruct((S, D), BF16),
                   jax.ShapeDtypeStruct((S, AW), F32), jax.ShapeDtypeStruct((S, AW), F32), jax.ShapeDtypeStruct((S, AW), BF16)]
        + [jax.ShapeDtypeStruct((4, *p.shape), p.dtype) for p in next_shards],
        scratch_shapes=[pltpu.VMEM((tm + HALO, AW), F32), pltpu.VMEM((tm + HALO, AW), F32), pltpu.VMEM((tm, D), BF16)]
        + [pltpu.VMEM(p.shape, p.dtype) for p in next_shards]
        + [pltpu.SemaphoreType.DMA((n, 6)), pltpu.SemaphoreType.DMA((n, 6)), pltpu.SemaphoreType.DMA((n,))],
        compiler_params=_params(),
    )(x, hb, post, win, shard, pwb, ps, wout, *next_shards)


def _chunks_side_by_side(a, h):
    return jnp.concatenate([a[n * CHUNK:(n + 1) * CHUNK, h * GD:(h + 1) * GD] for n in range(a.shape[0] // CHUNK)], axis=1)


def _odd_mix(proj_ref, lng, lnb, ws_ref, bias_ref, sv_ref):
    tm = proj_ref.shape[0]
    u = proj_ref[:, 0:D]
    v = proj_ref[:, D:2 * D]
    z = proj_ref[:, 2 * D:3 * D]
    mu = jnp.mean(v, axis=-1, keepdims=True)
    vc = v - mu
    rs = lax.rsqrt(jnp.mean(vc * vc, axis=-1, keepdims=True) + EPS)
    vh = vc * rs
    vnb = (vh * lng + lnb).astype(BF16)
    for h in range(HEADS):
        sv = _nn(ws_ref[h], _chunks_side_by_side(vnb, h))
        for n in range(tm // CHUNK):
            sv_ref[n * CHUNK:(n + 1) * CHUNK, h * GD:(h + 1) * GD] = sv[:, n * GD:(n + 1) * GD] + bias_ref[h]
    return dict(u=u, z=z, vh=vh, rs=rs, vnb=vnb, sz=_sigmoid(z))


def _odd_fwd(x1, tgt, pre, post, win, shard, wsb, bias, wout):
    S = x1.shape[0]
    tm = TM_FWD
    nt = S // tm

    def body(x_ref, tgt_ref, pre_ref, post_ref, win_ref, shard_ref, ws_ref, bias_ref, wout_ref,
             proj_ref, m_ref, hb_ref, yb_ref, dx2_ref, loss_ref, sv_ref):
        i = pl.program_id(0)

        @pl.when(i == 0)
        def _():
            loss_ref[...] = jnp.zeros((8, GD), F32)

        xv = x_ref[...]
        xh, _ = _rms_fwd(xv)
        hb = (xh * pre_ref[...]).astype(BF16)
        hb_ref[...] = hb
        for q in range(4):
            proj_ref[:, q * QW:(q + 1) * QW] = _nn(hb, win_ref[q])
        t = _odd_mix(proj_ref, _shard_vector(shard_ref, 8), _shard_vector(shard_ref, 16), ws_ref, bias_ref, sv_ref)
        yb = (t["u"] * sv_ref[...] * (t["z"] * t["sz"])).astype(BF16)
        _store_permuted(yb_ref, yb)
        m = _nn(yb, wout_ref[...])
        m_ref[...] = m
        mh, _ = _rms_fwd(m)
        err = xv + mh * post_ref[...] - tgt_ref[...]
        dx2_ref[...] = err * (1.0 / D)
        part = 0.5 * jnp.sum(jnp.mean(err * err, axis=-1, keepdims=True), axis=0, keepdims=True)
        loss_ref[...] += jnp.broadcast_to(part, (8, GD))

    return pl.pallas_call(
        body, name="odd_fwd", grid=(nt,),
        in_specs=[_rows(tm, D), _rows(tm, D), _full((1, D)), _full((1, D)), _full((4, D, QW)), _full((4, SHARD_ROWS, GD)),
                  _full((HEADS, CHUNK, CHUNK)), _full((HEADS, CHUNK, GD)), _full((D, D))],
        out_specs=[_rows(tm, W3), _rows(tm, D), _rows(tm, D), _rows(tm, D), _rows(tm, D), _full_out((8, GD))],
        out_shape=[jax.ShapeDtypeStruct((S, W3), F32), jax.ShapeDtypeStruct((S, D), F32), jax.ShapeDtypeStruct((S, D), BF16),
                   jax.ShapeDtypeStruct((S, D), BF16), jax.ShapeDtypeStruct((S, D), F32), jax.ShapeDtypeStruct((8, GD), F32)],
        scratch_shapes=[pltpu.VMEM((tm, D), F32)],
        compiler_params=_params(),
    )(x1, tgt, pre, post, win, shard, wsb, bias, wout)


def _store_rows(ref, row0, value):
    r, width = value.shape
    for a in range(r):
        for k in range(width // GD):
            ref[row0 + a * (width // GD) + k:row0 + a * (width // GD) + k + 1, :] = value[a:a + 1, k * GD:(k + 1) * GD]


def _proj_bwd(dproj, win_ref, x, dy, pre):
    dh = _nt(dproj[:, 0:QW], win_ref[0])
    for q in range(1, 4):
        dh += _nt(dproj[:, q * QW:(q + 1) * QW], win_ref[q])
    xh, r = _rms_fwd(x)
    dxn, dpre = _rms_bwd(dh, xh, r, pre)
    return dy + dxn, dpre


def _odd_bwd(dx2, x1, proj, m, loss, pre, post, win, shard, wsb, wsbt, bias, wout):
    S = x1.shape[0]
    tm = TM_BWD
    nt = S // tm

    def body(dy_ref, x_ref, proj_ref, m_ref, loss_ref, pre_ref, post_ref, win_ref, shard_ref, ws_ref, wst_ref, bias_ref,
             wout_ref, dx_ref, dproj_ref, dmb_ref, small_ref, sv_ref, dvn_ref, acc1024, dws_acc, dbs_acc):
        i = pl.program_id(0)
        lng = _shard_vector(shard_ref, 8)

        @pl.when(i == 0)
        def _():
            acc1024[...] = jnp.zeros_like(acc1024)
            dws_acc[...] = jnp.zeros_like(dws_acc)
            dbs_acc[...] = jnp.zeros_like(dbs_acc)

        dy = dy_ref[...]
        mh, rm = _rms_fwd(m_ref[...])
        dm, dpost = _rms_bwd(dy, mh, rm, post_ref[...])
        dmb = dm.astype(BF16)
        dmb_ref[...] = dmb
        dyv = _nt(dmb, wout_ref[...])
        t = _odd_mix(proj_ref, lng, _shard_vector(shard_ref, 16), ws_ref, bias_ref, sv_ref)
        u, z, sz, sv = t["u"], t["z"], t["sz"], sv_ref[...]
        dproj_ref[:, 0:D] = (dyv * sv * (z * sz)).astype(BF16)
        dproj_ref[:, 2 * D:3 * D] = (dyv * u * sv * (sz * (1.0 + z * (1.0 - sz)))).astype(BF16)
        dsv = dyv * u * (z * sz)
        dsvb = dsv.astype(BF16)
        for h in range(HEADS):
            dsv_h = _chunks_side_by_side(dsvb, h)
            dvn_h = _nn(wst_ref[h], dsv_h)
            dws_acc[h] += _nt(dsv_h, _chunks_side_by_side(t["vnb"], h))
            for n in range(tm // CHUNK):
                rows, cols = slice(n * CHUNK, (n + 1) * CHUNK), slice(h * GD, (h + 1) * GD)
                dvn_ref[rows, cols] = dvn_h[:, n * GD:(n + 1) * GD]
                dbs_acc[h] += dsv[rows, cols]
        dvn = dvn_ref[...]
        vh = t["vh"]
        dvh = dvn * lng
        dv = t["rs"] * (dvh - jnp.mean(dvh, axis=-1, keepdims=True) - vh * jnp.mean(dvh * vh, axis=-1, keepdims=True))
        dproj_ref[:, D:2 * D] = dv.astype(BF16)
        dx_ref[...], dpre = _proj_bwd(dproj_ref[...], win_ref, x_ref[...], dy, pre_ref[...])
        acc1024[0:1, :] += dpre
        acc1024[1:2, :] += dpost
        acc1024[2:3, :] += jnp.sum(dvn * vh, axis=0, keepdims=True)
        acc1024[3:4, :] += jnp.sum(dvn, axis=0, keepdims=True)

        @pl.when(i == nt - 1)
        def _():
            small_ref[...] = jnp.zeros_like(small_ref)
            _store_rows(small_ref, S1_PRE, acc1024[0:1, :])
            _store_rows(small_ref, S1_POST, acc1024[1:2, :])
            for q in range(4):
                _store_rows(small_ref, S1_LN + 8 * q, acc1024[2:3, 2 * q * GD:(2 * q + 2) * GD])
                _store_rows(small_ref, S1_LN + 8 * q + 2, acc1024[3:4, 2 * q * GD:(2 * q + 2) * GD])
            lower = lax.broadcasted_iota(jnp.int32, (CHUNK, CHUNK), 0) >= lax.broadcasted_iota(jnp.int32, (CHUNK, CHUNK), 1)
            for h in range(HEADS):
                small_ref[S1_WS + h * CHUNK:S1_WS + (h + 1) * CHUNK, :] = jnp.where(lower, dws_acc[h], 0.0)
                small_ref[S1_BS + h:S1_BS + h + 1, :] = jnp.sum(dbs_acc[h].T, axis=0, keepdims=True)
            small_ref[S1_LOSS:S1_LOSS + 8, :] = loss_ref[...]

    return pl.pallas_call(
        body, name="odd_bwd", grid=(nt,),
        in_specs=[_rows(tm, D), _rows(tm, D), _rows(tm, W3), _rows(tm, D), _full((8, GD)), _full((1, D)), _full((1, D)),
                  _full((4, D, QW)), _full((4, SHARD_ROWS, GD)), _full((HEADS, CHUNK, CHUNK)), _full((HEADS, CHUNK, CHUNK)),
                  _full((HEADS, CHUNK, GD)), _full((D, D))],
        out_specs=[_rows(tm, D), _rows(tm, W3), _rows(tm, D), _full_out((S1_ROWS, GD))],
        out_shape=[jax.ShapeDtypeStruct((S, D), F32), jax.ShapeDtypeStruct((S, W3), BF16), jax.ShapeDtypeStruct((S, D), BF16),
                   jax.ShapeDtypeStruct((S1_ROWS, GD), F32)],
        scratch_shapes=[pltpu.VMEM((tm, D), F32), pltpu.VMEM((tm, D), F32), pltpu.VMEM((8, D), F32),
                        pltpu.VMEM((HEADS, CHUNK, CHUNK), F32), pltpu.VMEM((HEADS, CHUNK, GD), F32)],
        compiler_params=_params(),
    )(dx2, x1, proj, m, loss, pre, post, win, shard, wsb, wsbt, bias, wout)


def _even_bwd(dx1, x, proj, conv, mixed, pooled, m, pre, post, win, shard, pwb, ps, wout):
    S = x.shape[0]
    tm = TM_BWD
    nt = S // tm
    L = tm + HALO

    def rev(i):
        return (nt - 1 - i, 0)

    def body(dy_ref, x_ref, proj_ref, conv_ref, mixed_ref, pooled_ref, m_ref, pre_ref, post_ref, win_ref, shard_ref, pw_ref, ps_ref,
             wout_ref, dx_ref, dproj_ref, dmb_ref, small_ref, dconv_ext, q_ext, acc1024, acc512, dpw_acc):
        i = pl.program_id(0)
        tile = nt - 1 - i

        @pl.when(i == 0)
        def _():
            dconv_ext[tm:L, :] = jnp.zeros((HALO, AW), F32)
            q_ext[tm:L, :] = jnp.zeros((HALO, AW), F32)
            acc1024[...] = jnp.zeros_like(acc1024)
            acc512[...] = jnp.zeros_like(acc512)
            dpw_acc[...] = jnp.zeros_like(dpw_acc)

        dy = dy_ref[...]
        mh, rm = _rms_fwd(m_ref[...])
        dm, dpost = _rms_bwd(dy, mh, rm, post_ref[0:1, :])
        dmb = dm.astype(BF16)
        dmb_ref[...] = dmb
        dmix = _nt(dmb, wout_ref[...])
        dya, dyb = dmix[:, 0:AW], dmix[:, AW:2 * AW]
        xa, gb, gc, za = (proj_ref[:, k * AW:(k + 1) * AW] for k in range(4))
        zp = proj_ref[:, 5 * AW:6 * AW]
        hc = gc * xa
        conv = conv_ref[...]
        sa = _sigmoid(za)
        silu_a = za * sa
        dproj_ref[:, AW:2 * AW] = (dya * conv * silu_a).astype(BF16)
        dproj_ref[:, 3 * AW:4 * AW] = (dya * gb * conv * (sa * (1.0 + za * (1.0 - sa)))).astype(BF16)
        dconv = dya * gb * silu_a
        dconv_ext[0:tm, :] = dconv
        e = dconv_ext[...]
        dc1 = pltpu.roll(e, L - 1, 0)[0:tm]
        dc2 = pltpu.roll(e, L - 2, 0)[0:tm]
        taps = [_shard_row(shard_ref, k) for k in range(3)]
        dhc = taps[2] * dconv + taps[1] * dc1 + taps[0] * dc2
        dproj_ref[:, 0:AW] = (dhc * gc).astype(BF16)
        dproj_ref[:, 2 * AW:3 * AW] = (dhc * xa).astype(BF16)
        acc512[0:1, :] += jnp.sum(dc2 * hc, axis=0, keepdims=True)
        acc512[1:2, :] += jnp.sum(dc1 * hc, axis=0, keepdims=True)
        acc512[2:3, :] += jnp.sum(dconv * hc, axis=0, keepdims=True)

        sb, mixed = _sigmoid(zp), mixed_ref[...]
        silu_b = zp * sb
        acc512[3:4, :] += jnp.sum(dyb * mixed * silu_b, axis=0, keepdims=True)
        dmixedb = (dyb * ps_ref[...] * silu_b).astype(BF16)
        dproj_ref[:, 5 * AW:6 * AW] = (dyb * mixed * ps_ref[...] * (sb * (1.0 + zp * (1.0 - sb)))).astype(BF16)
        pos = tile * tm + lax.broadcasted_iota(jnp.int32, (tm, 1), 0)
        for g, w in enumerate(POOL_WINDOWS):
            cols = slice(g * GD, (g + 1) * GD)
            dpw_acc[g] += _tn(pooled_ref[:, cols], dmixedb[:, cols])
            dpooled = _nt(dmixedb[:, cols], pw_ref[g])
            q_ext[0:tm, cols] = dpooled / jnp.minimum(pos + 1, w).astype(F32)
            s = q_ext[:, cols]
            for k in range(g + 1):
                s = s + pltpu.roll(s, L - 2 ** k, 0)
            dproj_ref[:, 4 * AW + g * GD:4 * AW + (g + 1) * GD] = (s[0:tm] - dpooled).astype(BF16)
        dconv_ext[tm:L, :] = dconv_ext[0:HALO, :]
        q_ext[tm:L, :] = q_ext[0:HALO, :]

        dx_ref[...], dpre = _proj_bwd(dproj_ref[...], win_ref, x_ref[...], dy, pre_ref[0:1, :])
        acc1024[0:1, :] += dpre
        acc1024[1:2, :] += dpost

        @pl.when(i == nt - 1)
        def _():
            small_ref[...] = jnp.zeros_like(small_ref)
            _store_rows(small_ref, S0_PRE, acc1024[0:1, :])
            _store_rows(small_ref, S0_POST, acc1024[1:2, :])
            for q in range(4):
                for k in range(3):
                    small_ref[S0_CONV + 8 * q + k:S0_CONV + 8 * q + k + 1, :] = acc512[k:k + 1, q * GD:(q + 1) * GD]
            _store_rows(small_ref, S0_PS, acc512[3:4, :])
            for g in range(4):
                small_ref[S0_PW + g * GD:S0_PW + (g + 1) * GD, :] = dpw_acc[g]

    return pl.pallas_call(
        body, name="even_bwd", grid=(nt,),
        in_specs=[_rows(tm, D, rev), _rows(tm, D, rev), _rows(tm, W3, rev), _rows(tm, AW, rev), _rows(tm, AW, rev), _rows(tm, AW, rev),
                  _rows(tm, D, rev),
                  _full((2, D)), _full((2, D)), _full((4, D, QW)), _full((4, SHARD_ROWS, GD)), _full((4, GD, GD)), _full((1, AW)),
                  _full((D, D))],
        out_specs=[_rows(tm, D, rev), _rows(tm, W3, rev), _rows(tm, D, rev), _full_out((S0_ROWS, GD))],
        out_shape=[jax.ShapeDtypeStruct((S, D), F32), jax.ShapeDtypeStruct((S, W3), BF16), jax.ShapeDtypeStruct((S, D), BF16),
                   jax.ShapeDtypeStruct((S0_ROWS, GD), F32)],
        scratch_shapes=[pltpu.VMEM((L, AW), F32), pltpu.VMEM((L, AW), F32),
                        pltpu.VMEM((8, D), F32), pltpu.VMEM((8, AW), F32), pltpu.VMEM((4, GD, GD), F32)],
        compiler_params=_params(),
    )(dx1, x, proj, conv, mixed, pooled, m, pre, post, win, shard, pwb, ps, wout)


def _owner_id(me, relation, c):
    q = jnp.bitwise_xor(me, relation)
    return (q // 2, q % 2, c)


def _small_gather_steps(small_ref, all_ref, stage, send_sems, recv_sems, local_sem):
    x, y, c = _position()
    chips = _chips(x, y)

    def slot(chip, core):
        return 4 * chip[0] + 2 * chip[1] + core

    def copy(k, src, block, to):
        return pltpu.make_async_remote_copy(src_ref=src, dst_ref=all_ref.at[block], send_sem=send_sems.at[k],
                                            recv_sem=recv_sems.at[k], device_id=to, device_id_type=MESH)

    def own_copy():
        return pltpu.make_async_copy(stage, all_ref.at[slot((x, y), c)], local_sem)

    def first_sends():
        mine = slot((x, y), c)
        return [copy(0, small_ref, mine, (x, y, 1 - c))] + [copy(1 + j, small_ref, mine, (*chip, c)) for j, chip in enumerate(chips)]

    def relays():
        return [copy(4 + j, all_ref.at[slot(chip, c)], slot(chip, c), (x, y, 1 - c)) for j, chip in enumerate(chips)]

    def start():
        for cp in first_sends():
            cp.start()
        load = pltpu.make_async_copy(small_ref, stage, local_sem)
        load.start()
        load.wait()
        own_copy().start()

    def relay():
        for j, chip in enumerate(chips):
            copy(1 + j, small_ref, slot(chip, c), (*chip, c)).wait_recv()
        for cp in relays():
            cp.start()

    def finish():
        copy(0, small_ref, slot((x, y), 1 - c), (x, y, 1 - c)).wait_recv()
        for j, chip in enumerate(chips):
            copy(4 + j, small_ref, slot(chip, 1 - c), (x, y, 1 - c)).wait_recv()
        for cp in first_sends() + relays():
            cp.wait_send()
        own_copy().wait()

    return start, relay, finish


def _wgrad_layer(a_out, b_out, a_in, b_in, small, pos, name):
    S = a_in.shape[0]
    nko = S // TK
    nki = S // TK_IN
    hm = D // 2
    qr = hm // 4

    def out_index(s, pos_ref):
        return (jnp.minimum(s, nko - 1), 0)

    def a_in_index(s, pos_ref):
        return (jnp.where(s >= nko, (s - nko) % nki, 0), 0)

    def b_in_index(s, pos_ref):
        return (jnp.where(s >= nko, (s - nko) % nki, 0), jnp.bitwise_xor(pos_ref[1], 3 - jnp.maximum((s - nko) // nki, 0)))

    def body(pos_ref, ao_ref, bo_ref, a_ref, b_ref, small_ref, gout_ref, gin_ref, all_ref,
             acc, rbuf, sbuf, arr, mine, acc_o, rbuf_o, total_o, sbuf_o, arr_o, mine_o, stage,
             d2d_send, d2d_recv, ici_send, ici_recv, d2d_o_send, d2d_o_recv, ici_o_send, ici_o_recv,
             share_send, share_recv, local_sems, g_send, g_recv, g_local):
        s = pl.program_id(0)
        in_step = jnp.maximum(s - nko, 0)
        blk = jnp.where(s < nko, 0, 1 + in_step // nki)
        k = jnp.where(s < nko, s, in_step % nki)
        j = blk - 1
        x, y, c = _position()
        me = 2 * x + y
        sibling = (x, y, 1 - c)
        last = k == jnp.where(s < nko, nko - 1, nki - 1)
        gather_start, gather_relay, gather_finish = _small_gather_steps(small_ref, all_ref, stage, g_send, g_recv, g_local)

        def other_half(ref):
            return ref.at[pl.ds(pl.multiple_of((1 - c) * hm, hm), hm), :]

        def own_half(ref):
            return ref[pl.ds(pl.multiple_of(c * hm, hm), hm), :]

        def to_sibling(jj):
            return pltpu.make_async_remote_copy(
                src_ref=other_half(acc.at[jj % 2]), dst_ref=rbuf.at[jj], send_sem=d2d_send.at[jj], recv_sem=d2d_recv.at[jj],
                device_id=sibling, device_id_type=MESH)

        def to_owner(jj):
            return pltpu.make_async_remote_copy(
                src_ref=sbuf.at[jj], dst_ref=arr.at[2 - jj], send_sem=ici_send.at[jj], recv_sem=ici_recv.at[jj],
                device_id=_owner_id(me, 3 - jj, c), device_id_type=MESH)

        def out_to_sibling():
            return pltpu.make_async_remote_copy(
                src_ref=other_half(acc_o), dst_ref=rbuf_o, send_sem=d2d_o_send, recv_sem=d2d_o_recv,
                device_id=sibling, device_id_type=MESH)

        def out_to_owner(r):
            return pltpu.make_async_remote_copy(
                src_ref=sbuf_o.at[r], dst_ref=arr_o.at[r], send_sem=ici_o_send.at[r], recv_sem=ici_o_recv.at[r],
                device_id=_owner_id(me, r + 1, c), device_id_type=MESH)

        def pair_sum(jj):
            to_sibling(jj).wait_recv()
            return own_half(acc.at[jj % 2]) + rbuf[jj]

        def send_block(jj):
            sbuf[jj] = pair_sum(jj).astype(BF16)
            to_owner(jj).start()

        @pl.when(s == 0)
        def _():
            gather_start()

        @pl.when((blk == 3) & (k == 0))
        def _():
            gather_relay()

        @pl.when((blk == 0) & (k == 0))
        def _():
            acc_o[...] = jnp.zeros((D, D), F32)

        @pl.when(blk == 0)
        def _():
            acc_o[...] += _tn(ao_ref[...], bo_ref[...])

        @pl.when((blk == 0) & last)
        def _():
            out_to_sibling().start()

        @pl.when((blk >= 3) & (k == 0))
        def _():
            to_sibling(j - 2).wait_send()

        @pl.when((blk >= 1) & (k == 0))
        def _():
            acc[j % 2] = jnp.zeros((D, QW), F32)

        @pl.when(blk >= 1)
        def _():
            acc[j % 2] += _tn(a_ref[...], b_ref[...])

        @pl.when((blk >= 1) & last)
        def _():
            to_sibling(j).start()

        @pl.when((blk == 1) & last)
        def _():
            out_to_sibling().wait_recv()
            total_o[...] = own_half(acc_o) + rbuf_o[...]
            for r in range(3):
                q = jnp.bitwise_xor(me, r + 1)
                sbuf_o[r] = total_o[pl.ds(pl.multiple_of(q * qr, qr), qr), :].astype(BF16)
                out_to_owner(r).start()

        @pl.when((blk == 2) & last)
        def _():
            send_block(0)

        @pl.when((blk == 3) & last)
        def _():
            send_block(1)
            send_block(2)

        @pl.when((blk == 4) & last)
        def _():
            g_in = pair_sum(3)
            g_out = total_o[pl.ds(pl.multiple_of(me * qr, qr), qr), :]
            to_sibling(2).wait_send()
            to_sibling(3).wait_send()
            out_to_sibling().wait_send()
            for r in range(3):
                to_owner(r).wait()
                out_to_owner(r).wait()
            for r in range(3):
                g_in = g_in + arr[r].astype(F32)
                g_out = g_out + arr_o[r].astype(F32)
            mine[...] = g_in
            mine_o[...] = g_out
            copies = []
            for idx, (src, dst) in enumerate([(mine, gin_ref), (mine_o, gout_ref)]):
                copies.append(pltpu.make_async_remote_copy(
                    src_ref=src, dst_ref=dst.at[c], send_sem=share_send.at[idx], recv_sem=share_recv.at[idx],
                    device_id=sibling, device_id_type=MESH))
                copies.append(pltpu.make_async_copy(src, dst.at[c], local_sems.at[idx]))
            for cp in copies:
                cp.start()
            for cp in copies:
                cp.wait()
            gather_finish()

    any_spec = pl.BlockSpec(memory_space=pl.ANY)
    grid_spec = pltpu.PrefetchScalarGridSpec(
        num_scalar_prefetch=1, grid=(nko + 4 * nki,),
        in_specs=[pl.BlockSpec((TK, D), out_index), pl.BlockSpec((TK, D), out_index),
                  pl.BlockSpec((TK_IN, D), a_in_index), pl.BlockSpec((TK_IN, QW), b_in_index), any_spec],
        out_specs=[any_spec, any_spec, any_spec],
        scratch_shapes=[pltpu.VMEM((2, D, QW), F32), pltpu.VMEM((4, hm, QW), F32), pltpu.VMEM((3, hm, QW), BF16),
                        pltpu.VMEM((3, hm, QW), BF16), pltpu.VMEM((hm, QW), F32),
                        pltpu.VMEM((D, D), F32), pltpu.VMEM((hm, D), F32), pltpu.VMEM((hm, D), F32), pltpu.VMEM((3, qr, D), BF16),
                        pltpu.VMEM((3, qr, D), BF16), pltpu.VMEM((qr, D), F32),
                        pltpu.VMEM(small.shape, F32),
                        pltpu.SemaphoreType.DMA((4,)), pltpu.SemaphoreType.DMA((4,)),
                        pltpu.SemaphoreType.DMA((3,)), pltpu.SemaphoreType.DMA((3,)),
                        pltpu.SemaphoreType.DMA, pltpu.SemaphoreType.DMA,
                        pltpu.SemaphoreType.DMA((3,)), pltpu.SemaphoreType.DMA((3,)),
                        pltpu.SemaphoreType.DMA((2,)), pltpu.SemaphoreType.DMA((2,)), pltpu.SemaphoreType.DMA((2,)),
                        pltpu.SemaphoreType.DMA((7,)), pltpu.SemaphoreType.DMA((7,)), pltpu.SemaphoreType.DMA])
    return pl.pallas_call(
        body, name=name, grid_spec=grid_spec,
        out_shape=[jax.ShapeDtypeStruct((2, qr, D), F32), jax.ShapeDtypeStruct((2, hm, QW), F32),
                   jax.ShapeDtypeStruct((8, *small.shape), F32)],
        compiler_params=pltpu.CompilerParams(dimension_semantics=("arbitrary",), vmem_limit_bytes=VMEM_LIMIT),
    )(pos, a_out, b_out, a_in, b_in, small)


def _chips(x, y):
    return [(1 - x, y), (x, 1 - y), (1 - x, 1 - y)]


def _gather_weights(parts, split, x, pre, next_parts, pool_w, w_s, b_s):
    n = len(parts)
    nn = len(next_parts)
    S = x.shape[0]
    tm = TM_FWD
    nt = S // tm
    staged = [a for a in range(n) if split[a]]

    ns = nn + 3

    def body(x_ref, pre_ref, *refs):
        raw, refs = refs[:n], refs[n:]
        side_hbm, hb_ref = refs[:ns], refs[ns]
        refs = refs[ns + 1:]
        outs, refs = refs[:n], refs[n:]
        next_outs, poolb_ref, tril_ref, trilt_ref, bias_ref = refs[:nn], refs[nn], refs[nn + 1], refs[nn + 2], refs[nn + 3]
        refs = refs[nn + 4:]
        stage, refs = refs[:len(staged)], refs[len(staged):]
        side, (send_sems, recv_sems, local_sems, side_sems) = refs[:ns], refs[ns:]
        next_raw, pool_ref, ws_ref, bs_ref = side[:nn], side[nn], side[nn + 1], side[nn + 2]
        ins = [stage[staged.index(a)] if split[a] else raw[a] for a in range(n)]
        i = pl.program_id(0)

        def fetch(k):
            return pltpu.make_async_copy(side_hbm[k], side[k], side_sems.at[k])

        x, y, c = _position()
        me = 2 * x + y
        nbr_x, nbr_y, diag = _chips(x, y)
        id_x, id_y, id_d = (2 * chip[0] + chip[1] for chip in (nbr_x, nbr_y, diag))
        sibling = (x, y, 1 - c)

        def rows(a, ref, who, piece=None):
            r = parts[a].shape[0] // 2
            if piece is None:
                return ref.at[pl.ds(pl.multiple_of(who * r, 16), r), :]
            return ref.at[pl.ds(pl.multiple_of(who * r + piece * (r // 2), 16), r // 2), :]

        def copy(a, k, src, dst, to):
            return pltpu.make_async_remote_copy(src_ref=src, dst_ref=dst, send_sem=send_sems.at[a, k], recv_sem=recv_sems.at[a, k],
                                                device_id=to, device_id_type=MESH)

        def direct(a, k, chip):
            if split[a]:
                return copy(a, k, rows(a, ins[a], c), rows(a, outs[a].at[me], c), (*chip, c))
            return copy(a, k, ins[a], outs[a].at[me], (*chip, c))

        def arrival(a, k, src_id):
            if split[a]:
                return copy(a, k, rows(a, ins[a], c), rows(a, outs[a].at[src_id], c), (*nbr_x, c))
            return copy(a, k, ins[a], outs[a].at[src_id], (*nbr_x, c))

        def pass_on(a, k, src_id, piece, chip):
            region = rows(a, outs[a].at[src_id], c, piece)
            return copy(a, k, region, region, (*chip, c))

        def hand_over(a, k, src_id, who):
            region = rows(a, outs[a].at[src_id], who)
            return copy(a, k, region, region, sibling)

        def local(a):
            return pltpu.make_async_copy(ins[a], outs[a].at[me], local_sems.at[a])

        def first_sends(a):
            return [direct(a, 0, nbr_x), direct(a, 1, nbr_y)] + ([] if split[a] else [direct(a, 2, diag)])

        def after_x(a):
            return [pass_on(a, 3, id_x, 1, nbr_y), hand_over(a, 4, id_x, c)] if split[a] else []

        def after_y(a):
            return [pass_on(a, 2, id_y, 0, nbr_x), hand_over(a, 5, id_y, c)] if split[a] else []

        def after_diag(a):
            return [hand_over(a, 6, id_d, c)] if split[a] else []

        @pl.when(i == 0)
        def _():
            for a in staged:
                ins[a][...] = raw[a][...].astype(BF16)
            for a in range(n):
                local(a).start()
                for cp in first_sends(a):
                    cp.start()
            for k in range(ns):
                fetch(k).start()

        @pl.when(i == 1)
        def _():
            for k in range(ns):
                fetch(k).wait()
            for a in range(nn):
                next_outs[a][...] = next_raw[a][...].astype(BF16)
            poolb_ref[...] = pool_ref[...].astype(BF16)
            lower = lax.broadcasted_iota(jnp.int32, (CHUNK, CHUNK), 0) >= lax.broadcasted_iota(jnp.int32, (CHUNK, CHUNK), 1)
            for h in range(HEADS):
                tril = jnp.where(lower, ws_ref[h], 0.0)
                tril_ref[h] = tril.astype(BF16)
                trilt_ref[h] = tril.T.astype(BF16)
                bias_ref[h] = jnp.broadcast_to(bs_ref[h:h + 1, :], (CHUNK, GD)).T

        xh, _ = _rms_fwd(x_ref[...])
        hb_ref[...] = (xh * pre_ref[0:1, :]).astype(BF16)

        @pl.when(i == nt - 2)
        def _():
            for a in range(n):
                arrival(a, 0, id_x).wait_recv()
                for cp in after_x(a):
                    cp.start()
            for a in range(n):
                arrival(a, 1, id_y).wait_recv()
                for cp in after_y(a):
                    cp.start()

        @pl.when(i == nt - 1)
        def _():
            for a in range(n):
                if split[a]:
                    pass_on(a, 2, id_d, 0, nbr_x).wait_recv()
                    pass_on(a, 3, id_d, 1, nbr_y).wait_recv()
                    for cp in after_diag(a):
                        cp.start()
                else:
                    arrival(a, 2, id_d).wait_recv()
            for a in range(n):
                if split[a]:
                    for k, src_id in ((4, id_x), (5, id_y), (6, id_d)):
                        hand_over(a, k, src_id, 1 - c).wait_recv()
            for a in range(n):
                for cp in first_sends(a) + after_x(a) + after_y(a) + after_diag(a):
                    cp.wait_send()
                local(a).wait()

    any_spec = pl.BlockSpec(memory_space=pl.ANY)
    vmem = pl.BlockSpec(memory_space=pltpu.VMEM)
    sent = [BF16 if split[a] else parts[a].dtype for a in range(n)]
    return pl.pallas_call(
        body, name="gather_weights", grid=(nt,),
        in_specs=[_rows(tm, D), _full((2, D))] + [vmem] * n + [any_spec] * ns,
        out_specs=[_rows(tm, D)] + [any_spec] * n + [vmem] * (nn + 4),
        out_shape=[jax.ShapeDtypeStruct((S, D), BF16)] + [jax.ShapeDtypeStruct((4, *p.shape), t) for p, t in zip(parts, sent)]
        + [jax.ShapeDtypeStruct(p.shape, BF16) for p in next_parts]
        + [jax.ShapeDtypeStruct(pool_w.shape, BF16), jax.ShapeDtypeStruct(w_s.shape, BF16), jax.ShapeDtypeStruct(w_s.shape, BF16),
           jax.ShapeDtypeStruct((HEADS, CHUNK, GD), F32)],
        scratch_shapes=[pltpu.VMEM(parts[a].shape, BF16) for a in staged]
        + [pltpu.VMEM(p.shape, F32) for p in (*next_parts, pool_w, w_s, b_s)]
        + [pltpu.SemaphoreType.DMA((n, 7)), pltpu.SemaphoreType.DMA((n, 7)), pltpu.SemaphoreType.DMA((n,)),
           pltpu.SemaphoreType.DMA((ns,))],
        compiler_params=_params(),
    )(x, pre, *parts, *next_parts, pool_w, w_s, b_s)


def _adamw(w, g, m, v):
    m = ADAM_B1 * m + (1.0 - ADAM_B1) * g
    v = ADAM_B2 * v + (1.0 - ADAM_B2) * (g * g)
    m_hat = m / (1.0 - ADAM_B1 ** ADAM_STEP)
    v_hat = v / (1.0 - ADAM_B2 ** ADAM_STEP)
    delta = -ADAM_LR * (m_hat / (jnp.sqrt(v_hat) + ADAM_EPS) + ADAM_WD * w)
    return delta, m, v


def _adamw_big(ws, gs, ms, vs):
    steps = 4
    n = len(ws)

    def body(*refs):
        ins, outs = refs[:4 * n], refs[4 * n:]
        for a in range(n):
            w_ref, g_ref, m_ref, v_ref = ins[4 * a:4 * a + 4]
            go_ref, d_ref, mo_ref, vo_ref = outs[4 * a:4 * a + 4]
            gv = g_ref[...]
            go_ref[...] = gv
            d_ref[...], mo_ref[...], vo_ref[...] = _adamw(w_ref[...], gv, m_ref[...], v_ref[...])

    specs, shapes, operands = [], [], []
    for w, g, m, v in zip(ws, gs, ms, vs):
        rows, cols = w.shape
        specs += [pl.BlockSpec((rows // steps, cols), lambda i: (i, 0))] * 4
        shapes += [jax.ShapeDtypeStruct((rows, cols), F32)] * 4
        operands += [w, g, m, v]
    res = pl.pallas_call(
        body, name="adamw_big", grid=(steps,),
        in_specs=specs, out_specs=specs, out_shape=shapes,
        compiler_params=pltpu.CompilerParams(dimension_semantics=("arbitrary",), vmem_limit_bytes=VMEM_LIMIT),
    )(*operands)
    return [res[4 * a:4 * a + 4] for a in range(n)]


def _adamw_small(g0, g1, weights, moms, vels):
    names = ["pre", "post", "conv", "pw", "ps", "lng", "lnb", "ws", "bs"]
    shapes = [w.shape for w in weights]

    def body(*refs):
        me = 2 * lax.axis_index("x") + lax.axis_index("y")
        g0_ref, g1_ref = refs[0], refs[1]
        w_refs, m_refs, v_refs = refs[2:11], refs[11:20], refs[20:29]
        outs = refs[29:29 + 36]
        loss_ref = refs[65]
        t0_ref, t1_ref = refs[66], refs[67]
        t0 = g0_ref[0]
        t1 = g1_ref[0]
        for d in range(1, 8):
            t0 = t0 + g0_ref[d]
            t1 = t1 + g1_ref[d]
        t0_ref[...] = t0
        t1_ref[...] = t1
        loss_ref[...] = t1_ref[S1_LOSS:S1_LOSS + 1, 0:1]
        my_conv = pl.multiple_of(S0_CONV + 8 * me, 8)
        my_ln = pl.multiple_of(S1_LN + 8 * me, 8)

        def update(idx, piece, grad):
            go, do, mo, vo = outs[4 * idx:4 * idx + 4]
            go[piece] = grad
            do[piece], mo[piece], vo[piece] = _adamw(w_refs[idx][piece], grad, m_refs[idx][piece], v_refs[idx][piece])

        for layer in range(2):
            for k in range(D // GD):
                lanes = slice(k * GD, (k + 1) * GD)
                tref, pre0, post0 = (t0_ref, S0_PRE, S0_POST) if layer == 0 else (t1_ref, S1_PRE, S1_POST)
                update(0, (slice(layer, layer + 1), lanes), tref[pre0 + k:pre0 + k + 1, :])
                update(1, (slice(layer, layer + 1), lanes), tref[post0 + k:post0 + k + 1, :])
        conv_rows = t0_ref[pl.ds(my_conv, 8), :]
        update(2, (slice(0, 3), 0, slice(None)), conv_rows[0:3, :])
        for g in range(4):
            update(3, (g,), t0_ref[S0_PW + g * GD:S0_PW + (g + 1) * GD, :])
            update(4, (slice(0, 1), slice(g * GD, (g + 1) * GD)), t0_ref[S0_PS + g:S0_PS + g + 1, :])
        ln_rows = t1_ref[pl.ds(my_ln, 8), :]
        for k in range(2):
            update(5, (slice(0, 1), slice(k * GD, (k + 1) * GD)), ln_rows[k:k + 1, :])
            update(6, (slice(0, 1), slice(k * GD, (k + 1) * GD)), ln_rows[2 + k:3 + k, :])
        for h in range(HEADS):
            update(7, (h,), t1_ref[S1_WS + h * CHUNK:S1_WS + (h + 1) * CHUNK, :])
        update(8, (slice(None), slice(None)), t1_ref[S1_BS:S1_BS + HEADS, :])

    vm = pl.BlockSpec(memory_space=pltpu.VMEM)
    out_shape = []
    for s in shapes:
        out_shape += [jax.ShapeDtypeStruct(s, F32)] * 4
    out_shape.append(jax.ShapeDtypeStruct((1, 1), F32))
    res = pl.pallas_call(
        body, name="adamw_small",
        in_specs=[vm] * 29, out_specs=[vm] * 37, out_shape=out_shape,
        scratch_shapes=[pltpu.VMEM((S0_ROWS, GD), F32), pltpu.VMEM((S1_ROWS, GD), F32)],
        compiler_params=pltpu.CompilerParams(vmem_limit_bytes=VMEM_LIMIT),
    )(g0, g1, *weights, *moms, *vels)
    per_weight = {nm: res[4 * i:4 * i + 4] for i, nm in enumerate(names)}
    return per_weight, res[36]


def _pad8(a):
    return jnp.pad(a, ((0, 8 - a.shape[0]), (0, 0)))


def kernel(x, pre_norm, post_norm, even_w_in, even_conv_w, even_pool_w, even_pool_scale, even_w_out, odd_w_in, odd_ln_g, odd_ln_b, odd_w_s, odd_b_s, odd_w_out, loss_target, m_pre_norm, m_post_norm, m_even_w_in, m_even_conv_w, m_even_pool_w, m_even_pool_scale, m_even_w_out, m_odd_w_in, m_odd_ln_g, m_odd_ln_b, m_odd_w_s, m_odd_b_s, m_odd_w_out, v_pre_norm, v_post_norm, v_even_w_in, v_even_conv_w, v_even_pool_w, v_even_pool_scale, v_even_w_out, v_odd_w_in, v_odd_ln_g, v_odd_ln_b, v_odd_w_s, v_odd_b_s, v_odd_w_out):
    xs = x[0]
    tgt = loss_target[0]

    small_shard = jnp.concatenate([_pad8(even_conv_w[0]), _pad8(odd_ln_g.reshape(2, GD)), _pad8(odd_ln_b.reshape(2, GD))], axis=0)
    hb0, win0, wout0, shard, win1_shard, wout1_shard, pool_wb, ws_tril, ws_tril_t, bias = _gather_weights(
        [even_w_in[0], even_w_out[0], small_shard], [True, True, False], xs, pre_norm, [odd_w_in[0], odd_w_out[0]],
        even_pool_w[0], odd_w_s[0], odd_b_s[0])
    wout0 = wout0.reshape(D, D)
    px, py, pc = _position()
    pos = jnp.stack([pc, 2 * px + py]).astype(jnp.int32)

    x1, proj0, m0, mixp0, conv0, mixed0, pooled0, win1, wout1 = _even_fwd(
        xs, hb0, post_norm, win0, shard, pool_wb, even_pool_scale, wout0, [win1_shard, wout1_shard])
    wout1 = wout1.reshape(D, D)
    pre1, post1 = pre_norm[1:2], post_norm[1:2]
    proj1, m1, hb1, yp1, dx2, loss_part = _odd_fwd(x1, tgt, pre1, post1, win1, shard, ws_tril, bias, wout1)
    dx1, dproj1, dmb1, small1 = _odd_bwd(dx2, x1, proj1, m1, loss_part, pre1, post1, win1, shard, ws_tril, ws_tril_t, bias, wout1)
    g_out1, g_in1, all1 = _wgrad_layer(yp1, dmb1, hb1, dproj1, small1, pos, "wgrad_odd")
    gx, dproj0, dmb0, small0 = _even_bwd(dx1, xs, proj0, conv0, mixed0, pooled0, m0, pre_norm, post_norm, win0, shard, pool_wb,
                                         even_pool_scale, wout0)
    g_out0, g_in0, all0 = _wgrad_layer(mixp0, dmb0, hb0, dproj0, small0, pos, "wgrad_even")

    big_w = [even_w_in[0], even_w_out[0], odd_w_in[0], odd_w_out[0]]
    big_g = [g.reshape(w.shape) for g, w in zip([g_in0, g_out0, g_in1, g_out1], big_w)]
    big_m = [m_even_w_in[0], m_even_w_out[0], m_odd_w_in[0], m_odd_w_out[0]]
    big_v = [v_even_w_in[0], v_even_w_out[0], v_odd_w_in[0], v_odd_w_out[0]]
    big = _adamw_big(big_w, big_g, big_m, big_v)

    def taps_first(a):
        return jnp.swapaxes(a, 0, 1)

    small_w = [pre_norm, post_norm, taps_first(even_conv_w), even_pool_w[0], even_pool_scale, odd_ln_g, odd_ln_b, odd_w_s[0], odd_b_s[0]]
    small_m = [m_pre_norm, m_post_norm, taps_first(m_even_conv_w), m_even_pool_w[0], m_even_pool_scale, m_odd_ln_g, m_odd_ln_b,
               m_odd_w_s[0], m_odd_b_s[0]]
    small_v = [v_pre_norm, v_post_norm, taps_first(v_even_conv_w), v_even_pool_w[0], v_even_pool_scale, v_odd_ln_g, v_odd_ln_b,
               v_odd_w_s[0], v_odd_b_s[0]]
    sm, loss = _adamw_small(all0, all1, small_w, small_m, small_v)

    def lead(a):
        return a[None]

    per = {
        "pre_norm": sm["pre"], "post_norm": sm["post"],
        "even_w_in": [lead(a) for a in big[0]], "even_conv_w": [taps_first(a) for a in sm["conv"]],
        "even_pool_w": [lead(a) for a in sm["pw"]], "even_pool_scale": sm["ps"],
        "even_w_out": [lead(a) for a in big[1]], "odd_w_in": [lead(a) for a in big[2]],
        "odd_ln_g": sm["lng"], "odd_ln_b": sm["lnb"],
        "odd_w_s": [lead(a) for a in sm["ws"]], "odd_b_s": [lead(a) for a in sm["bs"]],
        "odd_w_out": [lead(a) for a in big[3]],
    }
    order = ["pre_norm", "post_norm", "even_w_in", "even_conv_w", "even_pool_w", "even_pool_scale", "even_w_out", "odd_w_in",
             "odd_ln_g", "odd_ln_b", "odd_w_s", "odd_b_s", "odd_w_out"]
    outs = [loss.reshape(()), gx[None]]
    for kind in range(4):
        outs += [per[nm][kind] for nm in order]
    return tuple(outs)
```

```python
import jax
import jax.numpy as jnp
from jax import lax
from jax.experimental import pallas as pl
from jax.experimental.pallas import tpu as pltpu

F32 = jnp.float32
BF16 = jnp.bfloat16
MESH = pl.DeviceIdType.MESH

D = 1024
W3 = 3 * D
QW = W3 // 4
AW = 512
GD = 128
CHUNK = 128
HEADS = 8
HALO = 16
POOL_WINDOWS = (2, 4, 8, 16)
EPS = 1e-6
TM_FWD = 512
TM_BWD = 256
TK = 1024
TK_IN = 2048
VMEM_LIMIT = 56 * 1024 * 1024

ADAM_LR, ADAM_B1, ADAM_B2, ADAM_EPS, ADAM_WD, ADAM_STEP = 0.001, 0.9, 0.999, 1e-08, 0.01, 10

S0_PRE, S0_POST, S0_CONV, S0_PS, S0_PW, S0_ROWS = 0, 8, 16, 48, 56, 568
S1_PRE, S1_POST, S1_LN, S1_BS, S1_WS, S1_LOSS, S1_ROWS = 0, 8, 16, 48, 56, 1080, 1088


def _nn(a, b):
    return jnp.dot(a, b, preferred_element_type=F32)


def _nt(a, b):
    return lax.dot_general(a, b, (((1,), (1,)), ((), ())), preferred_element_type=F32)


def _tn(a, b):
    return lax.dot_general(a, b, (((0,), (0,)), ((), ())), preferred_element_type=F32)


def _sigmoid(z):
    return 1.0 / (1.0 + jnp.exp(-z))


def _rms_fwd(x):
    r = lax.rsqrt(jnp.mean(x * x, axis=-1, keepdims=True) + EPS)
    return x * r, r


def _rms_bwd(dy, xh, r, g):
    dn = dy * g
    dx = r * (dn - xh * jnp.mean(xh * dn, axis=-1, keepdims=True))
    return dx, jnp.sum(dy * xh, axis=0, keepdims=True)


def _full(shape):
    nd = len(shape)
    return pl.BlockSpec(shape, lambda i, _n=nd: (0,) * _n, pipeline_mode=pl.Buffered(1))


def _full_out(shape):
    nd = len(shape)
    return pl.BlockSpec(shape, lambda i, _n=nd: (0,) * _n)


def _rows(tm, width, index=None):
    return pl.BlockSpec((tm, width), (lambda i: (i, 0)) if index is None else index)


def _params():
    return pltpu.CompilerParams(dimension_semantics=("arbitrary",), vmem_limit_bytes=VMEM_LIMIT)


def _position():
    x, y, c = lax.axis_index("x"), lax.axis_index("y"), lax.axis_index("c")
    return x, y, c


SHARD_ROWS = 24


def _shard_row(shard_ref, row):
    return jnp.concatenate([shard_ref[q, row:row + 1, :] for q in range(4)], axis=1)


def _shard_vector(shard_ref, first_row):
    return jnp.concatenate([shard_ref[q, first_row + k:first_row + k + 1, :] for q in range(4) for k in range(2)], axis=1)


def _even_mix(proj_ref, hc_ext, xp_ext, taps, pw_ref, ps_ref, first_row):
    tm = proj_ref.shape[0]
    xa = proj_ref[:, 0:AW]
    gb = proj_ref[:, AW:2 * AW]
    gc = proj_ref[:, 2 * AW:3 * AW]
    za = proj_ref[:, 3 * AW:4 * AW]
    xp = proj_ref[:, 4 * AW:5 * AW]
    zp = proj_ref[:, 5 * AW:6 * AW]
    hc = gc * xa
    hc_ext[HALO:, :] = hc
    e = hc_ext[...]
    conv = taps[2] * hc + taps[1] * pltpu.roll(e, 1, 0)[HALO:] + taps[0] * pltpu.roll(e, 2, 0)[HALO:]
    sa = _sigmoid(za)
    xp_ext[HALO:, :] = xp
    pos = first_row + lax.broadcasted_iota(jnp.int32, (tm, 1), 0)
    pooled, mixed, counts = [], [], []
    for g, w in enumerate(POOL_WINDOWS):
        cols = slice(g * GD, (g + 1) * GD)
        s = xp_ext[:, cols]
        for k in range(g + 1):
            s = s + pltpu.roll(s, 2 ** k, 0)
        count = jnp.minimum(pos + 1, w).astype(F32)
        pg = s[HALO:] / count - xp[:, cols]
        pooled.append(pg.astype(BF16))
        mixed.append(_nn(pooled[-1], pw_ref[g]))
        counts.append(count)
    mixed = jnp.concatenate(mixed, axis=-1)
    sb = _sigmoid(zp)
    return dict(xa=xa, gb=gb, gc=gc, za=za, zp=zp, hc=hc, conv=conv, sa=sa, sb=sb, pooled=pooled, mixed=mixed, counts=counts)


def _half_rows(ref, rows, who):
    return ref.at[pl.ds(pl.multiple_of(who * (rows // 2), 8), rows // 2), :]


def _store_permuted(ref, value):
    for ob in range(D // GD):
        nb = 4 * (ob % 2) + ob // 2
        ref[:, nb * GD:(nb + 1) * GD] = value[:, ob * GD:(ob + 1) * GD]


def _even_fwd(x, hb, post, win, shard, pwb, ps, wout, next_shards):
    S = x.shape[0]
    tm = TM_FWD
    nt = S // tm
    relay = (3 * nt) // 4
    n = len(next_shards)
    shard_rows = [p.shape[0] for p in next_shards]

    def body(x_ref, hb_ref, post_ref, win_ref, shard_ref, pw_ref, ps_ref, wout_ref, *rest):
        shard_refs, rest = rest[:n], rest[n:]
        x1_ref, proj_ref, m_ref, mixp_ref, conv_ref, mixed_ref, pooled_ref = rest[:7]
        full_refs, rest = rest[7:7 + n], rest[7 + n:]
        hc_ext, xp_ext, mix_sc = rest[:3]
        stage, rest = rest[3:3 + n], rest[3 + n:]
        send_sems, recv_sems, local_sems = rest
        i = pl.program_id(0)
        px, py, pc = _position()
        me = 2 * px + py
        chips = _chips(px, py)

        def ici(a, j):
            return pltpu.make_async_remote_copy(
                src_ref=_half_rows(shard_refs[a], shard_rows[a], pc), dst_ref=_half_rows(full_refs[a].at[me], shard_rows[a], pc),
                send_sem=send_sems.at[a, j], recv_sem=recv_sems.at[a, j], device_id=(*chips[j], pc), device_id_type=MESH)

        def ici_arrival(a, j):
            src = 2 * chips[j][0] + chips[j][1]
            return pltpu.make_async_remote_copy(
                src_ref=_half_rows(shard_refs[a], shard_rows[a], pc), dst_ref=_half_rows(full_refs[a].at[src], shard_rows[a], pc),
                send_sem=send_sems.at[a, j], recv_sem=recv_sems.at[a, j], device_id=(*chips[j], pc), device_id_type=MESH)

        def relay_copy(a, j, who):
            src = 2 * chips[j][0] + chips[j][1]
            region = _half_rows(full_refs[a].at[src], shard_rows[a], who)
            return pltpu.make_async_remote_copy(
                src_ref=region, dst_ref=region, send_sem=send_sems.at[a, 3 + j], recv_sem=recv_sems.at[a, 3 + j],
                device_id=(px, py, 1 - pc), device_id_type=MESH)

        def own_copy(a):
            return pltpu.make_async_copy(stage[a], full_refs[a].at[me], local_sems.at[a])

        @pl.when(i == 0)
        def _():
            hc_ext[0:HALO, :] = jnp.zeros((HALO, AW), F32)
            xp_ext[0:HALO, :] = jnp.zeros((HALO, AW), F32)
            for a in range(n):
                for j in range(3):
                    ici(a, j).start()
            for a in range(n):
                load = pltpu.make_async_copy(shard_refs[a], stage[a], local_sems.at[a])
                load.start()
                load.wait()
                own_copy(a).start()

        @pl.when(i == relay)
        def _():
            for j in range(3):
                for a in range(n):
                    ici_arrival(a, j).wait_recv()
                    relay_copy(a, j, pc).start()

        hb = hb_ref[...]
        for q in range(4):
            proj_ref[:, q * QW:(q + 1) * QW] = _nn(hb, win_ref[q])
        t = _even_mix(proj_ref, hc_ext, xp_ext, [_shard_row(shard_ref, k) for k in range(3)], pw_ref, ps_ref, i * tm)
        conv_ref[...] = t["conv"]
        mixed_ref[...] = t["mixed"]
        for g in range(4):
            pooled_ref[:, g * GD:(g + 1) * GD] = t["pooled"][g]
        mix_sc[:, 0:AW] = (t["gb"] * t["conv"] * (t["za"] * t["sa"])).astype(BF16)
        mix_sc[:, AW:2 * AW] = (t["mixed"] * ps_ref[...] * (t["zp"] * t["sb"])).astype(BF16)
        mix = mix_sc[...]
        _store_permuted(mixp_ref, mix)
        m = _nn(mix, wout_ref[...])
        m_ref[...] = m
        mh, _ = _rms_fwd(m)
        x1_ref[...] = x_ref[...] + mh * post_ref[0:1, :]
        hc_ext[0:HALO, :] = hc_ext[tm:tm + HALO, :]
        xp_ext[0:HALO, :] = xp_ext[tm:tm + HALO, :]

        @pl.when(i == nt - 1)
        def _():
            for j in range(3):
                for a in range(n):
                    relay_copy(a, j, 1 - pc).wait_recv()
            for a in range(n):
                for j in range(3):
                    ici(a, j).wait_send()
                    relay_copy(a, j, pc).wait_send()
                own_copy(a).wait()

    any_spec = pl.BlockSpec(memory_space=pl.ANY)
    return pl.pallas_call(
        body, name="even_fwd", grid=(nt,),
        in_specs=[_rows(tm, D), _rows(tm, D), _full((2, D)), _full((4, D, QW)), _full((4, SHARD_ROWS, GD)), _full((4, GD, GD)),
                  _full((1, AW)), _full((D, D))] + [any_spec] * n,
        out_specs=[_rows(tm, D), _rows(tm, W3), _rows(tm, D), _rows(tm, D), _rows(tm, AW), _rows(tm, AW), _rows(tm, AW)]
        + [any_spec] * n,
        out_shape=[jax.ShapeDtypeStruct((S, D), F32), jax.ShapeDtypeStruct((S, W3), F32), jax.ShapeDtypeStruct((S, D), F32),
                   jax.ShapeDtypeStruct((S, D), BF16),
                   jax.ShapeDtypeStruct((S, AW), F32), jax.ShapeDtypeStruct((S, AW), F32), jax.ShapeDtypeStruct((S, AW), BF16)]
        + [jax.ShapeDtypeStruct((4, *p.shape), p.dtype) for p in next_shards],
        scratch_shapes=[pltpu.VMEM((tm + HALO, AW), F32), pltpu.VMEM((tm + HALO, AW), F32), pltpu.VMEM((tm, D), BF16)]
        + [pltpu.VMEM(p.shape, p.dtype) for p in next_shards]
        + [pltpu.SemaphoreType.DMA((n, 6)), pltpu.SemaphoreType.DMA((n, 6)), pltpu.SemaphoreType.DMA((n,))],
        compiler_params=_params(),
    )(x, hb, post, win, shard, pwb, ps, wout, *next_shards)


def _chunks_side_by_side(a, h):
    return jnp.concatenate([a[n * CHUNK:(n + 1) * CHUNK, h * GD:(h + 1) * GD] for n in range(a.shape[0] // CHUNK)], axis=1)


def _odd_mix(proj_ref, lng, lnb, ws_ref, bias_ref, sv_ref):
    tm = proj_ref.shape[0]
    u = proj_ref[:, 0:D]
    v = proj_ref[:, D:2 * D]
    z = proj_ref[:, 2 * D:3 * D]
    mu = jnp.mean(v, axis=-1, keepdims=True)
    vc = v - mu
    rs = lax.rsqrt(jnp.mean(vc * vc, axis=-1, keepdims=True) + EPS)
    vh = vc * rs
    vnb = (vh * lng + lnb).astype(BF16)
    for h in range(HEADS):
        sv = _nn(ws_ref[h], _chunks_side_by_side(vnb, h))
        for n in range(tm // CHUNK):
            sv_ref[n * CHUNK:(n + 1) * CHUNK, h * GD:(h + 1) * GD] = sv[:, n * GD:(n + 1) * GD] + bias_ref[h]
    return dict(u=u, z=z, vh=vh, rs=rs, vnb=vnb, sz=_sigmoid(z))


def _odd_fwd(x1, tgt, pre, post, win, shard, wsb, bias, wout):
    S = x1.shape[0]
    tm = TM_FWD
    nt = S // tm

    def body(x_ref, tgt_ref, pre_ref, post_ref, win_ref, shard_ref, ws_ref, bias_ref, wout_ref,
             proj_ref, m_ref, hb_ref, yb_ref, dx2_ref, loss_ref, sv_ref):
        i = pl.program_id(0)

        @pl.when(i == 0)
        def _():
            loss_ref[...] = jnp.zeros((8, GD), F32)

        xv = x_ref[...]
        xh, _ = _rms_fwd(xv)
        hb = (xh * pre_ref[...]).astype(BF16)
        hb_ref[...] = hb
        for q in range(4):
            proj_ref[:, q * QW:(q + 1) * QW] = _nn(hb, win_ref[q])
        t = _odd_mix(proj_ref, _shard_vector(shard_ref, 8), _shard_vector(shard_ref, 16), ws_ref, bias_ref, sv_ref)
        yb = (t["u"] * sv_ref[...] * (t["z"] * t["sz"])).astype(BF16)
        _store_permuted(yb_ref, yb)
        m = _nn(yb, wout_ref[...])
        m_ref[...] = m
        mh, _ = _rms_fwd(m)
        err = xv + mh * post_ref[...] - tgt_ref[...]
        dx2_ref[...] = err * (1.0 / D)
        part = 0.5 * jnp.sum(jnp.mean(err * err, axis=-1, keepdims=True), axis=0, keepdims=True)
        loss_ref[...] += jnp.broadcast_to(part, (8, GD))

    return pl.pallas_call(
        body, name="odd_fwd", grid=(nt,),
        in_specs=[_rows(tm, D), _rows(tm, D), _full((1, D)), _full((1, D)), _full((4, D, QW)), _full((4, SHARD_ROWS, GD)),
                  _full((HEADS, CHUNK, CHUNK)), _full((HEADS, CHUNK, GD)), _full((D, D))],
        out_specs=[_rows(tm, W3), _rows(tm, D), _rows(tm, D), _rows(tm, D), _rows(tm, D), _full_out((8, GD))],
        out_shape=[jax.ShapeDtypeStruct((S, W3), F32), jax.ShapeDtypeStruct((S, D), F32), jax.ShapeDtypeStruct((S, D), BF16),
                   jax.ShapeDtypeStruct((S, D), BF16), jax.ShapeDtypeStruct((S, D), F32), jax.ShapeDtypeStruct((8, GD), F32)],
        scratch_shapes=[pltpu.VMEM((tm, D), F32)],
        compiler_params=_params(),
    )(x1, tgt, pre, post, win, shard, wsb, bias, wout)


def _store_rows(ref, row0, value):
    r, width = value.shape
    for a in range(r):
        for k in range(width // GD):
            ref[row0 + a * (width // GD) + k:row0 + a * (width // GD) + k + 1, :] = value[a:a + 1, k * GD:(k + 1) * GD]


def _proj_bwd(dproj, win_ref, x, dy, pre):
    dh = _nt(dproj[:, 0:QW], win_ref[0])
    for q in range(1, 4):
        dh += _nt(dproj[:, q * QW:(q + 1) * QW], win_ref[q])
    xh, r = _rms_fwd(x)
    dxn, dpre = _rms_bwd(dh, xh, r, pre)
    return dy + dxn, dpre


def _odd_bwd(dx2, x1, proj, m, loss, pre, post, win, shard, wsb, wsbt, bias, wout):
    S = x1.shape[0]
    tm = TM_BWD
    nt = S // tm

    def body(dy_ref, x_ref, proj_ref, m_ref, loss_ref, pre_ref, post_ref, win_ref, shard_ref, ws_ref, wst_ref, bias_ref,
             wout_ref, dx_ref, dproj_ref, dmb_ref, small_ref, sv_ref, dvn_ref, acc1024, dws_acc, dbs_acc):
        i = pl.program_id(0)
        lng = _shard_vector(shard_ref, 8)

        @pl.when(i == 0)
        def _():
            acc1024[...] = jnp.zeros_like(acc1024)
            dws_acc[...] = jnp.zeros_like(dws_acc)
            dbs_acc[...] = jnp.zeros_like(dbs_acc)

        dy = dy_ref[...]
        mh, rm = _rms_fwd(m_ref[...])
        dm, dpost = _rms_bwd(dy, mh, rm, post_ref[...])
        dmb = dm.astype(BF16)
        dmb_ref[...] = dmb
        dyv = _nt(dmb, wout_ref[...])
        t = _odd_mix(proj_ref, lng, _shard_vector(shard_ref, 16), ws_ref, bias_ref, sv_ref)
        u, z, sz, sv = t["u"], t["z"], t["sz"], sv_ref[...]
        dproj_ref[:, 0:D] = (dyv * sv * (z * sz)).astype(BF16)
        dproj_ref[:, 2 * D:3 * D] = (dyv * u * sv * (sz * (1.0 + z * (1.0 - sz)))).astype(BF16)
        dsv = dyv * u * (z * sz)
        dsvb = dsv.astype(BF16)
        for h in range(HEADS):
            dsv_h = _chunks_side_by_side(dsvb, h)
            dvn_h = _nn(wst_ref[h], dsv_h)
            dws_acc[h] += _nt(dsv_h, _chunks_side_by_side(t["vnb"], h))
            for n in range(tm // CHUNK):
                rows, cols = slice(n * CHUNK, (n + 1) * CHUNK), slice(h * GD, (h + 1) * GD)
                dvn_ref[rows, cols] = dvn_h[:, n * GD:(n + 1) * GD]
                dbs_acc[h] += dsv[rows, cols]
        dvn = dvn_ref[...]
        vh = t["vh"]
        dvh = dvn * lng
        dv = t["rs"] * (dvh - jnp.mean(dvh, axis=-1, keepdims=True) - vh * jnp.mean(dvh * vh, axis=-1, keepdims=True))
        dproj_ref[:, D:2 * D] = dv.astype(BF16)
        dx_ref[...], dpre = _proj_bwd(dproj_ref[...], win_ref, x_ref[...], dy, pre_ref[...])
        acc1024[0:1, :] += dpre
        acc1024[1:2, :] += dpost
        acc1024[2:3, :] += jnp.sum(dvn * vh, axis=0, keepdims=True)
        acc1024[3:4, :] += jnp.sum(dvn, axis=0, keepdims=True)

        @pl.when(i == nt - 1)
        def _():
            small_ref[...] = jnp.zeros_like(small_ref)
            _store_rows(small_ref, S1_PRE, acc1024[0:1, :])
            _store_rows(small_ref, S1_POST, acc1024[1:2, :])
            for q in range(4):
                _store_rows(small_ref, S1_LN + 8 * q, acc1024[2:3, 2 * q * GD:(2 * q + 2) * GD])
                _store_rows(small_ref, S1_LN + 8 * q + 2, acc1024[3:4, 2 * q * GD:(2 * q + 2) * GD])
            lower = lax.broadcasted_iota(jnp.int32, (CHUNK, CHUNK), 0) >= lax.broadcasted_iota(jnp.int32, (CHUNK, CHUNK), 1)
            for h in range(HEADS):
                small_ref[S1_WS + h * CHUNK:S1_WS + (h + 1) * CHUNK, :] = jnp.where(lower, dws_acc[h], 0.0)
                small_ref[S1_BS + h:S1_BS + h + 1, :] = jnp.sum(dbs_acc[h].T, axis=0, keepdims=True)
            small_ref[S1_LOSS:S1_LOSS + 8, :] = loss_ref[...]

    return pl.pallas_call(
        body, name="odd_bwd", grid=(nt,),
        in_specs=[_rows(tm, D), _rows(tm, D), _rows(tm, W3), _rows(tm, D), _full((8, GD)), _full((1, D)), _full((1, D)),
                  _full((4, D, QW)), _full((4, SHARD_ROWS, GD)), _full((HEADS, CHUNK, CHUNK)), _full((HEADS, CHUNK, CHUNK)),
                  _full((HEADS, CHUNK, GD)), _full((D, D))],
        out_specs=[_rows(tm, D), _rows(tm, W3), _rows(tm, D), _full_out((S1_ROWS, GD))],
        out_shape=[jax.ShapeDtypeStruct((S, D), F32), jax.ShapeDtypeStruct((S, W3), BF16), jax.ShapeDtypeStruct((S, D), BF16),
                   jax.ShapeDtypeStruct((S1_ROWS, GD), F32)],
        scratch_shapes=[pltpu.VMEM((tm, D), F32), pltpu.VMEM((tm, D), F32), pltpu.VMEM((8, D), F32),
                        pltpu.VMEM((HEADS, CHUNK, CHUNK), F32), pltpu.VMEM((HEADS, CHUNK, GD), F32)],
        compiler_params=_params(),
    )(dx2, x1, proj, m, loss, pre, post, win, shard, wsb, wsbt, bias, wout)


def _even_bwd(dx1, x, proj, conv, mixed, pooled, m, pre, post, win, shard, pwb, ps, wout):
    S = x.shape[0]
    tm = TM_BWD
    nt = S // tm
    L = tm + HALO

    def rev(i):
        return (nt - 1 - i, 0)

    def body(dy_ref, x_ref, proj_ref, conv_ref, mixed_ref, pooled_ref, m_ref, pre_ref, post_ref, win_ref, shard_ref, pw_ref, ps_ref,
             wout_ref, dx_ref, dproj_ref, dmb_ref, small_ref, dconv_ext, q_ext, acc1024, acc512, dpw_acc):
        i = pl.program_id(0)
        tile = nt - 1 - i

        @pl.when(i == 0)
        def _():
            dconv_ext[tm:L, :] = jnp.zeros((HALO, AW), F32)
            q_ext[tm:L, :] = jnp.zeros((HALO, AW), F32)
            acc1024[...] = jnp.zeros_like(acc1024)
            acc512[...] = jnp.zeros_like(acc512)
            dpw_acc[...] = jnp.zeros_like(dpw_acc)

        dy = dy_ref[...]
        mh, rm = _rms_fwd(m_ref[...])
        dm, dpost = _rms_bwd(dy, mh, rm, post_ref[0:1, :])
        dmb = dm.astype(BF16)
        dmb_ref[...] = dmb
        dmix = _nt(dmb, wout_ref[...])
        dya, dyb = dmix[:, 0:AW], dmix[:, AW:2 * AW]
        xa, gb, gc, za = (proj_ref[:, k * AW:(k + 1) * AW] for k in range(4))
        zp = proj_ref[:, 5 * AW:6 * AW]
        hc = gc * xa
        conv = conv_ref[...]
        sa = _sigmoid(za)
        silu_a = za * sa
        dproj_ref[:, AW:2 * AW] = (dya * conv * silu_a).astype(BF16)
        dproj_ref[:, 3 * AW:4 * AW] = (dya * gb * conv * (sa * (1.0 + za * (1.0 - sa)))).astype(BF16)
        dconv = dya * gb * silu_a
        dconv_ext[0:tm, :] = dconv
        e = dconv_ext[...]
        dc1 = pltpu.roll(e, L - 1, 0)[0:tm]
        dc2 = pltpu.roll(e, L - 2, 0)[0:tm]
        taps = [_shard_row(shard_ref, k) for k in range(3)]
        dhc = taps[2] * dconv + taps[1] * dc1 + taps[0] * dc2
        dproj_ref[:, 0:AW] = (dhc * gc).astype(BF16)
        dproj_ref[:, 2 * AW:3 * AW] = (dhc * xa).astype(BF16)
        acc512[0:1, :] += jnp.sum(dc2 * hc, axis=0, keepdims=True)
        acc512[1:2, :] += jnp.sum(dc1 * hc, axis=0, keepdims=True)
        acc512[2:3, :] += jnp.sum(dconv * hc, axis=0, keepdims=True)

        sb, mixed = _sigmoid(zp), mixed_ref[...]
        silu_b = zp * sb
        acc512[3:4, :] += jnp.sum(dyb * mixed * silu_b, axis=0, keepdims=True)
        dmixedb = (dyb * ps_ref[...] * silu_b).astype(BF16)
        dproj_ref[:, 5 * AW:6 * AW] = (dyb * mixed * ps_ref[...] * (sb * (1.0 + zp * (1.0 - sb)))).astype(BF16)
        pos = tile * tm + lax.broadcasted_iota(jnp.int32, (tm, 1), 0)
        for g, w in enumerate(POOL_WINDOWS):
            cols = slice(g * GD, (g + 1) * GD)
            dpw_acc[g] += _tn(pooled_ref[:, cols], dmixedb[:, cols])
            dpooled = _nt(dmixedb[:, cols], pw_ref[g])
            q_ext[0:tm, cols] = dpooled / jnp.minimum(pos + 1, w).astype(F32)
            s = q_ext[:, cols]
            for k in range(g + 1):
                s = s + pltpu.roll(s, L - 2 ** k, 0)
            dproj_ref[:, 4 * AW + g * GD:4 * AW + (g + 1) * GD] = (s[0:tm] - dpooled).astype(BF16)
        dconv_ext[tm:L, :] = dconv_ext[0:HALO, :]
        q_ext[tm:L, :] = q_ext[0:HALO, :]

        dx_ref[...], dpre = _proj_bwd(dproj_ref[...], win_ref, x_ref[...], dy, pre_ref[0:1, :])
        acc1024[0:1, :] += dpre
        acc1024[1:2, :] += dpost

        @pl.when(i == nt - 1)
        def _():
            small_ref[...] = jnp.zeros_like(small_ref)
            _store_rows(small_ref, S0_PRE, acc1024[0:1, :])
            _store_rows(small_ref, S0_POST, acc1024[1:2, :])
            for q in range(4):
                for k in range(3):
                    small_ref[S0_CONV + 8 * q + k:S0_CONV + 8 * q + k + 1, :] = acc512[k:k + 1, q * GD:(q + 1) * GD]
            _store_rows(small_ref, S0_PS, acc512[3:4, :])
            for g in range(4):
                small_ref[S0_PW + g * GD:S0_PW + (g + 1) * GD, :] = dpw_acc[g]

    return pl.pallas_call(
        body, name="even_bwd", grid=(nt,),
        in_specs=[_rows(tm, D, rev), _rows(tm, D, rev), _rows(tm, W3, rev), _rows(tm, AW, rev), _rows(tm, AW, rev), _rows(tm, AW, rev),
                  _rows(tm, D, rev),
                  _full((2, D)), _full((2, D)), _full((4, D, QW)), _full((4, SHARD_ROWS, GD)), _full((4, GD, GD)), _full((1, AW)),
                  _full((D, D))],
        out_specs=[_rows(tm, D, rev), _rows(tm, W3, rev), _rows(tm, D, rev), _full_out((S0_ROWS, GD))],
        out_shape=[jax.ShapeDtypeStruct((S, D), F32), jax.ShapeDtypeStruct((S, W3), BF16), jax.ShapeDtypeStruct((S, D), BF16),
                   jax.ShapeDtypeStruct((S0_ROWS, GD), F32)],
        scratch_shapes=[pltpu.VMEM((L, AW), F32), pltpu.VMEM((L, AW), F32),
                        pltpu.VMEM((8, D), F32), pltpu.VMEM((8, AW), F32), pltpu.VMEM((4, GD, GD), F32)],
        compiler_params=_params(),
    )(dx1, x, proj, conv, mixed, pooled, m, pre, post, win, shard, pwb, ps, wout)


def _owner_id(me, relation, c):
    q = jnp.bitwise_xor(me, relation)
    return (q // 2, q % 2, c)


def _small_gather_steps(small_ref, all_ref, stage, send_sems, recv_sems, local_sem):
    x, y, c = _position()
    chips = _chips(x, y)

    def slot(chip, core):
        return 4 * chip[0] + 2 * chip[1] + core

    def copy(k, src, block, to):
        return pltpu.make_async_remote_copy(src_ref=src, dst_ref=all_ref.at[block], send_sem=send_sems.at[k],
                                            recv_sem=recv_sems.at[k], device_id=to, device_id_type=MESH)

    def own_copy():
        return pltpu.make_async_copy(stage, all_ref.at[slot((x, y), c)], local_sem)

    def first_sends():
        mine = slot((x, y), c)
        return [copy(0, small_ref, mine, (x, y, 1 - c))] + [copy(1 + j, small_ref, mine, (*chip, c)) for j, chip in enumerate(chips)]

    def relays():
        return [copy(4 + j, all_ref.at[slot(chip, c)], slot(chip, c), (x, y, 1 - c)) for j, chip in enumerate(chips)]

    def start():
        for cp in first_sends():
            cp.start()
        load = pltpu.make_async_copy(small_ref, stage, local_sem)
        load.start()
        load.wait()
        own_copy().start()

    def relay():
        for j, chip in enumerate(chips):
            copy(1 + j, small_ref, slot(chip, c), (*chip, c)).wait_recv()
        for cp in relays():
            cp.start()

    def finish():
        copy(0, small_ref, slot((x, y), 1 - c), (x, y, 1 - c)).wait_recv()
        for j, chip in enumerate(chips):
            copy(4 + j, small_ref, slot(chip, 1 - c), (x, y, 1 - c)).wait_recv()
        for cp in first_sends() + relays():
            cp.wait_send()
        own_copy().wait()

    return start, relay, finish


def _wgrad_layer(a_out, b_out, a_in, b_in, small, pos, name):
    S = a_in.shape[0]
    nko = S // TK
    nki = S // TK_IN
    hm = D // 2
    qr = hm // 4

    def out_index(s, pos_ref):
        return (jnp.minimum(s, nko - 1), 0)

    def a_in_index(s, pos_ref):
        return (jnp.where(s >= nko, (s - nko) % nki, 0), 0)

    def b_in_index(s, pos_ref):
        return (jnp.where(s >= nko, (s - nko) % nki, 0), jnp.bitwise_xor(pos_ref[1], 3 - jnp.maximum((s - nko) // nki, 0)))

    def body(pos_ref, ao_ref, bo_ref, a_ref, b_ref, small_ref, gout_ref, gin_ref, all_ref,
             acc, rbuf, sbuf, arr, mine, acc_o, rbuf_o, total_o, sbuf_o, arr_o, mine_o, stage,
             d2d_send, d2d_recv, ici_send, ici_recv, d2d_o_send, d2d_o_recv, ici_o_send, ici_o_recv,
             share_send, share_recv, local_sems, g_send, g_recv, g_local):
        s = pl.program_id(0)
        in_step = jnp.maximum(s - nko, 0)
        blk = jnp.where(s < nko, 0, 1 + in_step // nki)
        k = jnp.where(s < nko, s, in_step % nki)
        j = blk - 1
        x, y, c = _position()
        me = 2 * x + y
        sibling = (x, y, 1 - c)
        last = k == jnp.where(s < nko, nko - 1, nki - 1)
        gather_start, gather_relay, gather_finish = _small_gather_steps(small_ref, all_ref, stage, g_send, g_recv, g_local)

        def other_half(ref):
            return ref.at[pl.ds(pl.multiple_of((1 - c) * hm, hm), hm), :]

        def own_half(ref):
            return ref[pl.ds(pl.multiple_of(c * hm, hm), hm), :]

        def to_sibling(jj):
            return pltpu.make_async_remote_copy(
                src_ref=other_half(acc.at[jj % 2]), dst_ref=rbuf.at[jj], send_sem=d2d_send.at[jj], recv_sem=d2d_recv.at[jj],
                device_id=sibling, device_id_type=MESH)

        def to_owner(jj):
            return pltpu.make_async_remote_copy(
                src_ref=sbuf.at[jj], dst_ref=arr.at[2 - jj], send_sem=ici_send.at[jj], recv_sem=ici_recv.at[jj],
                device_id=_owner_id(me, 3 - jj, c), device_id_type=MESH)

        def out_to_sibling():
            return pltpu.make_async_remote_copy(
                src_ref=other_half(acc_o), dst_ref=rbuf_o, send_sem=d2d_o_send, recv_sem=d2d_o_recv,
                device_id=sibling, device_id_type=MESH)

        def out_to_owner(r):
            return pltpu.make_async_remote_copy(
                src_ref=sbuf_o.at[r], dst_ref=arr_o.at[r], send_sem=ici_o_send.at[r], recv_sem=ici_o_recv.at[r],
                device_id=_owner_id(me, r + 1, c), device_id_type=MESH)

        def pair_sum(jj):
            to_sibling(jj).wait_recv()
            return own_half(acc.at[jj % 2]) + rbuf[jj]

        def send_block(jj):
            sbuf[jj] = pair_sum(jj).astype(BF16)
            to_owner(jj).start()

        @pl.when(s == 0)
        def _():
            gather_start()

        @pl.when((blk == 3) & (k == 0))
        def _():
            gather_relay()

        @pl.when((blk == 0) & (k == 0))
        def _():
            acc_o[...] = _tn(ao_ref[...], bo_ref[...])

        @pl.when((blk == 0) & (k > 0))
        def _():
            acc_o[...] += _tn(ao_ref[...], bo_ref[...])

        @pl.when((blk == 0) & last)
        def _():
            out_to_sibling().start()

        @pl.when((blk >= 3) & (k == 0))
        def _():
            to_sibling(j - 2).wait_send()

        @pl.when((blk >= 1) & (k == 0))
        def _():
            acc[j % 2] = _tn(a_ref[...], b_ref[...])

        @pl.when((blk >= 1) & (k > 0))
        def _():
            acc[j % 2] += _tn(a_ref[...], b_ref[...])

        @pl.when((blk >= 1) & last)
        def _():
            to_sibling(j).start()

        @pl.when((blk == 1) & last)
        def _():
            out_to_sibling().wait_recv()
            total_o[...] = own_half(acc_o) + rbuf_o[...]
            for r in range(3):
                q = jnp.bitwise_xor(me, r + 1)
                sbuf_o[r] = total_o[pl.ds(pl.multiple_of(q * qr, qr), qr), :].astype(BF16)
                out_to_owner(r).start()

        @pl.when((blk == 2) & last)
        def _():
            send_block(0)

        @pl.when((blk == 3) & last)
        def _():
            send_block(1)
            send_block(2)

        @pl.when((blk == 4) & last)
        def _():
            g_in = pair_sum(3)
            g_out = total_o[pl.ds(pl.multiple_of(me * qr, qr), qr), :]
            to_sibling(2).wait_send()
            to_sibling(3).wait_send()
            out_to_sibling().wait_send()
            for r in range(3):
                to_owner(r).wait()
                out_to_owner(r).wait()
            for r in range(3):
                g_in = g_in + arr[r].astype(F32)
                g_out = g_out + arr_o[r].astype(F32)
            mine[...] = g_in
            mine_o[...] = g_out
            copies = []
            for idx, (src, dst) in enumerate([(mine, gin_ref), (mine_o, gout_ref)]):
                copies.append(pltpu.make_async_remote_copy(
                    src_ref=src, dst_ref=dst.at[c], send_sem=share_send.at[idx], recv_sem=share_recv.at[idx],
                    device_id=sibling, device_id_type=MESH))
                copies.append(pltpu.make_async_copy(src, dst.at[c], local_sems.at[idx]))
            for cp in copies:
                cp.start()
            for cp in copies:
                cp.wait()
            gather_finish()

    any_spec = pl.BlockSpec(memory_space=pl.ANY)
    grid_spec = pltpu.PrefetchScalarGridSpec(
        num_scalar_prefetch=1, grid=(nko + 4 * nki,),
        in_specs=[pl.BlockSpec((TK, D), out_index), pl.BlockSpec((TK, D), out_index),
                  pl.BlockSpec((TK_IN, D), a_in_index), pl.BlockSpec((TK_IN, QW), b_in_index), any_spec],
        out_specs=[any_spec, any_spec, any_spec],
        scratch_shapes=[pltpu.VMEM((2, D, QW), F32), pltpu.VMEM((4, hm, QW), F32), pltpu.VMEM((3, hm, QW), BF16),
                        pltpu.VMEM((3, hm, QW), BF16), pltpu.VMEM((hm, QW), F32),
                        pltpu.VMEM((D, D), F32), pltpu.VMEM((hm, D), F32), pltpu.VMEM((hm, D), F32), pltpu.VMEM((3, qr, D), BF16),
                        pltpu.VMEM((3, qr, D), BF16), pltpu.VMEM((qr, D), F32),
                        pltpu.VMEM(small.shape, F32),
                        pltpu.SemaphoreType.DMA((4,)), pltpu.SemaphoreType.DMA((4,)),
                        pltpu.SemaphoreType.DMA((3,)), pltpu.SemaphoreType.DMA((3,)),
                        pltpu.SemaphoreType.DMA, pltpu.SemaphoreType.DMA,
                        pltpu.SemaphoreType.DMA((3,)), pltpu.SemaphoreType.DMA((3,)),
                        pltpu.SemaphoreType.DMA((2,)), pltpu.SemaphoreType.DMA((2,)), pltpu.SemaphoreType.DMA((2,)),
                        pltpu.SemaphoreType.DMA((7,)), pltpu.SemaphoreType.DMA((7,)), pltpu.SemaphoreType.DMA])
    return pl.pallas_call(
        body, name=name, grid_spec=grid_spec,
        out_shape=[jax.ShapeDtypeStruct((2, qr, D), F32), jax.ShapeDtypeStruct((2, hm, QW), F32),
                   jax.ShapeDtypeStruct((8, *small.shape), F32)],
        compiler_params=pltpu.CompilerParams(dimension_semantics=("arbitrary",), vmem_limit_bytes=VMEM_LIMIT),
    )(pos, a_out, b_out, a_in, b_in, small)


def _chips(x, y):
    return [(1 - x, y), (x, 1 - y), (1 - x, 1 - y)]


def _gather_weights(parts, split, x, pre, next_parts, pool_w, w_s, b_s):
    n = len(parts)
    nn = len(next_parts)
    S = x.shape[0]
    tm = TM_FWD
    nt = S // tm
    staged = [a for a in range(n) if split[a]]

    ns = nn + 3

    def body(x_ref, pre_ref, *refs):
        raw, refs = refs[:n], refs[n:]
        side_hbm, hb_ref = refs[:ns], refs[ns]
        refs = refs[ns + 1:]
        outs, refs = refs[:n], refs[n:]
        next_outs, poolb_ref, tril_ref, trilt_ref, bias_ref = refs[:nn], refs[nn], refs[nn + 1], refs[nn + 2], refs[nn + 3]
        refs = refs[nn + 4:]
        stage, refs = refs[:len(staged)], refs[len(staged):]
        side, (send_sems, recv_sems, local_sems, side_sems) = refs[:ns], refs[ns:]
        next_raw, pool_ref, ws_ref, bs_ref = side[:nn], side[nn], side[nn + 1], side[nn + 2]
        ins = [stage[staged.index(a)] if split[a] else raw[a] for a in range(n)]
        i = pl.program_id(0)

        def fetch(k):
            return pltpu.make_async_copy(side_hbm[k], side[k], side_sems.at[k])

        x, y, c = _position()
        me = 2 * x + y
        nbr_x, nbr_y, diag = _chips(x, y)
        id_x, id_y, id_d = (2 * chip[0] + chip[1] for chip in (nbr_x, nbr_y, diag))
        sibling = (x, y, 1 - c)

        def rows(a, ref, who, piece=None):
            r = parts[a].shape[0] // 2
            if piece is None:
                return ref.at[pl.ds(pl.multiple_of(who * r, 16), r), :]
            return ref.at[pl.ds(pl.multiple_of(who * r + piece * (r // 2), 16), r // 2), :]

        def copy(a, k, src, dst, to):
            return pltpu.make_async_remote_copy(src_ref=src, dst_ref=dst, send_sem=send_sems.at[a, k], recv_sem=recv_sems.at[a, k],
                                                device_id=to, device_id_type=MESH)

        def direct(a, k, chip):
            if split[a]:
                return copy(a, k, rows(a, ins[a], c), rows(a, outs[a].at[me], c), (*chip, c))
            return copy(a, k, ins[a], outs[a].at[me], (*chip, c))

        def arrival(a, k, src_id):
            if split[a]:
                return copy(a, k, rows(a, ins[a], c), rows(a, outs[a].at[src_id], c), (*nbr_x, c))
            return copy(a, k, ins[a], outs[a].at[src_id], (*nbr_x, c))

        def pass_on(a, k, src_id, piece, chip):
            region = rows(a, outs[a].at[src_id], c, piece)
            return copy(a, k, region, region, (*chip, c))

        def hand_over(a, k, src_id, who):
            region = rows(a, outs[a].at[src_id], who)
            return copy(a, k, region, region, sibling)

        def local(a):
            return pltpu.make_async_copy(ins[a], outs[a].at[me], local_sems.at[a])

        def first_sends(a):
            return [direct(a, 0, nbr_x), direct(a, 1, nbr_y)] + ([] if split[a] else [direct(a, 2, diag)])

        def after_x(a):
            return [pass_on(a, 3, id_x, 1, nbr_y), hand_over(a, 4, id_x, c)] if split[a] else []

        def after_y(a):
            return [pass_on(a, 2, id_y, 0, nbr_x), hand_over(a, 5, id_y, c)] if split[a] else []

        def after_diag(a):
            return [hand_over(a, 6, id_d, c)] if split[a] else []

        @pl.when(i == 0)
        def _():
            for a in staged:
                ins[a][...] = raw[a][...].astype(BF16)
            for a in range(n):
                local(a).start()
                for cp in first_sends(a):
                    cp.start()
            for k in range(ns):
                fetch(k).start()

        @pl.when(i == 1)
        def _():
            for k in range(ns):
                fetch(k).wait()
            for a in range(nn):
                next_outs[a][...] = next_raw[a][...].astype(BF16)
            poolb_ref[...] = pool_ref[...].astype(BF16)
            lower = lax.broadcasted_iota(jnp.int32, (CHUNK, CHUNK), 0) >= lax.broadcasted_iota(jnp.int32, (CHUNK, CHUNK), 1)
            for h in range(HEADS):
                tril = jnp.where(lower, ws_ref[h], 0.0)
                tril_ref[h] = tril.astype(BF16)
                trilt_ref[h] = tril.T.astype(BF16)
                bias_ref[h] = jnp.broadcast_to(bs_ref[h:h + 1, :], (CHUNK, GD)).T

        xh, _ = _rms_fwd(x_ref[...])
        hb_ref[...] = (xh * pre_ref[0:1, :]).astype(BF16)

        @pl.when(i == nt - 2)
        def _():
            for a in range(n):
                arrival(a, 0, id_x).wait_recv()
                for cp in after_x(a):
                    cp.start()
            for a in range(n):
                arrival(a, 1, id_y).wait_recv()
                for cp in after_y(a):
                    cp.start()

        @pl.when(i == nt - 1)
        def _():
            for a in range(n):
                if split[a]:
                    pass_on(a, 2, id_d, 0, nbr_x).wait_recv()
                    pass_on(a, 3, id_d, 1, nbr_y).wait_recv()
                    for cp in after_diag(a):
                        cp.start()
                else:
                    arrival(a, 2, id_d).wait_recv()
            for a in range(n):
                if split[a]:
                    for k, src_id in ((4, id_x), (5, id_y), (6, id_d)):
                        hand_over(a, k, src_id, 1 - c).wait_recv()
            for a in range(n):
                for cp in first_sends(a) + after_x(a) + after_y(a) + after_diag(a):
                    cp.wait_send()
                local(a).wait()

    any_spec = pl.BlockSpec(memory_space=pl.ANY)
    vmem = pl.BlockSpec(memory_space=pltpu.VMEM)
    sent = [BF16 if split[a] else parts[a].dtype for a in range(n)]
    return pl.pallas_call(
        body, name="gather_weights", grid=(nt,),
        in_specs=[_rows(tm, D), _full((2, D))] + [vmem] * n + [any_spec] * ns,
        out_specs=[_rows(tm, D)] + [any_spec] * n + [vmem] * (nn + 4),
        out_shape=[jax.ShapeDtypeStruct((S, D), BF16)] + [jax.ShapeDtypeStruct((4, *p.shape), t) for p, t in zip(parts, sent)]
        + [jax.ShapeDtypeStruct(p.shape, BF16) for p in next_parts]
        + [jax.ShapeDtypeStruct(pool_w.shape, BF16), jax.ShapeDtypeStruct(w_s.shape, BF16), jax.ShapeDtypeStruct(w_s.shape, BF16),
           jax.ShapeDtypeStruct((HEADS, CHUNK, GD), F32)],
        scratch_shapes=[pltpu.VMEM(parts[a].shape, BF16) for a in staged]
        + [pltpu.VMEM(p.shape, F32) for p in (*next_parts, pool_w, w_s, b_s)]
        + [pltpu.SemaphoreType.DMA((n, 7)), pltpu.SemaphoreType.DMA((n, 7)), pltpu.SemaphoreType.DMA((n,)),
           pltpu.SemaphoreType.DMA((ns,))],
        compiler_params=_params(),
    )(x, pre, *parts, *next_parts, pool_w, w_s, b_s)


def _adamw(w, g, m, v):
    m = ADAM_B1 * m + (1.0 - ADAM_B1) * g
    v = ADAM_B2 * v + (1.0 - ADAM_B2) * (g * g)
    m_hat = m / (1.0 - ADAM_B1 ** ADAM_STEP)
    v_hat = v / (1.0 - ADAM_B2 ** ADAM_STEP)
    delta = -ADAM_LR * (m_hat / (jnp.sqrt(v_hat) + ADAM_EPS) + ADAM_WD * w)
    return delta, m, v


def _adamw_big(ws, gs, ms, vs):
    steps = 4
    n = len(ws)

    def body(*refs):
        ins, outs = refs[:4 * n], refs[4 * n:]
        for a in range(n):
            w_ref, g_ref, m_ref, v_ref = ins[4 * a:4 * a + 4]
            go_ref, d_ref, mo_ref, vo_ref = outs[4 * a:4 * a + 4]
            gv = g_ref[...]
            go_ref[...] = gv
            d_ref[...], mo_ref[...], vo_ref[...] = _adamw(w_ref[...], gv, m_ref[...], v_ref[...])

    specs, shapes, operands = [], [], []
    for w, g, m, v in zip(ws, gs, ms, vs):
        rows, cols = w.shape
        specs += [pl.BlockSpec((rows // steps, cols), lambda i: (i, 0))] * 4
        shapes += [jax.ShapeDtypeStruct((rows, cols), F32)] * 4
        operands += [w, g, m, v]
    res = pl.pallas_call(
        body, name="adamw_big", grid=(steps,),
        in_specs=specs, out_specs=specs, out_shape=shapes,
        compiler_params=pltpu.CompilerParams(dimension_semantics=("arbitrary",), vmem_limit_bytes=VMEM_LIMIT),
    )(*operands)
    return [res[4 * a:4 * a + 4] for a in range(n)]


def _adamw_small(g0, g1, weights, moms, vels):
    names = ["pre", "post", "conv", "pw", "ps", "lng", "lnb", "ws", "bs"]
    shapes = [w.shape for w in weights]

    def body(*refs):
        me = 2 * lax.axis_index("x") + lax.axis_index("y")
        g0_ref, g1_ref = refs[0], refs[1]
        w_refs, m_refs, v_refs = refs[2:11], refs[11:20], refs[20:29]
        outs = refs[29:29 + 36]
        loss_ref = refs[65]
        t0_ref, t1_ref = refs[66], refs[67]
        t0 = g0_ref[0]
        t1 = g1_ref[0]
        for d in range(1, 8):
            t0 = t0 + g0_ref[d]
            t1 = t1 + g1_ref[d]
        t0_ref[...] = t0
        t1_ref[...] = t1
        loss_ref[...] = t1_ref[S1_LOSS:S1_LOSS + 1, 0:1]
        my_conv = pl.multiple_of(S0_CONV + 8 * me, 8)
        my_ln = pl.multiple_of(S1_LN + 8 * me, 8)

        def update(idx, piece, grad):
            go, do, mo, vo = outs[4 * idx:4 * idx + 4]
            go[piece] = grad
            do[piece], mo[piece], vo[piece] = _adamw(w_refs[idx][piece], grad, m_refs[idx][piece], v_refs[idx][piece])

        for layer in range(2):
            for k in range(D // GD):
                lanes = slice(k * GD, (k + 1) * GD)
                tref, pre0, post0 = (t0_ref, S0_PRE, S0_POST) if layer == 0 else (t1_ref, S1_PRE, S1_POST)
                update(0, (slice(layer, layer + 1), lanes), tref[pre0 + k:pre0 + k + 1, :])
                update(1, (slice(layer, layer + 1), lanes), tref[post0 + k:post0 + k + 1, :])
        conv_rows = t0_ref[pl.ds(my_conv, 8), :]
        update(2, (slice(0, 3), 0, slice(None)), conv_rows[0:3, :])
        for g in range(4):
            update(3, (g,), t0_ref[S0_PW + g * GD:S0_PW + (g + 1) * GD, :])
            update(4, (slice(0, 1), slice(g * GD, (g + 1) * GD)), t0_ref[S0_PS + g:S0_PS + g + 1, :])
        ln_rows = t1_ref[pl.ds(my_ln, 8), :]
        for k in range(2):
            update(5, (slice(0, 1), slice(k * GD, (k + 1) * GD)), ln_rows[k:k + 1, :])
            update(6, (slice(0, 1), slice(k * GD, (k + 1) * GD)), ln_rows[2 + k:3 + k, :])
        for h in range(HEADS):
            update(7, (h,), t1_ref[S1_WS + h * CHUNK:S1_WS + (h + 1) * CHUNK, :])
        update(8, (slice(None), slice(None)), t1_ref[S1_BS:S1_BS + HEADS, :])

    vm = pl.BlockSpec(memory_space=pltpu.VMEM)
    out_shape = []
    for s in shapes:
        out_shape += [jax.ShapeDtypeStruct(s, F32)] * 4
    out_shape.append(jax.ShapeDtypeStruct((1, 1), F32))
    res = pl.pallas_call(
        body, name="adamw_small",
        in_specs=[vm] * 29, out_specs=[vm] * 37, out_shape=out_shape,
        scratch_shapes=[pltpu.VMEM((S0_ROWS, GD), F32), pltpu.VMEM((S1_ROWS, GD), F32)],
        compiler_params=pltpu.CompilerParams(vmem_limit_bytes=VMEM_LIMIT),
    )(g0, g1, *weights, *moms, *vels)
    per_weight = {nm: res[4 * i:4 * i + 4] for i, nm in enumerate(names)}
    return per_weight, res[36]


def _pad8(a):
    return jnp.pad(a, ((0, 8 - a.shape[0]), (0, 0)))


def kernel(x, pre_norm, post_norm, even_w_in, even_conv_w, even_pool_w, even_pool_scale, even_w_out, odd_w_in, odd_ln_g, odd_ln_b, odd_w_s, odd_b_s, odd_w_out, loss_target, m_pre_norm, m_post_norm, m_even_w_in, m_even_conv_w, m_even_pool_w, m_even_pool_scale, m_even_w_out, m_odd_w_in, m_odd_ln_g, m_odd_ln_b, m_odd_w_s, m_odd_b_s, m_odd_w_out, v_pre_norm, v_post_norm, v_even_w_in, v_even_conv_w, v_even_pool_w, v_even_pool_scale, v_even_w_out, v_odd_w_in, v_odd_ln_g, v_odd_ln_b, v_odd_w_s, v_odd_b_s, v_odd_w_out):
    xs = x[0]
    tgt = loss_target[0]

    small_shard = jnp.concatenate([_pad8(even_conv_w[0]), _pad8(odd_ln_g.reshape(2, GD)), _pad8(odd_ln_b.reshape(2, GD))], axis=0)
    hb0, win0, wout0, shard, win1_shard, wout1_shard, pool_wb, ws_tril, ws_tril_t, bias = _gather_weights(
        [even_w_in[0], even_w_out[0], small_shard], [True, True, False], xs, pre_norm, [odd_w_in[0], odd_w_out[0]],
        even_pool_w[0], odd_w_s[0], odd_b_s[0])
    wout0 = wout0.reshape(D, D)
    px, py, pc = _position()
    pos = jnp.stack([pc, 2 * px + py]).astype(jnp.int32)

    x1, proj0, m0, mixp0, conv0, mixed0, pooled0, win1, wout1 = _even_fwd(
        xs, hb0, post_norm, win0, shard, pool_wb, even_pool_scale, wout0, [win1_shard, wout1_shard])
    wout1 = wout1.reshape(D, D)
    pre1, post1 = pre_norm[1:2], post_norm[1:2]
    proj1, m1, hb1, yp1, dx2, loss_part = _odd_fwd(x1, tgt, pre1, post1, win1, shard, ws_tril, bias, wout1)
    dx1, dproj1, dmb1, small1 = _odd_bwd(dx2, x1, proj1, m1, loss_part, pre1, post1, win1, shard, ws_tril, ws_tril_t, bias, wout1)
    g_out1, g_in1, all1 = _wgrad_layer(yp1, dmb1, hb1, dproj1, small1, pos, "wgrad_odd")
    gx, dproj0, dmb0, small0 = _even_bwd(dx1, xs, proj0, conv0, mixed0, pooled0, m0, pre_norm, post_norm, win0, shard, pool_wb,
                                         even_pool_scale, wout0)
    g_out0, g_in0, all0 = _wgrad_layer(mixp0, dmb0, hb0, dproj0, small0, pos, "wgrad_even")

    big_w = [even_w_in[0], even_w_out[0], odd_w_in[0], odd_w_out[0]]
    big_g = [g.reshape(w.shape) for g, w in zip([g_in0, g_out0, g_in1, g_out1], big_w)]
    big_m = [m_even_w_in[0], m_even_w_out[0], m_odd_w_in[0], m_odd_w_out[0]]
    big_v = [v_even_w_in[0], v_even_w_out[0], v_odd_w_in[0], v_odd_w_out[0]]
    big = _adamw_big(big_w, big_g, big_m, big_v)

    def taps_first(a):
        return jnp.swapaxes(a, 0, 1)

    small_w = [pre_norm, post_norm, taps_first(even_conv_w), even_pool_w[0], even_pool_scale, odd_ln_g, odd_ln_b, odd_w_s[0], odd_b_s[0]]
    small_m = [m_pre_norm, m_post_norm, taps_first(m_even_conv_w), m_even_pool_w[0], m_even_pool_scale, m_odd_ln_g, m_odd_ln_b,
               m_odd_w_s[0], m_odd_b_s[0]]
    small_v = [v_pre_norm, v_post_norm, taps_first(v_even_conv_w), v_even_pool_w[0], v_even_pool_scale, v_odd_ln_g, v_odd_ln_b,
               v_odd_w_s[0], v_odd_b_s[0]]
    sm, loss = _adamw_small(all0, all1, small_w, small_m, small_v)

    def lead(a):
        return a[None]

    per = {
        "pre_norm": sm["pre"], "post_norm": sm["post"],
        "even_w_in": [lead(a) for a in big[0]], "even_conv_w": [taps_first(a) for a in sm["conv"]],
        "even_pool_w": [lead(a) for a in sm["pw"]], "even_pool_scale": sm["ps"],
        "even_w_out": [lead(a) for a in big[1]], "odd_w_in": [lead(a) for a in big[2]],
        "odd_ln_g": sm["lng"], "odd_ln_b": sm["lnb"],
        "odd_w_s": [lead(a) for a in sm["ws"]], "odd_b_s": [lead(a) for a in sm["bs"]],
        "odd_w_out": [lead(a) for a in big[3]],
    }
    order = ["pre_norm", "post_norm", "even_w_in", "even_conv_w", "even_pool_w", "even_pool_scale", "even_w_out", "odd_w_in",
             "odd_ln_g", "odd_ln_b", "odd_w_s", "odd_b_s", "odd_w_out"]
    outs = [loss.reshape(()), gx[None]]
    for kind in range(4):
        outs += [per[nm][kind] for nm in order]
    return tuple(outs)
```

```python
import jax
import jax.numpy as jnp
from jax import lax
from jax.experimental import pallas as pl
from jax.experimental.pallas import tpu as pltpu

F32 = jnp.float32
BF16 = jnp.bfloat16
MESH = pl.DeviceIdType.MESH

D = 1024
W3 = 3 * D
QW = W3 // 4
AW = 512
GD = 128
CHUNK = 128
HEADS = 8
HALO = 16
POOL_WINDOWS = (2, 4, 8, 16)
EPS = 1e-6
TM_FWD = 512
TM_BWD = 256
TK = 1024
TK_IN = 2048
VMEM_LIMIT = 56 * 1024 * 1024

ADAM_LR, ADAM_B1, ADAM_B2, ADAM_EPS, ADAM_WD, ADAM_STEP = 0.001, 0.9, 0.999, 1e-08, 0.01, 10

S0_PRE, S0_POST, S0_CONV, S0_PS, S0_PW, S0_ROWS = 0, 8, 16, 48, 56, 568
S1_PRE, S1_POST, S1_LN, S1_BS, S1_WS, S1_LOSS, S1_ROWS = 0, 8, 16, 48, 56, 1080, 1088


def _nn(a, b):
    return jnp.dot(a, b, preferred_element_type=F32)


def _nt(a, b):
    return lax.dot_general(a, b, (((1,), (1,)), ((), ())), preferred_element_type=F32)


def _tn(a, b):
    return lax.dot_general(a, b, (((0,), (0,)), ((), ())), preferred_element_type=F32)


def _sigmoid(z):
    return 1.0 / (1.0 + jnp.exp(-z))


def _rms_fwd(x):
    r = lax.rsqrt(jnp.mean(x * x, axis=-1, keepdims=True) + EPS)
    return x * r, r


def _rms_bwd(dy, xh, r, g):
    dn = dy * g
    dx = r * (dn - xh * jnp.mean(xh * dn, axis=-1, keepdims=True))
    return dx, jnp.sum(dy * xh, axis=0, keepdims=True)


def _full(shape):
    nd = len(shape)
    return pl.BlockSpec(shape, lambda i, _n=nd: (0,) * _n, pipeline_mode=pl.Buffered(1))


def _full_out(shape):
    nd = len(shape)
    return pl.BlockSpec(shape, lambda i, _n=nd: (0,) * _n)


def _rows(tm, width, index=None):
    return pl.BlockSpec((tm, width), (lambda i: (i, 0)) if index is None else index)


def _params():
    return pltpu.CompilerParams(dimension_semantics=("arbitrary",), vmem_limit_bytes=VMEM_LIMIT)


def _position():
    x, y, c = lax.axis_index("x"), lax.axis_index("y"), lax.axis_index("c")
    return x, y, c


SHARD_ROWS = 24


def _shard_row(shard_ref, row):
    return jnp.concatenate([shard_ref[q, row:row + 1, :] for q in range(4)], axis=1)


def _shard_vector(shard_ref, first_row):
    return jnp.concatenate([shard_ref[q, first_row + k:first_row + k + 1, :] for q in range(4) for k in range(2)], axis=1)


def _even_mix(proj_ref, hc_ext, xp_ext, taps, pw_ref, ps_ref, first_row):
    tm = proj_ref.shape[0]
    xa = proj_ref[:, 0:AW]
    gb = proj_ref[:, AW:2 * AW]
    gc = proj_ref[:, 2 * AW:3 * AW]
    za = proj_ref[:, 3 * AW:4 * AW]
    xp = proj_ref[:, 4 * AW:5 * AW]
    zp = proj_ref[:, 5 * AW:6 * AW]
    hc = gc * xa
    hc_ext[HALO:, :] = hc
    e = hc_ext[...]
    conv = taps[2] * hc + taps[1] * pltpu.roll(e, 1, 0)[HALO:] + taps[0] * pltpu.roll(e, 2, 0)[HALO:]
    sa = _sigmoid(za)
    xp_ext[HALO:, :] = xp
    pos = first_row + lax.broadcasted_iota(jnp.int32, (tm, 1), 0)
    pooled, mixed, counts = [], [], []
    for g, w in enumerate(POOL_WINDOWS):
        cols = slice(g * GD, (g + 1) * GD)
        s = xp_ext[:, cols]
        for k in range(g + 1):
            s = s + pltpu.roll(s, 2 ** k, 0)
        count = jnp.minimum(pos + 1, w).astype(F32)
        pg = s[HALO:] / count - xp[:, cols]
        pooled.append(pg.astype(BF16))
        mixed.append(_nn(pooled[-1], pw_ref[g]))
        counts.append(count)
    mixed = jnp.concatenate(mixed, axis=-1)
    sb = _sigmoid(zp)
    return dict(xa=xa, gb=gb, gc=gc, za=za, zp=zp, hc=hc, conv=conv, sa=sa, sb=sb, pooled=pooled, mixed=mixed, counts=counts)


def _half_rows(ref, rows, who):
    return ref.at[pl.ds(pl.multiple_of(who * (rows // 2), 8), rows // 2), :]


def _store_permuted(ref, value):
    for ob in range(D // GD):
        nb = 4 * (ob % 2) + ob // 2
        ref[:, nb * GD:(nb + 1) * GD] = value[:, ob * GD:(ob + 1) * GD]


def _even_fwd(x, hb, post, win, shard, pwb, ps, wout, next_shards):
    S = x.shape[0]
    tm = TM_FWD
    nt = S // tm
    relay = (3 * nt) // 4
    n = len(next_shards)
    shard_rows = [p.shape[0] for p in next_shards]

    def body(x_ref, hb_ref, post_ref, win_ref, shard_ref, pw_ref, ps_ref, wout_ref, *rest):
        shard_refs, rest = rest[:n], rest[n:]
        x1_ref, proj_ref, m_ref, mixp_ref, conv_ref, mixed_ref, pooled_ref = rest[:7]
        full_refs, rest = rest[7:7 + n], rest[7 + n:]
        hc_ext, xp_ext, mix_sc = rest[:3]
        stage, rest = rest[3:3 + n], rest[3 + n:]
        send_sems, recv_sems, local_sems = rest
        i = pl.program_id(0)
        px, py, pc = _position()
        me = 2 * px + py
        chips = _chips(px, py)

        def ici(a, j):
            return pltpu.make_async_remote_copy(
                src_ref=_half_rows(shard_refs[a], shard_rows[a], pc), dst_ref=_half_rows(full_refs[a].at[me], shard_rows[a], pc),
                send_sem=send_sems.at[a, j], recv_sem=recv_sems.at[a, j], device_id=(*chips[j], pc), device_id_type=MESH)

        def ici_arrival(a, j):
            src = 2 * chips[j][0] + chips[j][1]
            return pltpu.make_async_remote_copy(
                src_ref=_half_rows(shard_refs[a], shard_rows[a], pc), dst_ref=_half_rows(full_refs[a].at[src], shard_rows[a], pc),
                send_sem=send_sems.at[a, j], recv_sem=recv_sems.at[a, j], device_id=(*chips[j], pc), device_id_type=MESH)

        def relay_copy(a, j, who):
            src = 2 * chips[j][0] + chips[j][1]
            region = _half_rows(full_refs[a].at[src], shard_rows[a], who)
            return pltpu.make_async_remote_copy(
                src_ref=region, dst_ref=region, send_sem=send_sems.at[a, 3 + j], recv_sem=recv_sems.at[a, 3 + j],
                device_id=(px, py, 1 - pc), device_id_type=MESH)

        def own_copy(a):
            return pltpu.make_async_copy(stage[a], full_refs[a].at[me], local_sems.at[a])

        @pl.when(i == 0)
        def _():
            hc_ext[0:HALO, :] = jnp.zeros((HALO, AW), F32)
            xp_ext[0:HALO, :] = jnp.zeros((HALO, AW), F32)
            for a in range(n):
                for j in range(3):
                    ici(a, j).start()
            for a in range(n):
                load = pltpu.make_async_copy(shard_refs[a], stage[a], local_sems.at[a])
                load.start()
                load.wait()
                own_copy(a).start()

        @pl.when(i == relay)
        def _():
            for j in range(3):
                for a in range(n):
                    ici_arrival(a, j).wait_recv()
                    relay_copy(a, j, pc).start()

        hb = hb_ref[...]
        for q in range(4):
            proj_ref[:, q * QW:(q + 1) * QW] = _nn(hb, win_ref[q])
        t = _even_mix(proj_ref, hc_ext, xp_ext, [_shard_row(shard_ref, k) for k in range(3)], pw_ref, ps_ref, i * tm)
        conv_ref[...] = t["conv"]
        mixed_ref[...] = t["mixed"]
        for g in range(4):
            pooled_ref[:, g * GD:(g + 1) * GD] = t["pooled"][g]
        mix_sc[:, 0:AW] = (t["gb"] * t["conv"] * (t["za"] * t["sa"])).astype(BF16)
        mix_sc[:, AW:2 * AW] = (t["mixed"] * ps_ref[...] * (t["zp"] * t["sb"])).astype(BF16)
        mix = mix_sc[...]
        _store_permuted(mixp_ref, mix)
        m = _nn(mix, wout_ref[...])
        m_ref[...] = m
        mh, _ = _rms_fwd(m)
        x1_ref[...] = x_ref[...] + mh * post_ref[0:1, :]
        hc_ext[0:HALO, :] = hc_ext[tm:tm + HALO, :]
        xp_ext[0:HALO, :] = xp_ext[tm:tm + HALO, :]

        @pl.when(i == nt - 1)
        def _():
            for j in range(3):
                for a in range(n):
                    relay_copy(a, j, 1 - pc).wait_recv()
            for a in range(n):
                for j in range(3):
                    ici(a, j).wait_send()
                    relay_copy(a, j, pc).wait_send()
                own_copy(a).wait()

    any_spec = pl.BlockSpec(memory_space=pl.ANY)
    return pl.pallas_call(
        body, name="even_fwd", grid=(nt,),
        in_specs=[_rows(tm, D), _rows(tm, D), _full((2, D)), _full((4, D, QW)), _full((4, SHARD_ROWS, GD)), _full((4, GD, GD)),
                  _full((1, AW)), _full((D, D))] + [any_spec] * n,
        out_specs=[_rows(tm, D), _rows(tm, W3), _rows(tm, D), _rows(tm, D), _rows(tm, AW), _rows(tm, AW), _rows(tm, AW)]
        + [any_spec] * n,
        out_shape=[jax.ShapeDtypeStruct((S, D), F32), jax.ShapeDtypeStruct((S, W3), F32), jax.ShapeDtypeStruct((S, D), F32),
                   jax.ShapeDtypeStruct((S, D), BF16),
                   jax.ShapeDtypeStruct((S, AW), F32), jax.ShapeDtypeStruct((S, AW), F32), jax.ShapeDtypeStruct((S, AW), BF16)]
        + [jax.ShapeDtypeStruct((4, *p.shape), p.dtype) for p in next_shards],
        scratch_shapes=[pltpu.VMEM((tm + HALO, AW), F32), pltpu.VMEM((tm + HALO, AW), F32), pltpu.VMEM((tm, D), BF16)]
        + [pltpu.VMEM(p.shape, p.dtype) for p in next_shards]
        + [pltpu.SemaphoreType.DMA((n, 6)), pltpu.SemaphoreType.DMA((n, 6)), pltpu.SemaphoreType.DMA((n,))],
        compiler_params=_params(),
    )(x, hb, post, win, shard, pwb, ps, wout, *next_shards)


def _chunks_side_by_side(a, h):
    return jnp.concatenate([a[n * CHUNK:(n + 1) * CHUNK, h * GD:(h + 1) * GD] for n in range(a.shape[0] // CHUNK)], axis=1)


def _odd_mix(proj_ref, lng, lnb, ws_ref, bias_ref, sv_ref):
    tm = proj_ref.shape[0]
    u = proj_ref[:, 0:D]
    v = proj_ref[:, D:2 * D]
    z = proj_ref[:, 2 * D:3 * D]
    mu = jnp.mean(v, axis=-1, keepdims=True)
    vc = v - mu
    rs = lax.rsqrt(jnp.mean(vc * vc, axis=-1, keepdims=True) + EPS)
    vh = vc * rs
    vnb = (vh * lng + lnb).astype(BF16)
    for h in range(HEADS):
        sv = _nn(ws_ref[h], _chunks_side_by_side(vnb, h))
        for n in range(tm // CHUNK):
            sv_ref[n * CHUNK:(n + 1) * CHUNK, h * GD:(h + 1) * GD] = sv[:, n * GD:(n + 1) * GD] + bias_ref[h]
    return dict(u=u, z=z, vh=vh, rs=rs, vnb=vnb, sz=_sigmoid(z))


def _odd_fwd(x1, tgt, pre, post, win, shard, wsb, bias, wout):
    S = x1.shape[0]
    tm = TM_FWD
    nt = S // tm

    def body(x_ref, tgt_ref, pre_ref, post_ref, win_ref, shard_ref, ws_ref, bias_ref, wout_ref,
             proj_ref, m_ref, hb_ref, yb_ref, dx2_ref, loss_ref, sv_ref):
        i = pl.program_id(0)

        @pl.when(i == 0)
        def _():
            loss_ref[...] = jnp.zeros((8, GD), F32)

        xv = x_ref[...]
        xh, _ = _rms_fwd(xv)
        hb = (xh * pre_ref[...]).astype(BF16)
        hb_ref[...] = hb
        for q in range(4):
            proj_ref[:, q * QW:(q + 1) * QW] = _nn(hb, win_ref[q])
        t = _odd_mix(proj_ref, _shard_vector(shard_ref, 8), _shard_vector(shard_ref, 16), ws_ref, bias_ref, sv_ref)
        yb = (t["u"] * sv_ref[...] * (t["z"] * t["sz"])).astype(BF16)
        _store_permuted(yb_ref, yb)
        m = _nn(yb, wout_ref[...])
        m_ref[...] = m
        mh, _ = _rms_fwd(m)
        err = xv + mh * post_ref[...] - tgt_ref[...]
        dx2_ref[...] = err * (1.0 / D)
        part = 0.5 * jnp.sum(jnp.mean(err * err, axis=-1, keepdims=True), axis=0, keepdims=True)
        loss_ref[...] += jnp.broadcast_to(part, (8, GD))

    return pl.pallas_call(
        body, name="odd_fwd", grid=(nt,),
        in_specs=[_rows(tm, D), _rows(tm, D), _full((1, D)), _full((1, D)), _full((4, D, QW)), _full((4, SHARD_ROWS, GD)),
                  _full((HEADS, CHUNK, CHUNK)), _full((HEADS, CHUNK, GD)), _full((D, D))],
        out_specs=[_rows(tm, W3), _rows(tm, D), _rows(tm, D), _rows(tm, D), _rows(tm, D), _full_out((8, GD))],
        out_shape=[jax.ShapeDtypeStruct((S, W3), F32), jax.ShapeDtypeStruct((S, D), F32), jax.ShapeDtypeStruct((S, D), BF16),
                   jax.ShapeDtypeStruct((S, D), BF16), jax.ShapeDtypeStruct((S, D), F32), jax.ShapeDtypeStruct((8, GD), F32)],
        scratch_shapes=[pltpu.VMEM((tm, D), F32)],
        compiler_params=_params(),
    )(x1, tgt, pre, post, win, shard, wsb, bias, wout)


def _store_rows(ref, row0, value):
    r, width = value.shape
    for a in range(r):
        for k in range(width // GD):
            ref[row0 + a * (width // GD) + k:row0 + a * (width // GD) + k + 1, :] = value[a:a + 1, k * GD:(k + 1) * GD]


def _proj_bwd(dproj, win_ref, x, dy, pre):
    dh = _nt(dproj[:, 0:QW], win_ref[0])
    for q in range(1, 4):
        dh += _nt(dproj[:, q * QW:(q + 1) * QW], win_ref[q])
    xh, r = _rms_fwd(x)
    dxn, dpre = _rms_bwd(dh, xh, r, pre)
    return dy + dxn, dpre


def _odd_bwd(dx2, x1, proj, m, loss, pre, post, win, shard, wsb, wsbt, bias, wout):
    S = x1.shape[0]
    tm = TM_BWD
    nt = S // tm

    def body(dy_ref, x_ref, proj_ref, m_ref, loss_ref, pre_ref, post_ref, win_ref, shard_ref, ws_ref, wst_ref, bias_ref,
             wout_ref, dx_ref, dproj_ref, dmb_ref, small_ref, sv_ref, dvn_ref, acc1024, dws_acc, dbs_acc):
        i = pl.program_id(0)
        lng = _shard_vector(shard_ref, 8)

        @pl.when(i == 0)
        def _():
            acc1024[...] = jnp.zeros_like(acc1024)
            dws_acc[...] = jnp.zeros_like(dws_acc)
            dbs_acc[...] = jnp.zeros_like(dbs_acc)

        dy = dy_ref[...]
        mh, rm = _rms_fwd(m_ref[...])
        dm, dpost = _rms_bwd(dy, mh, rm, post_ref[...])
        dmb = dm.astype(BF16)
        dmb_ref[...] = dmb
        dyv = _nt(dmb, wout_ref[...])
        t = _odd_mix(proj_ref, lng, _shard_vector(shard_ref, 16), ws_ref, bias_ref, sv_ref)
        u, z, sz, sv = t["u"], t["z"], t["sz"], sv_ref[...]
        dproj_ref[:, 0:D] = (dyv * sv * (z * sz)).astype(BF16)
        dproj_ref[:, 2 * D:3 * D] = (dyv * u * sv * (sz * (1.0 + z * (1.0 - sz)))).astype(BF16)
        dsv = dyv * u * (z * sz)
        dsvb = dsv.astype(BF16)
        for h in range(HEADS):
            dsv_h = _chunks_side_by_side(dsvb, h)
            dvn_h = _nn(wst_ref[h], dsv_h)
            dws_acc[h] += _nt(dsv_h, _chunks_side_by_side(t["vnb"], h))
            for n in range(tm // CHUNK):
                rows, cols = slice(n * CHUNK, (n + 1) * CHUNK), slice(h * GD, (h + 1) * GD)
                dvn_ref[rows, cols] = dvn_h[:, n * GD:(n + 1) * GD]
                dbs_acc[h] += dsv[rows, cols]
        dvn = dvn_ref[...]
        vh = t["vh"]
        dvh = dvn * lng
        dv = t["rs"] * (dvh - jnp.mean(dvh, axis=-1, keepdims=True) - vh * jnp.mean(dvh * vh, axis=-1, keepdims=True))
        dproj_ref[:, D:2 * D] = dv.astype(BF16)
        dx_ref[...], dpre = _proj_bwd(dproj_ref[...], win_ref, x_ref[...], dy, pre_ref[...])
        acc1024[0:1, :] += dpre
        acc1024[1:2, :] += dpost
        acc1024[2:3, :] += jnp.sum(dvn * vh, axis=0, keepdims=True)
        acc1024[3:4, :] += jnp.sum(dvn, axis=0, keepdims=True)

        @pl.when(i == nt - 1)
        def _():
            small_ref[...] = jnp.zeros_like(small_ref)
            _store_rows(small_ref, S1_PRE, acc1024[0:1, :])
            _store_rows(small_ref, S1_POST, acc1024[1:2, :])
            for q in range(4):
                _store_rows(small_ref, S1_LN + 8 * q, acc1024[2:3, 2 * q * GD:(2 * q + 2) * GD])
                _store_rows(small_ref, S1_LN + 8 * q + 2, acc1024[3:4, 2 * q * GD:(2 * q + 2) * GD])
            lower = lax.broadcasted_iota(jnp.int32, (CHUNK, CHUNK), 0) >= lax.broadcasted_iota(jnp.int32, (CHUNK, CHUNK), 1)
            for h in range(HEADS):
                small_ref[S1_WS + h * CHUNK:S1_WS + (h + 1) * CHUNK, :] = jnp.where(lower, dws_acc[h], 0.0)
                small_ref[S1_BS + h:S1_BS + h + 1, :] = jnp.sum(dbs_acc[h].T, axis=0, keepdims=True)
            small_ref[S1_LOSS:S1_LOSS + 8, :] = loss_ref[...]

    return pl.pallas_call(
        body, name="odd_bwd", grid=(nt,),
        in_specs=[_rows(tm, D), _rows(tm, D), _rows(tm, W3), _rows(tm, D), _full((8, GD)), _full((1, D)), _full((1, D)),
                  _full((4, D, QW)), _full((4, SHARD_ROWS, GD)), _full((HEADS, CHUNK, CHUNK)), _full((HEADS, CHUNK, CHUNK)),
                  _full((HEADS, CHUNK, GD)), _full((D, D))],
        out_specs=[_rows(tm, D), _rows(tm, W3), _rows(tm, D), _full_out((S1_ROWS, GD))],
        out_shape=[jax.ShapeDtypeStruct((S, D), F32), jax.ShapeDtypeStruct((S, W3), BF16), jax.ShapeDtypeStruct((S, D), BF16),
                   jax.ShapeDtypeStruct((S1_ROWS, GD), F32)],
        scratch_shapes=[pltpu.VMEM((tm, D), F32), pltpu.VMEM((tm, D), F32), pltpu.VMEM((8, D), F32),
                        pltpu.VMEM((HEADS, CHUNK, CHUNK), F32), pltpu.VMEM((HEADS, CHUNK, GD), F32)],
        compiler_params=_params(),
    )(dx2, x1, proj, m, loss, pre, post, win, shard, wsb, wsbt, bias, wout)


def _even_bwd(dx1, x, proj, conv, mixed, pooled, m, pre, post, win, shard, pwb, ps, wout):
    S = x.shape[0]
    tm = TM_BWD
    nt = S // tm
    L = tm + HALO

    def rev(i):
        return (nt - 1 - i, 0)

    def body(dy_ref, x_ref, proj_ref, conv_ref, mixed_ref, pooled_ref, m_ref, pre_ref, post_ref, win_ref, shard_ref, pw_ref, ps_ref,
             wout_ref, dx_ref, dproj_ref, dmb_ref, small_ref, dconv_ext, q_ext, acc1024, acc512, dpw_acc):
        i = pl.program_id(0)
        tile = nt - 1 - i

        @pl.when(i == 0)
        def _():
            dconv_ext[tm:L, :] = jnp.zeros((HALO, AW), F32)
            q_ext[tm:L, :] = jnp.zeros((HALO, AW), F32)
            acc1024[...] = jnp.zeros_like(acc1024)
            acc512[...] = jnp.zeros_like(acc512)
            dpw_acc[...] = jnp.zeros_like(dpw_acc)

        dy = dy_ref[...]
        mh, rm = _rms_fwd(m_ref[...])
        dm, dpost = _rms_bwd(dy, mh, rm, post_ref[0:1, :])
        dmb = dm.astype(BF16)
        dmb_ref[...] = dmb
        dmix = _nt(dmb, wout_ref[...])
        dya, dyb = dmix[:, 0:AW], dmix[:, AW:2 * AW]
        xa, gb, gc, za = (proj_ref[:, k * AW:(k + 1) * AW] for k in range(4))
        zp = proj_ref[:, 5 * AW:6 * AW]
        hc = gc * xa
        conv = conv_ref[...]
        sa = _sigmoid(za)
        silu_a = za * sa
        dproj_ref[:, AW:2 * AW] = (dya * conv * silu_a).astype(BF16)
        dproj_ref[:, 3 * AW:4 * AW] = (dya * gb * conv * (sa * (1.0 + za * (1.0 - sa)))).astype(BF16)
        dconv = dya * gb * silu_a
        dconv_ext[0:tm, :] = dconv
        e = dconv_ext[...]
        dc1 = pltpu.roll(e, L - 1, 0)[0:tm]
        dc2 = pltpu.roll(e, L - 2, 0)[0:tm]
        taps = [_shard_row(shard_ref, k) for k in range(3)]
        dhc = taps[2] * dconv + taps[1] * dc1 + taps[0] * dc2
        dproj_ref[:, 0:AW] = (dhc * gc).astype(BF16)
        dproj_ref[:, 2 * AW:3 * AW] = (dhc * xa).astype(BF16)
        acc512[0:1, :] += jnp.sum(dc2 * hc, axis=0, keepdims=True)
        acc512[1:2, :] += jnp.sum(dc1 * hc, axis=0, keepdims=True)
        acc512[2:3, :] += jnp.sum(dconv * hc, axis=0, keepdims=True)

        sb, mixed, ps = _sigmoid(zp), mixed_ref[...], ps_ref[...]
        silu_b = zp * sb
        acc512[3:4, :] += jnp.sum(dyb * mixed * silu_b, axis=0, keepdims=True)
        dmixedb = (dyb * ps * silu_b).astype(BF16)
        dproj_ref[:, 5 * AW:6 * AW] = (dyb * mixed * ps * (sb * (1.0 + zp * (1.0 - sb)))).astype(BF16)
        pos = tile * tm + lax.broadcasted_iota(jnp.int32, (tm, 1), 0)
        for g, w in enumerate(POOL_WINDOWS):
            cols = slice(g * GD, (g + 1) * GD)
            dpw_acc[g] += _tn(pooled_ref[:, cols], dmixedb[:, cols])
            dpooled = _nt(dmixedb[:, cols], pw_ref[g])
            q_ext[0:tm, cols] = dpooled / jnp.minimum(pos + 1, w).astype(F32)
            s = q_ext[:, cols]
            for k in range(g + 1):
                s = s + pltpu.roll(s, L - 2 ** k, 0)
            dproj_ref[:, 4 * AW + g * GD:4 * AW + (g + 1) * GD] = (s[0:tm] - dpooled).astype(BF16)
        dconv_ext[tm:L, :] = dconv_ext[0:HALO, :]
        q_ext[tm:L, :] = q_ext[0:HALO, :]

        dx_ref[...], dpre = _proj_bwd(dproj_ref[...], win_ref, x_ref[...], dy, pre_ref[0:1, :])
        acc1024[0:1, :] += dpre
        acc1024[1:2, :] += dpost

        @pl.when(i == nt - 1)
        def _():
            small_ref[...] = jnp.zeros_like(small_ref)
            _store_rows(small_ref, S0_PRE, acc1024[0:1, :])
            _store_rows(small_ref, S0_POST, acc1024[1:2, :])
            for q in range(4):
                for k in range(3):
                    small_ref[S0_CONV + 8 * q + k:S0_CONV + 8 * q + k + 1, :] = acc512[k:k + 1, q * GD:(q + 1) * GD]
            _store_rows(small_ref, S0_PS, acc512[3:4, :])
            for g in range(4):
                small_ref[S0_PW + g * GD:S0_PW + (g + 1) * GD, :] = dpw_acc[g]

    return pl.pallas_call(
        body, name="even_bwd", grid=(nt,),
        in_specs=[_rows(tm, D, rev), _rows(tm, D, rev), _rows(tm, W3, rev), _rows(tm, AW, rev), _rows(tm, AW, rev), _rows(tm, AW, rev),
                  _rows(tm, D, rev),
                  _full((2, D)), _full((2, D)), _full((4, D, QW)), _full((4, SHARD_ROWS, GD)), _full((4, GD, GD)), _full((1, AW)),
                  _full((D, D))],
        out_specs=[_rows(tm, D, rev), _rows(tm, W3, rev), _rows(tm, D, rev), _full_out((S0_ROWS, GD))],
        out_shape=[jax.ShapeDtypeStruct((S, D), F32), jax.ShapeDtypeStruct((S, W3), BF16), jax.ShapeDtypeStruct((S, D), BF16),
                   jax.ShapeDtypeStruct((S0_ROWS, GD), F32)],
        scratch_shapes=[pltpu.VMEM((L, AW), F32), pltpu.VMEM((L, AW), F32),
                        pltpu.VMEM((8, D), F32), pltpu.VMEM((8, AW), F32), pltpu.VMEM((4, GD, GD), F32)],
        compiler_params=_params(),
    )(dx1, x, proj, conv, mixed, pooled, m, pre, post, win, shard, pwb, ps, wout)


def _owner_id(me, relation, c):
    q = jnp.bitwise_xor(me, relation)
    return (q // 2, q % 2, c)


def _small_gather_steps(small_ref, all_ref, stage, send_sems, recv_sems, local_sem):
    x, y, c = _position()
    chips = _chips(x, y)

    def slot(chip, core):
        return 4 * chip[0] + 2 * chip[1] + core

    def copy(k, src, block, to):
        return pltpu.make_async_remote_copy(src_ref=src, dst_ref=all_ref.at[block], send_sem=send_sems.at[k],
                                            recv_sem=recv_sems.at[k], device_id=to, device_id_type=MESH)

    def own_copy():
        return pltpu.make_async_copy(stage, all_ref.at[slot((x, y), c)], local_sem)

    def first_sends():
        mine = slot((x, y), c)
        return [copy(0, small_ref, mine, (x, y, 1 - c))] + [copy(1 + j, small_ref, mine, (*chip, c)) for j, chip in enumerate(chips)]

    def relays():
        return [copy(4 + j, all_ref.at[slot(chip, c)], slot(chip, c), (x, y, 1 - c)) for j, chip in enumerate(chips)]

    def start():
        for cp in first_sends():
            cp.start()
        load = pltpu.make_async_copy(small_ref, stage, local_sem)
        load.start()
        load.wait()
        own_copy().start()

    def relay():
        for j, chip in enumerate(chips):
            copy(1 + j, small_ref, slot(chip, c), (*chip, c)).wait_recv()
        for cp in relays():
            cp.start()

    def finish():
        copy(0, small_ref, slot((x, y), 1 - c), (x, y, 1 - c)).wait_recv()
        for j, chip in enumerate(chips):
            copy(4 + j, small_ref, slot(chip, 1 - c), (x, y, 1 - c)).wait_recv()
        for cp in first_sends() + relays():
            cp.wait_send()
        own_copy().wait()

    return start, relay, finish


def _wgrad_layer(a_out, b_out, a_in, b_in, small, pos, name):
    S = a_in.shape[0]
    nko = S // TK
    nki = S // TK_IN
    hm = D // 2
    qr = hm // 4

    def out_index(s, pos_ref):
        return (jnp.minimum(s, nko - 1), 0)

    def a_in_index(s, pos_ref):
        return (jnp.where(s >= nko, (s - nko) % nki, 0), 0)

    def b_in_index(s, pos_ref):
        return (jnp.where(s >= nko, (s - nko) % nki, 0), jnp.bitwise_xor(pos_ref[1], 3 - jnp.maximum((s - nko) // nki, 0)))

    def body(pos_ref, ao_ref, bo_ref, a_ref, b_ref, small_ref, gout_ref, gin_ref, all_ref,
             acc, rbuf, sbuf, arr, mine, acc_o, rbuf_o, total_o, sbuf_o, arr_o, mine_o, stage,
             d2d_send, d2d_recv, ici_send, ici_recv, d2d_o_send, d2d_o_recv, ici_o_send, ici_o_recv,
             share_send, share_recv, local_sems, g_send, g_recv, g_local):
        s = pl.program_id(0)
        in_step = jnp.maximum(s - nko, 0)
        blk = jnp.where(s < nko, 0, 1 + in_step // nki)
        k = jnp.where(s < nko, s, in_step % nki)
        j = blk - 1
        x, y, c = _position()
        me = 2 * x + y
        sibling = (x, y, 1 - c)
        last = k == jnp.where(s < nko, nko - 1, nki - 1)
        gather_start, gather_relay, gather_finish = _small_gather_steps(small_ref, all_ref, stage, g_send, g_recv, g_local)

        def other_half(ref):
            return ref.at[pl.ds(pl.multiple_of((1 - c) * hm, hm), hm), :]

        def own_half(ref):
            return ref[pl.ds(pl.multiple_of(c * hm, hm), hm), :]

        def to_sibling(jj):
            return pltpu.make_async_remote_copy(
                src_ref=other_half(acc.at[jj % 2]), dst_ref=rbuf.at[jj], send_sem=d2d_send.at[jj], recv_sem=d2d_recv.at[jj],
                device_id=sibling, device_id_type=MESH)

        def to_owner(jj):
            return pltpu.make_async_remote_copy(
                src_ref=sbuf.at[jj], dst_ref=arr.at[2 - jj], send_sem=ici_send.at[jj], recv_sem=ici_recv.at[jj],
                device_id=_owner_id(me, 3 - jj, c), device_id_type=MESH)

        def out_to_sibling():
            return pltpu.make_async_remote_copy(
                src_ref=other_half(acc_o), dst_ref=rbuf_o, send_sem=d2d_o_send, recv_sem=d2d_o_recv,
                device_id=sibling, device_id_type=MESH)

        def out_to_owner(r):
            return pltpu.make_async_remote_copy(
                src_ref=sbuf_o.at[r], dst_ref=arr_o.at[r], send_sem=ici_o_send.at[r], recv_sem=ici_o_recv.at[r],
                device_id=_owner_id(me, r + 1, c), device_id_type=MESH)

        def pair_sum(jj):
            to_sibling(jj).wait_recv()
            return own_half(acc.at[jj % 2]) + rbuf[jj]

        def send_block(jj):
            sbuf[jj] = pair_sum(jj).astype(BF16)
            to_owner(jj).start()

        @pl.when(s == 0)
        def _():
            gather_start()

        @pl.when((blk == 3) & (k == 0))
        def _():
            gather_relay()

        @pl.when((blk == 0) & (k == 0))
        def _():
            acc_o[...] = _tn(ao_ref[...], bo_ref[...])

        @pl.when((blk == 0) & (k > 0))
        def _():
            acc_o[...] += _tn(ao_ref[...], bo_ref[...])

        @pl.when((blk == 0) & last)
        def _():
            out_to_sibling().start()

        @pl.when((blk >= 3) & (k == 0))
        def _():
            to_sibling(j - 2).wait_send()

        @pl.when((blk >= 1) & (k == 0))
        def _():
            acc[j % 2] = _tn(a_ref[...], b_ref[...])

        @pl.when((blk >= 1) & (k > 0))
        def _():
            acc[j % 2] += _tn(a_ref[...], b_ref[...])

        @pl.when((blk >= 1) & last)
        def _():
            to_sibling(j).start()

        @pl.when((blk == 1) & last)
        def _():
            out_to_sibling().wait_recv()
            total_o[...] = own_half(acc_o) + rbuf_o[...]
            for r in range(3):
                q = jnp.bitwise_xor(me, r + 1)
                sbuf_o[r] = total_o[pl.ds(pl.multiple_of(q * qr, qr), qr), :].astype(BF16)
                out_to_owner(r).start()

        @pl.when((blk == 2) & last)
        def _():
            send_block(0)

        @pl.when((blk == 3) & last)
        def _():
            send_block(1)
            send_block(2)

        @pl.when((blk == 4) & last)
        def _():
            g_in = pair_sum(3)
            g_out = total_o[pl.ds(pl.multiple_of(me * qr, qr), qr), :]
            to_sibling(2).wait_send()
            to_sibling(3).wait_send()
            out_to_sibling().wait_send()
            for r in range(3):
                to_owner(r).wait()
                out_to_owner(r).wait()
            for r in range(3):
                g_in = g_in + arr[r].astype(F32)
                g_out = g_out + arr_o[r].astype(F32)
            mine[...] = g_in
            mine_o[...] = g_out
            copies = []
            for idx, (src, dst) in enumerate([(mine, gin_ref), (mine_o, gout_ref)]):
                copies.append(pltpu.make_async_remote_copy(
                    src_ref=src, dst_ref=dst.at[c], send_sem=share_send.at[idx], recv_sem=share_recv.at[idx],
                    device_id=sibling, device_id_type=MESH))
                copies.append(pltpu.make_async_copy(src, dst.at[c], local_sems.at[idx]))
            for cp in copies:
                cp.start()
            for cp in copies:
                cp.wait()
            gather_finish()

    any_spec = pl.BlockSpec(memory_space=pl.ANY)
    grid_spec = pltpu.PrefetchScalarGridSpec(
        num_scalar_prefetch=1, grid=(nko + 4 * nki,),
        in_specs=[pl.BlockSpec((TK, D), out_index), pl.BlockSpec((TK, D), out_index),
                  pl.BlockSpec((TK_IN, D), a_in_index), pl.BlockSpec((TK_IN, QW), b_in_index), any_spec],
        out_specs=[any_spec, any_spec, any_spec],
        scratch_shapes=[pltpu.VMEM((2, D, QW), F32), pltpu.VMEM((4, hm, QW), F32), pltpu.VMEM((3, hm, QW), BF16),
                        pltpu.VMEM((3, hm, QW), BF16), pltpu.VMEM((hm, QW), F32),
                        pltpu.VMEM((D, D), F32), pltpu.VMEM((hm, D), F32), pltpu.VMEM((hm, D), F32), pltpu.VMEM((3, qr, D), BF16),
                        pltpu.VMEM((3, qr, D), BF16), pltpu.VMEM((qr, D), F32),
                        pltpu.VMEM(small.shape, F32),
                        pltpu.SemaphoreType.DMA((4,)), pltpu.SemaphoreType.DMA((4,)),
                        pltpu.SemaphoreType.DMA((3,)), pltpu.SemaphoreType.DMA((3,)),
                        pltpu.SemaphoreType.DMA, pltpu.SemaphoreType.DMA,
                        pltpu.SemaphoreType.DMA((3,)), pltpu.SemaphoreType.DMA((3,)),
                        pltpu.SemaphoreType.DMA((2,)), pltpu.SemaphoreType.DMA((2,)), pltpu.SemaphoreType.DMA((2,)),
                        pltpu.SemaphoreType.DMA((7,)), pltpu.SemaphoreType.DMA((7,)), pltpu.SemaphoreType.DMA])
    return pl.pallas_call(
        body, name=name, grid_spec=grid_spec,
        out_shape=[jax.ShapeDtypeStruct((2, qr, D), F32), jax.ShapeDtypeStruct((2, hm, QW), F32),
                   jax.ShapeDtypeStruct((8, *small.shape), F32)],
        compiler_params=pltpu.CompilerParams(dimension_semantics=("arbitrary",), vmem_limit_bytes=VMEM_LIMIT),
    )(pos, a_out, b_out, a_in, b_in, small)


def _chips(x, y):
    return [(1 - x, y), (x, 1 - y), (1 - x, 1 - y)]


def _gather_weights(parts, split, x, pre, next_parts, pool_w, w_s, b_s):
    n = len(parts)
    nn = len(next_parts)
    S = x.shape[0]
    tm = TM_FWD
    nt = S // tm
    staged = [a for a in range(n) if split[a]]

    ns = nn + 3

    def body(x_ref, pre_ref, *refs):
        raw, refs = refs[:n], refs[n:]
        side_hbm, hb_ref = refs[:ns], refs[ns]
        refs = refs[ns + 1:]
        outs, refs = refs[:n], refs[n:]
        next_outs, poolb_ref, tril_ref, trilt_ref, bias_ref = refs[:nn], refs[nn], refs[nn + 1], refs[nn + 2], refs[nn + 3]
        refs = refs[nn + 4:]
        stage, refs = refs[:len(staged)], refs[len(staged):]
        side, (send_sems, recv_sems, local_sems, side_sems) = refs[:ns], refs[ns:]
        next_raw, pool_ref, ws_ref, bs_ref = side[:nn], side[nn], side[nn + 1], side[nn + 2]
        ins = [stage[staged.index(a)] if split[a] else raw[a] for a in range(n)]
        i = pl.program_id(0)

        def fetch(k):
            return pltpu.make_async_copy(side_hbm[k], side[k], side_sems.at[k])

        x, y, c = _position()
        me = 2 * x + y
        nbr_x, nbr_y, diag = _chips(x, y)
        id_x, id_y, id_d = (2 * chip[0] + chip[1] for chip in (nbr_x, nbr_y, diag))
        sibling = (x, y, 1 - c)

        def rows(a, ref, who, piece=None):
            r = parts[a].shape[0] // 2
            if piece is None:
                return ref.at[pl.ds(pl.multiple_of(who * r, 16), r), :]
            return ref.at[pl.ds(pl.multiple_of(who * r + piece * (r // 2), 16), r // 2), :]

        def copy(a, k, src, dst, to):
            return pltpu.make_async_remote_copy(src_ref=src, dst_ref=dst, send_sem=send_sems.at[a, k], recv_sem=recv_sems.at[a, k],
                                                device_id=to, device_id_type=MESH)

        def direct(a, k, chip):
            if split[a]:
                return copy(a, k, rows(a, ins[a], c), rows(a, outs[a].at[me], c), (*chip, c))
            return copy(a, k, ins[a], outs[a].at[me], (*chip, c))

        def arrival(a, k, src_id):
            if split[a]:
                return copy(a, k, rows(a, ins[a], c), rows(a, outs[a].at[src_id], c), (*nbr_x, c))
            return copy(a, k, ins[a], outs[a].at[src_id], (*nbr_x, c))

        def pass_on(a, k, src_id, piece, chip):
            region = rows(a, outs[a].at[src_id], c, piece)
            return copy(a, k, region, region, (*chip, c))

        def hand_over(a, k, src_id, who):
            region = rows(a, outs[a].at[src_id], who)
            return copy(a, k, region, region, sibling)

        def local(a):
            return pltpu.make_async_copy(ins[a], outs[a].at[me], local_sems.at[a])

        def first_sends(a):
            return [direct(a, 0, nbr_x), direct(a, 1, nbr_y)] + ([] if split[a] else [direct(a, 2, diag)])

        def after_x(a):
            return [pass_on(a, 3, id_x, 1, nbr_y), hand_over(a, 4, id_x, c)] if split[a] else []

        def after_y(a):
            return [pass_on(a, 2, id_y, 0, nbr_x), hand_over(a, 5, id_y, c)] if split[a] else []

        def after_diag(a):
            return [hand_over(a, 6, id_d, c)] if split[a] else []

        @pl.when(i == 0)
        def _():
            for a in staged:
                ins[a][...] = raw[a][...].astype(BF16)
            for a in range(n):
                local(a).start()
                for cp in first_sends(a):
                    cp.start()
            for k in range(ns):
                fetch(k).start()

        @pl.when(i == 1)
        def _():
            for k in range(ns):
                fetch(k).wait()
            for a in range(nn):
                next_outs[a][...] = next_raw[a][...].astype(BF16)
            poolb_ref[...] = pool_ref[...].astype(BF16)
            lower = lax.broadcasted_iota(jnp.int32, (CHUNK, CHUNK), 0) >= lax.broadcasted_iota(jnp.int32, (CHUNK, CHUNK), 1)
            for h in range(HEADS):
                tril = jnp.where(lower, ws_ref[h], 0.0)
                tril_ref[h] = tril.astype(BF16)
                trilt_ref[h] = tril.T.astype(BF16)
                bias_ref[h] = jnp.broadcast_to(bs_ref[h:h + 1, :], (CHUNK, GD)).T

        xh, _ = _rms_fwd(x_ref[...])
        hb_ref[...] = (xh * pre_ref[0:1, :]).astype(BF16)

        @pl.when(i == nt - 2)
        def _():
            for a in range(n):
                arrival(a, 0, id_x).wait_recv()
                for cp in after_x(a):
                    cp.start()
            for a in range(n):
                arrival(a, 1, id_y).wait_recv()
                for cp in after_y(a):
                    cp.start()

        @pl.when(i == nt - 1)
        def _():
            for a in range(n):
                if split[a]:
                    pass_on(a, 2, id_d, 0, nbr_x).wait_recv()
                    pass_on(a, 3, id_d, 1, nbr_y).wait_recv()
                    for cp in after_diag(a):
                        cp.start()
                else:
                    arrival(a, 2, id_d).wait_recv()
            for a in range(n):
                if split[a]:
                    for k, src_id in ((4, id_x), (5, id_y), (6, id_d)):
                        hand_over(a, k, src_id, 1 - c).wait_recv()
            for a in range(n):
                for cp in first_sends(a) + after_x(a) + after_y(a) + after_diag(a):
                    cp.wait_send()
                local(a).wait()

    any_spec = pl.BlockSpec(memory_space=pl.ANY)
    vmem = pl.BlockSpec(memory_space=pltpu.VMEM)
    sent = [BF16 if split[a] else parts[a].dtype for a in range(n)]
    return pl.pallas_call(
        body, name="gather_weights", grid=(nt,),
        in_specs=[_rows(tm, D), _full((2, D))] + [vmem] * n + [any_spec] * ns,
        out_specs=[_rows(tm, D)] + [any_spec] * n + [vmem] * (nn + 4),
        out_shape=[jax.ShapeDtypeStruct((S, D), BF16)] + [jax.ShapeDtypeStruct((4, *p.shape), t) for p, t in zip(parts, sent)]
        + [jax.ShapeDtypeStruct(p.shape, BF16) for p in next_parts]
        + [jax.ShapeDtypeStruct(pool_w.shape, BF16), jax.ShapeDtypeStruct(w_s.shape, BF16), jax.ShapeDtypeStruct(w_s.shape, BF16),
           jax.ShapeDtypeStruct((HEADS, CHUNK, GD), F32)],
        scratch_shapes=[pltpu.VMEM(parts[a].shape, BF16) for a in staged]
        + [pltpu.VMEM(p.shape, F32) for p in (*next_parts, pool_w, w_s, b_s)]
        + [pltpu.SemaphoreType.DMA((n, 7)), pltpu.SemaphoreType.DMA((n, 7)), pltpu.SemaphoreType.DMA((n,)),
           pltpu.SemaphoreType.DMA((ns,))],
        compiler_params=_params(),
    )(x, pre, *parts, *next_parts, pool_w, w_s, b_s)


def _adamw(w, g, m, v):
    m = ADAM_B1 * m + (1.0 - ADAM_B1) * g
    v = ADAM_B2 * v + (1.0 - ADAM_B2) * (g * g)
    m_hat = m / (1.0 - ADAM_B1 ** ADAM_STEP)
    v_hat = v / (1.0 - ADAM_B2 ** ADAM_STEP)
    delta = -ADAM_LR * (m_hat / (jnp.sqrt(v_hat) + ADAM_EPS) + ADAM_WD * w)
    return delta, m, v


def _adamw_big(ws, gs, ms, vs):
    steps = 4
    n = len(ws)

    def body(*refs):
        ins, outs = refs[:4 * n], refs[4 * n:]
        for a in range(n):
            w_ref, g_ref, m_ref, v_ref = ins[4 * a:4 * a + 4]
            go_ref, d_ref, mo_ref, vo_ref = outs[4 * a:4 * a + 4]
            gv = g_ref[...]
            go_ref[...] = gv
            d_ref[...], mo_ref[...], vo_ref[...] = _adamw(w_ref[...], gv, m_ref[...], v_ref[...])

    specs, shapes, operands = [], [], []
    for w, g, m, v in zip(ws, gs, ms, vs):
        rows, cols = w.shape
        specs += [pl.BlockSpec((rows // steps, cols), lambda i: (i, 0))] * 4
        shapes += [jax.ShapeDtypeStruct((rows, cols), F32)] * 4
        operands += [w, g, m, v]
    res = pl.pallas_call(
        body, name="adamw_big", grid=(steps,),
        in_specs=specs, out_specs=specs, out_shape=shapes,
        compiler_params=pltpu.CompilerParams(dimension_semantics=("arbitrary",), vmem_limit_bytes=VMEM_LIMIT),
    )(*operands)
    return [res[4 * a:4 * a + 4] for a in range(n)]


def _adamw_small(g0, g1, weights, moms, vels):
    names = ["pre", "post", "conv", "pw", "ps", "lng", "lnb", "ws", "bs"]
    shapes = [w.shape for w in weights]

    def body(*refs):
        me = 2 * lax.axis_index("x") + lax.axis_index("y")
        g0_ref, g1_ref = refs[0], refs[1]
        w_refs, m_refs, v_refs = refs[2:11], refs[11:20], refs[20:29]
        outs = refs[29:29 + 36]
        loss_ref = refs[65]
        t0_ref, t1_ref = refs[66], refs[67]
        t0 = g0_ref[0]
        t1 = g1_ref[0]
        for d in range(1, 8):
            t0 = t0 + g0_ref[d]
            t1 = t1 + g1_ref[d]
        t0_ref[...] = t0
        t1_ref[...] = t1
        loss_ref[...] = t1_ref[S1_LOSS:S1_LOSS + 1, 0:1]
        my_conv = pl.multiple_of(S0_CONV + 8 * me, 8)
        my_ln = pl.multiple_of(S1_LN + 8 * me, 8)

        def update(idx, piece, grad):
            go, do, mo, vo = outs[4 * idx:4 * idx + 4]
            go[piece] = grad
            do[piece], mo[piece], vo[piece] = _adamw(w_refs[idx][piece], grad, m_refs[idx][piece], v_refs[idx][piece])

        for layer in range(2):
            for k in range(D // GD):
                lanes = slice(k * GD, (k + 1) * GD)
                tref, pre0, post0 = (t0_ref, S0_PRE, S0_POST) if layer == 0 else (t1_ref, S1_PRE, S1_POST)
                update(0, (slice(layer, layer + 1), lanes), tref[pre0 + k:pre0 + k + 1, :])
                update(1, (slice(layer, layer + 1), lanes), tref[post0 + k:post0 + k + 1, :])
        conv_rows = t0_ref[pl.ds(my_conv, 8), :]
        update(2, (slice(0, 3), 0, slice(None)), conv_rows[0:3, :])
        for g in range(4):
            update(3, (g,), t0_ref[S0_PW + g * GD:S0_PW + (g + 1) * GD, :])
            update(4, (slice(0, 1), slice(g * GD, (g + 1) * GD)), t0_ref[S0_PS + g:S0_PS + g + 1, :])
        ln_rows = t1_ref[pl.ds(my_ln, 8), :]
        for k in range(2):
            update(5, (slice(0, 1), slice(k * GD, (k + 1) * GD)), ln_rows[k:k + 1, :])
            update(6, (slice(0, 1), slice(k * GD, (k + 1) * GD)), ln_rows[2 + k:3 + k, :])
        for h in range(HEADS):
            update(7, (h,), t1_ref[S1_WS + h * CHUNK:S1_WS + (h + 1) * CHUNK, :])
        update(8, (slice(None), slice(None)), t1_ref[S1_BS:S1_BS + HEADS, :])

    vm = pl.BlockSpec(memory_space=pltpu.VMEM)
    out_shape = []
    for s in shapes:
        out_shape += [jax.ShapeDtypeStruct(s, F32)] * 4
    out_shape.append(jax.ShapeDtypeStruct((1, 1), F32))
    res = pl.pallas_call(
        body, name="adamw_small",
        in_specs=[vm] * 29, out_specs=[vm] * 37, out_shape=out_shape,
        scratch_shapes=[pltpu.VMEM((S0_ROWS, GD), F32), pltpu.VMEM((S1_ROWS, GD), F32)],
        compiler_params=pltpu.CompilerParams(vmem_limit_bytes=VMEM_LIMIT),
    )(g0, g1, *weights, *moms, *vels)
    per_weight = {nm: res[4 * i:4 * i + 4] for i, nm in enumerate(names)}
    return per_weight, res[36]


def _pad8(a):
    return jnp.pad(a, ((0, 8 - a.shape[0]), (0, 0)))


def kernel(x, pre_norm, post_norm, even_w_in, even_conv_w, even_pool_w, even_pool_scale, even_w_out, odd_w_in, odd_ln_g, odd_ln_b, odd_w_s, odd_b_s, odd_w_out, loss_target, m_pre_norm, m_post_norm, m_even_w_in, m_even_conv_w, m_even_pool_w, m_even_pool_scale, m_even_w_out, m_odd_w_in, m_odd_ln_g, m_odd_ln_b, m_odd_w_s, m_odd_b_s, m_odd_w_out, v_pre_norm, v_post_norm, v_even_w_in, v_even_conv_w, v_even_pool_w, v_even_pool_scale, v_even_w_out, v_odd_w_in, v_odd_ln_g, v_odd_ln_b, v_odd_w_s, v_odd_b_s, v_odd_w_out):
    xs = x[0]
    tgt = loss_target[0]

    small_shard = jnp.concatenate([_pad8(even_conv_w[0]), _pad8(odd_ln_g.reshape(2, GD)), _pad8(odd_ln_b.reshape(2, GD))], axis=0)
    hb0, win0, wout0, shard, win1_shard, wout1_shard, pool_wb, ws_tril, ws_tril_t, bias = _gather_weights(
        [even_w_in[0], even_w_out[0], small_shard], [True, True, False], xs, pre_norm, [odd_w_in[0], odd_w_out[0]],
        even_pool_w[0], odd_w_s[0], odd_b_s[0])
    wout0 = wout0.reshape(D, D)
    px, py, pc = _position()
    pos = jnp.stack([pc, 2 * px + py]).astype(jnp.int32)

    x1, proj0, m0, mixp0, conv0, mixed0, pooled0, win1, wout1 = _even_fwd(
        xs, hb0, post_norm, win0, shard, pool_wb, even_pool_scale, wout0, [win1_shard, wout1_shard])
    wout1 = wout1.reshape(D, D)
    pre1, post1 = pre_norm[1:2], post_norm[1:2]
    proj1, m1, hb1, yp1, dx2, loss_part = _odd_fwd(x1, tgt, pre1, post1, win1, shard, ws_tril, bias, wout1)
    dx1, dproj1, dmb1, small1 = _odd_bwd(dx2, x1, proj1, m1, loss_part, pre1, post1, win1, shard, ws_tril, ws_tril_t, bias, wout1)
    g_out1, g_in1, all1 = _wgrad_layer(yp1, dmb1, hb1, dproj1, small1, pos, "wgrad_odd")
    gx, dproj0, dmb0, small0 = _even_bwd(dx1, xs, proj0, conv0, mixed0, pooled0, m0, pre_norm, post_norm, win0, shard, pool_wb,
                                         even_pool_scale, wout0)
    g_out0, g_in0, all0 = _wgrad_layer(mixp0, dmb0, hb0, dproj0, small0, pos, "wgrad_even")

    big_w = [even_w_in[0], even_w_out[0], odd_w_in[0], odd_w_out[0]]
    big_g = [g.reshape(w.shape) for g, w in zip([g_in0, g_out0, g_in1, g_out1], big_w)]
    big_m = [m_even_w_in[0], m_even_w_out[0], m_odd_w_in[0], m_odd_w_out[0]]
    big_v = [v_even_w_in[0], v_even_w_out[0], v_odd_w_in[0], v_odd_w_out[0]]
    big = _adamw_big(big_w, big_g, big_m, big_v)

    def taps_first(a):
        return jnp.swapaxes(a, 0, 1)

    small_w = [pre_norm, post_norm, taps_first(even_conv_w), even_pool_w[0], even_pool_scale, odd_ln_g, odd_ln_b, odd_w_s[0], odd_b_s[0]]
    small_m = [m_pre_norm, m_post_norm, taps_first(m_even_conv_w), m_even_pool_w[0], m_even_pool_scale, m_odd_ln_g, m_odd_ln_b,
               m_odd_w_s[0], m_odd_b_s[0]]
    small_v = [v_pre_norm, v_post_norm, taps_first(v_even_conv_w), v_even_pool_w[0], v_even_pool_scale, v_odd_ln_g, v_odd_ln_b,
               v_odd_w_s[0], v_odd_b_s[0]]
    sm, loss = _adamw_small(all0, all1, small_w, small_m, small_v)

    def lead(a):
        return a[None]

    per = {
        "pre_norm": sm["pre"], "post_norm": sm["post"],
        "even_w_in": [lead(a) for a in big[0]], "even_conv_w": [taps_first(a) for a in sm["conv"]],
        "even_pool_w": [lead(a) for a in sm["pw"]], "even_pool_scale": sm["ps"],
        "even_w_out": [lead(a) for a in big[1]], "odd_w_in": [lead(a) for a in big[2]],
        "odd_ln_g": sm["lng"], "odd_ln_b": sm["lnb"],
        "odd_w_s": [lead(a) for a in sm["ws"]], "odd_b_s": [lead(a) for a in sm["bs"]],
        "odd_w_out": [lead(a) for a in big[3]],
    }
    order = ["pre_norm", "post_norm", "even_w_in", "even_conv_w", "even_pool_w", "even_pool_scale", "even_w_out", "odd_w_in",
             "odd_ln_g", "odd_ln_b", "odd_w_s", "odd_b_s", "odd_w_out"]
    outs = [loss.reshape(()), gx[None]]
    for kind in range(4):
        outs += [per[nm][kind] for nm in order]
    return tuple(outs)
```
